```python
import math
import jax, jax.numpy as jnp
from jax import lax
import numpy as np


D_MODEL = 1024
BATCH = 8
SEQ = 2048
DEPTH = 2

N_MEM = 256
GLA_HEADS = 4
GLA_DK = 64
GLA_DV = 128
GLA_RANK = 16
GLA_TAU = 16.0
GLA_CHUNK = 64
NSA_HEADS = 8
NSA_GROUPS = 2
NSA_DH = 64
CMP_LEN = 32
CMP_STRIDE = 16
CMP_HIDDEN = 256
SLC_LEN = 64
N_SEL = 8
WINDOW = 512
DIFF_HEADS = 8
DIFF_DH = 64
DIFF_DV = 128
XATTN_HEADS = 4
XATTN_DH = 64
D_FF = 2816
N_EXPERTS = 8
TOP_K = 2
Q_BLOCK = 128
EPS = 1e-6
NEG = -1e30
TINY = 1e-30
FORCE_SCORE = 1e4

GLA_SIZES = [GLA_HEADS * GLA_DK, GLA_HEADS * GLA_DK, GLA_HEADS * GLA_DV, GLA_HEADS * GLA_DV, GLA_RANK]
NSA_KV = NSA_GROUPS * NSA_DH
NSA_SIZES = [NSA_HEADS * NSA_DH, NSA_KV, NSA_KV, NSA_KV, NSA_KV, NSA_KV, NSA_KV, 3 * NSA_HEADS]
P0_SIZES = GLA_SIZES + NSA_SIZES
P0 = sum(P0_SIZES)
MIX0_OUT = GLA_HEADS * GLA_DV + NSA_HEADS * NSA_DH
P1_SIZES = [DIFF_HEADS * DIFF_DH] * 4 + [DIFF_HEADS * DIFF_DV]
P1 = sum(P1_SIZES)
MIX1_OUT = DIFF_HEADS * DIFF_DV

kernel_name = 'hybrid_gla_nsa_diffattn_moe'


def _split_points(sizes):
    return np.cumsum(sizes)[:-1].tolist()


def rms_norm(x, g):
    xf = x.astype(jnp.float32)
    y = xf * lax.rsqrt(jnp.mean(xf * xf, axis=-1, keepdims=True) + EPS)
    return (y * g.astype(jnp.float32)).astype(x.dtype)


def alibi_slopes(n):
    return jnp.asarray(2.0 ** (-8.0 * np.arange(1, n + 1) / n), dtype=jnp.float32)


def masked_softmax(s, mask, axis=-1):
    s = jnp.where(mask, s, NEG)
    p = jnp.exp(s - jnp.max(s, axis=axis, keepdims=True)) * mask
    return p / jnp.maximum(jnp.sum(p, axis=axis, keepdims=True), TINY)


def swiglu(h, wg, wu, wd):
    return (jax.nn.silu(h @ wg) * (h @ wu)) @ wd


def gla_mixer(q, k, v, r, a_lr, w_alpha, b_alpha, g_norm):
    B, S, _ = q.shape
    H, C = GLA_HEADS, GLA_CHUNK
    n_chunks = S // C
    f32 = jnp.float32

    def heads(t, d):
        return t.reshape(B, S, H, d).transpose(0, 2, 1, 3).astype(f32)

    log_alpha = jax.nn.log_sigmoid(heads(a_lr @ w_alpha + b_alpha, GLA_DK)) / GLA_TAU
    qh = heads(q, GLA_DK) * (GLA_DK ** -0.5)
    kh = heads(k, GLA_DK)
    vh = heads(v, GLA_DV)

    def chunks(t):
        return jnp.moveaxis(t.reshape(B, H, n_chunks, C, t.shape[-1]), 2, 0)

    causal = jnp.tril(jnp.ones((C, C), dtype=bool))[:, :, None]

    def step(state, inp):
        qc, kc, vc, gc = inp
        b = jnp.cumsum(gc, axis=2)
        o_inter = jnp.einsum('bhtk,bhkv->bhtv', qc * jnp.exp(b), state)
        rel = b[:, :, :, None, :] - b[:, :, None, :, :]
        decay = jnp.exp(jnp.where(causal, rel, -jnp.inf))
        scores = jnp.einsum('bhtk,bhsk,bhtsk->bhts', qc, kc, decay)
        o_intra = jnp.einsum('bhts,bhsv->bhtv', scores, vc)
        b_end = b[:, :, -1:, :]
        state = jnp.exp(b_end[:, :, 0, :, None]) * state + jnp.einsum('bhsk,bhsv->bhkv', kc * jnp.exp(b_end - b), vc)
        return state, o_inter + o_intra

    state0 = jnp.zeros((B, H, GLA_DK, GLA_DV), f32)
    _, o = lax.scan(step, state0, (chunks(qh), chunks(kh), chunks(vh), chunks(log_alpha)))
    o = jnp.moveaxis(o, 0, 2).reshape(B, H, S, GLA_DV)
    o = rms_norm(o, g_norm).transpose(0, 2, 1, 3).reshape(B, S, H * GLA_DV)
    return (o * jax.nn.silu(r.astype(f32))).astype(q.dtype)


def _nsa_geometry(S):
    n_cmp = (S - CMP_LEN) // CMP_STRIDE + 1
    c_start = np.arange(n_cmp) * CMP_STRIDE
    c_end = c_start + CMP_LEN - 1
    n_slc = S // SLC_LEN
    j_start = np.arange(n_slc) * SLC_LEN
    overlap = (c_end[:, None] >= j_start[None]) & (c_start[:, None] <= j_start[None] + SLC_LEN - 1)
    return n_cmp, c_start, c_end, n_slc, overlap.astype(np.float32)


def nsa_compress(t, pos_emb, w1, w2, c_start):
    idx = c_start[:, None] + np.arange(CMP_LEN)[None, :]
    blocks = t[:, :, idx] + pos_emb
    flat = blocks.reshape(blocks.shape[0], blocks.shape[1], blocks.shape[2], CMP_LEN * NSA_DH)
    return (jax.nn.gelu(flat @ w1) @ w2).astype(jnp.float32)


def nsa_mixer(q, k_cmp, v_cmp, k_slc, v_slc, k_win, v_win, gates, pos_k, w1_k, w2_k, pos_v, w1_v, w2_v):
    B, S, _ = q.shape
    G, Hg, dh = NSA_GROUPS, NSA_HEADS // NSA_GROUPS, NSA_DH
    f32 = jnp.float32
    n_cmp, c_start, c_end, n_slc, overlap = _nsa_geometry(S)
    nqb = S // Q_BLOCK
    pos = jnp.arange(S)
    slopes = alibi_slopes(NSA_HEADS).reshape(G, Hg)
    qh = q.reshape(B, S, G, Hg, dh).transpose(0, 2, 3, 1, 4).astype(f32) * (dh ** -0.5)

    def kv(t):
        return t.reshape(B, S, G, dh).transpose(0, 2, 1, 3).astype(f32)

    def q_blocks(t):
        return jnp.moveaxis(t.reshape(B, G, Hg, nqb, Q_BLOCK, dh), 3, 0)

    def unblock(t):
        return jnp.moveaxis(t, 0, 3).reshape(B, G, Hg, S, dh)

    kc = nsa_compress(kv(k_cmp), pos_k, w1_k, w2_k, c_start)
    vc = nsa_compress(kv(v_cmp), pos_v, w1_v, w2_v, c_start)
    dist_c = pos[:, None] - jnp.asarray(c_end)[None, :]
    s_c = jnp.einsum('bghtd,bgcd->bghtc', qh, kc) - slopes[None, :, :, None, None] * dist_c
    p_c = masked_softmax(s_c, dist_c >= 0)
    o_cmp = jnp.einsum('bghtc,bgcd->bghtd', p_c, vc)

    imp = jnp.einsum('bghtc,cj->bgtj', p_c, jnp.asarray(overlap))
    blk = jnp.arange(n_slc)
    cur = pos // SLC_LEN
    forced = (blk[None, :] == cur[:, None]) | (blk[None, :] == 0)
    valid = blk[None, :] <= cur[:, None]
    score = jnp.where(forced, FORCE_SCORE, jnp.where(valid, imp, NEG))
    _, sel = lax.top_k(score, min(N_SEL, n_slc))
    ks = kv(k_slc).reshape(B, G, n_slc, SLC_LEN, dh)
    vs = kv(v_slc).reshape(B, G, n_slc, SLC_LEN, dh)
    b_ix = jnp.arange(B)[:, None, None, None]
    g_ix = jnp.arange(G)[None, :, None, None]
    sel_b = jnp.moveaxis(sel.reshape(B, G, nqb, Q_BLOCK, sel.shape[-1]), 2, 0)
    pos_b = pos.reshape(nqb, Q_BLOCK)

    def slc_block(args):
        qb, ib, tq = args
        kg = ks[b_ix, g_ix, ib]
        vg = vs[b_ix, g_ix, ib]
        key_pos = ib[..., None] * SLC_LEN + jnp.arange(SLC_LEN)
        dist = tq[None, None, :, None, None] - key_pos
        s = jnp.einsum('bghtd,bgtkld->bghtkl', qb, kg) - slopes[None, :, :, None, None, None] * dist[:, :, None]
        p = masked_softmax(s, (dist >= 0)[:, :, None], axis=(-2, -1))
        return jnp.einsum('bghtkl,bgtkld->bghtd', p, vg)

    o_slc = unblock(lax.map(slc_block, (q_blocks(qh), sel_b, pos_b)))

    kw = jnp.pad(kv(k_win), ((0, 0), (0, 0), (WINDOW, 0), (0, 0)))
    vw = jnp.pad(kv(v_win), ((0, 0), (0, 0), (WINDOW, 0), (0, 0)))
    span = WINDOW + Q_BLOCK

    def win_block(args):
        qb, start = args
        kb = lax.dynamic_slice_in_dim(kw, start, span, axis=2)
        vb = lax.dynamic_slice_in_dim(vw, start, span, axis=2)
        tq = start + jnp.arange(Q_BLOCK)
        ts = start - WINDOW + jnp.arange(span)
        dist = tq[:, None] - ts[None, :]
        mask = (dist >= 0) & (dist < WINDOW) & (ts[None, :] >= 0)
        s = jnp.einsum('bghtd,bgsd->bghts', qb, kb) - slopes[None, :, :, None, None] * dist
        p = masked_softmax(s, mask)
        return jnp.einsum('bghts,bgsd->bghtd', p, vb)

    o_win = unblock(lax.map(win_block, (q_blocks(qh), jnp.arange(nqb) * Q_BLOCK)))

    g = jax.nn.sigmoid(gates.astype(f32)).reshape(B, S, G, Hg, 3).transpose(0, 2, 3, 1, 4)
    o = g[..., 0:1] * o_cmp + g[..., 1:2] * o_slc + g[..., 2:3] * o_win
    return o.transpose(0, 3, 1, 2, 4).reshape(B, S, NSA_HEADS * dh).astype(q.dtype)


def diff_mixer(q1, q2, k1, k2, v, lq1, lk1, lq2, lk2, g_norm, lambda_init):
    B, S, _ = q1.shape
    H = DIFF_HEADS
    f32 = jnp.float32
    nqb = S // Q_BLOCK

    def heads(t, d):
        return t.reshape(B, S, H, d).transpose(0, 2, 1, 3).astype(f32)

    q = jnp.stack([heads(q1, DIFF_DH), heads(q2, DIFF_DH)], axis=2) * (DIFF_DH ** -0.5)
    k = jnp.stack([heads(k1, DIFF_DH), heads(k2, DIFF_DH)], axis=2)
    vh = heads(v, DIFF_DV)
    lam = (jnp.exp(jnp.sum(lq1.astype(f32) * lk1.astype(f32))) - jnp.exp(jnp.sum(lq2.astype(f32) * lk2.astype(f32))) + lambda_init)
    slopes = alibi_slopes(H)
    kpos = jnp.arange(S)
    qb_all = jnp.moveaxis(q.reshape(B, H, 2, nqb, Q_BLOCK, DIFF_DH), 3, 0)

    def block(args):
        qb, tq = args
        dist = tq[:, None] - kpos[None, :]
        s = jnp.einsum('bhiqd,bhisd->bhiqs', qb, k) - slopes[None, :, None, None, None] * dist
        p = masked_softmax(s, dist >= 0)
        return jnp.einsum('bhqs,bhsd->bhqd', p[:, :, 0] - lam * p[:, :, 1], vh)

    o = lax.map(block, (qb_all, kpos.reshape(nqb, Q_BLOCK)))
    o = jnp.moveaxis(o, 0, 2).reshape(B, H, S, DIFF_DV)
    o = rms_norm(o, g_norm) * (1.0 - lambda_init)
    return o.transpose(0, 2, 1, 3).reshape(B, S, H * DIFF_DV).astype(q1.dtype)


def memory_cross_attention(h, mem_n, wq, wk, wv, wo):
    B, S, _ = h.shape
    M = mem_n.shape[1]
    q = (h @ wq).reshape(B, S, XATTN_HEADS, XATTN_DH)
    k = (mem_n @ wk).reshape(B, M, XATTN_HEADS, XATTN_DH)
    v = (mem_n @ wv).reshape(B, M, XATTN_HEADS, XATTN_DH)
    s = jnp.einsum('bthd,bmhd->bhtm', q, k).astype(jnp.float32) * (XATTN_DH ** -0.5)
    p = jax.nn.softmax(s, axis=-1).astype(v.dtype)
    o = jnp.einsum('bhtm,bmhd->bthd', p, v).reshape(B, S, XATTN_HEADS * XATTN_DH)
    return o @ wo


def moe_swiglu(h, w_router, wg, wu, wd):
    B, S, D = h.shape
    t = h.reshape(B * S, D)
    logits = (t @ w_router).astype(jnp.float32)
    top_v, top_i = lax.top_k(logits, TOP_K)
    w = jax.nn.softmax(top_v, axis=-1)
    combine = jnp.sum(jax.nn.one_hot(top_i, N_EXPERTS, dtype=jnp.float32) * w[..., None], axis=1)
    y = jnp.zeros((B * S, D), jnp.float32)
    for e in range(N_EXPERTS):
        y = y + combine[:, e:e + 1] * swiglu(t, wg[e], wu[e], wd[e]).astype(jnp.float32)
    return y.reshape(B, S, D).astype(h.dtype)


def even_layer(x, mem, mix_norm, w_in, gla_w_alpha, gla_b_alpha, gla_out_norm, nsa_cmp_pos_k, nsa_cmp_w1_k, nsa_cmp_w2_k, nsa_cmp_pos_v, nsa_cmp_w1_v, nsa_cmp_w2_v, w_out, xattn_norm, xattn_mem_norm, xattn_wq, xattn_wk, xattn_wv, xattn_wo, ffn_norm, ffn_w_gate, ffn_w_up, ffn_w_down):
    h = rms_norm(x, mix_norm) @ w_in
    (g_q, g_k, g_v, g_r, g_a, n_q, n_kc, n_vc, n_ks, n_vs, n_kw, n_vw, n_g) = jnp.split(h, _split_points(P0_SIZES), axis=-1)
    o_a = gla_mixer(g_q, g_k, g_v, g_r, g_a, gla_w_alpha, gla_b_alpha, gla_out_norm)
    o_b = nsa_mixer(n_q, n_kc, n_vc, n_ks, n_vs, n_kw, n_vw, n_g, nsa_cmp_pos_k, nsa_cmp_w1_k, nsa_cmp_w2_k, nsa_cmp_pos_v, nsa_cmp_w1_v, nsa_cmp_w2_v)
    x = x + jnp.concatenate([o_a, o_b], axis=-1) @ w_out
    x = x + memory_cross_attention(rms_norm(x, xattn_norm), rms_norm(mem, xattn_mem_norm), xattn_wq, xattn_wk, xattn_wv, xattn_wo)
    x = x + swiglu(rms_norm(x, ffn_norm), ffn_w_gate, ffn_w_up, ffn_w_down)
    return x


def odd_layer(x, mem, layer_idx, mix_norm, w_in, diff_lq1, diff_lk1, diff_lq2, diff_lk2, diff_out_norm, w_out, xattn_norm, xattn_mem_norm, xattn_wq, xattn_wk, xattn_wv, xattn_wo, ffn_norm, moe_router, moe_w_gate, moe_w_up, moe_w_down):
    lambda_init = 0.8 - 0.6 * math.exp(-0.3 * layer_idx)
    h = rms_norm(x, mix_norm) @ w_in
    q1, q2, k1, k2, v = jnp.split(h, _split_points(P1_SIZES), axis=-1)
    x = x + diff_mixer(q1, q2, k1, k2, v, diff_lq1, diff_lk1, diff_lq2, diff_lk2, diff_out_norm, lambda_init) @ w_out
    x = x + memory_cross_attention(rms_norm(x, xattn_norm), rms_norm(mem, xattn_mem_norm), xattn_wq, xattn_wk, xattn_wv, xattn_wo)
    x = x + moe_swiglu(rms_norm(x, ffn_norm), moe_router, moe_w_gate, moe_w_up, moe_w_down)
    return x


def setup_inputs(seed: int = 0) -> dict:
    key = jax.random.key(seed)
    ks = iter(jax.random.split(key, 64))
    f32 = jnp.float32

    def nrm(shape, scale):
        return jax.random.normal(next(ks), shape, f32) * scale

    def gain(n):
        return 1.0 + 0.02 * jax.random.normal(next(ks), (n,), f32)

    D = D_MODEL
    XW = XATTN_HEADS * XATTN_DH
    inp = {}
    inp['x'] = nrm((BATCH, SEQ, D), 1.0)
    inp['mem'] = nrm((BATCH, N_MEM, D), 1.0)
    inp['mix_norm_0'] = gain(D)
    inp['w_in_0'] = nrm((D, P0), D ** -0.5)
    inp['gla_w_alpha_0'] = nrm((GLA_RANK, GLA_HEADS * GLA_DK), GLA_RANK ** -0.5)
    inp['gla_b_alpha_0'] = nrm((GLA_HEADS * GLA_DK,), 0.01)
    inp['gla_out_norm_0'] = gain(GLA_DV)
    inp['nsa_cmp_pos_k_0'] = nrm((CMP_LEN, NSA_DH), 0.02)
    inp['nsa_cmp_w1_k_0'] = nrm((CMP_LEN * NSA_DH, CMP_HIDDEN), (CMP_LEN * NSA_DH) ** -0.5)
    inp['nsa_cmp_w2_k_0'] = nrm((CMP_HIDDEN, NSA_DH), CMP_HIDDEN ** -0.5)
    inp['nsa_cmp_pos_v_0'] = nrm((CMP_LEN, NSA_DH), 0.02)
    inp['nsa_cmp_w1_v_0'] = nrm((CMP_LEN * NSA_DH, CMP_HIDDEN), (CMP_LEN * NSA_DH) ** -0.5)
    inp['nsa_cmp_w2_v_0'] = nrm((CMP_HIDDEN, NSA_DH), CMP_HIDDEN ** -0.5)
    inp['w_out_0'] = nrm((MIX0_OUT, D), MIX0_OUT ** -0.5)
    inp['xattn_norm_0'] = gain(D)
    inp['xattn_mem_norm_0'] = gain(D)
    inp['xattn_wq_0'] = nrm((D, XW), D ** -0.5)
    inp['xattn_wk_0'] = nrm((D, XW), D ** -0.5)
    inp['xattn_wv_0'] = nrm((D, XW), D ** -0.5)
    inp['xattn_wo_0'] = nrm((XW, D), XW ** -0.5)
    inp['ffn_norm_0'] = gain(D)
    inp['ffn_w_gate_0'] = nrm((D, D_FF), D ** -0.5)
    inp['ffn_w_up_0'] = nrm((D, D_FF), D ** -0.5)
    inp['ffn_w_down_0'] = nrm((D_FF, D), D_FF ** -0.5)
    inp['mix_norm_1'] = gain(D)
    inp['w_in_1'] = nrm((D, P1), D ** -0.5)
    inp['diff_lq1_1'] = nrm((DIFF_DH,), 0.1)
    inp['diff_lk1_1'] = nrm((DIFF_DH,), 0.1)
    inp['diff_lq2_1'] = nrm((DIFF_DH,), 0.1)
    inp['diff_lk2_1'] = nrm((DIFF_DH,), 0.1)
    inp['diff_out_norm_1'] = gain(DIFF_DV)
    inp['w_out_1'] = nrm((MIX1_OUT, D), MIX1_OUT ** -0.5)
    inp['xattn_norm_1'] = gain(D)
    inp['xattn_mem_norm_1'] = gain(D)
    inp['xattn_wq_1'] = nrm((D, XW), D ** -0.5)
    inp['xattn_wk_1'] = nrm((D, XW), D ** -0.5)
    inp['xattn_wv_1'] = nrm((D, XW), D ** -0.5)
    inp['xattn_wo_1'] = nrm((XW, D), XW ** -0.5)
    inp['ffn_norm_1'] = gain(D)
    inp['moe_router_1'] = nrm((D, N_EXPERTS), D ** -0.5)
    inp['moe_w_gate_1'] = nrm((N_EXPERTS, D, D_FF), D ** -0.5)
    inp['moe_w_up_1'] = nrm((N_EXPERTS, D, D_FF), D ** -0.5)
    inp['moe_w_down_1'] = nrm((N_EXPERTS, D_FF, D), D_FF ** -0.5)
    inp['final_norm'] = gain(D)
    return inp


def reference(x, mem, mix_norm_0, w_in_0, gla_w_alpha_0, gla_b_alpha_0, gla_out_norm_0, nsa_cmp_pos_k_0, nsa_cmp_w1_k_0, nsa_cmp_w2_k_0, nsa_cmp_pos_v_0, nsa_cmp_w1_v_0, nsa_cmp_w2_v_0, w_out_0, xattn_norm_0, xattn_mem_norm_0, xattn_wq_0, xattn_wk_0, xattn_wv_0, xattn_wo_0, ffn_norm_0, ffn_w_gate_0, ffn_w_up_0, ffn_w_down_0, mix_norm_1, w_in_1, diff_lq1_1, diff_lk1_1, diff_lq2_1, diff_lk2_1, diff_out_norm_1, w_out_1, xattn_norm_1, xattn_mem_norm_1, xattn_wq_1, xattn_wk_1, xattn_wv_1, xattn_wo_1, ffn_norm_1, moe_router_1, moe_w_gate_1, moe_w_up_1, moe_w_down_1, final_norm):
    layer_params = (
        (mix_norm_0, w_in_0, gla_w_alpha_0, gla_b_alpha_0, gla_out_norm_0, nsa_cmp_pos_k_0, nsa_cmp_w1_k_0, nsa_cmp_w2_k_0, nsa_cmp_pos_v_0, nsa_cmp_w1_v_0, nsa_cmp_w2_v_0, w_out_0, xattn_norm_0, xattn_mem_norm_0, xattn_wq_0, xattn_wk_0, xattn_wv_0, xattn_wo_0, ffn_norm_0, ffn_w_gate_0, ffn_w_up_0, ffn_w_down_0),
        (mix_norm_1, w_in_1, diff_lq1_1, diff_lk1_1, diff_lq2_1, diff_lk2_1, diff_out_norm_1, w_out_1, xattn_norm_1, xattn_mem_norm_1, xattn_wq_1, xattn_wk_1, xattn_wv_1, xattn_wo_1, ffn_norm_1, moe_router_1, moe_w_gate_1, moe_w_up_1, moe_w_down_1),
    )
    h = x
    for layer in range(DEPTH):
        p = layer_params[layer]
        if layer % 2 == 0:
            h = even_layer(h, mem, *p)
        else:
            h = odd_layer(h, mem, layer, *p)
    return rms_norm(h, final_norm)
```

```python
import functools
import math

import numpy as np
import jax
import jax.numpy as jnp
from jax import lax
from jax.experimental import pallas as pl
from jax.experimental.pallas import tpu as pltpu

F32 = jnp.float32
BF16 = jnp.bfloat16
I32 = jnp.int32

D_MODEL = 1024
GLA_HEADS, GLA_DK, GLA_DV, GLA_RANK, GLA_TAU, GLA_CHUNK = 4, 64, 128, 16, 16.0, 64
NSA_HEADS, NSA_GROUPS, NSA_DH = 8, 2, 64
CMP_LEN, CMP_STRIDE, CMP_HIDDEN, SLC_LEN, N_SEL, WINDOW = 32, 16, 256, 64, 8, 512
DIFF_HEADS, DIFF_DH, DIFF_DV = 8, 64, 128
XATTN_HEADS, XATTN_DH = 4, 64
D_FF, N_EXPERTS, TOP_K = 2816, 8, 2
EPS, NEG, TINY, FORCE_SCORE = 1e-6, -1e30, 1e-30, 1e4

LANES = 128
HALF = LANES // 2
VMEM_LIMIT = 56 * 1024 * 1024
LOWEST = -3.0e38

HB0_WIDTH, HF0_WIDTH = 17 * LANES, 7 * LANES
NSA_Q_BLK, NSA_KS_BLK, NSA_VS_BLK, NSA_KW_BLK, NSA_VW_BLK = 0, 4, 5, 6, 7
GLA_COLS = (8, 10, 12, 16, 0)
NSA_GATE_BLK, NSA_KC_BLK, NSA_VC_BLK = 4, 5, 6


def _cp(*sem):
    return pltpu.CompilerParams(dimension_semantics=sem, vmem_limit_bytes=VMEM_LIMIT)


def _rms(x, g):
    y = x * lax.rsqrt(jnp.mean(x * x, axis=-1, keepdims=True) + EPS)
    return y * g


def _dot(a, b):
    return jnp.dot(a, b, preferred_element_type=F32)


def _dot_nt(a, b):
    return lax.dot_general(a, b, (((1,), (1,)), ((), ())), preferred_element_type=F32)


def _lane(shape):
    return lax.broadcasted_iota(I32, shape, len(shape) - 1)


def _norm_proj_body(x_ref, g_ref, w_ref, *o_refs, widths):
    xn = _rms(x_ref[...], g_ref[...]).astype(BF16)
    off = 0
    for o_ref, wd in zip(o_refs, widths):
        for c0 in range(0, wd, 512):
            cw = min(512, wd - c0)
            o_ref[:, c0:c0 + cw] = _dot(xn, w_ref[:, off + c0:off + c0 + cw]).astype(o_ref.dtype)
        off += wd


def norm_proj(x, g, w, outs, tm, name):
    T, D = x.shape
    widths = tuple(o[0] for o in outs)
    return pl.pallas_call(
        functools.partial(_norm_proj_body, widths=widths),
        grid=(T // tm,),
        in_specs=[pl.BlockSpec((tm, D), lambda i: (i, 0)),
                  pl.BlockSpec((1, D), lambda i: (0, 0)),
                  pl.BlockSpec(w.shape, lambda i: (0, 0))],
        out_specs=[pl.BlockSpec((tm, wd), lambda i: (i, 0)) for wd, _ in outs],
        out_shape=[jax.ShapeDtypeStruct((T, wd), dt) for wd, dt in outs],
        compiler_params=_cp("parallel"), name=name)(x, g.reshape(1, D), w)


def _proj_res_body(*refs, n):
    res_ref = refs[0]
    a_refs = refs[1:1 + n]
    w_refs = refs[1 + n:1 + 2 * n]
    o_ref = refs[1 + 2 * n]
    acc = res_ref[...]
    for a_ref, w_ref in zip(a_refs, w_refs):
        acc = acc + _dot(a_ref[...].astype(BF16), w_ref[...])
    o_ref[...] = acc


def proj_residual(res, a_list, w_list, tm, name):
    T, D = res.shape
    n = len(a_list)
    return pl.pallas_call(
        functools.partial(_proj_res_body, n=n),
        grid=(T // tm,),
        in_specs=([pl.BlockSpec((tm, D), lambda i: (i, 0))]
                  + [pl.BlockSpec((tm, a.shape[1]), lambda i: (i, 0)) for a in a_list]
                  + [pl.BlockSpec(w.shape, lambda i: (0, 0)) for w in w_list]),
        out_specs=pl.BlockSpec((tm, D), lambda i: (i, 0)),
        out_shape=jax.ShapeDtypeStruct((T, D), F32),
        compiler_params=_cp("parallel"), name=name)(res, *a_list, *w_list)


def _xattn_body(x_ref, g_ref, wq_ref, k_ref, v_ref, wo_ref, o_ref):
    x = x_ref[...]
    q = _dot(_rms(x, g_ref[...]).astype(BF16), wq_ref[...]).astype(BF16)
    lane = _lane((1, LANES))
    blocks = []
    for blk in range(XATTN_HEADS // 2):
        qb = q[:, blk * LANES:(blk + 1) * LANES]
        kb = k_ref[0, :, blk * LANES:(blk + 1) * LANES]
        vb = v_ref[0, :, blk * LANES:(blk + 1) * LANES]
        pair = None
        for half in range(2):
            in_half = (lane // HALF) == half
            s = _dot_nt(jnp.where(in_half, qb, jnp.zeros_like(qb)), kb)
            p = jnp.exp(s - jnp.max(s, axis=-1, keepdims=True))
            p = p / jnp.sum(p, axis=-1, keepdims=True)
            o = _dot(p.astype(BF16), vb)
            pair = o if half == 0 else jnp.where(in_half, o, pair)
        blocks.append(pair)
    o = jnp.concatenate(blocks, axis=1).astype(BF16)
    o_ref[...] = x + _dot(o, wo_ref[...])


def xattn_block(x, g, wq, k, v, wo, seq, tm, name):
    T, D = x.shape
    M, W = k.shape[1], k.shape[2]
    per = seq // tm
    return pl.pallas_call(
        _xattn_body,
        grid=(T // tm,),
        in_specs=[pl.BlockSpec((tm, D), lambda i: (i, 0)),
                  pl.BlockSpec((1, D), lambda i: (0, 0)),
                  pl.BlockSpec(wq.shape, lambda i: (0, 0)),
                  pl.BlockSpec((1, M, W), lambda i: (i // per, 0, 0)),
                  pl.BlockSpec((1, M, W), lambda i: (i // per, 0, 0)),
                  pl.BlockSpec(wo.shape, lambda i: (0, 0))],
        out_specs=pl.BlockSpec((tm, D), lambda i: (i, 0)),
        out_shape=jax.ShapeDtypeStruct((T, D), F32),
        compiler_params=_cp("parallel"), name=name)(x, g.reshape(1, D), wq, k, v, wo)


def _ffn_body(x_ref, g_ref, wg_ref, wu_ref, wd_ref, o_ref, xn_ref, acc_ref):
    k = pl.program_id(1)

    @pl.when(k == 0)
    def _():
        xn_ref[...] = _rms(x_ref[...], g_ref[...]).astype(BF16)
        acc_ref[...] = x_ref[...]

    xn = xn_ref[...]
    hg = _dot(xn, wg_ref[...])
    hu = _dot(xn, wu_ref[...])
    h = (hg * jax.nn.sigmoid(hg) * hu).astype(BF16)
    acc_ref[...] += _dot(h, wd_ref[...])

    @pl.when(k == pl.num_programs(1) - 1)
    def _():
        o_ref[...] = acc_ref[...]


def ffn_block(x, g, wg, wu, wd, tm, tf, name):
    T, D = x.shape
    F = wg.shape[1]
    return pl.pallas_call(
        _ffn_body,
        grid=(T // tm, F // tf),
        in_specs=[pl.BlockSpec((tm, D), lambda i, k: (i, 0)),
                  pl.BlockSpec((1, D), lambda i, k: (0, 0)),
                  pl.BlockSpec((D, tf), lambda i, k: (0, k)),
                  pl.BlockSpec((D, tf), lambda i, k: (0, k)),
                  pl.BlockSpec((tf, D), lambda i, k: (k, 0))],
        out_specs=pl.BlockSpec((tm, D), lambda i, k: (i, 0)),
        out_shape=jax.ShapeDtypeStruct((T, D), F32),
        scratch_shapes=[pltpu.VMEM((tm, D), BF16), pltpu.VMEM((tm, D), F32)],
        compiler_params=_cp("parallel", "arbitrary"), name=name)(x, g.reshape(1, D), wg, wu, wd)


_GLA_LEVELS = (32, 16, 8, 4, 2, 1)


def _gla_constants():
    C = GLA_CHUNK
    t = np.arange(C)
    r = t[None, :]
    mats = [r <= t[:, None], r > t[:, None]]
    for hs in _GLA_LEVELS:
        c = (t // (2 * hs)) * (2 * hs) + hs - 1
        right = (t % (2 * hs)) >= hs
        m_right = (r > c[:, None]) & (r <= t[:, None])
        m_left = (r > t[:, None]) & (r <= c[:, None])
        mats.append(np.where(right[:, None], m_right, m_left))
    cmat = np.concatenate(mats, 0).astype(np.float32)
    x = t[:, None] ^ t[None, :]
    lvl = np.full((C, C), -1, np.int32)
    for i, hs in enumerate(_GLA_LEVELS):
        lvl[(t[:, None] > t[None, :]) & (x >= hs) & (x < 2 * hs)] = i
    lvl[t[:, None] == t[None, :]] = len(_GLA_LEVELS)
    return cmat, lvl


def _gla_body(q_ref, k_ref, v_ref, a_ref, r_ref, wa_ref, ba_ref, gn_ref, cmat_ref, lvl_ref, o_ref, st_ref, *, nb):
    C = GLA_CHUNK

    @pl.when(pl.program_id(2) == 0)
    def _():
        st_ref[...] = jnp.zeros_like(st_ref)

    lane = _lane((1, LANES))
    lvl = lvl_ref[...]
    cmat = cmat_ref[...]
    for bi in range(nb):
        q = q_ref[bi].astype(F32)
        k = k_ref[bi].astype(F32)
        z = _dot(a_ref[bi], wa_ref[...]) + ba_ref[...]
        la = (jnp.minimum(z, 0.0) - jnp.log1p(jnp.exp(-jnp.abs(z)))) / GLA_TAU
        e = jnp.exp(jnp.dot(cmat, la, precision=lax.Precision.HIGHEST, preferred_element_type=F32))
        qhat = q * e[0:C]
        kend = (k * e[C:2 * C]).astype(BF16)
        decay = jnp.broadcast_to(e[C - 1:C], (LANES, LANES)).T
        kb = k.astype(BF16)
        kl = [(k * e[(2 + i) * C:(3 + i) * C]).astype(BF16) for i in range(len(_GLA_LEVELS))]
        for half in range(2):
            in_half = (lane // HALF) == half
            zero = jnp.zeros_like(q)
            state = st_ref[bi, half]
            o = _dot(jnp.where(in_half, qhat, zero).astype(BF16), state.astype(BF16))
            sc = jnp.zeros((C, C), F32)
            for i in range(len(_GLA_LEVELS)):
                ql = jnp.where(in_half, q * e[(2 + i) * C:(3 + i) * C], zero).astype(BF16)
                sc = jnp.where(lvl == i, _dot_nt(ql, kl[i]), sc)
            sc = jnp.where(lvl == len(_GLA_LEVELS), _dot_nt(jnp.where(in_half, q, zero).astype(BF16), kb), sc)
            v = v_ref[bi, :, half * LANES:(half + 1) * LANES]
            o = o + _dot(sc.astype(BF16), v)
            st_ref[bi, half] = decay * state + lax.dot_general(
                kend, v, (((0,), (0,)), ((), ())), preferred_element_type=F32)
            y = _rms(o, gn_ref[...])
            rr = r_ref[bi, :, half * LANES:(half + 1) * LANES]
            o_ref[bi, :, half * LANES:(half + 1) * LANES] = (y * (rr * jax.nn.sigmoid(rr))).astype(o_ref.dtype)


def gla_mixer(hb, hf, w_alpha, b_alpha, g_norm, cols, nb):
    B, S, _ = hb.shape
    C = GLA_CHUNK
    cmat, lvl = _gla_constants()
    qb, kb_, vb, ab, rb = cols
    return pl.pallas_call(
        functools.partial(_gla_body, nb=nb),
        grid=(B // nb, GLA_HEADS // 2, S // C),
        in_specs=[pl.BlockSpec((nb, C, LANES), lambda b, p, c: (b, c, qb + p)),
                  pl.BlockSpec((nb, C, LANES), lambda b, p, c: (b, c, kb_ + p)),
                  pl.BlockSpec((nb, C, 2 * LANES), lambda b, p, c: (b, c, vb // 2 + p)),
                  pl.BlockSpec((nb, C, LANES), lambda b, p, c: (b, c, ab)),
                  pl.BlockSpec((nb, C, 2 * LANES), lambda b, p, c: (b, c, rb // 2 + p)),
                  pl.BlockSpec((LANES, LANES), lambda b, p, c: (0, p)),
                  pl.BlockSpec((1, LANES), lambda b, p, c: (0, p)),
                  pl.BlockSpec((1, LANES), lambda b, p, c: (0, 0)),
                  pl.BlockSpec(cmat.shape, lambda b, p, c: (0, 0)),
                  pl.BlockSpec(lvl.shape, lambda b, p, c: (0, 0))],
        out_specs=pl.BlockSpec((nb, C, 2 * LANES), lambda b, p, c: (b, c, p)),
        out_shape=jax.ShapeDtypeStruct((B, S, GLA_HEADS * GLA_DV), BF16),
        scratch_shapes=[pltpu.VMEM((nb, 2, LANES, LANES), F32)],
        compiler_params=_cp("parallel", "parallel", "arbitrary"), name="gla")(
            hb, hb, hb, hb, hf, w_alpha, b_alpha, g_norm.reshape(1, LANES), jnp.asarray(cmat), jnp.asarray(lvl))


N_CMP_PAD = 128
N_SLC = 32


def _gelu_tanh(x):
    return 0.5 * x * (1.0 + jnp.tanh(math.sqrt(2.0 / math.pi) * (x + 0.044715 * (x * x * x))))


def _nsa_compress_body(tk_ref, tv_ref, pk_ref, pv_ref, w1k_ref, w1v_ref, w2k_ref, w2v_ref, kc_ref, vc_ref):
    half_k = CMP_STRIDE * NSA_DH
    for t_ref, p_ref, w1_ref, w2_ref, o_ref in ((tk_ref, pk_ref, w1k_ref, w2k_ref, kc_ref),
                                                (tv_ref, pv_ref, w1v_ref, w2v_ref, vc_ref)):
        acc = jnp.zeros((N_CMP_PAD, LANES), F32)
        for g in range(NSA_GROUPS):
            view = t_ref[0, g]
            lo = _dot((view + p_ref[0:1]).astype(BF16), w1_ref[0:half_k])
            hi = _dot((view + p_ref[1:2]).astype(BF16), w1_ref[half_k:2 * half_k])
            pre = lo + pltpu.roll(hi, N_CMP_PAD - 1, 0)
            acc = acc + _dot(_gelu_tanh(pre).astype(BF16), w2_ref[g])
        o_ref[0] = acc.astype(o_ref.dtype)


def nsa_compress(tk, tv, pk, pv, w1k, w1v, w2k, w2v):
    B = tk.shape[0]
    blk4 = pl.BlockSpec((1,) + tk.shape[1:], lambda b: (b, 0, 0, 0))
    full = lambda a: pl.BlockSpec(a.shape, lambda b: (0,) * a.ndim)
    out = pl.BlockSpec((1, N_CMP_PAD, LANES), lambda b: (b, 0, 0))
    return pl.pallas_call(
        _nsa_compress_body,
        grid=(B,),
        in_specs=[blk4, blk4, full(pk), full(pv), full(w1k), full(w1v), full(w2k), full(w2v)],
        out_specs=[out, out],
        out_shape=[jax.ShapeDtypeStruct((B, N_CMP_PAD, LANES), BF16)] * 2,
        compiler_params=_cp("parallel"), name="nsa_compress")(tk, tv, pk, pv, w1k, w1v, w2k, w2v)


def _nsa_slope(g, j):
    return 2.0 ** (-(g * (NSA_HEADS // NSA_GROUPS) + j + 1))


def _nsa_cmp_body(q_ref, kc_ref, vc_ref, ov_ref, o_ref, sel_ref, *, tq):
    hg = NSA_HEADS // NSA_GROUPS
    qs = pl.program_id(1) * tq
    lane = _lane((1, LANES))
    tpos = qs + lax.broadcasted_iota(I32, (tq, LANES), 0)
    dist = tpos - (lane * CMP_STRIDE + (CMP_LEN - 1))
    valid = (dist >= 0) & (lane < N_CMP_PAD - 1)
    validf = valid.astype(F32)
    distf = dist.astype(F32)
    kc = kc_ref[0]
    vc = vc_ref[0]
    imp = jnp.zeros((tq, LANES), F32)
    blocks = [None] * hg
    for g in range(NSA_GROUPS):
        in_g = (lane // HALF) == g
        for j in range(hg):
            qb = q_ref[0, :, j * LANES:(j + 1) * LANES]
            s = _dot_nt(jnp.where(in_g, qb, jnp.zeros_like(qb)), kc) - _nsa_slope(g, j) * distf
            s = jnp.where(valid, s, NEG)
            p = jnp.exp(s - jnp.max(s, axis=-1, keepdims=True)) * validf
            p = (p / jnp.maximum(jnp.sum(p, axis=-1, keepdims=True), TINY)).astype(BF16)
            o = _dot(p, vc)
            blocks[j] = o if g == 0 else jnp.where(in_g, o, blocks[j])
            imp = imp + _dot(p, ov_ref[g])
    for j in range(hg):
        o_ref[0, :, j * LANES:(j + 1) * LANES] = blocks[j]
    blk = lane % N_SLC
    cur = tpos // SLC_LEN
    score = jnp.where((blk == cur) | (blk == 0), FORCE_SCORE, jnp.where(blk <= cur, imp, NEG))
    sel = jnp.zeros((tq, LANES), F32)
    for g in range(NSA_GROUPS):
        sg = jnp.where((lane // N_SLC) == g, score, LOWEST)
        for _ in range(N_SEL):
            m = jnp.max(sg, axis=-1, keepdims=True)
            idx = jnp.min(jnp.where(sg == m, lane, LANES), axis=-1, keepdims=True)
            hit = lane == idx
            sel = jnp.where(hit, 1.0, sel)
            sg = jnp.where(hit, LOWEST, sg)
    sel_ref[0] = sel.astype(sel_ref.dtype)


def nsa_cmp(hb, kc, vc, ov, tq):
    B, S, _ = hb.shape
    qw = NSA_HEADS * NSA_DH
    return pl.pallas_call(
        functools.partial(_nsa_cmp_body, tq=tq),
        grid=(B, S // tq),
        in_specs=[pl.BlockSpec((1, tq, qw), lambda b, i: (b, i, NSA_Q_BLK)),
                  pl.BlockSpec((1, N_CMP_PAD, LANES), lambda b, i: (b, 0, 0)),
                  pl.BlockSpec((1, N_CMP_PAD, LANES), lambda b, i: (b, 0, 0)),
                  pl.BlockSpec(ov.shape, lambda b, i: (0, 0, 0))],
        out_specs=[pl.BlockSpec((1, tq, qw), lambda b, i: (b, i, 0)),
                   pl.BlockSpec((1, tq, LANES), lambda b, i: (b, i, 0))],
        out_shape=[jax.ShapeDtypeStruct((B, S, qw), F32), jax.ShapeDtypeStruct((B, S, LANES), BF16)],
        compiler_params=_cp("parallel", "parallel"), name="nsa_cmp")(hb, kc, vc, ov)


def _softmax_step(s, maskf, v, m, l, acc):
    m_new = jnp.maximum(m, jnp.max(s, axis=-1, keepdims=True))
    p = jnp.exp(s - m_new) * maskf
    alpha = jnp.exp(m - m_new)
    l = alpha * l + jnp.sum(p, axis=-1, keepdims=True)
    acc = alpha * acc + _dot(p.astype(BF16), v)
    return m_new, l, acc


def _nsa_attn_body(q_ref, ks_ref, vs_ref, kw_ref, vw_ref, sel_ref, oc_ref, gt_ref, es_ref, o_ref, selx_ref,
                   *, tq, tk, tkw):
    hg = NSA_HEADS // NSA_GROUPS
    rows = hg * tq
    qs = pl.program_id(1) * tq
    lane = _lane((1, LANES))
    row = lax.broadcasted_iota(I32, (rows, 1), 0)
    tpos = qs + row % tq
    res = {}
    for g in range(NSA_GROUPS):
        in_g = (lane // HALF) == g
        qg = jnp.concatenate(
            [jnp.where(in_g, q_ref[0, :, j * LANES:(j + 1) * LANES], jnp.zeros((tq, LANES), BF16)) for j in range(hg)],
            axis=0)
        slope = jnp.zeros((rows, 1), F32)
        for j in range(hg):
            slope = jnp.where(row // tq == j, _nsa_slope(g, j), slope)
        selx_ref[...] = _dot(sel_ref[0], es_ref[g])
        init = (jnp.full((rows, 1), NEG, F32), jnp.zeros((rows, 1), F32), jnp.zeros((rows, LANES), F32))

        def slc_step(kt, carry, qg=qg, slope=slope):
            k0 = pl.multiple_of(kt * tk, tk)
            dist = tpos - (k0 + _lane((1, tk)))
            picked = selx_ref[:, pl.ds(k0, tk)] > 0.5
            mask = jnp.concatenate([picked] * hg, axis=0) & (dist >= 0)
            s = _dot_nt(qg, ks_ref[0, pl.ds(k0, tk), :]) - slope * dist.astype(F32)
            return _softmax_step(jnp.where(mask, s, NEG), mask.astype(F32), vs_ref[0, pl.ds(k0, tk), :], *carry)

        m, l, acc = lax.fori_loop(0, (qs + tq + tk - 1) // tk, slc_step, init)
        o_slc = acc / jnp.maximum(l, TINY)

        def win_step(kt, carry, qg=qg, slope=slope):
            k0 = pl.multiple_of(kt * tkw, tkw)
            dist = tpos - (k0 + _lane((1, tkw)))
            mask = (dist >= 0) & (dist < WINDOW)
            s = _dot_nt(qg, kw_ref[0, pl.ds(k0, tkw), :]) - slope * dist.astype(F32)
            return _softmax_step(jnp.where(mask, s, NEG), mask.astype(F32), vw_ref[0, pl.ds(k0, tkw), :], *carry)

        lo = jnp.maximum(qs - WINDOW, 0) // tkw
        m, l, acc = lax.fori_loop(lo, (qs + tq) // tkw, win_step, init)
        res[g] = (o_slc, acc / jnp.maximum(l, TINY))
    gates = jax.nn.sigmoid(gt_ref[0])
    first = lane < HALF
    for j in range(hg):
        out = jnp.zeros((tq, LANES), F32)
        for br in range(3):
            gm = jnp.where(first, gates[:, j * 3 + br:j * 3 + br + 1],
                           gates[:, hg * 3 + j * 3 + br:hg * 3 + j * 3 + br + 1])
            if br == 0:
                val = oc_ref[0, :, j * LANES:(j + 1) * LANES]
            else:
                val = jnp.where(first, res[0][br - 1][j * tq:(j + 1) * tq], res[1][br - 1][j * tq:(j + 1) * tq])
            out = out + gm * val
        o_ref[0, :, j * LANES:(j + 1) * LANES] = out.astype(o_ref.dtype)


def nsa_attn(hb, hf, sel, ocmp, esel, tq, tk, tkw):
    B, S, _ = hb.shape
    qw = NSA_HEADS * NSA_DH
    kv = lambda blk: pl.BlockSpec((1, S, LANES), lambda b, i: (b, 0, blk))
    return pl.pallas_call(
        functools.partial(_nsa_attn_body, tq=tq, tk=tk, tkw=tkw),
        grid=(B, S // tq),
        in_specs=[pl.BlockSpec((1, tq, qw), lambda b, i: (b, i, NSA_Q_BLK)),
                  kv(NSA_KS_BLK), kv(NSA_VS_BLK), kv(NSA_KW_BLK), kv(NSA_VW_BLK),
                  pl.BlockSpec((1, tq, LANES), lambda b, i: (b, i, 0)),
                  pl.BlockSpec((1, tq, qw), lambda b, i: (b, i, 0)),
                  pl.BlockSpec((1, tq, LANES), lambda b, i: (b, i, NSA_GATE_BLK)),
                  pl.BlockSpec(esel.shape, lambda b, i: (0, 0, 0))],
        out_specs=pl.BlockSpec((1, tq, qw), lambda b, i: (b, i, 0)),
        out_shape=jax.ShapeDtypeStruct((B, S, qw), BF16),
        scratch_shapes=[pltpu.VMEM((tq, S), F32)],
        compiler_params=_cp("parallel", "parallel"), name="nsa_attn")(
            hb, hb, hb, hb, hb, sel, ocmp, hf, esel)


def _nsa_tables(S):
    n_cmp = (S - CMP_LEN) // CMP_STRIDE + 1
    c_start = np.arange(n_cmp) * CMP_STRIDE
    c_end = c_start + CMP_LEN - 1
    j_start = np.arange(S // SLC_LEN) * SLC_LEN
    overlap = (c_end[:, None] >= j_start[None]) & (c_start[:, None] <= j_start[None] + SLC_LEN - 1)
    ov = np.zeros((NSA_GROUPS, N_CMP_PAD, LANES), np.float32)
    es = np.zeros((NSA_GROUPS, LANES, S), np.float32)
    for g in range(NSA_GROUPS):
        ov[g, :n_cmp, g * N_SLC:(g + 1) * N_SLC] = overlap
        es[g, g * N_SLC + np.arange(S) // SLC_LEN, np.arange(S)] = 1.0
    return jnp.asarray(ov, BF16), jnp.asarray(es, BF16)


NSA_PERM = np.array([(half * (NSA_HEADS // NSA_GROUPS) + j) * NSA_DH + d
                     for j in range(NSA_HEADS // NSA_GROUPS) for half in range(2) for d in range(NSA_DH)])


def nsa_mixer(hb, hf, pos_k, w1_k, w2_k, pos_v, w1_v, w2_v):
    B, S, _ = hb.shape
    rows = S // CMP_STRIDE

    def stride_view(blk):
        t = hf[:, :, blk * LANES:(blk + 1) * LANES]
        t = t.reshape(B, rows, CMP_STRIDE, NSA_GROUPS, NSA_DH).transpose(0, 3, 1, 2, 4)
        return t.reshape(B, NSA_GROUPS, rows, CMP_STRIDE * NSA_DH)

    def pad_w2(w2):
        out = jnp.zeros((NSA_GROUPS, CMP_HIDDEN, LANES), F32)
        for g in range(NSA_GROUPS):
            out = out.at[g, :, g * NSA_DH:(g + 1) * NSA_DH].set(w2)
        return out.astype(BF16)

    kc, vc = nsa_compress(stride_view(NSA_KC_BLK), stride_view(NSA_VC_BLK),
                          pos_k.reshape(2, CMP_STRIDE * NSA_DH), pos_v.reshape(2, CMP_STRIDE * NSA_DH),
                          w1_k.astype(BF16), w1_v.astype(BF16), pad_w2(w2_k), pad_w2(w2_v))
    ov, es = _nsa_tables(S)
    ocmp, sel = nsa_cmp(hb, kc, vc, ov, 256)
    return nsa_attn(hb, hf, sel, ocmp, es, 128, 256, 128)


def _diff_body(slope_ref, q1_ref, q2_ref, k1_ref, k2_ref, v_ref, lam_ref, gn_ref, o_ref, *, tq, lambda_init):
    h = pl.program_id(1)
    i = pl.program_id(2)
    qs = i * tq
    lane = _lane((1, LANES))
    in_half = (lane // HALF) == (h % 2)
    zero = jnp.zeros((tq, LANES), BF16)
    q1 = jnp.where(in_half, q1_ref[0], zero)
    q2 = jnp.where(in_half, q2_ref[0], zero)
    slope = slope_ref[h]
    tpos = qs + lax.broadcasted_iota(I32, (2 * tq, 1), 0) % tq
    init = (jnp.full((2 * tq, 1), NEG, F32), jnp.zeros((2 * tq, 1), F32), jnp.zeros((2 * tq, LANES), F32))

    def step(kt, carry):
        k0 = pl.multiple_of(kt * tq, tq)
        dist = tpos - (k0 + _lane((1, tq)))
        mask = dist >= 0
        s = jnp.concatenate([_dot_nt(q1, k1_ref[0, pl.ds(k0, tq), :]), _dot_nt(q2, k2_ref[0, pl.ds(k0, tq), :])], axis=0)
        s = s - slope * dist.astype(F32)
        return _softmax_step(jnp.where(mask, s, NEG), mask.astype(F32), v_ref[0, pl.ds(k0, tq), :], *carry)

    m, l, acc = lax.fori_loop(0, i + 1, step, init)
    o = acc / jnp.maximum(l, TINY)
    lam_rows = lam_ref[...]
    lam = (jnp.exp(jnp.sum(lam_rows[0:1] * lam_rows[1:2], axis=-1, keepdims=True))
           - jnp.exp(jnp.sum(lam_rows[2:3] * lam_rows[3:4], axis=-1, keepdims=True)) + lambda_init)
    y = _rms(o[0:tq] - lam * o[tq:2 * tq], gn_ref[...]) * (1.0 - lambda_init)
    o_ref[0] = y.astype(o_ref.dtype)


def diff_mixer(hd, lam_rows, g_norm, lambda_init, tq):
    B, S, _ = hd.shape
    nb = DIFF_HEADS // 2
    slopes = jnp.asarray(2.0 ** (-8.0 * np.arange(1, DIFF_HEADS + 1) / DIFF_HEADS), F32)
    grid_spec = pltpu.PrefetchScalarGridSpec(
        num_scalar_prefetch=1,
        grid=(B, DIFF_HEADS, S // tq),
        in_specs=[pl.BlockSpec((1, tq, LANES), lambda b, h, i, s: (b, i, h // 2)),
                  pl.BlockSpec((1, tq, LANES), lambda b, h, i, s: (b, i, nb + h // 2)),
                  pl.BlockSpec((1, S, LANES), lambda b, h, i, s: (b, 0, 2 * nb + h // 2)),
                  pl.BlockSpec((1, S, LANES), lambda b, h, i, s: (b, 0, 3 * nb + h // 2)),
                  pl.BlockSpec((1, S, LANES), lambda b, h, i, s: (b, 0, 4 * nb + h)),
                  pl.BlockSpec((4, LANES), lambda b, h, i, s: (0, 0)),
                  pl.BlockSpec((1, LANES), lambda b, h, i, s: (0, 0))],
        out_specs=pl.BlockSpec((1, tq, LANES), lambda b, h, i, s: (b, i, h)))
    return pl.pallas_call(
        functools.partial(_diff_body, tq=tq, lambda_init=lambda_init),
        grid_spec=grid_spec,
        out_shape=jax.ShapeDtypeStruct((B, S, DIFF_HEADS * DIFF_DV), BF16),
        compiler_params=_cp("parallel", "parallel", "parallel"), name="diff_attn")(
            slopes, hd, hd, hd, hd, hd, lam_rows, g_norm.reshape(1, LANES))


def _router_body(x_ref, g_ref, wr_ref, tri_ref, xn_ref, meta_ref, wts_ref, cnt_ref, run_ref):
    @pl.when(pl.program_id(0) == 0)
    def _():
        run_ref[...] = jnp.zeros_like(run_ref)

    xn = _rms(x_ref[...], g_ref[...])
    xn_ref[...] = xn
    lane = _lane((1, LANES))
    logits = jnp.dot(xn, wr_ref[...], precision=lax.Precision.HIGHEST, preferred_element_type=F32)
    logits = jnp.where(lane < N_EXPERTS, logits, LOWEST)
    m1 = jnp.max(logits, axis=-1, keepdims=True)
    i1 = jnp.min(jnp.where(logits == m1, lane, LANES), axis=-1, keepdims=True)
    rest = jnp.where(lane == i1, LOWEST, logits)
    m2 = jnp.max(rest, axis=-1, keepdims=True)
    i2 = jnp.min(jnp.where(rest == m2, lane, LANES), axis=-1, keepdims=True)
    e = jnp.exp(m2 - m1)
    w1 = 1.0 / (1.0 + e)
    w2 = e / (1.0 + e)
    onehot = ((lane == i1) | (lane == i2)).astype(F32)
    pos = _dot(tri_ref[...], onehot.astype(BF16)) + run_ref[...]
    p1 = jnp.sum(jnp.where(lane == i1, pos, 0.0), axis=-1, keepdims=True).astype(I32)
    p2 = jnp.sum(jnp.where(lane == i2, pos, 0.0), axis=-1, keepdims=True).astype(I32)
    run_ref[...] += jnp.sum(onehot, axis=0, keepdims=True)
    meta_ref[...] = jnp.where(lane == 0, i1, jnp.where(lane == 1, i2, jnp.where(lane == 2, p1, jnp.where(lane == 3, p2, 0))))
    wts_ref[...] = jnp.where(lane == 0, w1, jnp.where(lane == 1, w2, 0.0))
    cnt_ref[...] = run_ref[...]


def moe_router(x, g, wr, tm):
    T, D = x.shape
    tri = jnp.asarray(np.tril(np.ones((tm, tm), np.float32), -1), BF16)
    return pl.pallas_call(
        _router_body,
        grid=(T // tm,),
        in_specs=[pl.BlockSpec((tm, D), lambda i: (i, 0)),
                  pl.BlockSpec((1, D), lambda i: (0, 0)),
                  pl.BlockSpec((D, LANES), lambda i: (0, 0)),
                  pl.BlockSpec((tm, tm), lambda i: (0, 0))],
        out_specs=[pl.BlockSpec((tm, D), lambda i: (i, 0)),
                   pl.BlockSpec((tm, LANES), lambda i: (i, 0)),
                   pl.BlockSpec((tm, LANES), lambda i: (i, 0)),
                   pl.BlockSpec((1, LANES), lambda i: (0, 0))],
        out_shape=[jax.ShapeDtypeStruct((T, D), F32), jax.ShapeDtypeStruct((T, LANES), I32),
                   jax.ShapeDtypeStruct((T, LANES), F32), jax.ShapeDtypeStruct((1, LANES), F32)],
        scratch_shapes=[pltpu.VMEM((1, LANES), F32)],
        compiler_params=_cp("arbitrary"), name="moe_router")(x, g.reshape(1, D), wr, tri)


def _row_copy(src_ref, src_row, dst_ref, dst_row, sem):
    return pltpu.make_async_copy(src_ref.at[pl.ds(src_row, 1)], dst_ref.at[pl.ds(dst_row, 1)], sem)


def _dispatch_body(dest_ref, xn_ref, zero_ref, xs_ref, sem, *, tm):
    del zero_ref
    base = pl.program_id(0) * tm

    def issue(t, c):
        for slot in range(TOP_K):
            _row_copy(xn_ref, t, xs_ref, dest_ref[TOP_K * (base + t) + slot], sem).start()
        return c

    lax.fori_loop(0, tm, issue, 0)

    def drain(t, c):
        for slot in range(TOP_K):
            _row_copy(xn_ref, 0, xs_ref, 0, sem).wait()
        return c

    lax.fori_loop(0, tm, drain, 0)


def moe_dispatch(dest, xn, rows, tm):
    T, D = xn.shape
    grid_spec = pltpu.PrefetchScalarGridSpec(
        num_scalar_prefetch=1,
        grid=(T // tm,),
        in_specs=[pl.BlockSpec((tm, D), lambda i, d: (i, 0)),
                  pl.BlockSpec(memory_space=pl.ANY)],
        out_specs=pl.BlockSpec(memory_space=pl.ANY),
        scratch_shapes=[pltpu.SemaphoreType.DMA])
    return pl.pallas_call(
        functools.partial(_dispatch_body, tm=tm),
        grid_spec=grid_spec,
        out_shape=jax.ShapeDtypeStruct((rows, D), F32),
        input_output_aliases={2: 0},
        compiler_params=_cp("arbitrary"), name="moe_dispatch")(dest, xn, jnp.zeros((rows, D), F32))


def _experts_body(te_ref, tv_ref, xs_ref, wg_ref, wu_ref, wd_ref, o_ref, xb_ref, acc_ref):
    i = pl.program_id(0)
    k = pl.program_id(1)

    @pl.when(k == 0)
    def _():
        xb_ref[...] = xs_ref[...].astype(BF16)
        acc_ref[...] = jnp.zeros_like(acc_ref)

    @pl.when(tv_ref[i] > 0)
    def _():
        xb = xb_ref[...]
        hg = _dot(xb, wg_ref[0])
        hu = _dot(xb, wu_ref[0])
        acc_ref[...] += _dot((hg * jax.nn.sigmoid(hg) * hu).astype(BF16), wd_ref[0])

    @pl.when(k == pl.num_programs(1) - 1)
    def _():
        o_ref[...] = acc_ref[...]


def moe_experts(tile_expert, tile_valid, xs, wg, wu, wd, tr, tf):
    R, D = xs.shape
    F = wg.shape[2]
    grid_spec = pltpu.PrefetchScalarGridSpec(
        num_scalar_prefetch=2,
        grid=(R // tr, F // tf),
        in_specs=[pl.BlockSpec((tr, D), lambda i, k, te, tv: (i, 0)),
                  pl.BlockSpec((1, D, tf), lambda i, k, te, tv: (te[i], 0, k)),
                  pl.BlockSpec((1, D, tf), lambda i, k, te, tv: (te[i], 0, k)),
                  pl.BlockSpec((1, tf, D), lambda i, k, te, tv: (te[i], k, 0))],
        out_specs=pl.BlockSpec((tr, D), lambda i, k, te, tv: (i, 0)),
        scratch_shapes=[pltpu.VMEM((tr, D), BF16), pltpu.VMEM((tr, D), F32)])
    return pl.pallas_call(
        _experts_body,
        grid_spec=grid_spec,
        out_shape=jax.ShapeDtypeStruct((R, D), F32),
        compiler_params=_cp("parallel", "arbitrary"), name="moe_experts")(tile_expert, tile_valid, xs, wg, wu, wd)


def _combine_body(dest_ref, x_ref, wts_ref, g_ref, ys_ref, o_ref, buf_ref, sem, *, tm):
    base = pl.program_id(0) * tm

    def issue(t, c):
        for slot in range(TOP_K):
            _row_copy(ys_ref, dest_ref[TOP_K * (base + t) + slot], buf_ref.at[slot], t, sem).start()
        return c

    lax.fori_loop(0, tm, issue, 0)

    def drain(t, c):
        for slot in range(TOP_K):
            _row_copy(ys_ref, 0, buf_ref.at[slot], 0, sem).wait()
        return c

    lax.fori_loop(0, tm, drain, 0)
    wts = wts_ref[...]
    y = x_ref[...] + (wts[:, 0:1] * buf_ref[0] + wts[:, 1:2] * buf_ref[1])
    o_ref[...] = _rms(y, g_ref[...])


def moe_combine(dest, x, wts, g_final, ys, tm):
    T, D = x.shape
    grid_spec = pltpu.PrefetchScalarGridSpec(
        num_scalar_prefetch=1,
        grid=(T // tm,),
        in_specs=[pl.BlockSpec((tm, D), lambda i, d: (i, 0)),
                  pl.BlockSpec((tm, LANES), lambda i, d: (i, 0)),
                  pl.BlockSpec((1, D), lambda i, d: (0, 0)),
                  pl.BlockSpec(memory_space=pl.ANY)],
        out_specs=pl.BlockSpec((tm, D), lambda i, d: (i, 0)),
        scratch_shapes=[pltpu.VMEM((TOP_K, tm, D), F32), pltpu.SemaphoreType.DMA])
    return pl.pallas_call(
        functools.partial(_combine_body, tm=tm),
        grid_spec=grid_spec,
        out_shape=jax.ShapeDtypeStruct((T, D), F32),
        compiler_params=_cp("arbitrary"), name="moe_combine")(dest, x, wts, g_final.reshape(1, D), ys)


def moe_block(x, g, w_router, wg, wu, wd, g_final, tr):
    T, D = x.shape
    wr = _pad_cols(w_router, LANES)
    xn, meta, wts, counts = moe_router(x, g, wr, 512)
    cnt = counts[0, :N_EXPERTS].astype(I32)
    gsz = ((cnt + tr - 1) // tr) * tr
    gend = jnp.cumsum(gsz)
    goff = gend - gsz
    dest = (jnp.take(goff, meta[:, 0:TOP_K]) + meta[:, TOP_K:2 * TOP_K]).reshape(-1)
    n_tiles = (TOP_K * T) // tr + N_EXPERTS
    tile_start = jnp.arange(n_tiles, dtype=I32) * tr
    tile_valid = (tile_start < gend[-1]).astype(I32)
    tile_expert = jnp.minimum(jnp.searchsorted(gend, tile_start, side="right"), N_EXPERTS - 1).astype(I32)
    last_valid = jnp.take(tile_expert, jnp.maximum(gend[-1] // tr - 1, 0))
    tile_expert = jnp.where(tile_valid > 0, tile_expert, last_valid)
    xs = moe_dispatch(dest, xn, n_tiles * tr, 256)
    ys = moe_experts(tile_expert, tile_valid, xs, wg, wu, wd, tr, 256)
    return moe_combine(dest, x, wts, g_final, ys, 256)


def _pad_cols(w, width):
    return jnp.pad(w, ((0, 0), (0, width - w.shape[1])))


def _mem_kv(mem2, g, wk, wv, batch):
    w = jnp.concatenate([wk, wv], axis=1).astype(BF16)
    width = wk.shape[1]
    k, v = norm_proj(mem2, g, w, [(width, BF16), (width, BF16)], 256, "mem_kv")
    return k.reshape(batch, -1, width), v.reshape(batch, -1, width)


def kernel(x, mem, mix_norm_0, w_in_0, gla_w_alpha_0, gla_b_alpha_0, gla_out_norm_0, nsa_cmp_pos_k_0, nsa_cmp_w1_k_0, nsa_cmp_w2_k_0, nsa_cmp_pos_v_0, nsa_cmp_w1_v_0, nsa_cmp_w2_v_0, w_out_0, xattn_norm_0, xattn_mem_norm_0, xattn_wq_0, xattn_wk_0, xattn_wv_0, xattn_wo_0, ffn_norm_0, ffn_w_gate_0, ffn_w_up_0, ffn_w_down_0, mix_norm_1, w_in_1, diff_lq1_1, diff_lk1_1, diff_lq2_1, diff_lk2_1, diff_out_norm_1, w_out_1, xattn_norm_1, xattn_mem_norm_1, xattn_wq_1, xattn_wk_1, xattn_wv_1, xattn_wo_1, ffn_norm_1, moe_router_1, moe_w_gate_1, moe_w_up_1, moe_w_down_1, final_norm):
    B, S, D = x.shape
    T = B * S
    x2 = x.reshape(T, D)
    mem2 = mem.reshape(-1, D)
    q_scale = NSA_DH ** -0.5

    sizes = [GLA_HEADS * GLA_DK, GLA_HEADS * GLA_DK, GLA_HEADS * GLA_DV, GLA_HEADS * GLA_DV, GLA_RANK,
             NSA_HEADS * NSA_DH] + [NSA_GROUPS * NSA_DH] * 6 + [3 * NSA_HEADS]
    (g_q, g_k, g_v, g_r, g_a, n_q, n_kc, n_vc, n_ks, n_vs, n_kw, n_vw, n_g) = jnp.split(
        w_in_0, np.cumsum(sizes)[:-1].tolist(), axis=1)
    w0 = jnp.concatenate([n_q[:, NSA_PERM] * q_scale, n_ks, n_vs, n_kw, n_vw, g_q * q_scale, g_k, g_v,
                          _pad_cols(g_a, LANES), g_r, _pad_cols(n_g, LANES), n_kc, n_vc], axis=1).astype(BF16)
    hb, hf = norm_proj(x2, mix_norm_0, w0, [(HB0_WIDTH, BF16), (HF0_WIDTH, F32)], 512, "in_proj0")
    hb = hb.reshape(B, S, HB0_WIDTH)
    hf = hf.reshape(B, S, HF0_WIDTH)
    w_alpha = jnp.pad(gla_w_alpha_0, ((0, LANES - GLA_RANK), (0, 0))).astype(BF16)
    o_a = gla_mixer(hb, hf, w_alpha, gla_b_alpha_0.reshape(1, -1), gla_out_norm_0, GLA_COLS, 4)
    o_b = nsa_mixer(hb, hf, nsa_cmp_pos_k_0, nsa_cmp_w1_k_0, nsa_cmp_w2_k_0,
                    nsa_cmp_pos_v_0, nsa_cmp_w1_v_0, nsa_cmp_w2_v_0)
    n_a = GLA_HEADS * GLA_DV
    x2 = proj_residual(x2, [o_a.reshape(T, -1), o_b.reshape(T, -1)],
                       [w_out_0[:n_a].astype(BF16), w_out_0[n_a:][NSA_PERM].astype(BF16)], 512, "out_proj0")
    k0, v0 = _mem_kv(mem2, xattn_mem_norm_0, xattn_wk_0, xattn_wv_0, B)
    x2 = xattn_block(x2, xattn_norm_0, (xattn_wq_0 * q_scale).astype(BF16), k0, v0, xattn_wo_0.astype(BF16),
                     S, 512, "xattn0")
    x2 = ffn_block(x2, ffn_norm_0, ffn_w_gate_0.astype(BF16), ffn_w_up_0.astype(BF16), ffn_w_down_0.astype(BF16),
                   1024, 256, "ffn0")

    lambda_init = 0.8 - 0.6 * math.exp(-0.3 * 1)
    n_q1 = 2 * DIFF_HEADS * DIFF_DH
    w1 = jnp.concatenate([w_in_1[:, :n_q1] * q_scale, w_in_1[:, n_q1:]], axis=1).astype(BF16)
    (hd,) = norm_proj(x2, mix_norm_1, w1, [(w1.shape[1], BF16)], 512, "in_proj1")
    lam_rows = _pad_cols(jnp.stack([diff_lq1_1, diff_lk1_1, diff_lq2_1, diff_lk2_1]), LANES)
    o_d = diff_mixer(hd.reshape(B, S, -1), lam_rows, diff_out_norm_1, lambda_init, 256)
    x2 = proj_residual(x2, [o_d.reshape(T, -1)], [w_out_1.astype(BF16)], 512, "out_proj1")
    k1, v1 = _mem_kv(mem2, xattn_mem_norm_1, xattn_wk_1, xattn_wv_1, B)
    x2 = xattn_block(x2, xattn_norm_1, (xattn_wq_1 * q_scale).astype(BF16), k1, v1, xattn_wo_1.astype(BF16),
                     S, 512, "xattn1")
    out = moe_block(x2, ffn_norm_1, moe_router_1, moe_w_gate_1.astype(BF16), moe_w_up_1.astype(BF16),
                    moe_w_down_1.astype(BF16), final_norm, 512)
    return out.reshape(B, S, D)
```

```python
import functools
import math

import numpy as np
import jax
import jax.numpy as jnp
from jax import lax
from jax.experimental import pallas as pl
from jax.experimental.pallas import tpu as pltpu

F32 = jnp.float32
BF16 = jnp.bfloat16
I32 = jnp.int32

D_MODEL = 1024
GLA_HEADS, GLA_DK, GLA_DV, GLA_RANK, GLA_TAU, GLA_CHUNK = 4, 64, 128, 16, 16.0, 64
NSA_HEADS, NSA_GROUPS, NSA_DH = 8, 2, 64
CMP_LEN, CMP_STRIDE, CMP_HIDDEN, SLC_LEN, N_SEL, WINDOW = 32, 16, 256, 64, 8, 512
DIFF_HEADS, DIFF_DH, DIFF_DV = 8, 64, 128
XATTN_HEADS, XATTN_DH = 4, 64
D_FF, N_EXPERTS, TOP_K = 2816, 8, 2
EPS, NEG, TINY, FORCE_SCORE = 1e-6, -1e30, 1e-30, 1e4

LANES = 128
HALF = LANES // 2
VMEM_LIMIT = 56 * 1024 * 1024
LOWEST = -3.0e38

HB0_WIDTH, HF0_WIDTH = 17 * LANES, 7 * LANES
NSA_Q_BLK, NSA_KS_BLK, NSA_VS_BLK, NSA_KW_BLK, NSA_VW_BLK = 0, 4, 5, 6, 7
GLA_COLS = (8, 10, 12, 16, 0)
NSA_GATE_BLK, NSA_KC_BLK, NSA_VC_BLK = 4, 5, 6


def _cp(*sem):
    return pltpu.CompilerParams(dimension_semantics=sem, vmem_limit_bytes=VMEM_LIMIT)


def _rms(x, g):
    y = x * lax.rsqrt(jnp.mean(x * x, axis=-1, keepdims=True) + EPS)
    return y * g


def _dot(a, b):
    return jnp.dot(a, b, preferred_element_type=F32)


def _dot_nt(a, b):
    return lax.dot_general(a, b, (((1,), (1,)), ((), ())), preferred_element_type=F32)


def _lane(shape):
    return lax.broadcasted_iota(I32, shape, len(shape) - 1)


def _norm_proj_body(x_ref, g_ref, w_ref, *o_refs, widths):
    xn = _rms(x_ref[...], g_ref[...]).astype(BF16)
    off = 0
    for o_ref, wd in zip(o_refs, widths):
        for c0 in range(0, wd, 512):
            cw = min(512, wd - c0)
            o_ref[:, c0:c0 + cw] = _dot(xn, w_ref[:, off + c0:off + c0 + cw]).astype(o_ref.dtype)
        off += wd


def norm_proj(x, g, w, outs, tm, name):
    T, D = x.shape
    widths = tuple(o[0] for o in outs)
    return pl.pallas_call(
        functools.partial(_norm_proj_body, widths=widths),
        grid=(T // tm,),
        in_specs=[pl.BlockSpec((tm, D), lambda i: (i, 0)),
                  pl.BlockSpec((1, D), lambda i: (0, 0)),
                  pl.BlockSpec(w.shape, lambda i: (0, 0))],
        out_specs=[pl.BlockSpec((tm, wd), lambda i: (i, 0)) for wd, _ in outs],
        out_shape=[jax.ShapeDtypeStruct((T, wd), dt) for wd, dt in outs],
        compiler_params=_cp("parallel"), name=name)(x, g.reshape(1, D), w)


def _proj_res_body(*refs, n):
    res_ref = refs[0]
    a_refs = refs[1:1 + n]
    w_refs = refs[1 + n:1 + 2 * n]
    o_ref = refs[1 + 2 * n]
    acc = res_ref[...]
    for a_ref, w_ref in zip(a_refs, w_refs):
        acc = acc + _dot(a_ref[...].astype(BF16), w_ref[...])
    o_ref[...] = acc


def proj_residual(res, a_list, w_list, tm, name):
    T, D = res.shape
    n = len(a_list)
    return pl.pallas_call(
        functools.partial(_proj_res_body, n=n),
        grid=(T // tm,),
        in_specs=([pl.BlockSpec((tm, D), lambda i: (i, 0))]
                  + [pl.BlockSpec((tm, a.shape[1]), lambda i: (i, 0)) for a in a_list]
                  + [pl.BlockSpec(w.shape, lambda i: (0, 0)) for w in w_list]),
        out_specs=pl.BlockSpec((tm, D), lambda i: (i, 0)),
        out_shape=jax.ShapeDtypeStruct((T, D), F32),
        compiler_params=_cp("parallel"), name=name)(res, *a_list, *w_list)


def _xattn_body(x_ref, g_ref, wq_ref, k_ref, v_ref, wo_ref, o_ref):
    x = x_ref[...]
    q = _dot(_rms(x, g_ref[...]).astype(BF16), wq_ref[...]).astype(BF16)
    lane = _lane((1, LANES))
    blocks = []
    for blk in range(XATTN_HEADS // 2):
        qb = q[:, blk * LANES:(blk + 1) * LANES]
        kb = k_ref[0, :, blk * LANES:(blk + 1) * LANES]
        vb = v_ref[0, :, blk * LANES:(blk + 1) * LANES]
        pair = None
        for half in range(2):
            in_half = (lane // HALF) == half
            s = _dot_nt(jnp.where(in_half, qb, jnp.zeros_like(qb)), kb)
            p = jnp.exp(s - jnp.max(s, axis=-1, keepdims=True))
            p = p / jnp.sum(p, axis=-1, keepdims=True)
            o = _dot(p.astype(BF16), vb)
            pair = o if half == 0 else jnp.where(in_half, o, pair)
        blocks.append(pair)
    o = jnp.concatenate(blocks, axis=1).astype(BF16)
    o_ref[...] = x + _dot(o, wo_ref[...])


def xattn_block(x, g, wq, k, v, wo, seq, tm, name):
    T, D = x.shape
    M, W = k.shape[1], k.shape[2]
    per = seq // tm
    return pl.pallas_call(
        _xattn_body,
        grid=(T // tm,),
        in_specs=[pl.BlockSpec((tm, D), lambda i: (i, 0)),
                  pl.BlockSpec((1, D), lambda i: (0, 0)),
                  pl.BlockSpec(wq.shape, lambda i: (0, 0)),
                  pl.BlockSpec((1, M, W), lambda i: (i // per, 0, 0)),
                  pl.BlockSpec((1, M, W), lambda i: (i // per, 0, 0)),
                  pl.BlockSpec(wo.shape, lambda i: (0, 0))],
        out_specs=pl.BlockSpec((tm, D), lambda i: (i, 0)),
        out_shape=jax.ShapeDtypeStruct((T, D), F32),
        compiler_params=_cp("parallel"), name=name)(x, g.reshape(1, D), wq, k, v, wo)


def _ffn_body(x_ref, g_ref, wg_ref, wu_ref, wd_ref, o_ref, xn_ref, acc_ref):
    k = pl.program_id(1)

    @pl.when(k == 0)
    def _():
        xn_ref[...] = _rms(x_ref[...], g_ref[...]).astype(BF16)
        acc_ref[...] = x_ref[...]

    xn = xn_ref[...]
    hg = _dot(xn, wg_ref[...])
    hu = _dot(xn, wu_ref[...])
    h = (hg * jax.nn.sigmoid(hg) * hu).astype(BF16)
    acc_ref[...] += _dot(h, wd_ref[...])

    @pl.when(k == pl.num_programs(1) - 1)
    def _():
        o_ref[...] = acc_ref[...]


def ffn_block(x, g, wg, wu, wd, tm, tf, name):
    T, D = x.shape
    F = wg.shape[1]
    return pl.pallas_call(
        _ffn_body,
        grid=(T // tm, F // tf),
        in_specs=[pl.BlockSpec((tm, D), lambda i, k: (i, 0)),
                  pl.BlockSpec((1, D), lambda i, k: (0, 0)),
                  pl.BlockSpec((D, tf), lambda i, k: (0, k)),
                  pl.BlockSpec((D, tf), lambda i, k: (0, k)),
                  pl.BlockSpec((tf, D), lambda i, k: (k, 0))],
        out_specs=pl.BlockSpec((tm, D), lambda i, k: (i, 0)),
        out_shape=jax.ShapeDtypeStruct((T, D), F32),
        scratch_shapes=[pltpu.VMEM((tm, D), BF16), pltpu.VMEM((tm, D), F32)],
        compiler_params=_cp("parallel", "arbitrary"), name=name)(x, g.reshape(1, D), wg, wu, wd)


_GLA_LEVELS = (32, 16, 8, 4, 2, 1)


def _gla_constants():
    C = GLA_CHUNK
    t = np.arange(C)
    r = t[None, :]
    mats = [r <= t[:, None], r > t[:, None]]
    for hs in _GLA_LEVELS:
        c = (t // (2 * hs)) * (2 * hs) + hs - 1
        right = (t % (2 * hs)) >= hs
        m_right = (r > c[:, None]) & (r <= t[:, None])
        m_left = (r > t[:, None]) & (r <= c[:, None])
        mats.append(np.where(right[:, None], m_right, m_left))
    cmat = np.concatenate(mats, 0).astype(np.float32)
    x = t[:, None] ^ t[None, :]
    lvl = np.full((C, C), -1, np.int32)
    for i, hs in enumerate(_GLA_LEVELS):
        lvl[(t[:, None] > t[None, :]) & (x >= hs) & (x < 2 * hs)] = i
    lvl[t[:, None] == t[None, :]] = len(_GLA_LEVELS)
    return cmat, lvl


def _gla_body(q_ref, k_ref, v_ref, a_ref, r_ref, wa_ref, ba_ref, gn_ref, cmat_ref, lvl_ref, o_ref, st_ref, *, nb):
    C = GLA_CHUNK

    @pl.when(pl.program_id(2) == 0)
    def _():
        st_ref[...] = jnp.zeros_like(st_ref)

    lane = _lane((1, LANES))
    lvl = lvl_ref[...]
    cmat = cmat_ref[...]
    for bi in range(nb):
        q = q_ref[bi].astype(F32)
        k = k_ref[bi].astype(F32)
        z = _dot(a_ref[bi], wa_ref[...]) + ba_ref[...]
        la = (jnp.minimum(z, 0.0) - jnp.log1p(jnp.exp(-jnp.abs(z)))) / GLA_TAU
        e = jnp.exp(jnp.dot(cmat, la, precision=lax.Precision.HIGHEST, preferred_element_type=F32))
        qhat = q * e[0:C]
        kend = (k * e[C:2 * C]).astype(BF16)
        decay = jnp.broadcast_to(e[C - 1:C], (LANES, LANES)).T
        kb = k.astype(BF16)
        kl = [(k * e[(2 + i) * C:(3 + i) * C]).astype(BF16) for i in range(len(_GLA_LEVELS))]
        for half in range(2):
            in_half = (lane // HALF) == half
            zero = jnp.zeros_like(q)
            state = st_ref[bi, half]
            o = _dot(jnp.where(in_half, qhat, zero).astype(BF16), state.astype(BF16))
            sc = jnp.zeros((C, C), F32)
            for i in range(len(_GLA_LEVELS)):
                ql = jnp.where(in_half, q * e[(2 + i) * C:(3 + i) * C], zero).astype(BF16)
                sc = jnp.where(lvl == i, _dot_nt(ql, kl[i]), sc)
            sc = jnp.where(lvl == len(_GLA_LEVELS), _dot_nt(jnp.where(in_half, q, zero).astype(BF16), kb), sc)
            v = v_ref[bi, :, half * LANES:(half + 1) * LANES]
            o = o + _dot(sc.astype(BF16), v)
            st_ref[bi, half] = decay * state + lax.dot_general(
                kend, v, (((0,), (0,)), ((), ())), preferred_element_type=F32)
            y = _rms(o, gn_ref[...])
            rr = r_ref[bi, :, half * LANES:(half + 1) * LANES]
            o_ref[bi, :, half * LANES:(half + 1) * LANES] = (y * (rr * jax.nn.sigmoid(rr))).astype(o_ref.dtype)


def gla_mixer(hb, hf, w_alpha, b_alpha, g_norm, cols, nb):
    B, S, _ = hb.shape
    C = GLA_CHUNK
    cmat, lvl = _gla_constants()
    qb, kb_, vb, ab, rb = cols
    return pl.pallas_call(
        functools.partial(_gla_body, nb=nb),
        grid=(B // nb, GLA_HEADS // 2, S // C),
        in_specs=[pl.BlockSpec((nb, C, LANES), lambda b, p, c: (b, c, qb + p)),
                  pl.BlockSpec((nb, C, LANES), lambda b, p, c: (b, c, kb_ + p)),
                  pl.BlockSpec((nb, C, 2 * LANES), lambda b, p, c: (b, c, vb // 2 + p)),
                  pl.BlockSpec((nb, C, LANES), lambda b, p, c: (b, c, ab)),
                  pl.BlockSpec((nb, C, 2 * LANES), lambda b, p, c: (b, c, rb // 2 + p)),
                  pl.BlockSpec((LANES, LANES), lambda b, p, c: (0, p)),
                  pl.BlockSpec((1, LANES), lambda b, p, c: (0, p)),
                  pl.BlockSpec((1, LANES), lambda b, p, c: (0, 0)),
                  pl.BlockSpec(cmat.shape, lambda b, p, c: (0, 0)),
                  pl.BlockSpec(lvl.shape, lambda b, p, c: (0, 0))],
        out_specs=pl.BlockSpec((nb, C, 2 * LANES), lambda b, p, c: (b, c, p)),
        out_shape=jax.ShapeDtypeStruct((B, S, GLA_HEADS * GLA_DV), BF16),
        scratch_shapes=[pltpu.VMEM((nb, 2, LANES, LANES), F32)],
        compiler_params=_cp("parallel", "parallel", "arbitrary"), name="gla")(
            hb, hb, hb, hb, hf, w_alpha, b_alpha, g_norm.reshape(1, LANES), jnp.asarray(cmat), jnp.asarray(lvl))


N_CMP_PAD = 128
N_SLC = 32


def _gelu_tanh(x):
    return 0.5 * x * (1.0 + jnp.tanh(math.sqrt(2.0 / math.pi) * (x + 0.044715 * (x * x * x))))


def _nsa_compress_body(tk_ref, tv_ref, pk_ref, pv_ref, w1k_ref, w1v_ref, w2k_ref, w2v_ref, kc_ref, vc_ref):
    half_k = CMP_STRIDE * NSA_DH
    for t_ref, p_ref, w1_ref, w2_ref, o_ref in ((tk_ref, pk_ref, w1k_ref, w2k_ref, kc_ref),
                                                (tv_ref, pv_ref, w1v_ref, w2v_ref, vc_ref)):
        acc = jnp.zeros((N_CMP_PAD, LANES), F32)
        for g in range(NSA_GROUPS):
            view = t_ref[0, g]
            lo = _dot((view + p_ref[0:1]).astype(BF16), w1_ref[0:half_k])
            hi = _dot((view + p_ref[1:2]).astype(BF16), w1_ref[half_k:2 * half_k])
            pre = lo + pltpu.roll(hi, N_CMP_PAD - 1, 0)
            acc = acc + _dot(_gelu_tanh(pre).astype(BF16), w2_ref[g])
        o_ref[0] = acc.astype(o_ref.dtype)


def nsa_compress(tk, tv, pk, pv, w1k, w1v, w2k, w2v):
    B = tk.shape[0]
    blk4 = pl.BlockSpec((1,) + tk.shape[1:], lambda b: (b, 0, 0, 0))
    full = lambda a: pl.BlockSpec(a.shape, lambda b: (0,) * a.ndim)
    out = pl.BlockSpec((1, N_CMP_PAD, LANES), lambda b: (b, 0, 0))
    return pl.pallas_call(
        _nsa_compress_body,
        grid=(B,),
        in_specs=[blk4, blk4, full(pk), full(pv), full(w1k), full(w1v), full(w2k), full(w2v)],
        out_specs=[out, out],
        out_shape=[jax.ShapeDtypeStruct((B, N_CMP_PAD, LANES), BF16)] * 2,
        compiler_params=_cp("parallel"), name="nsa_compress")(tk, tv, pk, pv, w1k, w1v, w2k, w2v)


def _nsa_slope(g, j):
    return 2.0 ** (-(g * (NSA_HEADS // NSA_GROUPS) + j + 1))


def _nsa_cmp_body(q_ref, kc_ref, vc_ref, ov_ref, o_ref, sel_ref, *, tq):
    hg = NSA_HEADS // NSA_GROUPS
    qs = pl.program_id(1) * tq
    lane = _lane((1, LANES))
    tpos = qs + lax.broadcasted_iota(I32, (tq, LANES), 0)
    dist = tpos - (lane * CMP_STRIDE + (CMP_LEN - 1))
    valid = (dist >= 0) & (lane < N_CMP_PAD - 1)
    validf = valid.astype(F32)
    distf = dist.astype(F32)
    kc = kc_ref[0]
    vc = vc_ref[0]
    imp = jnp.zeros((tq, LANES), F32)
    blocks = [None] * hg
    for g in range(NSA_GROUPS):
        in_g = (lane // HALF) == g
        for j in range(hg):
            qb = q_ref[0, :, j * LANES:(j + 1) * LANES]
            s = _dot_nt(jnp.where(in_g, qb, jnp.zeros_like(qb)), kc) - _nsa_slope(g, j) * distf
            s = jnp.where(valid, s, NEG)
            p = jnp.exp(s - jnp.max(s, axis=-1, keepdims=True)) * validf
            p = (p / jnp.maximum(jnp.sum(p, axis=-1, keepdims=True), TINY)).astype(BF16)
            o = _dot(p, vc)
            blocks[j] = o if g == 0 else jnp.where(in_g, o, blocks[j])
            imp = imp + _dot(p, ov_ref[g])
    for j in range(hg):
        o_ref[0, :, j * LANES:(j + 1) * LANES] = blocks[j]
    blk = lane % N_SLC
    cur = tpos // SLC_LEN
    score = jnp.where((blk == cur) | (blk == 0), FORCE_SCORE, jnp.where(blk <= cur, imp, NEG))
    sel = jnp.zeros((tq, LANES), F32)
    for g in range(NSA_GROUPS):
        sg = jnp.where((lane // N_SLC) == g, score, LOWEST)
        for _ in range(N_SEL):
            m = jnp.max(sg, axis=-1, keepdims=True)
            idx = jnp.min(jnp.where(sg == m, lane, LANES), axis=-1, keepdims=True)
            hit = lane == idx
            sel = jnp.where(hit, 1.0, sel)
            sg = jnp.where(hit, LOWEST, sg)
    sel_ref[0] = sel.astype(sel_ref.dtype)


def nsa_cmp(hb, kc, vc, ov, tq):
    B, S, _ = hb.shape
    qw = NSA_HEADS * NSA_DH
    return pl.pallas_call(
        functools.partial(_nsa_cmp_body, tq=tq),
        grid=(B, S // tq),
        in_specs=[pl.BlockSpec((1, tq, qw), lambda b, i: (b, i, NSA_Q_BLK)),
                  pl.BlockSpec((1, N_CMP_PAD, LANES), lambda b, i: (b, 0, 0)),
                  pl.BlockSpec((1, N_CMP_PAD, LANES), lambda b, i: (b, 0, 0)),
                  pl.BlockSpec(ov.shape, lambda b, i: (0, 0, 0))],
        out_specs=[pl.BlockSpec((1, tq, qw), lambda b, i: (b, i, 0)),
                   pl.BlockSpec((1, tq, LANES), lambda b, i: (b, i, 0))],
        out_shape=[jax.ShapeDtypeStruct((B, S, qw), F32), jax.ShapeDtypeStruct((B, S, LANES), BF16)],
        compiler_params=_cp("parallel", "parallel"), name="nsa_cmp")(hb, kc, vc, ov)


def _softmax_step(s, maskf, v, m, l, acc):
    m_new = jnp.maximum(m, jnp.max(s, axis=-1, keepdims=True))
    p = jnp.exp(s - m_new) * maskf
    alpha = jnp.exp(m - m_new)
    l = alpha * l + jnp.sum(p, axis=-1, keepdims=True)
    acc = alpha * acc + _dot(p.astype(BF16), v)
    return m_new, l, acc


ROW_CHUNK = 32
LOG2E = math.log2(math.e)


def _flash_scratch(rows, tk):
    return ([pltpu.VMEM((rows, tk), F32)] * 2 + [pltpu.VMEM((rows, tk), BF16)] * 2
            + [pltpu.VMEM((rows, LANES), F32)] * 5)


def _softmax_passes(s_ref, p_ref, a_ref, m_ref, l_ref, adjust):
    rows, tk = s_ref.shape
    nl = tk // LANES
    for c in range(rows // ROW_CHUNK):
        r = slice(c * ROW_CHUNK, (c + 1) * ROW_CHUNK)
        s = adjust(c, s_ref[r, :]) * LOG2E
        s_ref[r, :] = s
        top = s[:, 0:LANES]
        for j in range(1, nl):
            top = jnp.maximum(top, s[:, j * LANES:(j + 1) * LANES])
        m_old = m_ref[r, :]
        m_new = jnp.maximum(m_old, jnp.broadcast_to(jnp.max(top, axis=-1, keepdims=True), (ROW_CHUNK, LANES)))
        m_ref[r, :] = m_new
        a_ref[r, :] = jnp.exp2(m_old - m_new)
    for c in range(rows // ROW_CHUNK):
        r = slice(c * ROW_CHUNK, (c + 1) * ROW_CHUNK)
        m_new = m_ref[r, :]
        part = None
        for j in range(nl):
            p = jnp.exp2(s_ref[r, j * LANES:(j + 1) * LANES] - m_new)
            p_ref[r, j * LANES:(j + 1) * LANES] = p.astype(BF16)
            part = p if part is None else part + p
        l_ref[r, :] = a_ref[r, :] * l_ref[r, :] + part


def _flash_causal(n_full, tk, scores, vtile, adjust, scratch):
    sa, sb, pa, pb, aa, ab, m_ref, l_ref, acc_ref = scratch
    s_bufs, p_bufs, a_bufs = (sa, sb), (pa, pb), (aa, ab)
    m_ref[...] = jnp.full(m_ref.shape, NEG, F32)
    l_ref[...] = jnp.zeros_like(l_ref)
    acc_ref[...] = jnp.zeros_like(acc_ref)
    pb[...] = jnp.zeros_like(pb)
    ab[...] = jnp.zeros_like(ab)
    scores(0, sa)

    def pv(par, k0):
        acc_ref[...] = a_bufs[par][...] * acc_ref[...] + _dot(p_bufs[par][...], vtile(k0))

    def stage(par, kt, last):
        k0 = pl.multiple_of(kt * tk, tk)
        pv(1 - par, pl.multiple_of(jnp.maximum(kt - 1, 0) * tk, tk))
        if not last:
            scores(k0 + tk, s_bufs[1 - par])
        _softmax_passes(s_bufs[par], p_bufs[par], a_bufs[par], m_ref, l_ref, lambda c, s: adjust(c, s, k0, last))
        if last:
            pv(par, k0)

    def body(kt, carry):
        for par in range(2):
            @pl.when(kt % 2 == par)
            def _():
                stage(par, kt, False)
        return carry

    lax.fori_loop(0, n_full, body, 0)
    for par in range(2):
        @pl.when(n_full % 2 == par)
        def _():
            stage(par, n_full, True)
    return acc_ref[...] / jnp.maximum(jnp.sum(l_ref[...], axis=-1, keepdims=True), TINY)


def _nsa_attn_body(q_ref, ks_ref, vs_ref, kw_ref, vw_ref, sel_ref, oc_ref, gt_ref, es_ref, o_ref,
                   selb_ref, ws_ref, wp_ref, wa_ref, wm_ref, wl_ref, *scratch, tq, tk):
    hg = NSA_HEADS // NSA_GROUPS
    span = WINDOW + tq
    qs = pl.program_id(1) * tq
    lane = _lane((1, LANES))
    res = {}
    for g in range(NSA_GROUPS):
        in_g = (lane // HALF) == g
        qg = jnp.concatenate(
            [jnp.where(in_g, q_ref[0, :, j * LANES:(j + 1) * LANES], jnp.zeros((tq, LANES), BF16)) for j in range(hg)],
            axis=0)
        selb_ref[...] = (_dot(sel_ref[0], es_ref[g]) - 1.0) * (-NEG)

        def t_of(c):
            r0 = (c * ROW_CHUNK) % tq
            return r0, qs + r0 + lax.broadcasted_iota(I32, (ROW_CHUNK, 1), 0), _nsa_slope(g, (c * ROW_CHUNK) // tq)

        def slc_scores(k0, dst, qg=qg):
            dst[...] = _dot_nt(qg, ks_ref[0, pl.ds(k0, tk), :])

        def slc_adjust(c, s, k0, last):
            r0, t, slope = t_of(c)
            kpos = k0 + _lane((1, tk))
            s = s + slope * kpos.astype(F32) + selb_ref[r0:r0 + ROW_CHUNK, pl.ds(k0, tk)]
            if last:
                s = jnp.where(t >= kpos, s, NEG)
            return s

        o_slc = _flash_causal(qs // tk, tk, slc_scores, lambda k0: vs_ref[0, pl.ds(k0, tk), :], slc_adjust, scratch)

        w0 = pl.multiple_of(jnp.maximum(qs - WINDOW, 0), tq)
        ws_ref[...] = _dot_nt(qg, kw_ref[0, pl.ds(w0, span), :])
        wm_ref[...] = jnp.full(wm_ref.shape, NEG, F32)
        wl_ref[...] = jnp.zeros_like(wl_ref)

        def win_adjust(c, s):
            _, t, slope = t_of(c)
            kpos = w0 + _lane((1, span))
            dist = t - kpos
            return jnp.where((dist >= 0) & (dist < WINDOW), s + slope * kpos.astype(F32), NEG)

        _softmax_passes(ws_ref, wp_ref, wa_ref, wm_ref, wl_ref, win_adjust)
        o_win = _dot(wp_ref[...], vw_ref[0, pl.ds(w0, span), :]) / jnp.maximum(
            jnp.sum(wl_ref[...], axis=-1, keepdims=True), TINY)
        res[g] = (o_slc, o_win)
    gates = jax.nn.sigmoid(gt_ref[0])
    first = lane < HALF
    for j in range(hg):
        out = jnp.zeros((tq, LANES), F32)
        for br in range(3):
            gm = jnp.where(first, gates[:, j * 3 + br:j * 3 + br + 1],
                           gates[:, hg * 3 + j * 3 + br:hg * 3 + j * 3 + br + 1])
            if br == 0:
                val = oc_ref[0, :, j * LANES:(j + 1) * LANES]
            else:
                val = jnp.where(first, res[0][br - 1][j * tq:(j + 1) * tq], res[1][br - 1][j * tq:(j + 1) * tq])
            out = out + gm * val
        o_ref[0, :, j * LANES:(j + 1) * LANES] = out.astype(o_ref.dtype)


def nsa_attn(hb, hf, sel, ocmp, esel, tq, tk):
    B, S, _ = hb.shape
    qw = NSA_HEADS * NSA_DH
    rows = (NSA_HEADS // NSA_GROUPS) * tq
    span = WINDOW + tq
    kv = lambda blk: pl.BlockSpec((1, S, LANES), lambda b, i: (b, 0, blk))
    return pl.pallas_call(
        functools.partial(_nsa_attn_body, tq=tq, tk=tk),
        grid=(B, S // tq),
        in_specs=[pl.BlockSpec((1, tq, qw), lambda b, i: (b, i, NSA_Q_BLK)),
                  kv(NSA_KS_BLK), kv(NSA_VS_BLK), kv(NSA_KW_BLK), kv(NSA_VW_BLK),
                  pl.BlockSpec((1, tq, LANES), lambda b, i: (b, i, 0)),
                  pl.BlockSpec((1, tq, qw), lambda b, i: (b, i, 0)),
                  pl.BlockSpec((1, tq, LANES), lambda b, i: (b, i, NSA_GATE_BLK)),
                  pl.BlockSpec(esel.shape, lambda b, i: (0, 0, 0))],
        out_specs=pl.BlockSpec((1, tq, qw), lambda b, i: (b, i, 0)),
        out_shape=jax.ShapeDtypeStruct((B, S, qw), BF16),
        scratch_shapes=([pltpu.VMEM((tq, S), F32), pltpu.VMEM((rows, span), F32), pltpu.VMEM((rows, span), BF16)]
                        + [pltpu.VMEM((rows, LANES), F32)] * 3 + _flash_scratch(rows, tk)),
        compiler_params=_cp("parallel", "parallel"), name="nsa_attn")(
            hb, hb, hb, hb, hb, sel, ocmp, hf, esel)


def _nsa_tables(S):
    n_cmp = (S - CMP_LEN) // CMP_STRIDE + 1
    c_start = np.arange(n_cmp) * CMP_STRIDE
    c_end = c_start + CMP_LEN - 1
    j_start = np.arange(S // SLC_LEN) * SLC_LEN
    overlap = (c_end[:, None] >= j_start[None]) & (c_start[:, None] <= j_start[None] + SLC_LEN - 1)
    ov = np.zeros((NSA_GROUPS, N_CMP_PAD, LANES), np.float32)
    es = np.zeros((NSA_GROUPS, LANES, S), np.float32)
    for g in range(NSA_GROUPS):
        ov[g, :n_cmp, g * N_SLC:(g + 1) * N_SLC] = overlap
        es[g, g * N_SLC + np.arange(S) // SLC_LEN, np.arange(S)] = 1.0
    return jnp.asarray(ov, BF16), jnp.asarray(es, BF16)


NSA_PERM = np.array([(half * (NSA_HEADS // NSA_GROUPS) + j) * NSA_DH + d
                     for j in range(NSA_HEADS // NSA_GROUPS) for half in range(2) for d in range(NSA_DH)])


def nsa_mixer(hb, hf, pos_k, w1_k, w2_k, pos_v, w1_v, w2_v):
    B, S, _ = hb.shape
    rows = S // CMP_STRIDE

    def stride_view(blk):
        t = hf[:, :, blk * LANES:(blk + 1) * LANES]
        t = t.reshape(B, rows, CMP_STRIDE, NSA_GROUPS, NSA_DH).transpose(0, 3, 1, 2, 4)
        return t.reshape(B, NSA_GROUPS, rows, CMP_STRIDE * NSA_DH)

    def pad_w2(w2):
        out = jnp.zeros((NSA_GROUPS, CMP_HIDDEN, LANES), F32)
        for g in range(NSA_GROUPS):
            out = out.at[g, :, g * NSA_DH:(g + 1) * NSA_DH].set(w2)
        return out.astype(BF16)

    kc, vc = nsa_compress(stride_view(NSA_KC_BLK), stride_view(NSA_VC_BLK),
                          pos_k.reshape(2, CMP_STRIDE * NSA_DH), pos_v.reshape(2, CMP_STRIDE * NSA_DH),
                          w1_k.astype(BF16), w1_v.astype(BF16), pad_w2(w2_k), pad_w2(w2_v))
    ov, es = _nsa_tables(S)
    ocmp, sel = nsa_cmp(hb, kc, vc, ov, 256)
    return nsa_attn(hb, hf, sel, ocmp, es, 128, 256)


def _diff_body(slope_ref, q1_ref, q2_ref, k1_ref, k2_ref, v_ref, lam_ref, gn_ref, o_ref, *scratch,
               tq, lambda_init):
    h = pl.program_id(1)
    i = pl.program_id(2)
    lane = _lane((1, LANES))
    in_half = (lane // HALF) == (h % 2)
    zero = jnp.zeros((tq, LANES), BF16)
    q1 = jnp.where(in_half, q1_ref[0], zero)
    q2 = jnp.where(in_half, q2_ref[0], zero)
    slope = slope_ref[h]

    def scores(k0, dst):
        dst[0:tq, :] = _dot_nt(q1, k1_ref[0, pl.ds(k0, tq), :])
        dst[tq:2 * tq, :] = _dot_nt(q2, k2_ref[0, pl.ds(k0, tq), :])

    def adjust(c, s, k0, last):
        s = s + slope * (k0 + _lane((1, tq))).astype(F32)
        if last:
            t_local = (c * ROW_CHUNK) % tq + lax.broadcasted_iota(I32, (ROW_CHUNK, 1), 0)
            s = jnp.where(t_local >= _lane((1, tq)), s, NEG)
        return s

    o = _flash_causal(i, tq, scores, lambda k0: v_ref[0, pl.ds(k0, tq), :], adjust, scratch)
    lam_rows = lam_ref[...]
    lam = (jnp.exp(jnp.sum(lam_rows[0:1] * lam_rows[1:2], axis=-1, keepdims=True))
           - jnp.exp(jnp.sum(lam_rows[2:3] * lam_rows[3:4], axis=-1, keepdims=True)) + lambda_init)
    y = _rms(o[0:tq] - lam * o[tq:2 * tq], gn_ref[...]) * (1.0 - lambda_init)
    o_ref[0] = y.astype(o_ref.dtype)


def diff_mixer(hd, lam_rows, g_norm, lambda_init, tq):
    B, S, _ = hd.shape
    nb = DIFF_HEADS // 2
    slopes = jnp.asarray(2.0 ** (-8.0 * np.arange(1, DIFF_HEADS + 1) / DIFF_HEADS), F32)
    grid_spec = pltpu.PrefetchScalarGridSpec(
        num_scalar_prefetch=1,
        grid=(B, DIFF_HEADS, S // tq),
        in_specs=[pl.BlockSpec((1, tq, LANES), lambda b, h, i, s: (b, i, h // 2)),
                  pl.BlockSpec((1, tq, LANES), lambda b, h, i, s: (b, i, nb + h // 2)),
                  pl.BlockSpec((1, S, LANES), lambda b, h, i, s: (b, 0, 2 * nb + h // 2)),
                  pl.BlockSpec((1, S, LANES), lambda b, h, i, s: (b, 0, 3 * nb + h // 2)),
                  pl.BlockSpec((1, S, LANES), lambda b, h, i, s: (b, 0, 4 * nb + h)),
                  pl.BlockSpec((4, LANES), lambda b, h, i, s: (0, 0)),
                  pl.BlockSpec((1, LANES), lambda b, h, i, s: (0, 0))],
        out_specs=pl.BlockSpec((1, tq, LANES), lambda b, h, i, s: (b, i, h)),
        scratch_shapes=_flash_scratch(2 * tq, tq))
    return pl.pallas_call(
        functools.partial(_diff_body, tq=tq, lambda_init=lambda_init),
        grid_spec=grid_spec,
        out_shape=jax.ShapeDtypeStruct((B, S, DIFF_HEADS * DIFF_DV), BF16),
        compiler_params=_cp("parallel", "parallel", "parallel"), name="diff_attn")(
            slopes, hd, hd, hd, hd, hd, lam_rows, g_norm.reshape(1, LANES))


def _router_body(x_ref, g_ref, wr_ref, tri_ref, xn_ref, meta_ref, wts_ref, cnt_ref, run_ref):
    @pl.when(pl.program_id(0) == 0)
    def _():
        run_ref[...] = jnp.zeros_like(run_ref)

    xn = _rms(x_ref[...], g_ref[...])
    xn_ref[...] = xn
    lane = _lane((1, LANES))
    logits = jnp.dot(xn, wr_ref[...], precision=lax.Precision.HIGHEST, preferred_element_type=F32)
    logits = jnp.where(lane < N_EXPERTS, logits, LOWEST)
    m1 = jnp.max(logits, axis=-1, keepdims=True)
    i1 = jnp.min(jnp.where(logits == m1, lane, LANES), axis=-1, keepdims=True)
    rest = jnp.where(lane == i1, LOWEST, logits)
    m2 = jnp.max(rest, axis=-1, keepdims=True)
    i2 = jnp.min(jnp.where(rest == m2, lane, LANES), axis=-1, keepdims=True)
    e = jnp.exp(m2 - m1)
    w1 = 1.0 / (1.0 + e)
    w2 = e / (1.0 + e)
    onehot = ((lane == i1) | (lane == i2)).astype(F32)
    pos = _dot(tri_ref[...], onehot.astype(BF16)) + run_ref[...]
    p1 = jnp.sum(jnp.where(lane == i1, pos, 0.0), axis=-1, keepdims=True).astype(I32)
    p2 = jnp.sum(jnp.where(lane == i2, pos, 0.0), axis=-1, keepdims=True).astype(I32)
    run_ref[...] += jnp.sum(onehot, axis=0, keepdims=True)
    meta_ref[...] = jnp.where(lane == 0, i1, jnp.where(lane == 1, i2, jnp.where(lane == 2, p1, jnp.where(lane == 3, p2, 0))))
    wts_ref[...] = jnp.where(lane == 0, w1, jnp.where(lane == 1, w2, 0.0))
    cnt_ref[...] = run_ref[...]


def moe_router(x, g, wr, tm):
    T, D = x.shape
    tri = jnp.asarray(np.tril(np.ones((tm, tm), np.float32), -1), BF16)
    return pl.pallas_call(
        _router_body,
        grid=(T // tm,),
        in_specs=[pl.BlockSpec((tm, D), lambda i: (i, 0)),
                  pl.BlockSpec((1, D), lambda i: (0, 0)),
                  pl.BlockSpec((D, LANES), lambda i: (0, 0)),
                  pl.BlockSpec((tm, tm), lambda i: (0, 0))],
        out_specs=[pl.BlockSpec((tm, D), lambda i: (i, 0)),
                   pl.BlockSpec((tm, LANES), lambda i: (i, 0)),
                   pl.BlockSpec((tm, LANES), lambda i: (i, 0)),
                   pl.BlockSpec((1, LANES), lambda i: (0, 0))],
        out_shape=[jax.ShapeDtypeStruct((T, D), F32), jax.ShapeDtypeStruct((T, LANES), I32),
                   jax.ShapeDtypeStruct((T, LANES), F32), jax.ShapeDtypeStruct((1, LANES), F32)],
        scratch_shapes=[pltpu.VMEM((1, LANES), F32)],
        compiler_params=_cp("arbitrary"), name="moe_router")(x, g.reshape(1, D), wr, tri)


def _row_copy(src_ref, src_row, dst_ref, dst_row, sem):
    return pltpu.make_async_copy(src_ref.at[pl.ds(src_row, 1)], dst_ref.at[pl.ds(dst_row, 1)], sem)


def _dispatch_body(dest_ref, xn_ref, zero_ref, xs_ref, sem, *, tm):
    del zero_ref
    base = pl.program_id(0) * tm

    def issue(t, c):
        for slot in range(TOP_K):
            _row_copy(xn_ref, t, xs_ref, dest_ref[TOP_K * (base + t) + slot], sem).start()
        return c

    lax.fori_loop(0, tm, issue, 0)

    def drain(t, c):
        for slot in range(TOP_K):
            _row_copy(xn_ref, 0, xs_ref, 0, sem).wait()
        return c

    lax.fori_loop(0, tm, drain, 0)


def moe_dispatch(dest, xn, rows, tm):
    T, D = xn.shape
    grid_spec = pltpu.PrefetchScalarGridSpec(
        num_scalar_prefetch=1,
        grid=(T // tm,),
        in_specs=[pl.BlockSpec((tm, D), lambda i, d: (i, 0)),
                  pl.BlockSpec(memory_space=pl.ANY)],
        out_specs=pl.BlockSpec(memory_space=pl.ANY),
        scratch_shapes=[pltpu.SemaphoreType.DMA])
    return pl.pallas_call(
        functools.partial(_dispatch_body, tm=tm),
        grid_spec=grid_spec,
        out_shape=jax.ShapeDtypeStruct((rows, D), F32),
        input_output_aliases={2: 0},
        compiler_params=_cp("arbitrary"), name="moe_dispatch")(dest, xn, jnp.zeros((rows, D), F32))


def _experts_body(te_ref, tv_ref, xs_ref, wg_ref, wu_ref, wd_ref, o_ref, xb_ref, acc_ref):
    i = pl.program_id(0)
    k = pl.program_id(1)

    @pl.when(k == 0)
    def _():
        xb_ref[...] = xs_ref[...].astype(BF16)
        acc_ref[...] = jnp.zeros_like(acc_ref)

    @pl.when(tv_ref[i] > 0)
    def _():
        xb = xb_ref[...]
        hg = _dot(xb, wg_ref[0])
        hu = _dot(xb, wu_ref[0])
        acc_ref[...] += _dot((hg * jax.nn.sigmoid(hg) * hu).astype(BF16), wd_ref[0])

    @pl.when(k == pl.num_programs(1) - 1)
    def _():
        o_ref[...] = acc_ref[...]


def moe_experts(tile_expert, tile_valid, xs, wg, wu, wd, tr, tf):
    R, D = xs.shape
    F = wg.shape[2]
    grid_spec = pltpu.PrefetchScalarGridSpec(
        num_scalar_prefetch=2,
        grid=(R // tr, F // tf),
        in_specs=[pl.BlockSpec((tr, D), lambda i, k, te, tv: (i, 0)),
                  pl.BlockSpec((1, D, tf), lambda i, k, te, tv: (te[i], 0, k)),
                  pl.BlockSpec((1, D, tf), lambda i, k, te, tv: (te[i], 0, k)),
                  pl.BlockSpec((1, tf, D), lambda i, k, te, tv: (te[i], k, 0))],
        out_specs=pl.BlockSpec((tr, D), lambda i, k, te, tv: (i, 0)),
        scratch_shapes=[pltpu.VMEM((tr, D), BF16), pltpu.VMEM((tr, D), F32)])
    return pl.pallas_call(
        _experts_body,
        grid_spec=grid_spec,
        out_shape=jax.ShapeDtypeStruct((R, D), F32),
        compiler_params=_cp("parallel", "arbitrary"), name="moe_experts")(tile_expert, tile_valid, xs, wg, wu, wd)


def _combine_body(dest_ref, x_ref, wts_ref, g_ref, ys_ref, o_ref, buf_ref, sem, *, tm):
    base = pl.program_id(0) * tm

    def issue(t, c):
        for slot in range(TOP_K):
            _row_copy(ys_ref, dest_ref[TOP_K * (base + t) + slot], buf_ref.at[slot], t, sem).start()
        return c

    lax.fori_loop(0, tm, issue, 0)

    def drain(t, c):
        for slot in range(TOP_K):
            _row_copy(ys_ref, 0, buf_ref.at[slot], 0, sem).wait()
        return c

    lax.fori_loop(0, tm, drain, 0)
    wts = wts_ref[...]
    y = x_ref[...] + (wts[:, 0:1] * buf_ref[0] + wts[:, 1:2] * buf_ref[1])
    o_ref[...] = _rms(y, g_ref[...])


def moe_combine(dest, x, wts, g_final, ys, tm):
    T, D = x.shape
    grid_spec = pltpu.PrefetchScalarGridSpec(
        num_scalar_prefetch=1,
        grid=(T // tm,),
        in_specs=[pl.BlockSpec((tm, D), lambda i, d: (i, 0)),
                  pl.BlockSpec((tm, LANES), lambda i, d: (i, 0)),
                  pl.BlockSpec((1, D), lambda i, d: (0, 0)),
                  pl.BlockSpec(memory_space=pl.ANY)],
        out_specs=pl.BlockSpec((tm, D), lambda i, d: (i, 0)),
        scratch_shapes=[pltpu.VMEM((TOP_K, tm, D), F32), pltpu.SemaphoreType.DMA])
    return pl.pallas_call(
        functools.partial(_combine_body, tm=tm),
        grid_spec=grid_spec,
        out_shape=jax.ShapeDtypeStruct((T, D), F32),
        compiler_params=_cp("arbitrary"), name="moe_combine")(dest, x, wts, g_final.reshape(1, D), ys)


def moe_block(x, g, w_router, wg, wu, wd, g_final, tr):
    T, D = x.shape
    wr = _pad_cols(w_router, LANES)
    xn, meta, wts, counts = moe_router(x, g, wr, 512)
    cnt = counts[0, :N_EXPERTS].astype(I32)
    gsz = ((cnt + tr - 1) // tr) * tr
    gend = jnp.cumsum(gsz)
    goff = gend - gsz
    dest = (jnp.take(goff, meta[:, 0:TOP_K]) + meta[:, TOP_K:2 * TOP_K]).reshape(-1)
    n_tiles = (TOP_K * T) // tr + N_EXPERTS
    tile_start = jnp.arange(n_tiles, dtype=I32) * tr
    tile_valid = (tile_start < gend[-1]).astype(I32)
    tile_expert = jnp.minimum(jnp.sum((tile_start[:, None] >= gend[None, :]).astype(I32), axis=1), N_EXPERTS - 1)
    last_valid = jnp.take(tile_expert, jnp.maximum(gend[-1] // tr - 1, 0))
    tile_expert = jnp.where(tile_valid > 0, tile_expert, last_valid)
    xs = moe_dispatch(dest, xn, n_tiles * tr, 256)
    ys = moe_experts(tile_expert, tile_valid, xs, wg, wu, wd, tr, 256)
    return moe_combine(dest, x, wts, g_final, ys, 256)


def _pad_cols(w, width):
    return jnp.pad(w, ((0, 0), (0, width - w.shape[1])))


def _mem_kv(mem2, g, wk, wv, batch):
    w = jnp.concatenate([wk, wv], axis=1).astype(BF16)
    width = wk.shape[1]
    k, v = norm_proj(mem2, g, w, [(width, BF16), (width, BF16)], 256, "mem_kv")
    return k.reshape(batch, -1, width), v.reshape(batch, -1, width)


def kernel(x, mem, mix_norm_0, w_in_0, gla_w_alpha_0, gla_b_alpha_0, gla_out_norm_0, nsa_cmp_pos_k_0, nsa_cmp_w1_k_0, nsa_cmp_w2_k_0, nsa_cmp_pos_v_0, nsa_cmp_w1_v_0, nsa_cmp_w2_v_0, w_out_0, xattn_norm_0, xattn_mem_norm_0, xattn_wq_0, xattn_wk_0, xattn_wv_0, xattn_wo_0, ffn_norm_0, ffn_w_gate_0, ffn_w_up_0, ffn_w_down_0, mix_norm_1, w_in_1, diff_lq1_1, diff_lk1_1, diff_lq2_1, diff_lk2_1, diff_out_norm_1, w_out_1, xattn_norm_1, xattn_mem_norm_1, xattn_wq_1, xattn_wk_1, xattn_wv_1, xattn_wo_1, ffn_norm_1, moe_router_1, moe_w_gate_1, moe_w_up_1, moe_w_down_1, final_norm):
    B, S, D = x.shape
    T = B * S
    x2 = x.reshape(T, D)
    mem2 = mem.reshape(-1, D)
    q_scale = NSA_DH ** -0.5

    sizes = [GLA_HEADS * GLA_DK, GLA_HEADS * GLA_DK, GLA_HEADS * GLA_DV, GLA_HEADS * GLA_DV, GLA_RANK,
             NSA_HEADS * NSA_DH] + [NSA_GROUPS * NSA_DH] * 6 + [3 * NSA_HEADS]
    (g_q, g_k, g_v, g_r, g_a, n_q, n_kc, n_vc, n_ks, n_vs, n_kw, n_vw, n_g) = jnp.split(
        w_in_0, np.cumsum(sizes)[:-1].tolist(), axis=1)
    w0 = jnp.concatenate([n_q[:, NSA_PERM] * q_scale, n_ks, n_vs, n_kw, n_vw, g_q * q_scale, g_k, g_v,
                          _pad_cols(g_a, LANES), g_r, _pad_cols(n_g, LANES), n_kc, n_vc], axis=1).astype(BF16)
    hb, hf = norm_proj(x2, mix_norm_0, w0, [(HB0_WIDTH, BF16), (HF0_WIDTH, F32)], 512, "in_proj0")
    hb = hb.reshape(B, S, HB0_WIDTH)
    hf = hf.reshape(B, S, HF0_WIDTH)
    w_alpha = jnp.pad(gla_w_alpha_0, ((0, LANES - GLA_RANK), (0, 0))).astype(BF16)
    o_a = gla_mixer(hb, hf, w_alpha, gla_b_alpha_0.reshape(1, -1), gla_out_norm_0, GLA_COLS, 4)
    o_b = nsa_mixer(hb, hf, nsa_cmp_pos_k_0, nsa_cmp_w1_k_0, nsa_cmp_w2_k_0,
                    nsa_cmp_pos_v_0, nsa_cmp_w1_v_0, nsa_cmp_w2_v_0)
    n_a = GLA_HEADS * GLA_DV
    x2 = proj_residual(x2, [o_a.reshape(T, -1), o_b.reshape(T, -1)],
                       [w_out_0[:n_a].astype(BF16), w_out_0[n_a:][NSA_PERM].astype(BF16)], 512, "out_proj0")
    k0, v0 = _mem_kv(mem2, xattn_mem_norm_0, xattn_wk_0, xattn_wv_0, B)
    x2 = xattn_block(x2, xattn_norm_0, (xattn_wq_0 * q_scale).astype(BF16), k0, v0, xattn_wo_0.astype(BF16),
                     S, 512, "xattn0")
    x2 = ffn_block(x2, ffn_norm_0, ffn_w_gate_0.astype(BF16), ffn_w_up_0.astype(BF16), ffn_w_down_0.astype(BF16),
                   1024, 256, "ffn0")

    lambda_init = 0.8 - 0.6 * math.exp(-0.3 * 1)
    n_q1 = 2 * DIFF_HEADS * DIFF_DH
    w1 = jnp.concatenate([w_in_1[:, :n_q1] * q_scale, w_in_1[:, n_q1:]], axis=1).astype(BF16)
    (hd,) = norm_proj(x2, mix_norm_1, w1, [(w1.shape[1], BF16)], 512, "in_proj1")
    lam_rows = _pad_cols(jnp.stack([diff_lq1_1, diff_lk1_1, diff_lq2_1, diff_lk2_1]), LANES)
    o_d = diff_mixer(hd.reshape(B, S, -1), lam_rows, diff_out_norm_1, lambda_init, 256)
    x2 = proj_residual(x2, [o_d.reshape(T, -1)], [w_out_1.astype(BF16)], 512, "out_proj1")
    k1, v1 = _mem_kv(mem2, xattn_mem_norm_1, xattn_wk_1, xattn_wv_1, B)
    x2 = xattn_block(x2, xattn_norm_1, (xattn_wq_1 * q_scale).astype(BF16), k1, v1, xattn_wo_1.astype(BF16),
                     S, 512, "xattn1")
    out = moe_block(x2, ffn_norm_1, moe_router_1, moe_w_gate_1.astype(BF16), moe_w_up_1.astype(BF16),
                    moe_w_down_1.astype(BF16), final_norm, 512)
    return out.reshape(B, S, D)
```

```python
import functools
import math

import numpy as np
import jax
import jax.numpy as jnp
from jax import lax
from jax.experimental import pallas as pl
from jax.experimental.pallas import tpu as pltpu

F32 = jnp.float32
BF16 = jnp.bfloat16
I32 = jnp.int32

D_MODEL = 1024
GLA_HEADS, GLA_DK, GLA_DV, GLA_RANK, GLA_TAU, GLA_CHUNK = 4, 64, 128, 16, 16.0, 64
NSA_HEADS, NSA_GROUPS, NSA_DH = 8, 2, 64
CMP_LEN, CMP_STRIDE, CMP_HIDDEN, SLC_LEN, N_SEL, WINDOW = 32, 16, 256, 64, 8, 512
DIFF_HEADS, DIFF_DH, DIFF_DV = 8, 64, 128
XATTN_HEADS, XATTN_DH = 4, 64
D_FF, N_EXPERTS, TOP_K = 2816, 8, 2
EPS, NEG, TINY, FORCE_SCORE = 1e-6, -1e30, 1e-30, 1e4

LANES = 128
HALF = LANES // 2
VMEM_LIMIT = 56 * 1024 * 1024
LOWEST = -3.0e38

HB0_WIDTH, HF0_WIDTH = 17 * LANES, 7 * LANES
NSA_Q_BLK, NSA_KS_BLK, NSA_VS_BLK, NSA_KW_BLK, NSA_VW_BLK = 0, 4, 5, 6, 7
GLA_COLS = (8, 10, 12, 16, 0)
NSA_GATE_BLK, NSA_KC_BLK, NSA_VC_BLK = 4, 5, 6


def _cp(*sem):
    return pltpu.CompilerParams(dimension_semantics=sem, vmem_limit_bytes=VMEM_LIMIT)


def _rms(x, g):
    y = x * lax.rsqrt(jnp.mean(x * x, axis=-1, keepdims=True) + EPS)
    return y * g


def _dot(a, b):
    return jnp.dot(a, b, preferred_element_type=F32)


def _dot_nt(a, b):
    return lax.dot_general(a, b, (((1,), (1,)), ((), ())), preferred_element_type=F32)


def _lane(shape):
    return lax.broadcasted_iota(I32, shape, len(shape) - 1)


def _norm_proj_body(x_ref, g_ref, w_ref, *o_refs, widths):
    xn = _rms(x_ref[...], g_ref[...]).astype(BF16)
    off = 0
    for o_ref, wd in zip(o_refs, widths):
        for c0 in range(0, wd, 512):
            cw = min(512, wd - c0)
            o_ref[:, c0:c0 + cw] = _dot(xn, w_ref[:, off + c0:off + c0 + cw]).astype(o_ref.dtype)
        off += wd


def norm_proj(x, g, w, outs, tm, name):
    T, D = x.shape
    widths = tuple(o[0] for o in outs)
    return pl.pallas_call(
        functools.partial(_norm_proj_body, widths=widths),
        grid=(T // tm,),
        in_specs=[pl.BlockSpec((tm, D), lambda i: (i, 0)),
                  pl.BlockSpec((1, D), lambda i: (0, 0)),
                  pl.BlockSpec(w.shape, lambda i: (0, 0))],
        out_specs=[pl.BlockSpec((tm, wd), lambda i: (i, 0)) for wd, _ in outs],
        out_shape=[jax.ShapeDtypeStruct((T, wd), dt) for wd, dt in outs],
        compiler_params=_cp("parallel"), name=name)(x, g.reshape(1, D), w)


def _proj_res_body(*refs, n):
    res_ref = refs[0]
    a_refs = refs[1:1 + n]
    w_refs = refs[1 + n:1 + 2 * n]
    o_ref = refs[1 + 2 * n]
    acc = res_ref[...]
    for a_ref, w_ref in zip(a_refs, w_refs):
        acc = acc + _dot(a_ref[...].astype(BF16), w_ref[...])
    o_ref[...] = acc


def proj_residual(res, a_list, w_list, tm, name):
    T, D = res.shape
    n = len(a_list)
    return pl.pallas_call(
        functools.partial(_proj_res_body, n=n),
        grid=(T // tm,),
        in_specs=([pl.BlockSpec((tm, D), lambda i: (i, 0))]
                  + [pl.BlockSpec((tm, a.shape[1]), lambda i: (i, 0)) for a in a_list]
                  + [pl.BlockSpec(w.shape, lambda i: (0, 0)) for w in w_list]),
        out_specs=pl.BlockSpec((tm, D), lambda i: (i, 0)),
        out_shape=jax.ShapeDtypeStruct((T, D), F32),
        compiler_params=_cp("parallel"), name=name)(res, *a_list, *w_list)


def _xattn_body(x_ref, g_ref, wq_ref, k_ref, v_ref, wo_ref, o_ref):
    x = x_ref[...]
    q = _dot(_rms(x, g_ref[...]).astype(BF16), wq_ref[...]).astype(BF16)
    lane = _lane((1, LANES))
    blocks = []
    for blk in range(XATTN_HEADS // 2):
        qb = q[:, blk * LANES:(blk + 1) * LANES]
        kb = k_ref[0, :, blk * LANES:(blk + 1) * LANES]
        vb = v_ref[0, :, blk * LANES:(blk + 1) * LANES]
        pair = None
        for half in range(2):
            in_half = (lane // HALF) == half
            s = _dot_nt(jnp.where(in_half, qb, jnp.zeros_like(qb)), kb)
            p = jnp.exp(s - jnp.max(s, axis=-1, keepdims=True))
            p = p / jnp.sum(p, axis=-1, keepdims=True)
            o = _dot(p.astype(BF16), vb)
            pair = o if half == 0 else jnp.where(in_half, o, pair)
        blocks.append(pair)
    o = jnp.concatenate(blocks, axis=1).astype(BF16)
    o_ref[...] = x + _dot(o, wo_ref[...])


def xattn_block(x, g, wq, k, v, wo, seq, tm, name):
    T, D = x.shape
    M, W = k.shape[1], k.shape[2]
    per = seq // tm
    return pl.pallas_call(
        _xattn_body,
        grid=(T // tm,),
        in_specs=[pl.BlockSpec((tm, D), lambda i: (i, 0)),
                  pl.BlockSpec((1, D), lambda i: (0, 0)),
                  pl.BlockSpec(wq.shape, lambda i: (0, 0)),
                  pl.BlockSpec((1, M, W), lambda i: (i // per, 0, 0)),
                  pl.BlockSpec((1, M, W), lambda i: (i // per, 0, 0)),
                  pl.BlockSpec(wo.shape, lambda i: (0, 0))],
        out_specs=pl.BlockSpec((tm, D), lambda i: (i, 0)),
        out_shape=jax.ShapeDtypeStruct((T, D), F32),
        compiler_params=_cp("parallel"), name=name)(x, g.reshape(1, D), wq, k, v, wo)


def _ffn_body(x_ref, g_ref, wg_ref, wu_ref, wd_ref, o_ref, xn_ref, acc_ref):
    k = pl.program_id(1)

    @pl.when(k == 0)
    def _():
        xn_ref[...] = _rms(x_ref[...], g_ref[...]).astype(BF16)
        acc_ref[...] = x_ref[...]

    xn = xn_ref[...]
    hg = _dot(xn, wg_ref[...])
    hu = _dot(xn, wu_ref[...])
    h = (hg * jax.nn.sigmoid(hg) * hu).astype(BF16)
    acc_ref[...] += _dot(h, wd_ref[...])

    @pl.when(k == pl.num_programs(1) - 1)
    def _():
        o_ref[...] = acc_ref[...]


def ffn_block(x, g, wg, wu, wd, tm, tf, name):
    T, D = x.shape
    F = wg.shape[1]
    return pl.pallas_call(
        _ffn_body,
        grid=(T // tm, F // tf),
        in_specs=[pl.BlockSpec((tm, D), lambda i, k: (i, 0)),
                  pl.BlockSpec((1, D), lambda i, k: (0, 0)),
                  pl.BlockSpec((D, tf), lambda i, k: (0, k)),
                  pl.BlockSpec((D, tf), lambda i, k: (0, k)),
                  pl.BlockSpec((tf, D), lambda i, k: (k, 0))],
        out_specs=pl.BlockSpec((tm, D), lambda i, k: (i, 0)),
        out_shape=jax.ShapeDtypeStruct((T, D), F32),
        scratch_shapes=[pltpu.VMEM((tm, D), BF16), pltpu.VMEM((tm, D), F32)],
        compiler_params=_cp("parallel", "arbitrary"), name=name)(x, g.reshape(1, D), wg, wu, wd)


_GLA_LEVELS = (32, 16, 8, 4, 2, 1)


def _gla_constants():
    C = GLA_CHUNK
    t = np.arange(C)
    r = t[None, :]
    mats = [r <= t[:, None], r > t[:, None]]
    for hs in _GLA_LEVELS:
        c = (t // (2 * hs)) * (2 * hs) + hs - 1
        right = (t % (2 * hs)) >= hs
        m_right = (r > c[:, None]) & (r <= t[:, None])
        m_left = (r > t[:, None]) & (r <= c[:, None])
        mats.append(np.where(right[:, None], m_right, m_left))
    cmat = np.concatenate(mats, 0).astype(np.float32)
    x = t[:, None] ^ t[None, :]
    lvl = np.full((C, C), -1, np.int32)
    for i, hs in enumerate(_GLA_LEVELS):
        lvl[(t[:, None] > t[None, :]) & (x >= hs) & (x < 2 * hs)] = i
    lvl[t[:, None] == t[None, :]] = len(_GLA_LEVELS)
    return cmat, lvl


def _gla_body(q_ref, k_ref, v_ref, a_ref, r_ref, wa_ref, ba_ref, gn_ref, cmat_ref, lvl_ref, o_ref, st_ref, *, nb):
    C = GLA_CHUNK

    @pl.when(pl.program_id(2) == 0)
    def _():
        st_ref[...] = jnp.zeros_like(st_ref)

    nlev = len(_GLA_LEVELS)
    in_a = _lane((1, LANES)) < HALF
    lvl = lvl_ref[...]
    cmat = cmat_ref[...]
    on_diag = (lax.broadcasted_iota(I32, (LANES, 2 * LANES), 0) // HALF
               == lax.broadcasted_iota(I32, (LANES, 2 * LANES), 1) // LANES)
    for bi in range(nb):
        q = q_ref[bi].astype(F32)
        k = k_ref[bi].astype(F32)
        z = _dot(a_ref[bi], wa_ref[...]) + ba_ref[...]
        la = (jnp.minimum(z, 0.0) - jnp.log1p(jnp.exp(-jnp.abs(z)))) / GLA_TAU
        hi = la.astype(BF16)
        rest = la - hi.astype(F32)
        mid = rest.astype(BF16)
        lo = (rest - mid.astype(F32)).astype(BF16)
        u3 = _dot(cmat, jnp.concatenate([hi, mid, lo], axis=1))
        e = jnp.exp(u3[:, 0:LANES] + u3[:, LANES:2 * LANES] + u3[:, 2 * LANES:3 * LANES])
        qhat = (q * e[0:C]).astype(BF16)
        kend = (k * e[C:2 * C]).astype(BF16)
        decay = jnp.broadcast_to(e[C - 1:C], (LANES, LANES)).T
        zero = jnp.zeros_like(q)

        def stacked(x):
            return jnp.concatenate([jnp.where(in_a, x, zero), jnp.where(in_a, zero, x)], axis=0).astype(BF16)

        sc = jnp.zeros((2 * C, C), F32)
        for i in range(nlev):
            ei = e[(2 + i) * C:(3 + i) * C]
            sc = jnp.where(lvl == i, _dot_nt(stacked(q * ei), (k * ei).astype(BF16)), sc)
        sc = jnp.where(lvl == nlev, _dot_nt(stacked(q), k.astype(BF16)), sc)
        v = v_ref[bi]
        state = st_ref[bi]
        o_inter = _dot(qhat, state.astype(BF16))
        o_intra = _dot(sc.astype(BF16), v)
        update = lax.dot_general(kend, v, (((0,), (0,)), ((), ())), preferred_element_type=F32)
        st_ref[bi] = jnp.where(on_diag, jnp.concatenate([decay, decay], axis=1) * state + update, 0.0)
        for half in range(2):
            cols = slice(half * LANES, (half + 1) * LANES)
            y = _rms(o_inter[:, cols] + o_intra[half * C:(half + 1) * C, cols], gn_ref[...])
            rr = r_ref[bi, :, cols]
            o_ref[bi, :, cols] = (y * (rr * jax.nn.sigmoid(rr))).astype(o_ref.dtype)


def gla_mixer(hb, hf, w_alpha, b_alpha, g_norm, cols, nb):
    B, S, _ = hb.shape
    C = GLA_CHUNK
    cmat, lvl = _gla_constants()
    lvl = np.concatenate([lvl, lvl], axis=0)
    qb, kb_, vb, ab, rb = cols
    return pl.pallas_call(
        functools.partial(_gla_body, nb=nb),
        grid=(B // nb, GLA_HEADS // 2, S // C),
        in_specs=[pl.BlockSpec((nb, C, LANES), lambda b, p, c: (b, c, qb + p)),
                  pl.BlockSpec((nb, C, LANES), lambda b, p, c: (b, c, kb_ + p)),
                  pl.BlockSpec((nb, C, 2 * LANES), lambda b, p, c: (b, c, vb // 2 + p)),
                  pl.BlockSpec((nb, C, LANES), lambda b, p, c: (b, c, ab)),
                  pl.BlockSpec((nb, C, 2 * LANES), lambda b, p, c: (b, c, rb // 2 + p)),
                  pl.BlockSpec((LANES, LANES), lambda b, p, c: (0, p)),
                  pl.BlockSpec((1, LANES), lambda b, p, c: (0, p)),
                  pl.BlockSpec((1, LANES), lambda b, p, c: (0, 0)),
                  pl.BlockSpec(cmat.shape, lambda b, p, c: (0, 0)),
                  pl.BlockSpec(lvl.shape, lambda b, p, c: (0, 0))],
        out_specs=pl.BlockSpec((nb, C, 2 * LANES), lambda b, p, c: (b, c, p)),
        out_shape=jax.ShapeDtypeStruct((B, S, GLA_HEADS * GLA_DV), BF16),
        scratch_shapes=[pltpu.VMEM((nb, LANES, 2 * LANES), F32)],
        compiler_params=_cp("parallel", "parallel", "arbitrary"), name="gla")(
            hb, hb, hb, hb, hf, w_alpha, b_alpha, g_norm.reshape(1, LANES), jnp.asarray(cmat, BF16), jnp.asarray(lvl))


N_CMP_PAD = 128
N_SLC = 32


def _gelu_tanh(x):
    return 0.5 * x * (1.0 + jnp.tanh(math.sqrt(2.0 / math.pi) * (x + 0.044715 * (x * x * x))))


def _nsa_compress_body(tk_ref, tv_ref, pk_ref, pv_ref, w1k_ref, w1v_ref, w2k_ref, w2v_ref, kc_ref, vc_ref):
    half_k = CMP_STRIDE * NSA_DH
    for t_ref, p_ref, w1_ref, w2_ref, o_ref in ((tk_ref, pk_ref, w1k_ref, w2k_ref, kc_ref),
                                                (tv_ref, pv_ref, w1v_ref, w2v_ref, vc_ref)):
        acc = jnp.zeros((N_CMP_PAD, LANES), F32)
        for g in range(NSA_GROUPS):
            view = t_ref[0, g]
            lo = _dot((view + p_ref[0:1]).astype(BF16), w1_ref[0:half_k])
            hi = _dot((view + p_ref[1:2]).astype(BF16), w1_ref[half_k:2 * half_k])
            pre = lo + pltpu.roll(hi, N_CMP_PAD - 1, 0)
            acc = acc + _dot(_gelu_tanh(pre).astype(BF16), w2_ref[g])
        o_ref[0] = acc.astype(o_ref.dtype)


def nsa_compress(tk, tv, pk, pv, w1k, w1v, w2k, w2v):
    B = tk.shape[0]
    blk4 = pl.BlockSpec((1,) + tk.shape[1:], lambda b: (b, 0, 0, 0))
    full = lambda a: pl.BlockSpec(a.shape, lambda b: (0,) * a.ndim)
    out = pl.BlockSpec((1, N_CMP_PAD, LANES), lambda b: (b, 0, 0))
    return pl.pallas_call(
        _nsa_compress_body,
        grid=(B,),
        in_specs=[blk4, blk4, full(pk), full(pv), full(w1k), full(w1v), full(w2k), full(w2v)],
        out_specs=[out, out],
        out_shape=[jax.ShapeDtypeStruct((B, N_CMP_PAD, LANES), BF16)] * 2,
        compiler_params=_cp("parallel"), name="nsa_compress")(tk, tv, pk, pv, w1k, w1v, w2k, w2v)


def _nsa_slope(g, j):
    return 2.0 ** (-(g * (NSA_HEADS // NSA_GROUPS) + j + 1))


def _nsa_cmp_body(q_ref, kc_ref, vc_ref, ov_ref, o_ref, sel_ref, *, tq):
    hg = NSA_HEADS // NSA_GROUPS
    qs = pl.program_id(1) * tq
    lane = _lane((1, LANES))
    tpos = qs + lax.broadcasted_iota(I32, (tq, LANES), 0)
    dist = tpos - (lane * CMP_STRIDE + (CMP_LEN - 1))
    valid = (dist >= 0) & (lane < N_CMP_PAD - 1)
    validf = valid.astype(F32)
    distf = dist.astype(F32)
    kc = kc_ref[0]
    vc = vc_ref[0]
    imp = jnp.zeros((tq, LANES), F32)
    blocks = [None] * hg
    for g in range(NSA_GROUPS):
        in_g = (lane // HALF) == g
        for j in range(hg):
            qb = q_ref[0, :, j * LANES:(j + 1) * LANES]
            s = _dot_nt(jnp.where(in_g, qb, jnp.zeros_like(qb)), kc) - _nsa_slope(g, j) * distf
            s = jnp.where(valid, s, NEG)
            p = jnp.exp(s - jnp.max(s, axis=-1, keepdims=True)) * validf
            p = (p / jnp.maximum(jnp.sum(p, axis=-1, keepdims=True), TINY)).astype(BF16)
            o = _dot(p, vc)
            blocks[j] = o if g == 0 else jnp.where(in_g, o, blocks[j])
            imp = imp + _dot(p, ov_ref[g])
    for j in range(hg):
        o_ref[0, :, j * LANES:(j + 1) * LANES] = blocks[j]
    blk = lane % N_SLC
    cur = tpos // SLC_LEN
    score = jnp.where((blk == cur) | (blk == 0), FORCE_SCORE, jnp.where(blk <= cur, imp, NEG))
    sel = jnp.zeros((tq, LANES), F32)
    for g in range(NSA_GROUPS):
        sg = jnp.where((lane // N_SLC) == g, score, LOWEST)
        for _ in range(N_SEL):
            m = jnp.max(sg, axis=-1, keepdims=True)
            idx = jnp.min(jnp.where(sg == m, lane, LANES), axis=-1, keepdims=True)
            hit = lane == idx
            sel = jnp.where(hit, 1.0, sel)
            sg = jnp.where(hit, LOWEST, sg)
    sel_ref[0] = sel.astype(sel_ref.dtype)


def nsa_cmp(hb, kc, vc, ov, tq):
    B, S, _ = hb.shape
    qw = NSA_HEADS * NSA_DH
    return pl.pallas_call(
        functools.partial(_nsa_cmp_body, tq=tq),
        grid=(B, S // tq),
        in_specs=[pl.BlockSpec((1, tq, qw), lambda b, i: (b, i, NSA_Q_BLK)),
                  pl.BlockSpec((1, N_CMP_PAD, LANES), lambda b, i: (b, 0, 0)),
                  pl.BlockSpec((1, N_CMP_PAD, LANES), lambda b, i: (b, 0, 0)),
                  pl.BlockSpec(ov.shape, lambda b, i: (0, 0, 0))],
        out_specs=[pl.BlockSpec((1, tq, qw), lambda b, i: (b, i, 0)),
                   pl.BlockSpec((1, tq, LANES), lambda b, i: (b, i, 0))],
        out_shape=[jax.ShapeDtypeStruct((B, S, qw), F32), jax.ShapeDtypeStruct((B, S, LANES), BF16)],
        compiler_params=_cp("parallel", "parallel"), name="nsa_cmp")(hb, kc, vc, ov)


def _softmax_step(s, maskf, v, m, l, acc):
    m_new = jnp.maximum(m, jnp.max(s, axis=-1, keepdims=True))
    p = jnp.exp(s - m_new) * maskf
    alpha = jnp.exp(m - m_new)
    l = alpha * l + jnp.sum(p, axis=-1, keepdims=True)
    acc = alpha * acc + _dot(p.astype(BF16), v)
    return m_new, l, acc


FFN_TF = D_FF // 2
ROW_CHUNK = 32
LOG2E = math.log2(math.e)


def _flash_scratch(rows, tk):
    return ([pltpu.VMEM((rows, tk), F32)] * 2 + [pltpu.VMEM((rows, tk), BF16)] * 2
            + [pltpu.VMEM((rows, LANES), F32)] * 5)


def _softmax_passes(s_ref, p_ref, a_ref, m_ref, l_ref, adjust):
    rows, tk = s_ref.shape
    nl = tk // LANES
    for c in range(rows // ROW_CHUNK):
        r = slice(c * ROW_CHUNK, (c + 1) * ROW_CHUNK)
        s = adjust(c, s_ref[r, :]) * LOG2E
        s_ref[r, :] = s
        top = s[:, 0:LANES]
        for j in range(1, nl):
            top = jnp.maximum(top, s[:, j * LANES:(j + 1) * LANES])
        m_old = m_ref[r, :]
        m_new = jnp.maximum(m_old, jnp.broadcast_to(jnp.max(top, axis=-1, keepdims=True), (ROW_CHUNK, LANES)))
        m_ref[r, :] = m_new
        a_ref[r, :] = jnp.exp2(m_old - m_new)
    for c in range(rows // ROW_CHUNK):
        r = slice(c * ROW_CHUNK, (c + 1) * ROW_CHUNK)
        m_new = m_ref[r, :]
        part = None
        for j in range(nl):
            p = jnp.exp2(s_ref[r, j * LANES:(j + 1) * LANES] - m_new)
            p_ref[r, j * LANES:(j + 1) * LANES] = p.astype(BF16)
            part = p if part is None else part + p
        l_ref[r, :] = a_ref[r, :] * l_ref[r, :] + part


def _flash_causal(n_full, tk, scores, vtile, adjust, scratch):
    sa, sb, pa, pb, aa, ab, m_ref, l_ref, acc_ref = scratch
    s_bufs, p_bufs, a_bufs = (sa, sb), (pa, pb), (aa, ab)
    m_ref[...] = jnp.full(m_ref.shape, NEG, F32)
    l_ref[...] = jnp.zeros_like(l_ref)
    acc_ref[...] = jnp.zeros_like(acc_ref)
    pb[...] = jnp.zeros_like(pb)
    ab[...] = jnp.zeros_like(ab)
    scores(0, sa)

    def pv(par, k0):
        acc_ref[...] = a_bufs[par][...] * acc_ref[...] + _dot(p_bufs[par][...], vtile(k0))

    def stage(par, kt, last):
        k0 = pl.multiple_of(kt * tk, tk)
        pv(1 - par, pl.multiple_of(jnp.maximum(kt - 1, 0) * tk, tk))
        if not last:
            scores(k0 + tk, s_bufs[1 - par])
        _softmax_passes(s_bufs[par], p_bufs[par], a_bufs[par], m_ref, l_ref, lambda c, s: adjust(c, s, k0, last))
        if last:
            pv(par, k0)

    def body(kt, carry):
        for par in range(2):
            @pl.when(kt % 2 == par)
            def _():
                stage(par, kt, False)
        return carry

    lax.fori_loop(0, n_full, body, 0)
    for par in range(2):
        @pl.when(n_full % 2 == par)
        def _():
            stage(par, n_full, True)
    return acc_ref[...] / jnp.maximum(jnp.sum(l_ref[...], axis=-1, keepdims=True), TINY)


def _nsa_attn_body(q_ref, ks_ref, vs_ref, kw_ref, vw_ref, sel_ref, oc_ref, gt_ref, es_ref, o_ref,
                   selb_ref, ws_ref, wp_ref, wa_ref, wm_ref, wl_ref, *scratch, tq, tk):
    hg = NSA_HEADS // NSA_GROUPS
    span = WINDOW + tq
    qs = pl.program_id(1) * tq
    lane = _lane((1, LANES))
    res = {}
    for g in range(NSA_GROUPS):
        in_g = (lane // HALF) == g
        qg = jnp.concatenate(
            [jnp.where(in_g, q_ref[0, :, j * LANES:(j + 1) * LANES], jnp.zeros((tq, LANES), BF16)) for j in range(hg)],
            axis=0)
        selb_ref[...] = (_dot(sel_ref[0], es_ref[g]) - 1.0) * (-NEG)

        def t_of(c):
            r0 = (c * ROW_CHUNK) % tq
            return r0, qs + r0 + lax.broadcasted_iota(I32, (ROW_CHUNK, 1), 0), _nsa_slope(g, (c * ROW_CHUNK) // tq)

        def slc_scores(k0, dst, qg=qg):
            dst[...] = _dot_nt(qg, ks_ref[0, pl.ds(k0, tk), :])

        def slc_adjust(c, s, k0, last):
            r0, t, slope = t_of(c)
            kpos = k0 + _lane((1, tk))
            s = s + slope * kpos.astype(F32) + selb_ref[r0:r0 + ROW_CHUNK, pl.ds(k0, tk)]
            if last:
                s = jnp.where(t >= kpos, s, NEG)
            return s

        o_slc = _flash_causal(qs // tk, tk, slc_scores, lambda k0: vs_ref[0, pl.ds(k0, tk), :], slc_adjust, scratch)

        w0 = pl.multiple_of(jnp.maximum(qs - WINDOW, 0), tq)
        ws_ref[...] = _dot_nt(qg, kw_ref[0, pl.ds(w0, span), :])
        wm_ref[...] = jnp.full(wm_ref.shape, NEG, F32)
        wl_ref[...] = jnp.zeros_like(wl_ref)

        def win_adjust(c, s):
            _, t, slope = t_of(c)
            kpos = w0 + _lane((1, span))
            dist = t - kpos
            return jnp.where((dist >= 0) & (dist < WINDOW), s + slope * kpos.astype(F32), NEG)

        _softmax_passes(ws_ref, wp_ref, wa_ref, wm_ref, wl_ref, win_adjust)
        o_win = _dot(wp_ref[...], vw_ref[0, pl.ds(w0, span), :]) / jnp.maximum(
            jnp.sum(wl_ref[...], axis=-1, keepdims=True), TINY)
        res[g] = (o_slc, o_win)
    gates = jax.nn.sigmoid(gt_ref[0])
    first = lane < HALF
    for j in range(hg):
        out = jnp.zeros((tq, LANES), F32)
        for br in range(3):
            gm = jnp.where(first, gates[:, j * 3 + br:j * 3 + br + 1],
                           gates[:, hg * 3 + j * 3 + br:hg * 3 + j * 3 + br + 1])
            if br == 0:
                val = oc_ref[0, :, j * LANES:(j + 1) * LANES]
            else:
                val = jnp.where(first, res[0][br - 1][j * tq:(j + 1) * tq], res[1][br - 1][j * tq:(j + 1) * tq])
            out = out + gm * val
        o_ref[0, :, j * LANES:(j + 1) * LANES] = out.astype(o_ref.dtype)


def nsa_attn(hb, hf, sel, ocmp, esel, tq, tk):
    B, S, _ = hb.shape
    qw = NSA_HEADS * NSA_DH
    rows = (NSA_HEADS // NSA_GROUPS) * tq
    span = WINDOW + tq
    kv = lambda blk: pl.BlockSpec((1, S, LANES), lambda b, i: (b, 0, blk))
    return pl.pallas_call(
        functools.partial(_nsa_attn_body, tq=tq, tk=tk),
        grid=(B, S // tq),
        in_specs=[pl.BlockSpec((1, tq, qw), lambda b, i: (b, i, NSA_Q_BLK)),
                  kv(NSA_KS_BLK), kv(NSA_VS_BLK), kv(NSA_KW_BLK), kv(NSA_VW_BLK),
                  pl.BlockSpec((1, tq, LANES), lambda b, i: (b, i, 0)),
                  pl.BlockSpec((1, tq, qw), lambda b, i: (b, i, 0)),
                  pl.BlockSpec((1, tq, LANES), lambda b, i: (b, i, NSA_GATE_BLK)),
                  pl.BlockSpec(esel.shape, lambda b, i: (0, 0, 0))],
        out_specs=pl.BlockSpec((1, tq, qw), lambda b, i: (b, i, 0)),
        out_shape=jax.ShapeDtypeStruct((B, S, qw), BF16),
        scratch_shapes=([pltpu.VMEM((tq, S), F32), pltpu.VMEM((rows, span), F32), pltpu.VMEM((rows, span), BF16)]
                        + [pltpu.VMEM((rows, LANES), F32)] * 3 + _flash_scratch(rows, tk)),
        compiler_params=_cp("parallel", "parallel"), name="nsa_attn")(
            hb, hb, hb, hb, hb, sel, ocmp, hf, esel)


def _nsa_tables(S):
    n_cmp = (S - CMP_LEN) // CMP_STRIDE + 1
    c_start = np.arange(n_cmp) * CMP_STRIDE
    c_end = c_start + CMP_LEN - 1
    j_start = np.arange(S // SLC_LEN) * SLC_LEN
    overlap = (c_end[:, None] >= j_start[None]) & (c_start[:, None] <= j_start[None] + SLC_LEN - 1)
    ov = np.zeros((NSA_GROUPS, N_CMP_PAD, LANES), np.float32)
    es = np.zeros((NSA_GROUPS, LANES, S), np.float32)
    for g in range(NSA_GROUPS):
        ov[g, :n_cmp, g * N_SLC:(g + 1) * N_SLC] = overlap
        es[g, g * N_SLC + np.arange(S) // SLC_LEN, np.arange(S)] = 1.0
    return jnp.asarray(ov, BF16), jnp.asarray(es, BF16)


NSA_PERM = np.array([(half * (NSA_HEADS // NSA_GROUPS) + j) * NSA_DH + d
                     for j in range(NSA_HEADS // NSA_GROUPS) for half in range(2) for d in range(NSA_DH)])


def nsa_mixer(hb, hf, pos_k, w1_k, w2_k, pos_v, w1_v, w2_v):
    B, S, _ = hb.shape
    rows = S // CMP_STRIDE

    def stride_view(blk):
        t = hf[:, :, blk * LANES:(blk + 1) * LANES]
        t = t.reshape(B, rows, CMP_STRIDE, NSA_GROUPS, NSA_DH).transpose(0, 3, 1, 2, 4)
        return t.reshape(B, NSA_GROUPS, rows, CMP_STRIDE * NSA_DH)

    def pad_w2(w2):
        out = jnp.zeros((NSA_GROUPS, CMP_HIDDEN, LANES), F32)
        for g in range(NSA_GROUPS):
            out = out.at[g, :, g * NSA_DH:(g + 1) * NSA_DH].set(w2)
        return out.astype(BF16)

    kc, vc = nsa_compress(stride_view(NSA_KC_BLK), stride_view(NSA_VC_BLK),
                          pos_k.reshape(2, CMP_STRIDE * NSA_DH), pos_v.reshape(2, CMP_STRIDE * NSA_DH),
                          w1_k.astype(BF16), w1_v.astype(BF16), pad_w2(w2_k), pad_w2(w2_v))
    ov, es = _nsa_tables(S)
    ocmp, sel = nsa_cmp(hb, kc, vc, ov, 256)
    return nsa_attn(hb, hf, sel, ocmp, es, 128, 256)


def _diff_body(slope_ref, q1_ref, q2_ref, k1_ref, k2_ref, v_ref, lam_ref, gn_ref, o_ref, *scratch,
               tq, tk, lambda_init):
    h = pl.program_id(1)
    qs = pl.program_id(2) * tq
    lane = _lane((1, LANES))
    in_half = (lane // HALF) == (h % 2)
    zero = jnp.zeros((tq, LANES), BF16)
    q1 = jnp.where(in_half, q1_ref[0], zero)
    q2 = jnp.where(in_half, q2_ref[0], zero)
    slope = slope_ref[h]

    def scores(k0, dst):
        dst[0:tq, :] = _dot_nt(q1, k1_ref[0, pl.ds(k0, tk), :])
        dst[tq:2 * tq, :] = _dot_nt(q2, k2_ref[0, pl.ds(k0, tk), :])

    def adjust(c, s, k0, last):
        kpos = k0 + _lane((1, tk))
        s = s + slope * kpos.astype(F32)
        if last:
            t = qs + (c * ROW_CHUNK) % tq + lax.broadcasted_iota(I32, (ROW_CHUNK, 1), 0)
            s = jnp.where(t >= kpos, s, NEG)
        return s

    o = _flash_causal(qs // tk, tk, scores, lambda k0: v_ref[0, pl.ds(k0, tk), :], adjust, scratch)
    lam_rows = lam_ref[...]
    lam = (jnp.exp(jnp.sum(lam_rows[0:1] * lam_rows[1:2], axis=-1, keepdims=True))
           - jnp.exp(jnp.sum(lam_rows[2:3] * lam_rows[3:4], axis=-1, keepdims=True)) + lambda_init)
    y = _rms(o[0:tq] - lam * o[tq:2 * tq], gn_ref[...]) * (1.0 - lambda_init)
    o_ref[0] = y.astype(o_ref.dtype)


def diff_mixer(hd, lam_rows, g_norm, lambda_init, tq, tk):
    B, S, _ = hd.shape
    nb = DIFF_HEADS // 2
    slopes = jnp.asarray(2.0 ** (-8.0 * np.arange(1, DIFF_HEADS + 1) / DIFF_HEADS), F32)
    grid_spec = pltpu.PrefetchScalarGridSpec(
        num_scalar_prefetch=1,
        grid=(B, DIFF_HEADS, S // tq),
        in_specs=[pl.BlockSpec((1, tq, LANES), lambda b, h, i, s: (b, i, h // 2)),
                  pl.BlockSpec((1, tq, LANES), lambda b, h, i, s: (b, i, nb + h // 2)),
                  pl.BlockSpec((1, S, LANES), lambda b, h, i, s: (b, 0, 2 * nb + h // 2)),
                  pl.BlockSpec((1, S, LANES), lambda b, h, i, s: (b, 0, 3 * nb + h // 2)),
                  pl.BlockSpec((1, S, LANES), lambda b, h, i, s: (b, 0, 4 * nb + h)),
                  pl.BlockSpec((4, LANES), lambda b, h, i, s: (0, 0)),
                  pl.BlockSpec((1, LANES), lambda b, h, i, s: (0, 0))],
        out_specs=pl.BlockSpec((1, tq, LANES), lambda b, h, i, s: (b, i, h)),
        scratch_shapes=_flash_scratch(2 * tq, tk))
    return pl.pallas_call(
        functools.partial(_diff_body, tq=tq, tk=tk, lambda_init=lambda_init),
        grid_spec=grid_spec,
        out_shape=jax.ShapeDtypeStruct((B, S, DIFF_HEADS * DIFF_DV), BF16),
        compiler_params=_cp("parallel", "parallel", "parallel"), name="diff_attn")(
            slopes, hd, hd, hd, hd, hd, lam_rows, g_norm.reshape(1, LANES))


def _router_body(x_ref, g_ref, wr_ref, tri_ref, xn_ref, meta_ref, wts_ref, cnt_ref, run_ref):
    @pl.when(pl.program_id(0) == 0)
    def _():
        run_ref[...] = jnp.zeros_like(run_ref)

    xn = _rms(x_ref[...], g_ref[...])
    xn_ref[...] = xn
    lane = _lane((1, LANES))
    logits = jnp.dot(xn, wr_ref[...], precision=lax.Precision.HIGHEST, preferred_element_type=F32)
    logits = jnp.where(lane < N_EXPERTS, logits, LOWEST)
    m1 = jnp.max(logits, axis=-1, keepdims=True)
    i1 = jnp.min(jnp.where(logits == m1, lane, LANES), axis=-1, keepdims=True)
    rest = jnp.where(lane == i1, LOWEST, logits)
    m2 = jnp.max(rest, axis=-1, keepdims=True)
    i2 = jnp.min(jnp.where(rest == m2, lane, LANES), axis=-1, keepdims=True)
    e = jnp.exp(m2 - m1)
    w1 = 1.0 / (1.0 + e)
    w2 = e / (1.0 + e)
    onehot = ((lane == i1) | (lane == i2)).astype(F32)
    pos = _dot(tri_ref[...], onehot.astype(BF16)) + run_ref[...]
    p1 = jnp.sum(jnp.where(lane == i1, pos, 0.0), axis=-1, keepdims=True).astype(I32)
    p2 = jnp.sum(jnp.where(lane == i2, pos, 0.0), axis=-1, keepdims=True).astype(I32)
    run_ref[...] += jnp.sum(onehot, axis=0, keepdims=True)
    meta_ref[...] = jnp.where(lane == 0, i1, jnp.where(lane == 1, i2, jnp.where(lane == 2, p1, jnp.where(lane == 3, p2, 0))))
    wts_ref[...] = jnp.where(lane == 0, w1, jnp.where(lane == 1, w2, 0.0))
    cnt_ref[...] = run_ref[...]


def moe_router(x, g, wr, tm):
    T, D = x.shape
    tri = jnp.asarray(np.tril(np.ones((tm, tm), np.float32), -1), BF16)
    return pl.pallas_call(
        _router_body,
        grid=(T // tm,),
        in_specs=[pl.BlockSpec((tm, D), lambda i: (i, 0)),
                  pl.BlockSpec((1, D), lambda i: (0, 0)),
                  pl.BlockSpec((D, LANES), lambda i: (0, 0)),
                  pl.BlockSpec((tm, tm), lambda i: (0, 0))],
        out_specs=[pl.BlockSpec((tm, D), lambda i: (i, 0)),
                   pl.BlockSpec((tm, LANES), lambda i: (i, 0)),
                   pl.BlockSpec((tm, LANES), lambda i: (i, 0)),
                   pl.BlockSpec((1, LANES), lambda i: (0, 0))],
        out_shape=[jax.ShapeDtypeStruct((T, D), F32), jax.ShapeDtypeStruct((T, LANES), I32),
                   jax.ShapeDtypeStruct((T, LANES), F32), jax.ShapeDtypeStruct((1, LANES), F32)],
        scratch_shapes=[pltpu.VMEM((1, LANES), F32)],
        compiler_params=_cp("arbitrary"), name="moe_router")(x, g.reshape(1, D), wr, tri)


ROW_DMA_UNROLL = 8


def _row_copy(src_ref, src_row, dst_ref, dst_row, sem):
    return pltpu.make_async_copy(src_ref.at[pl.ds(src_row, 1)], dst_ref.at[pl.ds(dst_row, 1)], sem)


def _dispatch_body(dest_ref, xn_ref, zero_ref, xs_ref, sem, *, tm):
    del zero_ref
    base = pl.program_id(0) * tm

    def issue(t, c):
        for slot in range(TOP_K):
            _row_copy(xn_ref, t, xs_ref, dest_ref[TOP_K * (base + t) + slot], sem).start()
        return c

    lax.fori_loop(0, tm, issue, 0, unroll=ROW_DMA_UNROLL)

    def drain(t, c):
        for slot in range(TOP_K):
            _row_copy(xn_ref, 0, xs_ref, 0, sem).wait()
        return c

    lax.fori_loop(0, tm, drain, 0, unroll=ROW_DMA_UNROLL)


def moe_dispatch(dest, xn, rows, tm):
    T, D = xn.shape
    grid_spec = pltpu.PrefetchScalarGridSpec(
        num_scalar_prefetch=1,
        grid=(T // tm,),
        in_specs=[pl.BlockSpec((tm, D), lambda i, d: (i, 0)),
                  pl.BlockSpec(memory_space=pl.ANY)],
        out_specs=pl.BlockSpec(memory_space=pl.ANY),
        scratch_shapes=[pltpu.SemaphoreType.DMA])
    return pl.pallas_call(
        functools.partial(_dispatch_body, tm=tm),
        grid_spec=grid_spec,
        out_shape=jax.ShapeDtypeStruct((rows, D), F32),
        input_output_aliases={2: 0},
        compiler_params=_cp("arbitrary"), name="moe_dispatch")(dest, xn, jnp.zeros((rows, D), F32))


def _experts_body(te_ref, tv_ref, xs_ref, wg_ref, wu_ref, wd_ref, o_ref, xb_ref, acc_ref):
    i = pl.program_id(0)
    k = pl.program_id(1)

    @pl.when(k == 0)
    def _():
        xb_ref[...] = xs_ref[...].astype(BF16)
        acc_ref[...] = jnp.zeros_like(acc_ref)

    @pl.when(tv_ref[i] > 0)
    def _():
        xb = xb_ref[...]
        hg = _dot(xb, wg_ref[0])
        hu = _dot(xb, wu_ref[0])
        acc_ref[...] += _dot((hg * jax.nn.sigmoid(hg) * hu).astype(BF16), wd_ref[0])

    @pl.when(k == pl.num_programs(1) - 1)
    def _():
        o_ref[...] = acc_ref[...]


def moe_experts(tile_expert, tile_valid, xs, wg, wu, wd, tr, tf):
    R, D = xs.shape
    F = wg.shape[2]
    grid_spec = pltpu.PrefetchScalarGridSpec(
        num_scalar_prefetch=2,
        grid=(R // tr, F // tf),
        in_specs=[pl.BlockSpec((tr, D), lambda i, k, te, tv: (i, 0)),
                  pl.BlockSpec((1, D, tf), lambda i, k, te, tv: (te[i], 0, k)),
                  pl.BlockSpec((1, D, tf), lambda i, k, te, tv: (te[i], 0, k)),
                  pl.BlockSpec((1, tf, D), lambda i, k, te, tv: (te[i], k, 0))],
        out_specs=pl.BlockSpec((tr, D), lambda i, k, te, tv: (i, 0)),
        scratch_shapes=[pltpu.VMEM((tr, D), BF16), pltpu.VMEM((tr, D), F32)])
    return pl.pallas_call(
        _experts_body,
        grid_spec=grid_spec,
        out_shape=jax.ShapeDtypeStruct((R, D), F32),
        compiler_params=_cp("parallel", "arbitrary"), name="moe_experts")(tile_expert, tile_valid, xs, wg, wu, wd)


def _combine_body(dest_ref, x_ref, wts_ref, g_ref, ys_ref, o_ref, buf_ref, sem, *, tm):
    base = pl.program_id(0) * tm

    def issue(t, c):
        for slot in range(TOP_K):
            _row_copy(ys_ref, dest_ref[TOP_K * (base + t) + slot], buf_ref.at[slot], t, sem).start()
        return c

    lax.fori_loop(0, tm, issue, 0, unroll=ROW_DMA_UNROLL)

    def drain(t, c):
        for slot in range(TOP_K):
            _row_copy(ys_ref, 0, buf_ref.at[slot], 0, sem).wait()
        return c

    lax.fori_loop(0, tm, drain, 0, unroll=ROW_DMA_UNROLL)
    wts = wts_ref[...]
    y = x_ref[...] + (wts[:, 0:1] * buf_ref[0] + wts[:, 1:2] * buf_ref[1])
    o_ref[...] = _rms(y, g_ref[...])


def moe_combine(dest, x, wts, g_final, ys, tm):
    T, D = x.shape
    grid_spec = pltpu.PrefetchScalarGridSpec(
        num_scalar_prefetch=1,
        grid=(T // tm,),
        in_specs=[pl.BlockSpec((tm, D), lambda i, d: (i, 0)),
                  pl.BlockSpec((tm, LANES), lambda i, d: (i, 0)),
                  pl.BlockSpec((1, D), lambda i, d: (0, 0)),
                  pl.BlockSpec(memory_space=pl.ANY)],
        out_specs=pl.BlockSpec((tm, D), lambda i, d: (i, 0)),
        scratch_shapes=[pltpu.VMEM((TOP_K, tm, D), F32), pltpu.SemaphoreType.DMA])
    return pl.pallas_call(
        functools.partial(_combine_body, tm=tm),
        grid_spec=grid_spec,
        out_shape=jax.ShapeDtypeStruct((T, D), F32),
        compiler_params=_cp("arbitrary"), name="moe_combine")(dest, x, wts, g_final.reshape(1, D), ys)


def moe_block(x, g, w_router, wg, wu, wd, g_final, tr):
    T, D = x.shape
    wr = _pad_cols(w_router, LANES)
    xn, meta, wts, counts = moe_router(x, g, wr, 512)
    cnt = counts[0, :N_EXPERTS].astype(I32)
    gsz = ((cnt + tr - 1) // tr) * tr
    gend = jnp.cumsum(gsz)
    goff = gend - gsz
    dest = (jnp.take(goff, meta[:, 0:TOP_K]) + meta[:, TOP_K:2 * TOP_K]).reshape(-1)
    n_tiles = (TOP_K * T) // tr + N_EXPERTS
    tile_start = jnp.arange(n_tiles, dtype=I32) * tr
    tile_valid = (tile_start < gend[-1]).astype(I32)
    tile_expert = jnp.minimum(jnp.sum((tile_start[:, None] >= gend[None, :]).astype(I32), axis=1), N_EXPERTS - 1)
    last_valid = jnp.take(tile_expert, jnp.maximum(gend[-1] // tr - 1, 0))
    tile_expert = jnp.where(tile_valid > 0, tile_expert, last_valid)
    xs = moe_dispatch(dest, xn, n_tiles * tr, 256)
    ys = moe_experts(tile_expert, tile_valid, xs, wg, wu, wd, tr, FFN_TF)
    return moe_combine(dest, x, wts, g_final, ys, 256)


def _pad_cols(w, width):
    return jnp.pad(w, ((0, 0), (0, width - w.shape[1])))


def _mem_kv(mem2, g, wk, wv, batch):
    w = jnp.concatenate([wk, wv], axis=1).astype(BF16)
    width = wk.shape[1]
    k, v = norm_proj(mem2, g, w, [(width, BF16), (width, BF16)], 256, "mem_kv")
    return k.reshape(batch, -1, width), v.reshape(batch, -1, width)


def kernel(x, mem, mix_norm_0, w_in_0, gla_w_alpha_0, gla_b_alpha_0, gla_out_norm_0, nsa_cmp_pos_k_0, nsa_cmp_w1_k_0, nsa_cmp_w2_k_0, nsa_cmp_pos_v_0, nsa_cmp_w1_v_0, nsa_cmp_w2_v_0, w_out_0, xattn_norm_0, xattn_mem_norm_0, xattn_wq_0, xattn_wk_0, xattn_wv_0, xattn_wo_0, ffn_norm_0, ffn_w_gate_0, ffn_w_up_0, ffn_w_down_0, mix_norm_1, w_in_1, diff_lq1_1, diff_lk1_1, diff_lq2_1, diff_lk2_1, diff_out_norm_1, w_out_1, xattn_norm_1, xattn_mem_norm_1, xattn_wq_1, xattn_wk_1, xattn_wv_1, xattn_wo_1, ffn_norm_1, moe_router_1, moe_w_gate_1, moe_w_up_1, moe_w_down_1, final_norm):
    B, S, D = x.shape
    T = B * S
    x2 = x.reshape(T, D)
    mem2 = mem.reshape(-1, D)
    q_scale = NSA_DH ** -0.5

    sizes = [GLA_HEADS * GLA_DK, GLA_HEADS * GLA_DK, GLA_HEADS * GLA_DV, GLA_HEADS * GLA_DV, GLA_RANK,
             NSA_HEADS * NSA_DH] + [NSA_GROUPS * NSA_DH] * 6 + [3 * NSA_HEADS]
    (g_q, g_k, g_v, g_r, g_a, n_q, n_kc, n_vc, n_ks, n_vs, n_kw, n_vw, n_g) = jnp.split(
        w_in_0, np.cumsum(sizes)[:-1].tolist(), axis=1)
    w0 = jnp.concatenate([n_q[:, NSA_PERM] * q_scale, n_ks, n_vs, n_kw, n_vw, g_q * q_scale, g_k, g_v,
                          _pad_cols(g_a, LANES), g_r, _pad_cols(n_g, LANES), n_kc, n_vc], axis=1).astype(BF16)
    hb, hf = norm_proj(x2, mix_norm_0, w0, [(HB0_WIDTH, BF16), (HF0_WIDTH, F32)], 512, "in_proj0")
    hb = hb.reshape(B, S, HB0_WIDTH)
    hf = hf.reshape(B, S, HF0_WIDTH)
    w_alpha = jnp.pad(gla_w_alpha_0, ((0, LANES - GLA_RANK), (0, 0))).astype(BF16)
    o_a = gla_mixer(hb, hf, w_alpha, gla_b_alpha_0.reshape(1, -1), gla_out_norm_0, GLA_COLS, 8)
    o_b = nsa_mixer(hb, hf, nsa_cmp_pos_k_0, nsa_cmp_w1_k_0, nsa_cmp_w2_k_0,
                    nsa_cmp_pos_v_0, nsa_cmp_w1_v_0, nsa_cmp_w2_v_0)
    n_a = GLA_HEADS * GLA_DV
    x2 = proj_residual(x2, [o_a.reshape(T, -1), o_b.reshape(T, -1)],
                       [w_out_0[:n_a].astype(BF16), w_out_0[n_a:][NSA_PERM].astype(BF16)], 512, "out_proj0")
    k0, v0 = _mem_kv(mem2, xattn_mem_norm_0, xattn_wk_0, xattn_wv_0, B)
    x2 = xattn_block(x2, xattn_norm_0, (xattn_wq_0 * q_scale).astype(BF16), k0, v0, xattn_wo_0.astype(BF16),
                     S, 512, "xattn0")
    x2 = ffn_block(x2, ffn_norm_0, ffn_w_gate_0.astype(BF16), ffn_w_up_0.astype(BF16), ffn_w_down_0.astype(BF16),
                   512, FFN_TF, "ffn0")

    lambda_init = 0.8 - 0.6 * math.exp(-0.3 * 1)
    n_q1 = 2 * DIFF_HEADS * DIFF_DH
    w1 = jnp.concatenate([w_in_1[:, :n_q1] * q_scale, w_in_1[:, n_q1:]], axis=1).astype(BF16)
    (hd,) = norm_proj(x2, mix_norm_1, w1, [(w1.shape[1], BF16)], 512, "in_proj1")
    lam_rows = _pad_cols(jnp.stack([diff_lq1_1, diff_lk1_1, diff_lq2_1, diff_lk2_1]), LANES)
    o_d = diff_mixer(hd.reshape(B, S, -1), lam_rows, diff_out_norm_1, lambda_init, 256, 256)
    x2 = proj_residual(x2, [o_d.reshape(T, -1)], [w_out_1.astype(BF16)], 512, "out_proj1")
    k1, v1 = _mem_kv(mem2, xattn_mem_norm_1, xattn_wk_1, xattn_wv_1, B)
    x2 = xattn_block(x2, xattn_norm_1, (xattn_wq_1 * q_scale).astype(BF16), k1, v1, xattn_wo_1.astype(BF16),
                     S, 512, "xattn1")
    out = moe_block(x2, ffn_norm_1, moe_router_1, moe_w_gate_1.astype(BF16), moe_w_up_1.astype(BF16),
                    moe_w_down_1.astype(BF16), final_norm, 512)
    return out.reshape(B, S, D)
```

```python
import functools
import math

import numpy as np
import jax
import jax.numpy as jnp
from jax import lax
from jax.experimental import pallas as pl
from jax.experimental.pallas import tpu as pltpu

F32 = jnp.float32
BF16 = jnp.bfloat16
I32 = jnp.int32

D_MODEL = 1024
GLA_HEADS, GLA_DK, GLA_DV, GLA_RANK, GLA_TAU, GLA_CHUNK = 4, 64, 128, 16, 16.0, 64
NSA_HEADS, NSA_GROUPS, NSA_DH = 8, 2, 64
CMP_LEN, CMP_STRIDE, CMP_HIDDEN, SLC_LEN, N_SEL, WINDOW = 32, 16, 256, 64, 8, 512
DIFF_HEADS, DIFF_DH, DIFF_DV = 8, 64, 128
XATTN_HEADS, XATTN_DH = 4, 64
D_FF, N_EXPERTS, TOP_K = 2816, 8, 2
EPS, NEG, TINY, FORCE_SCORE = 1e-6, -1e30, 1e-30, 1e4

LANES = 128
HALF = LANES // 2
VMEM_LIMIT = 56 * 1024 * 1024
LOWEST = -3.0e38

HB0_WIDTH, HF0_WIDTH = 17 * LANES, 7 * LANES
NSA_Q_BLK, NSA_KS_BLK, NSA_VS_BLK, NSA_KW_BLK, NSA_VW_BLK = 0, 4, 5, 6, 7
GLA_COLS = (8, 10, 12, 16, 0)
NSA_GATE_BLK, NSA_KC_BLK, NSA_VC_BLK = 4, 5, 6


def _cp(*sem, flags=None):
    return pltpu.CompilerParams(dimension_semantics=sem, vmem_limit_bytes=VMEM_LIMIT, flags=flags)


def _rms(x, g):
    y = x * lax.rsqrt(jnp.mean(x * x, axis=-1, keepdims=True) + EPS)
    return y * g


def _dot(a, b):
    return jnp.dot(a, b, preferred_element_type=F32)


def _dot_nt(a, b):
    return lax.dot_general(a, b, (((1,), (1,)), ((), ())), preferred_element_type=F32)


def _lane(shape):
    return lax.broadcasted_iota(I32, shape, len(shape) - 1)


def _norm_proj_body(x_ref, g_ref, w_ref, *o_refs, widths):
    xn = _rms(x_ref[...], g_ref[...]).astype(BF16)
    off = 0
    for o_ref, wd in zip(o_refs, widths):
        for c0 in range(0, wd, 512):
            cw = min(512, wd - c0)
            o_ref[:, c0:c0 + cw] = _dot(xn, w_ref[:, off + c0:off + c0 + cw]).astype(o_ref.dtype)
        off += wd


def norm_proj(x, g, w, outs, tm, name):
    T, D = x.shape
    widths = tuple(o[0] for o in outs)
    return pl.pallas_call(
        functools.partial(_norm_proj_body, widths=widths),
        grid=(T // tm,),
        in_specs=[pl.BlockSpec((tm, D), lambda i: (i, 0)),
                  pl.BlockSpec((1, D), lambda i: (0, 0)),
                  pl.BlockSpec(w.shape, lambda i: (0, 0))],
        out_specs=[pl.BlockSpec((tm, wd), lambda i: (i, 0)) for wd, _ in outs],
        out_shape=[jax.ShapeDtypeStruct((T, wd), dt) for wd, dt in outs],
        compiler_params=_cp("parallel"), name=name)(x, g.reshape(1, D), w)


def _proj_xattn_body(*refs, n):
    res_ref = refs[0]
    a_refs = refs[1:1 + n]
    w_refs = refs[1 + n:1 + 2 * n]
    g_ref, wq_ref, k_ref, v_ref, wo_ref, o_ref = refs[1 + 2 * n:]
    x = res_ref[...]
    for a_ref, w_ref in zip(a_refs, w_refs):
        x = x + _dot(a_ref[...].astype(BF16), w_ref[...])
    q = _dot(_rms(x, g_ref[...]).astype(BF16), wq_ref[...]).astype(BF16)
    lane = _lane((1, LANES))
    blocks = []
    for blk in range(XATTN_HEADS // 2):
        qb = q[:, blk * LANES:(blk + 1) * LANES]
        kb = k_ref[0, :, blk * LANES:(blk + 1) * LANES]
        vb = v_ref[0, :, blk * LANES:(blk + 1) * LANES]
        pair = None
        for half in range(2):
            in_half = (lane // HALF) == half
            s = _dot_nt(jnp.where(in_half, qb, jnp.zeros_like(qb)), kb)
            p = jnp.exp(s - jnp.max(s, axis=-1, keepdims=True))
            p = p / jnp.sum(p, axis=-1, keepdims=True)
            o = _dot(p.astype(BF16), vb)
            pair = o if half == 0 else jnp.where(in_half, o, pair)
        blocks.append(pair)
    o = jnp.concatenate(blocks, axis=1).astype(BF16)
    o_ref[...] = x + _dot(o, wo_ref[...])


def proj_xattn_block(res, a_list, w_list, g, wq, k, v, wo, seq, tm, name):
    T, D = res.shape
    n = len(a_list)
    M, W = k.shape[1], k.shape[2]
    per = seq // tm
    return pl.pallas_call(
        functools.partial(_proj_xattn_body, n=n),
        grid=(T // tm,),
        in_specs=([pl.BlockSpec((tm, D), lambda i: (i, 0))]
                  + [pl.BlockSpec((tm, a.shape[1]), lambda i: (i, 0)) for a in a_list]
                  + [pl.BlockSpec(w.shape, lambda i: (0, 0)) for w in w_list]
                  + [pl.BlockSpec((1, D), lambda i: (0, 0)),
                     pl.BlockSpec(wq.shape, lambda i: (0, 0)),
                     pl.BlockSpec((1, M, W), lambda i: (i // per, 0, 0)),
                     pl.BlockSpec((1, M, W), lambda i: (i // per, 0, 0)),
                     pl.BlockSpec(wo.shape, lambda i: (0, 0))]),
        out_specs=pl.BlockSpec((tm, D), lambda i: (i, 0)),
        out_shape=jax.ShapeDtypeStruct((T, D), F32),
        compiler_params=_cp("parallel"), name=name)(res, *a_list, *w_list, g.reshape(1, D), wq, k, v, wo)


def _ffn_body(x_ref, g_ref, wg_ref, wu_ref, wd_ref, o_ref, xn_ref, acc_ref):
    k = pl.program_id(1)

    @pl.when(k == 0)
    def _():
        xn_ref[...] = _rms(x_ref[...], g_ref[...]).astype(BF16)
        acc_ref[...] = x_ref[...]

    xn = xn_ref[...]
    hg = _dot(xn, wg_ref[...])
    hu = _dot(xn, wu_ref[...])
    h = (hg * jax.nn.sigmoid(hg) * hu).astype(BF16)
    acc_ref[...] += _dot(h, wd_ref[...])

    @pl.when(k == pl.num_programs(1) - 1)
    def _():
        o_ref[...] = acc_ref[...]


def ffn_block(x, g, wg, wu, wd, tm, tf, name):
    T, D = x.shape
    F = wg.shape[1]
    return pl.pallas_call(
        _ffn_body,
        grid=(T // tm, F // tf),
        in_specs=[pl.BlockSpec((tm, D), lambda i, k: (i, 0)),
                  pl.BlockSpec((1, D), lambda i, k: (0, 0)),
                  pl.BlockSpec((D, tf), lambda i, k: (0, k)),
                  pl.BlockSpec((D, tf), lambda i, k: (0, k)),
                  pl.BlockSpec((tf, D), lambda i, k: (k, 0))],
        out_specs=pl.BlockSpec((tm, D), lambda i, k: (i, 0)),
        out_shape=jax.ShapeDtypeStruct((T, D), F32),
        scratch_shapes=[pltpu.VMEM((tm, D), BF16), pltpu.VMEM((tm, D), F32)],
        compiler_params=_cp("parallel", "arbitrary"), name=name)(x, g.reshape(1, D), wg, wu, wd)


_GLA_LEVELS = (32, 16, 8, 4, 2, 1)


def _gla_constants():
    C = GLA_CHUNK
    t = np.arange(C)
    r = t[None, :]
    mats = [r <= t[:, None], r > t[:, None]]
    for hs in _GLA_LEVELS:
        c = (t // (2 * hs)) * (2 * hs) + hs - 1
        right = (t % (2 * hs)) >= hs
        m_right = (r > c[:, None]) & (r <= t[:, None])
        m_left = (r > t[:, None]) & (r <= c[:, None])
        mats.append(np.where(right[:, None], m_right, m_left))
    cmat = np.concatenate(mats, 0).astype(np.float32)
    x = t[:, None] ^ t[None, :]
    lvl = np.full((C, C), -1, np.int32)
    for i, hs in enumerate(_GLA_LEVELS):
        lvl[(t[:, None] > t[None, :]) & (x >= hs) & (x < 2 * hs)] = i
    lvl[t[:, None] == t[None, :]] = len(_GLA_LEVELS)
    return cmat, lvl


def _gla_body(q_ref, k_ref, v_ref, a_ref, r_ref, wa_ref, ba_ref, gn_ref, cmat_ref, lvl_ref, o_ref, st_ref, *, nb):
    C = GLA_CHUNK

    @pl.when(pl.program_id(2) == 0)
    def _():
        st_ref[...] = jnp.zeros_like(st_ref)

    nlev = len(_GLA_LEVELS)
    in_a = _lane((1, LANES)) < HALF
    lvl = lvl_ref[...]
    cmat = cmat_ref[...]
    on_diag = (lax.broadcasted_iota(I32, (LANES, 2 * LANES), 0) // HALF
               == lax.broadcasted_iota(I32, (LANES, 2 * LANES), 1) // LANES)
    for bi in range(nb):
        q = q_ref[bi].astype(F32)
        k = k_ref[bi].astype(F32)
        z = _dot(a_ref[bi], wa_ref[...]) + ba_ref[...]
        la = (jnp.minimum(z, 0.0) - jnp.log1p(jnp.exp(-jnp.abs(z)))) / GLA_TAU
        hi = la.astype(BF16)
        rest = la - hi.astype(F32)
        mid = rest.astype(BF16)
        lo = (rest - mid.astype(F32)).astype(BF16)
        u3 = _dot(cmat, jnp.concatenate([hi, mid, lo], axis=1))
        e = jnp.exp(u3[:, 0:LANES] + u3[:, LANES:2 * LANES] + u3[:, 2 * LANES:3 * LANES])
        qhat = (q * e[0:C]).astype(BF16)
        kend = (k * e[C:2 * C]).astype(BF16)
        decay = jnp.broadcast_to(e[C - 1:C], (LANES, LANES)).T
        zero = jnp.zeros_like(q)

        def stacked(x):
            return jnp.concatenate([jnp.where(in_a, x, zero), jnp.where(in_a, zero, x)], axis=0).astype(BF16)

        sc = jnp.zeros((2 * C, C), F32)
        for i in range(nlev):
            ei = e[(2 + i) * C:(3 + i) * C]
            sc = jnp.where(lvl == i, _dot_nt(stacked(q * ei), (k * ei).astype(BF16)), sc)
        sc = jnp.where(lvl == nlev, _dot_nt(stacked(q), k.astype(BF16)), sc)
        v = v_ref[bi]
        state = st_ref[bi]
        o_inter = _dot(qhat, state.astype(BF16))
        o_intra = _dot(sc.astype(BF16), v)
        update = lax.dot_general(kend, v, (((0,), (0,)), ((), ())), preferred_element_type=F32)
        st_ref[bi] = jnp.where(on_diag, jnp.concatenate([decay, decay], axis=1) * state + update, 0.0)
        for half in range(2):
            cols = slice(half * LANES, (half + 1) * LANES)
            y = _rms(o_inter[:, cols] + o_intra[half * C:(half + 1) * C, cols], gn_ref[...])
            rr = r_ref[bi, :, cols]
            o_ref[bi, :, cols] = (y * (rr * jax.nn.sigmoid(rr))).astype(o_ref.dtype)


def gla_mixer(hb, hf, w_alpha, b_alpha, g_norm, cols, nb):
    B, S, _ = hb.shape
    C = GLA_CHUNK
    cmat, lvl = _gla_constants()
    lvl = np.concatenate([lvl, lvl], axis=0)
    qb, kb_, vb, ab, rb = cols
    return pl.pallas_call(
        functools.partial(_gla_body, nb=nb),
        grid=(B // nb, GLA_HEADS // 2, S // C),
        in_specs=[pl.BlockSpec((nb, C, LANES), lambda b, p, c: (b, c, qb + p)),
                  pl.BlockSpec((nb, C, LANES), lambda b, p, c: (b, c, kb_ + p)),
                  pl.BlockSpec((nb, C, 2 * LANES), lambda b, p, c: (b, c, vb // 2 + p)),
                  pl.BlockSpec((nb, C, LANES), lambda b, p, c: (b, c, ab)),
                  pl.BlockSpec((nb, C, 2 * LANES), lambda b, p, c: (b, c, rb // 2 + p)),
                  pl.BlockSpec((LANES, LANES), lambda b, p, c: (0, p)),
                  pl.BlockSpec((1, LANES), lambda b, p, c: (0, p)),
                  pl.BlockSpec((1, LANES), lambda b, p, c: (0, 0)),
                  pl.BlockSpec(cmat.shape, lambda b, p, c: (0, 0)),
                  pl.BlockSpec(lvl.shape, lambda b, p, c: (0, 0))],
        out_specs=pl.BlockSpec((nb, C, 2 * LANES), lambda b, p, c: (b, c, p)),
        out_shape=jax.ShapeDtypeStruct((B, S, GLA_HEADS * GLA_DV), BF16),
        scratch_shapes=[pltpu.VMEM((nb, LANES, 2 * LANES), F32)],
        compiler_params=_cp("parallel", "parallel", "arbitrary"), name="gla")(
            hb, hb, hb, hb, hf, w_alpha, b_alpha, g_norm.reshape(1, LANES), jnp.asarray(cmat, BF16), jnp.asarray(lvl))


N_CMP_PAD = 128
N_SLC = 32


def _gelu_tanh(x):
    return 0.5 * x * (1.0 + jnp.tanh(math.sqrt(2.0 / math.pi) * (x + 0.044715 * (x * x * x))))


def _nsa_compress_body(tk_ref, tv_ref, pk_ref, pv_ref, w1k_ref, w1v_ref, w2k_ref, w2v_ref, kc_ref, vc_ref):
    for t_ref, p_ref, w1_ref, w2_ref, o_ref in ((tk_ref, pk_ref, w1k_ref, w2k_ref, kc_ref),
                                                (tv_ref, pv_ref, w1v_ref, w2v_ref, vc_ref)):
        lo = jnp.zeros((N_CMP_PAD, NSA_GROUPS * CMP_HIDDEN), F32)
        hi = jnp.zeros((N_CMP_PAD, NSA_GROUPS * CMP_HIDDEN), F32)
        for l in range(CMP_STRIDE):
            rows = t_ref[0, pl.ds(l, N_CMP_PAD, stride=CMP_STRIDE), :]
            lo = lo + _dot((rows + p_ref[l:l + 1]).astype(BF16), w1_ref[l])
            hi = hi + _dot((rows + p_ref[CMP_STRIDE + l:CMP_STRIDE + l + 1]).astype(BF16), w1_ref[CMP_STRIDE + l])
        pre = lo + pltpu.roll(hi, N_CMP_PAD - 1, 0)
        o_ref[0] = _dot(_gelu_tanh(pre).astype(BF16), w2_ref[...]).astype(o_ref.dtype)


def nsa_compress(hf, pk, pv, w1k, w1v, w2k, w2v):
    B, S, _ = hf.shape
    full = lambda a: pl.BlockSpec(a.shape, lambda b: (0,) * a.ndim)
    out = pl.BlockSpec((1, N_CMP_PAD, LANES), lambda b: (b, 0, 0))
    return pl.pallas_call(
        _nsa_compress_body,
        grid=(B,),
        in_specs=[pl.BlockSpec((1, S, LANES), lambda b: (b, 0, NSA_KC_BLK)),
                  pl.BlockSpec((1, S, LANES), lambda b: (b, 0, NSA_VC_BLK)),
                  full(pk), full(pv), full(w1k), full(w1v), full(w2k), full(w2v)],
        out_specs=[out, out],
        out_shape=[jax.ShapeDtypeStruct((B, N_CMP_PAD, LANES), BF16)] * 2,
        compiler_params=_cp("parallel"), name="nsa_compress")(hf, hf, pk, pv, w1k, w1v, w2k, w2v)


def _nsa_slope(g, j):
    return 2.0 ** (-(g * (NSA_HEADS // NSA_GROUPS) + j + 1))


def _nsa_cmp_body(q_ref, kc_ref, vc_ref, ov_ref, o_ref, selt_ref, sct_ref, *, tq):
    hg = NSA_HEADS // NSA_GROUPS
    qs = pl.program_id(1) * tq
    lane = _lane((1, LANES))
    tpos = qs + lax.broadcasted_iota(I32, (tq, LANES), 0)
    dist = tpos - (lane * CMP_STRIDE + (CMP_LEN - 1))
    valid = (dist >= 0) & (lane < N_CMP_PAD - 1)
    validf = valid.astype(F32)
    distf = dist.astype(F32)
    kc = kc_ref[0]
    vc = vc_ref[0]
    imp = jnp.zeros((tq, LANES), F32)
    blocks = [None] * hg
    for g in range(NSA_GROUPS):
        in_g = (lane // HALF) == g
        for j in range(hg):
            qb = q_ref[0, :, j * LANES:(j + 1) * LANES]
            s = _dot_nt(jnp.where(in_g, qb, jnp.zeros_like(qb)), kc) - _nsa_slope(g, j) * distf
            s = jnp.where(valid, s, NEG)
            p = jnp.exp(s - jnp.max(s, axis=-1, keepdims=True)) * validf
            p = (p / jnp.maximum(jnp.sum(p, axis=-1, keepdims=True), TINY)).astype(BF16)
            o = _dot(p, vc)
            blocks[j] = o if g == 0 else jnp.where(in_g, o, blocks[j])
            imp = imp + _dot(p, ov_ref[g])
    for j in range(hg):
        o_ref[0, :, j * LANES:(j + 1) * LANES] = blocks[j]
    blk = lane % N_SLC
    cur = tpos // SLC_LEN
    score = jnp.where((blk == cur) | (blk == 0), FORCE_SCORE, jnp.where(blk <= cur, imp, NEG))
    sct_ref[...] = score.T
    row8 = lax.broadcasted_iota(I32, (8, tq), 0)
    for g in range(NSA_GROUPS):
        base = g * N_SLC
        tiles = [sct_ref[base + 8 * v:base + 8 * (v + 1), :] for v in range(N_SLC // 8)]
        beaten = [jnp.zeros((8, tq), F32) for _ in tiles]
        for i in range(N_SLC):
            cand = jnp.broadcast_to(sct_ref[base + i:base + i + 1, :], (8, tq))
            for v, tile in enumerate(tiles):
                ge = jnp.where(cand >= tile, 1.0, 0.0)
                gt = jnp.where(cand > tile, 1.0, 0.0)
                if 8 * v > i:
                    wins = ge
                elif 8 * v + 7 <= i:
                    wins = gt
                else:
                    wins = jnp.where(row8 > i - 8 * v, ge, gt)
                beaten[v] = beaten[v] + wins
        for v in range(N_SLC // 8):
            selt_ref[0, base + 8 * v:base + 8 * (v + 1), :] = jnp.where(beaten[v] < N_SEL, 1.0, 0.0).astype(selt_ref.dtype)
    selt_ref[0, NSA_GROUPS * N_SLC:LANES, :] = jnp.zeros((LANES - NSA_GROUPS * N_SLC, tq), selt_ref.dtype)


def nsa_cmp(hb, kc, vc, ov, tq):
    B, S, _ = hb.shape
    qw = NSA_HEADS * NSA_DH
    return pl.pallas_call(
        functools.partial(_nsa_cmp_body, tq=tq),
        grid=(B, S // tq),
        in_specs=[pl.BlockSpec((1, tq, qw), lambda b, i: (b, i, NSA_Q_BLK)),
                  pl.BlockSpec((1, N_CMP_PAD, LANES), lambda b, i: (b, 0, 0)),
                  pl.BlockSpec((1, N_CMP_PAD, LANES), lambda b, i: (b, 0, 0)),
                  pl.BlockSpec(ov.shape, lambda b, i: (0, 0, 0))],
        out_specs=[pl.BlockSpec((1, tq, qw), lambda b, i: (b, i, 0)),
                   pl.BlockSpec((1, LANES, tq), lambda b, i: (b, 0, i))],
        out_shape=[jax.ShapeDtypeStruct((B, S, qw), F32), jax.ShapeDtypeStruct((B, LANES, S), BF16)],
        scratch_shapes=[pltpu.VMEM((LANES, tq), F32)],
        compiler_params=_cp("parallel", "parallel"), name="nsa_cmp")(hb, kc, vc, ov)


def _softmax_step(s, maskf, v, m, l, acc):
    m_new = jnp.maximum(m, jnp.max(s, axis=-1, keepdims=True))
    p = jnp.exp(s - m_new) * maskf
    alpha = jnp.exp(m - m_new)
    l = alpha * l + jnp.sum(p, axis=-1, keepdims=True)
    acc = alpha * acc + _dot(p.astype(BF16), v)
    return m_new, l, acc


FFN_TF = D_FF // 2
ROW_CHUNK = 32
LOG2E = math.log2(math.e)


FLASH_SLOTS = 2


def _flash_scratch(rows, tk):
    return ([pltpu.VMEM((rows, tk), F32)] * FLASH_SLOTS + [pltpu.VMEM((rows, tk), BF16)] * FLASH_SLOTS
            + [pltpu.VMEM((rows, LANES), F32)] * (FLASH_SLOTS + 3))


def _softmax_pass1(s_ref, m_in, m_out, a_ref, adjust):
    rows, tk = s_ref.shape
    for c in range(rows // ROW_CHUNK):
        r = slice(c * ROW_CHUNK, (c + 1) * ROW_CHUNK)
        s = adjust(c, s_ref[r, :]) * LOG2E
        s_ref[r, :] = s
        top = s[:, 0:LANES]
        for j in range(1, tk // LANES):
            top = jnp.maximum(top, s[:, j * LANES:(j + 1) * LANES])
        m_old = m_in[r, :]
        m_new = jnp.maximum(m_old, jnp.broadcast_to(jnp.max(top, axis=-1, keepdims=True), (ROW_CHUNK, LANES)))
        m_out[r, :] = m_new
        a_ref[r, :] = jnp.exp2(m_old - m_new)


def _softmax_pass2(s_ref, p_ref, m_ref, a_ref, l_ref):
    rows, tk = s_ref.shape
    for c in range(rows // ROW_CHUNK):
        r = slice(c * ROW_CHUNK, (c + 1) * ROW_CHUNK)
        m_new = m_ref[r, :]
        part = None
        for j in range(tk // LANES):
            p = jnp.exp2(s_ref[r, j * LANES:(j + 1) * LANES] - m_new)
            p_ref[r, j * LANES:(j + 1) * LANES] = p.astype(BF16)
            part = p if part is None else part + p
        l_ref[r, :] = a_ref[r, :] * l_ref[r, :] + part


def _flash_causal(n_full, tk, scores, vtile, adjust, scratch):
    n = FLASH_SLOTS
    s_bufs, p_bufs, a_bufs = (scratch[i * n:(i + 1) * n] for i in range(3))
    m_ref, l_ref, acc_ref = scratch[3 * n:]
    m_ref[...] = jnp.full(m_ref.shape, NEG, F32)
    a_bufs[n - 1][...] = jnp.zeros_like(a_bufs[0])
    p_bufs[n - 1][...] = jnp.zeros_like(p_bufs[0])
    l_ref[...] = jnp.zeros_like(l_ref)
    acc_ref[...] = jnp.zeros_like(acc_ref)

    def k_of(t):
        return pl.multiple_of(t * tk, tk)

    def pv(u, t):
        acc_ref[...] = a_bufs[u][...] * acc_ref[...] + _dot(p_bufs[u][...], vtile(k_of(t)))

    def stage(u, t, last):
        pv((u - 1) % n, jnp.maximum(t - 1, 0))
        if not last:
            scores(k_of(t + 1), s_bufs[(u + 1) % n])
        _softmax_pass1(s_bufs[u], m_ref, m_ref, a_bufs[u], lambda c, s: adjust(c, s, k_of(t), last))
        _softmax_pass2(s_bufs[u], p_bufs[u], m_ref, a_bufs[u], l_ref)
        if last:
            pv(u, t)

    def by_slot(t, last):
        for u in range(n):
            @pl.when(t % n == u)
            def _():
                stage(u, t, last)

    scores(k_of(0), s_bufs[0])

    def body(t, carry):
        by_slot(t, False)
        return carry

    lax.fori_loop(0, n_full, body, 0)
    by_slot(n_full, True)
    return acc_ref[...] / jnp.maximum(jnp.sum(l_ref[...], axis=-1, keepdims=True), TINY)


def _nsa_attn_body(q_ref, ks_ref, vs_ref, kw_ref, vw_ref, sel_ref, oc_ref, gt_ref, es_ref, o_ref,
                   selb_ref, ws_ref, wp_ref, wa_ref, wm_ref, wl_ref, *scratch, tq, tk):
    hg = NSA_HEADS // NSA_GROUPS
    span = WINDOW + tq
    qs = pl.program_id(1) * tq
    lane = _lane((1, LANES))
    res = {}
    for g in range(NSA_GROUPS):
        in_g = (lane // HALF) == g
        qg = jnp.concatenate(
            [jnp.where(in_g, q_ref[0, :, j * LANES:(j + 1) * LANES], jnp.zeros((tq, LANES), BF16)) for j in range(hg)],
            axis=0)
        picked = lax.dot_general(sel_ref[0], es_ref[g], (((0,), (0,)), ((), ())), preferred_element_type=F32)
        selb_ref[...] = (picked - 1.0) * (-NEG)

        def t_of(c):
            r0 = (c * ROW_CHUNK) % tq
            return r0, qs + r0 + lax.broadcasted_iota(I32, (ROW_CHUNK, 1), 0), _nsa_slope(g, (c * ROW_CHUNK) // tq)

        def slc_scores(k0, dst, qg=qg):
            dst[...] = _dot_nt(qg, ks_ref[0, pl.ds(k0, tk), :])

        def slc_adjust(c, s, k0, last):
            r0, t, slope = t_of(c)
            kpos = k0 + _lane((1, tk))
            s = s + slope * kpos.astype(F32) + selb_ref[r0:r0 + ROW_CHUNK, pl.ds(k0, tk)]
            if last:
                s = jnp.where(t >= kpos, s, NEG)
            return s

        o_slc = _flash_causal(qs // tk, tk, slc_scores, lambda k0: vs_ref[0, pl.ds(k0, tk), :], slc_adjust, scratch)

        w0 = pl.multiple_of(jnp.maximum(qs - WINDOW, 0), tq)
        ws_ref[...] = _dot_nt(qg, kw_ref[0, pl.ds(w0, span), :])
        wm_ref[...] = jnp.full(wm_ref.shape, NEG, F32)
        wl_ref[...] = jnp.zeros_like(wl_ref)

        def win_adjust(c, s):
            _, t, slope = t_of(c)
            kpos = w0 + _lane((1, span))
            dist = t - kpos
            return jnp.where((dist >= 0) & (dist < WINDOW), s + slope * kpos.astype(F32), NEG)

        _softmax_pass1(ws_ref, wm_ref, wm_ref, wa_ref, win_adjust)
        _softmax_pass2(ws_ref, wp_ref, wm_ref, wa_ref, wl_ref)
        o_win = _dot(wp_ref[...], vw_ref[0, pl.ds(w0, span), :]) / jnp.maximum(
            jnp.sum(wl_ref[...], axis=-1, keepdims=True), TINY)
        res[g] = (o_slc, o_win)
    gates = jax.nn.sigmoid(gt_ref[0])
    first = lane < HALF
    for j in range(hg):
        out = jnp.zeros((tq, LANES), F32)
        for br in range(3):
            gm = jnp.where(first, gates[:, j * 3 + br:j * 3 + br + 1],
                           gates[:, hg * 3 + j * 3 + br:hg * 3 + j * 3 + br + 1])
            if br == 0:
                val = oc_ref[0, :, j * LANES:(j + 1) * LANES]
            else:
                val = jnp.where(first, res[0][br - 1][j * tq:(j + 1) * tq], res[1][br - 1][j * tq:(j + 1) * tq])
            out = out + gm * val
        o_ref[0, :, j * LANES:(j + 1) * LANES] = out.astype(o_ref.dtype)


def nsa_attn(hb, hf, sel, ocmp, esel, tq, tk):
    B, S, _ = hb.shape
    qw = NSA_HEADS * NSA_DH
    rows = (NSA_HEADS // NSA_GROUPS) * tq
    span = WINDOW + tq
    kv = lambda blk: pl.BlockSpec((1, S, LANES), lambda b, i: (b, 0, blk))
    return pl.pallas_call(
        functools.partial(_nsa_attn_body, tq=tq, tk=tk),
        grid=(B, S // tq),
        in_specs=[pl.BlockSpec((1, tq, qw), lambda b, i: (b, i, NSA_Q_BLK)),
                  kv(NSA_KS_BLK), kv(NSA_VS_BLK), kv(NSA_KW_BLK), kv(NSA_VW_BLK),
                  pl.BlockSpec((1, LANES, tq), lambda b, i: (b, 0, i)),
                  pl.BlockSpec((1, tq, qw), lambda b, i: (b, i, 0)),
                  pl.BlockSpec((1, tq, LANES), lambda b, i: (b, i, NSA_GATE_BLK)),
                  pl.BlockSpec(esel.shape, lambda b, i: (0, 0, 0))],
        out_specs=pl.BlockSpec((1, tq, qw), lambda b, i: (b, i, 0)),
        out_shape=jax.ShapeDtypeStruct((B, S, qw), BF16),
        scratch_shapes=([pltpu.VMEM((tq, S), F32), pltpu.VMEM((rows, span), F32), pltpu.VMEM((rows, span), BF16)]
                        + [pltpu.VMEM((rows, LANES), F32)] * 3 + _flash_scratch(rows, tk)),
        compiler_params=_cp("parallel", "parallel"), name="nsa_attn")(
            hb, hb, hb, hb, hb, sel, ocmp, hf, esel)


def _nsa_tables(S):
    n_cmp = (S - CMP_LEN) // CMP_STRIDE + 1
    c_start = np.arange(n_cmp) * CMP_STRIDE
    c_end = c_start + CMP_LEN - 1
    j_start = np.arange(S // SLC_LEN) * SLC_LEN
    overlap = (c_end[:, None] >= j_start[None]) & (c_start[:, None] <= j_start[None] + SLC_LEN - 1)
    ov = np.zeros((NSA_GROUPS, N_CMP_PAD, LANES), np.float32)
    es = np.zeros((NSA_GROUPS, LANES, S), np.float32)
    for g in range(NSA_GROUPS):
        ov[g, :n_cmp, g * N_SLC:(g + 1) * N_SLC] = overlap
        es[g, g * N_SLC + np.arange(S) // SLC_LEN, np.arange(S)] = 1.0
    return jnp.asarray(ov, BF16), jnp.asarray(es, BF16)


NSA_PERM = np.array([(half * (NSA_HEADS // NSA_GROUPS) + j) * NSA_DH + d
                     for j in range(NSA_HEADS // NSA_GROUPS) for half in range(2) for d in range(NSA_DH)])


def nsa_mixer(hb, hf, pos_k, w1_k, w2_k, pos_v, w1_v, w2_v):
    B, S, _ = hb.shape
    assert S // CMP_STRIDE == N_CMP_PAD

    def both_groups(w):
        z = jnp.zeros_like(w)
        return jnp.concatenate([jnp.concatenate([w, z], axis=-1), jnp.concatenate([z, w], axis=-1)], axis=-2)

    def w1_blocks(w1):
        return both_groups(w1.reshape(CMP_LEN, NSA_DH, CMP_HIDDEN)).astype(BF16)

    kc, vc = nsa_compress(hf, jnp.tile(pos_k, (1, NSA_GROUPS)), jnp.tile(pos_v, (1, NSA_GROUPS)),
                          w1_blocks(w1_k), w1_blocks(w1_v),
                          both_groups(w2_k).astype(BF16), both_groups(w2_v).astype(BF16))
    ov, es = _nsa_tables(S)
    ocmp, sel = nsa_cmp(hb, kc, vc, ov, 256)
    return nsa_attn(hb, hf, sel, ocmp, es, 128, 256)


def _diff_body(slope_ref, q1_ref, q2_ref, k1_ref, k2_ref, v_ref, lam_ref, gn_ref, o_ref, *scratch,
               tq, tk, lambda_init):
    h = pl.program_id(1)
    qs = pl.program_id(2) * tq
    lane = _lane((1, LANES))
    in_half = (lane // HALF) == (h % 2)
    zero = jnp.zeros((tq, LANES), BF16)
    q1 = jnp.where(in_half, q1_ref[0], zero)
    q2 = jnp.where(in_half, q2_ref[0], zero)
    slope = slope_ref[h]

    def scores(k0, dst):
        dst[0:tq, :] = _dot_nt(q1, k1_ref[0, pl.ds(k0, tk), :])
        dst[tq:2 * tq, :] = _dot_nt(q2, k2_ref[0, pl.ds(k0, tk), :])

    def adjust(c, s, k0, last):
        kpos = k0 + _lane((1, tk))
        s = s + slope * kpos.astype(F32)
        if last:
            t = qs + (c * ROW_CHUNK) % tq + lax.broadcasted_iota(I32, (ROW_CHUNK, 1), 0)
            s = jnp.where(t >= kpos, s, NEG)
        return s

    o = _flash_causal(qs // tk, tk, scores, lambda k0: v_ref[0, pl.ds(k0, tk), :], adjust, scratch)
    lam_rows = lam_ref[...]
    lam = (jnp.exp(jnp.sum(lam_rows[0:1] * lam_rows[1:2], axis=-1, keepdims=True))
           - jnp.exp(jnp.sum(lam_rows[2:3] * lam_rows[3:4], axis=-1, keepdims=True)) + lambda_init)
    y = _rms(o[0:tq] - lam * o[tq:2 * tq], gn_ref[...]) * (1.0 - lambda_init)
    o_ref[0] = y.astype(o_ref.dtype)


def diff_mixer(hd, lam_rows, g_norm, lambda_init, tq, tk):
    B, S, _ = hd.shape
    nb = DIFF_HEADS // 2
    slopes = jnp.asarray(2.0 ** (-8.0 * np.arange(1, DIFF_HEADS + 1) / DIFF_HEADS), F32)
    grid_spec = pltpu.PrefetchScalarGridSpec(
        num_scalar_prefetch=1,
        grid=(B, DIFF_HEADS, S // tq),
        in_specs=[pl.BlockSpec((1, tq, LANES), lambda b, h, i, s: (b, i, h // 2)),
                  pl.BlockSpec((1, tq, LANES), lambda b, h, i, s: (b, i, nb + h // 2)),
                  pl.BlockSpec((1, S, LANES), lambda b, h, i, s: (b, 0, 2 * nb + h // 2)),
                  pl.BlockSpec((1, S, LANES), lambda b, h, i, s: (b, 0, 3 * nb + h // 2)),
                  pl.BlockSpec((1, S, LANES), lambda b, h, i, s: (b, 0, 4 * nb + h)),
                  pl.BlockSpec((4, LANES), lambda b, h, i, s: (0, 0)),
                  pl.BlockSpec((1, LANES), lambda b, h, i, s: (0, 0))],
        out_specs=pl.BlockSpec((1, tq, LANES), lambda b, h, i, s: (b, i, h)),
        scratch_shapes=_flash_scratch(2 * tq, tk))
    return pl.pallas_call(
        functools.partial(_diff_body, tq=tq, tk=tk, lambda_init=lambda_init),
        grid_spec=grid_spec,
        out_shape=jax.ShapeDtypeStruct((B, S, DIFF_HEADS * DIFF_DV), BF16),
        compiler_params=_cp("parallel", "parallel", "parallel"), name="diff_attn")(
            slopes, hd, hd, hd, hd, hd, lam_rows, g_norm.reshape(1, LANES))


def _router_body(x_ref, g_ref, wr_ref, tri_ref, xn_ref, meta_ref, wts_ref, cnt_ref, run_ref):
    @pl.when(pl.program_id(0) == 0)
    def _():
        run_ref[...] = jnp.zeros_like(run_ref)

    xn = _rms(x_ref[...], g_ref[...])
    xn_ref[...] = xn
    lane = _lane((1, LANES))
    logits = jnp.dot(xn, wr_ref[...], precision=lax.Precision.HIGHEST, preferred_element_type=F32)
    logits = jnp.where(lane < N_EXPERTS, logits, LOWEST)
    m1 = jnp.max(logits, axis=-1, keepdims=True)
    i1 = jnp.min(jnp.where(logits == m1, lane, LANES), axis=-1, keepdims=True)
    rest = jnp.where(lane == i1, LOWEST, logits)
    m2 = jnp.max(rest, axis=-1, keepdims=True)
    i2 = jnp.min(jnp.where(rest == m2, lane, LANES), axis=-1, keepdims=True)
    e = jnp.exp(m2 - m1)
    w1 = 1.0 / (1.0 + e)
    w2 = e / (1.0 + e)
    onehot = ((lane == i1) | (lane == i2)).astype(F32)
    pos = _dot(tri_ref[...], onehot.astype(BF16)) + run_ref[...]
    p1 = jnp.sum(jnp.where(lane == i1, pos, 0.0), axis=-1, keepdims=True).astype(I32)
    p2 = jnp.sum(jnp.where(lane == i2, pos, 0.0), axis=-1, keepdims=True).astype(I32)
    run_ref[...] += jnp.sum(onehot, axis=0, keepdims=True)
    meta_ref[...] = jnp.where(lane == 0, i1, jnp.where(lane == 1, i2, jnp.where(lane == 2, p1, jnp.where(lane == 3, p2, 0))))
    wts_ref[...] = jnp.where(lane == 0, w1, jnp.where(lane == 1, w2, 0.0))
    cnt_ref[...] = run_ref[...]


def moe_router(x, g, wr, tm):
    T, D = x.shape
    tri = jnp.asarray(np.tril(np.ones((tm, tm), np.float32), -1), BF16)
    return pl.pallas_call(
        _router_body,
        grid=(T // tm,),
        in_specs=[pl.BlockSpec((tm, D), lambda i: (i, 0)),
                  pl.BlockSpec((1, D), lambda i: (0, 0)),
                  pl.BlockSpec((D, LANES), lambda i: (0, 0)),
                  pl.BlockSpec((tm, tm), lambda i: (0, 0))],
        out_specs=[pl.BlockSpec((tm, D), lambda i: (i, 0)),
                   pl.BlockSpec((tm, LANES), lambda i: (i, 0)),
                   pl.BlockSpec((tm, LANES), lambda i: (i, 0)),
                   pl.BlockSpec((1, LANES), lambda i: (0, 0))],
        out_shape=[jax.ShapeDtypeStruct((T, D), F32), jax.ShapeDtypeStruct((T, LANES), I32),
                   jax.ShapeDtypeStruct((T, LANES), F32), jax.ShapeDtypeStruct((1, LANES), F32)],
        scratch_shapes=[pltpu.VMEM((1, LANES), F32)],
        compiler_params=_cp("arbitrary"), name="moe_router")(x, g.reshape(1, D), wr, tri)


ROW_DMA_UNROLL = 8


def _row_copy(src_ref, src_row, dst_ref, dst_row, sem):
    return pltpu.make_async_copy(src_ref.at[pl.ds(src_row, 1)], dst_ref.at[pl.ds(dst_row, 1)], sem)


def _dispatch_body(dest_ref, xn_ref, zero_ref, xs_ref, sem, *, tm):
    del zero_ref
    base = pl.program_id(0) * tm

    def issue(t, c):
        for slot in range(TOP_K):
            _row_copy(xn_ref, t, xs_ref, dest_ref[TOP_K * (base + t) + slot], sem).start()
        return c

    lax.fori_loop(0, tm, issue, 0, unroll=ROW_DMA_UNROLL)

    def drain(t, c):
        for slot in range(TOP_K):
            _row_copy(xn_ref, 0, xs_ref, 0, sem).wait()
        return c

    lax.fori_loop(0, tm, drain, 0, unroll=ROW_DMA_UNROLL)


def moe_dispatch(dest, xn, rows, tm):
    T, D = xn.shape
    grid_spec = pltpu.PrefetchScalarGridSpec(
        num_scalar_prefetch=1,
        grid=(T // tm,),
        in_specs=[pl.BlockSpec((tm, D), lambda i, d: (i, 0)),
                  pl.BlockSpec(memory_space=pl.ANY)],
        out_specs=pl.BlockSpec(memory_space=pl.ANY),
        scratch_shapes=[pltpu.SemaphoreType.DMA])
    return pl.pallas_call(
        functools.partial(_dispatch_body, tm=tm),
        grid_spec=grid_spec,
        out_shape=jax.ShapeDtypeStruct((rows, D), F32),
        input_output_aliases={2: 0},
        compiler_params=_cp("arbitrary"), name="moe_dispatch")(dest, xn, jnp.zeros((rows, D), F32))


def _experts_body(te_ref, tv_ref, xs_ref, wg_ref, wu_ref, wd_ref, o_ref, xb_ref, acc_ref):
    i = pl.program_id(0)
    k = pl.program_id(1)

    @pl.when(k == 0)
    def _():
        xb_ref[...] = xs_ref[...].astype(BF16)
        acc_ref[...] = jnp.zeros_like(acc_ref)

    @pl.when(tv_ref[i] > 0)
    def _():
        xb = xb_ref[...]
        hg = _dot(xb, wg_ref[0])
        hu = _dot(xb, wu_ref[0])
        acc_ref[...] += _dot((hg * jax.nn.sigmoid(hg) * hu).astype(BF16), wd_ref[0])

    @pl.when(k == pl.num_programs(1) - 1)
    def _():
        o_ref[...] = acc_ref[...]


def moe_experts(tile_expert, tile_valid, xs, wg, wu, wd, tr, tf):
    R, D = xs.shape
    F = wg.shape[2]
    grid_spec = pltpu.PrefetchScalarGridSpec(
        num_scalar_prefetch=2,
        grid=(R // tr, F // tf),
        in_specs=[pl.BlockSpec((tr, D), lambda i, k, te, tv: (i, 0)),
                  pl.BlockSpec((1, D, tf), lambda i, k, te, tv: (te[i], 0, k)),
                  pl.BlockSpec((1, D, tf), lambda i, k, te, tv: (te[i], 0, k)),
                  pl.BlockSpec((1, tf, D), lambda i, k, te, tv: (te[i], k, 0))],
        out_specs=pl.BlockSpec((tr, D), lambda i, k, te, tv: (i, 0)),
        scratch_shapes=[pltpu.VMEM((tr, D), BF16), pltpu.VMEM((tr, D), F32)])
    return pl.pallas_call(
        _experts_body,
        grid_spec=grid_spec,
        out_shape=jax.ShapeDtypeStruct((R, D), F32),
        compiler_params=_cp("parallel", "arbitrary"), name="moe_experts")(tile_expert, tile_valid, xs, wg, wu, wd)


def _combine_body(dest_ref, x_ref, wts_ref, g_ref, ys_ref, o_ref, buf_ref, sem, *, tm):
    base = pl.program_id(0) * tm

    def issue(t, c):
        for slot in range(TOP_K):
            _row_copy(ys_ref, dest_ref[TOP_K * (base + t) + slot], buf_ref.at[slot], t, sem).start()
        return c

    lax.fori_loop(0, tm, issue, 0, unroll=ROW_DMA_UNROLL)

    def drain(t, c):
        for slot in range(TOP_K):
            _row_copy(ys_ref, 0, buf_ref.at[slot], 0, sem).wait()
        return c

    lax.fori_loop(0, tm, drain, 0, unroll=ROW_DMA_UNROLL)
    wts = wts_ref[...]
    y = x_ref[...] + (wts[:, 0:1] * buf_ref[0] + wts[:, 1:2] * buf_ref[1])
    o_ref[...] = _rms(y, g_ref[...])


def moe_combine(dest, x, wts, g_final, ys, tm):
    T, D = x.shape
    grid_spec = pltpu.PrefetchScalarGridSpec(
        num_scalar_prefetch=1,
        grid=(T // tm,),
        in_specs=[pl.BlockSpec((tm, D), lambda i, d: (i, 0)),
                  pl.BlockSpec((tm, LANES), lambda i, d: (i, 0)),
                  pl.BlockSpec((1, D), lambda i, d: (0, 0)),
                  pl.BlockSpec(memory_space=pl.ANY)],
        out_specs=pl.BlockSpec((tm, D), lambda i, d: (i, 0)),
        scratch_shapes=[pltpu.VMEM((TOP_K, tm, D), F32), pltpu.SemaphoreType.DMA])
    return pl.pallas_call(
        functools.partial(_combine_body, tm=tm),
        grid_spec=grid_spec,
        out_shape=jax.ShapeDtypeStruct((T, D), F32),
        compiler_params=_cp("arbitrary"), name="moe_combine")(dest, x, wts, g_final.reshape(1, D), ys)


def moe_block(x, g, w_router, wg, wu, wd, g_final, tr):
    T, D = x.shape
    wr = _pad_cols(w_router, LANES)
    xn, meta, wts, counts = moe_router(x, g, wr, 512)
    cnt = counts[0, :N_EXPERTS].astype(I32)
    gsz = ((cnt + tr - 1) // tr) * tr
    gend = jnp.cumsum(gsz)
    goff = gend - gsz
    dest = (jnp.take(goff, meta[:, 0:TOP_K]) + meta[:, TOP_K:2 * TOP_K]).reshape(-1)
    n_tiles = (TOP_K * T) // tr + N_EXPERTS
    tile_start = jnp.arange(n_tiles, dtype=I32) * tr
    tile_valid = (tile_start < gend[-1]).astype(I32)
    tile_expert = jnp.minimum(jnp.sum((tile_start[:, None] >= gend[None, :]).astype(I32), axis=1), N_EXPERTS - 1)
    last_valid = jnp.take(tile_expert, jnp.maximum(gend[-1] // tr - 1, 0))
    tile_expert = jnp.where(tile_valid > 0, tile_expert, last_valid)
    xs = moe_dispatch(dest, xn, n_tiles * tr, 256)
    ys = moe_experts(tile_expert, tile_valid, xs, wg, wu, wd, tr, FFN_TF)
    return moe_combine(dest, x, wts, g_final, ys, 256)


def _pad_cols(w, width):
    return jnp.pad(w, ((0, 0), (0, width - w.shape[1])))


def _mem_kv(mem2, g, wk, wv, batch):
    w = jnp.concatenate([wk, wv], axis=1).astype(BF16)
    width = wk.shape[1]
    k, v = norm_proj(mem2, g, w, [(width, BF16), (width, BF16)], 256, "mem_kv")
    return k.reshape(batch, -1, width), v.reshape(batch, -1, width)


def kernel(x, mem, mix_norm_0, w_in_0, gla_w_alpha_0, gla_b_alpha_0, gla_out_norm_0, nsa_cmp_pos_k_0, nsa_cmp_w1_k_0, nsa_cmp_w2_k_0, nsa_cmp_pos_v_0, nsa_cmp_w1_v_0, nsa_cmp_w2_v_0, w_out_0, xattn_norm_0, xattn_mem_norm_0, xattn_wq_0, xattn_wk_0, xattn_wv_0, xattn_wo_0, ffn_norm_0, ffn_w_gate_0, ffn_w_up_0, ffn_w_down_0, mix_norm_1, w_in_1, diff_lq1_1, diff_lk1_1, diff_lq2_1, diff_lk2_1, diff_out_norm_1, w_out_1, xattn_norm_1, xattn_mem_norm_1, xattn_wq_1, xattn_wk_1, xattn_wv_1, xattn_wo_1, ffn_norm_1, moe_router_1, moe_w_gate_1, moe_w_up_1, moe_w_down_1, final_norm):
    B, S, D = x.shape
    T = B * S
    x2 = x.reshape(T, D)
    mem2 = mem.reshape(-1, D)
    q_scale = NSA_DH ** -0.5

    sizes = [GLA_HEADS * GLA_DK, GLA_HEADS * GLA_DK, GLA_HEADS * GLA_DV, GLA_HEADS * GLA_DV, GLA_RANK,
             NSA_HEADS * NSA_DH] + [NSA_GROUPS * NSA_DH] * 6 + [3 * NSA_HEADS]
    (g_q, g_k, g_v, g_r, g_a, n_q, n_kc, n_vc, n_ks, n_vs, n_kw, n_vw, n_g) = jnp.split(
        w_in_0, np.cumsum(sizes)[:-1].tolist(), axis=1)
    w0 = jnp.concatenate([n_q[:, NSA_PERM] * q_scale, n_ks, n_vs, n_kw, n_vw, g_q * q_scale, g_k, g_v,
                          _pad_cols(g_a, LANES), g_r, _pad_cols(n_g, LANES), n_kc, n_vc], axis=1).astype(BF16)
    hb, hf = norm_proj(x2, mix_norm_0, w0, [(HB0_WIDTH, BF16), (HF0_WIDTH, F32)], 512, "in_proj0")
    hb = hb.reshape(B, S, HB0_WIDTH)
    hf = hf.reshape(B, S, HF0_WIDTH)
    w_alpha = jnp.pad(gla_w_alpha_0, ((0, LANES - GLA_RANK), (0, 0))).astype(BF16)
    o_a = gla_mixer(hb, hf, w_alpha, gla_b_alpha_0.reshape(1, -1), gla_out_norm_0, GLA_COLS, 8)
    o_b = nsa_mixer(hb, hf, nsa_cmp_pos_k_0, nsa_cmp_w1_k_0, nsa_cmp_w2_k_0,
                    nsa_cmp_pos_v_0, nsa_cmp_w1_v_0, nsa_cmp_w2_v_0)
    n_a = GLA_HEADS * GLA_DV
    k0, v0 = _mem_kv(mem2, xattn_mem_norm_0, xattn_wk_0, xattn_wv_0, B)
    x2 = proj_xattn_block(x2, [o_a.reshape(T, -1), o_b.reshape(T, -1)],
                          [w_out_0[:n_a].astype(BF16), w_out_0[n_a:][NSA_PERM].astype(BF16)],
                          xattn_norm_0, (xattn_wq_0 * q_scale).astype(BF16), k0, v0, xattn_wo_0.astype(BF16),
                          S, 512, "out_xattn0")
    x2 = ffn_block(x2, ffn_norm_0, ffn_w_gate_0.astype(BF16), ffn_w_up_0.astype(BF16), ffn_w_down_0.astype(BF16),
                   512, FFN_TF, "ffn0")

    lambda_init = 0.8 - 0.6 * math.exp(-0.3 * 1)
    n_q1 = 2 * DIFF_HEADS * DIFF_DH
    w1 = jnp.concatenate([w_in_1[:, :n_q1] * q_scale, w_in_1[:, n_q1:]], axis=1).astype(BF16)
    (hd,) = norm_proj(x2, mix_norm_1, w1, [(w1.shape[1], BF16)], 512, "in_proj1")
    lam_rows = _pad_cols(jnp.stack([diff_lq1_1, diff_lk1_1, diff_lq2_1, diff_lk2_1]), LANES)
    o_d = diff_mixer(hd.reshape(B, S, -1), lam_rows, diff_out_norm_1, lambda_init, 256, 256)
    k1, v1 = _mem_kv(mem2, xattn_mem_norm_1, xattn_wk_1, xattn_wv_1, B)
    x2 = proj_xattn_block(x2, [o_d.reshape(T, -1)], [w_out_1.astype(BF16)],
                          xattn_norm_1, (xattn_wq_1 * q_scale).astype(BF16), k1, v1, xattn_wo_1.astype(BF16),
                          S, 512, "out_xattn1")
    out = moe_block(x2, ffn_norm_1, moe_router_1, moe_w_gate_1.astype(BF16), moe_w_up_1.astype(BF16),
                    moe_w_down_1.astype(BF16), final_norm, 512)
    return out.reshape(B, S, D)
```

```python
import functools
import math

import numpy as np
import jax
import jax.numpy as jnp
from jax import lax
from jax.experimental import pallas as pl
from jax.experimental.pallas import tpu as pltpu

F32 = jnp.float32
BF16 = jnp.bfloat16
I32 = jnp.int32

D_MODEL = 1024
GLA_HEADS, GLA_DK, GLA_DV, GLA_RANK, GLA_TAU, GLA_CHUNK = 4, 64, 128, 16, 16.0, 64
NSA_HEADS, NSA_GROUPS, NSA_DH = 8, 2, 64
CMP_LEN, CMP_STRIDE, CMP_HIDDEN, SLC_LEN, N_SEL, WINDOW = 32, 16, 256, 64, 8, 512
DIFF_HEADS, DIFF_DH, DIFF_DV = 8, 64, 128
XATTN_HEADS, XATTN_DH = 4, 64
D_FF, N_EXPERTS, TOP_K = 2816, 8, 2
EPS, NEG, TINY, FORCE_SCORE = 1e-6, -1e30, 1e-30, 1e4

LANES = 128
HALF = LANES // 2
VMEM_LIMIT = 56 * 1024 * 1024
LOWEST = -3.0e38

HB0_WIDTH, HF0_WIDTH = 17 * LANES, 7 * LANES
NSA_Q_BLK, NSA_KS_BLK, NSA_VS_BLK, NSA_KW_BLK, NSA_VW_BLK = 0, 4, 5, 6, 7
GLA_COLS = (8, 10, 12, 16, 0)
NSA_GATE_BLK, NSA_KC_BLK, NSA_VC_BLK = 4, 5, 6


def _cp(*sem, flags=None):
    return pltpu.CompilerParams(dimension_semantics=sem, vmem_limit_bytes=VMEM_LIMIT, flags=flags)


def _rms(x, g):
    y = x * lax.rsqrt(jnp.mean(x * x, axis=-1, keepdims=True) + EPS)
    return y * g


def _dot(a, b):
    return jnp.dot(a, b, preferred_element_type=F32)


def _dot_nt(a, b):
    return lax.dot_general(a, b, (((1,), (1,)), ((), ())), preferred_element_type=F32)


def _lane(shape):
    return lax.broadcasted_iota(I32, shape, len(shape) - 1)


def _norm_proj_body(x_ref, g_ref, w_ref, *o_refs, widths):
    xn = _rms(x_ref[...], g_ref[...]).astype(BF16)
    off = 0
    for o_ref, wd in zip(o_refs, widths):
        for c0 in range(0, wd, 512):
            cw = min(512, wd - c0)
            o_ref[:, c0:c0 + cw] = _dot(xn, w_ref[:, off + c0:off + c0 + cw]).astype(o_ref.dtype)
        off += wd


def norm_proj(x, g, w, outs, tm, name):
    T, D = x.shape
    widths = tuple(o[0] for o in outs)
    return pl.pallas_call(
        functools.partial(_norm_proj_body, widths=widths),
        grid=(T // tm,),
        in_specs=[pl.BlockSpec((tm, D), lambda i: (i, 0)),
                  pl.BlockSpec((1, D), lambda i: (0, 0)),
                  pl.BlockSpec(w.shape, lambda i: (0, 0))],
        out_specs=[pl.BlockSpec((tm, wd), lambda i: (i, 0)) for wd, _ in outs],
        out_shape=[jax.ShapeDtypeStruct((T, wd), dt) for wd, dt in outs],
        compiler_params=_cp("parallel"), name=name)(x, g.reshape(1, D), w)


def _proj_xattn_body(*refs, n):
    res_ref = refs[0]
    a_refs = refs[1:1 + n]
    w_refs = refs[1 + n:1 + 2 * n]
    g_ref, wq_ref, k_ref, v_ref, wo_ref, o_ref = refs[1 + 2 * n:]
    x = res_ref[...]
    for a_ref, w_ref in zip(a_refs, w_refs):
        x = x + _dot(a_ref[...].astype(BF16), w_ref[...])
    q = _dot(_rms(x, g_ref[...]).astype(BF16), wq_ref[...]).astype(BF16)
    lane = _lane((1, LANES))
    blocks = []
    for blk in range(XATTN_HEADS // 2):
        qb = q[:, blk * LANES:(blk + 1) * LANES]
        kb = k_ref[0, :, blk * LANES:(blk + 1) * LANES]
        vb = v_ref[0, :, blk * LANES:(blk + 1) * LANES]
        pair = None
        for half in range(2):
            in_half = (lane // HALF) == half
            s = _dot_nt(jnp.where(in_half, qb, jnp.zeros_like(qb)), kb)
            p = jnp.exp(s - jnp.max(s, axis=-1, keepdims=True))
            p = p / jnp.sum(p, axis=-1, keepdims=True)
            o = _dot(p.astype(BF16), vb)
            pair = o if half == 0 else jnp.where(in_half, o, pair)
        blocks.append(pair)
    o = jnp.concatenate(blocks, axis=1).astype(BF16)
    o_ref[...] = x + _dot(o, wo_ref[...])


def proj_xattn_block(res, a_list, w_list, g, wq, k, v, wo, seq, tm, name):
    T, D = res.shape
    n = len(a_list)
    M, W = k.shape[1], k.shape[2]
    per = seq // tm
    return pl.pallas_call(
        functools.partial(_proj_xattn_body, n=n),
        grid=(T // tm,),
        in_specs=([pl.BlockSpec((tm, D), lambda i: (i, 0))]
                  + [pl.BlockSpec((tm, a.shape[1]), lambda i: (i, 0)) for a in a_list]
                  + [pl.BlockSpec(w.shape, lambda i: (0, 0)) for w in w_list]
                  + [pl.BlockSpec((1, D), lambda i: (0, 0)),
                     pl.BlockSpec(wq.shape, lambda i: (0, 0)),
                     pl.BlockSpec((1, M, W), lambda i: (i // per, 0, 0)),
                     pl.BlockSpec((1, M, W), lambda i: (i // per, 0, 0)),
                     pl.BlockSpec(wo.shape, lambda i: (0, 0))]),
        out_specs=pl.BlockSpec((tm, D), lambda i: (i, 0)),
        out_shape=jax.ShapeDtypeStruct((T, D), F32),
        compiler_params=_cp("parallel"), name=name)(res, *a_list, *w_list, g.reshape(1, D), wq, k, v, wo)


def _ffn_body(x_ref, g_ref, wg_ref, wu_ref, wd_ref, o_ref, xn_ref, acc_ref):
    k = pl.program_id(1)

    @pl.when(k == 0)
    def _():
        xn_ref[...] = _rms(x_ref[...], g_ref[...]).astype(BF16)
        acc_ref[...] = x_ref[...]

    xn = xn_ref[...]
    hg = _dot(xn, wg_ref[...])
    hu = _dot(xn, wu_ref[...])
    h = (hg * jax.nn.sigmoid(hg) * hu).astype(BF16)
    acc_ref[...] += _dot(h, wd_ref[...])

    @pl.when(k == pl.num_programs(1) - 1)
    def _():
        o_ref[...] = acc_ref[...]


def ffn_block(x, g, wg, wu, wd, tm, tf, name):
    T, D = x.shape
    F = wg.shape[1]
    return pl.pallas_call(
        _ffn_body,
        grid=(T // tm, F // tf),
        in_specs=[pl.BlockSpec((tm, D), lambda i, k: (i, 0)),
                  pl.BlockSpec((1, D), lambda i, k: (0, 0)),
                  pl.BlockSpec((D, tf), lambda i, k: (0, k)),
                  pl.BlockSpec((D, tf), lambda i, k: (0, k)),
                  pl.BlockSpec((tf, D), lambda i, k: (k, 0))],
        out_specs=pl.BlockSpec((tm, D), lambda i, k: (i, 0)),
        out_shape=jax.ShapeDtypeStruct((T, D), F32),
        scratch_shapes=[pltpu.VMEM((tm, D), BF16), pltpu.VMEM((tm, D), F32)],
        compiler_params=_cp("parallel", "arbitrary"), name=name)(x, g.reshape(1, D), wg, wu, wd)


_GLA_LEVELS = (32, 16, 8, 4, 2, 1)


def _gla_constants():
    C = GLA_CHUNK
    t = np.arange(C)
    r = t[None, :]
    mats = [r <= t[:, None], r > t[:, None]]
    for hs in _GLA_LEVELS:
        c = (t // (2 * hs)) * (2 * hs) + hs - 1
        right = (t % (2 * hs)) >= hs
        m_right = (r > c[:, None]) & (r <= t[:, None])
        m_left = (r > t[:, None]) & (r <= c[:, None])
        mats.append(np.where(right[:, None], m_right, m_left))
    cmat = np.concatenate(mats, 0).astype(np.float32)
    x = t[:, None] ^ t[None, :]
    lvl = np.full((C, C), -1, np.int32)
    for i, hs in enumerate(_GLA_LEVELS):
        lvl[(t[:, None] > t[None, :]) & (x >= hs) & (x < 2 * hs)] = i
    lvl[t[:, None] == t[None, :]] = len(_GLA_LEVELS)
    return cmat, lvl


def _gla_body(q_ref, k_ref, v_ref, a_ref, r_ref, wa_ref, ba_ref, gn_ref, cmat_ref, lvl_ref, o_ref, st_ref, *, nb):
    C = GLA_CHUNK

    @pl.when(pl.program_id(2) == 0)
    def _():
        st_ref[...] = jnp.zeros_like(st_ref)

    nlev = len(_GLA_LEVELS)
    in_a = _lane((1, LANES)) < HALF
    lvl = lvl_ref[...]
    cmat = cmat_ref[...]
    on_diag = (lax.broadcasted_iota(I32, (LANES, 2 * LANES), 0) // HALF
               == lax.broadcasted_iota(I32, (LANES, 2 * LANES), 1) // LANES)
    for bi in range(nb):
        q = q_ref[bi].astype(F32)
        k = k_ref[bi].astype(F32)
        z = _dot(a_ref[bi], wa_ref[...]) + ba_ref[...]
        la = (jnp.minimum(z, 0.0) - jnp.log1p(jnp.exp(-jnp.abs(z)))) / GLA_TAU
        hi = la.astype(BF16)
        rest = la - hi.astype(F32)
        mid = rest.astype(BF16)
        lo = (rest - mid.astype(F32)).astype(BF16)
        u3 = _dot(cmat, jnp.concatenate([hi, mid, lo], axis=1))
        e = jnp.exp(u3[:, 0:LANES] + u3[:, LANES:2 * LANES] + u3[:, 2 * LANES:3 * LANES])
        qhat = (q * e[0:C]).astype(BF16)
        kend = (k * e[C:2 * C]).astype(BF16)
        decay = jnp.broadcast_to(e[C - 1:C], (LANES, LANES)).T
        zero = jnp.zeros_like(q)

        def stacked(x):
            return jnp.concatenate([jnp.where(in_a, x, zero), jnp.where(in_a, zero, x)], axis=0).astype(BF16)

        sc = jnp.zeros((2 * C, C), F32)
        for i in range(nlev):
            ei = e[(2 + i) * C:(3 + i) * C]
            sc = jnp.where(lvl == i, _dot_nt(stacked(q * ei), (k * ei).astype(BF16)), sc)
        sc = jnp.where(lvl == nlev, _dot_nt(stacked(q), k.astype(BF16)), sc)
        v = v_ref[bi]
        state = st_ref[bi]
        o_inter = _dot(qhat, state.astype(BF16))
        o_intra = _dot(sc.astype(BF16), v)
        update = lax.dot_general(kend, v, (((0,), (0,)), ((), ())), preferred_element_type=F32)
        st_ref[bi] = jnp.where(on_diag, jnp.concatenate([decay, decay], axis=1) * state + update, 0.0)
        for half in range(2):
            cols = slice(half * LANES, (half + 1) * LANES)
            y = _rms(o_inter[:, cols] + o_intra[half * C:(half + 1) * C, cols], gn_ref[...])
            rr = r_ref[bi, :, cols]
            o_ref[bi, :, cols] = (y * (rr * jax.nn.sigmoid(rr))).astype(o_ref.dtype)


def gla_mixer(hb, hf, w_alpha, b_alpha, g_norm, cols, nb):
    B, S, _ = hb.shape
    C = GLA_CHUNK
    cmat, lvl = _gla_constants()
    lvl = np.concatenate([lvl, lvl], axis=0)
    qb, kb_, vb, ab, rb = cols
    return pl.pallas_call(
        functools.partial(_gla_body, nb=nb),
        grid=(B // nb, GLA_HEADS // 2, S // C),
        in_specs=[pl.BlockSpec((nb, C, LANES), lambda b, p, c: (b, c, qb + p)),
                  pl.BlockSpec((nb, C, LANES), lambda b, p, c: (b, c, kb_ + p)),
                  pl.BlockSpec((nb, C, 2 * LANES), lambda b, p, c: (b, c, vb // 2 + p)),
                  pl.BlockSpec((nb, C, LANES), lambda b, p, c: (b, c, ab)),
                  pl.BlockSpec((nb, C, 2 * LANES), lambda b, p, c: (b, c, rb // 2 + p)),
                  pl.BlockSpec((LANES, LANES), lambda b, p, c: (0, p)),
                  pl.BlockSpec((1, LANES), lambda b, p, c: (0, p)),
                  pl.BlockSpec((1, LANES), lambda b, p, c: (0, 0)),
                  pl.BlockSpec(cmat.shape, lambda b, p, c: (0, 0)),
                  pl.BlockSpec(lvl.shape, lambda b, p, c: (0, 0))],
        out_specs=pl.BlockSpec((nb, C, 2 * LANES), lambda b, p, c: (b, c, p)),
        out_shape=jax.ShapeDtypeStruct((B, S, GLA_HEADS * GLA_DV), BF16),
        scratch_shapes=[pltpu.VMEM((nb, LANES, 2 * LANES), F32)],
        compiler_params=_cp("parallel", "parallel", "arbitrary"), name="gla")(
            hb, hb, hb, hb, hf, w_alpha, b_alpha, g_norm.reshape(1, LANES), jnp.asarray(cmat, BF16), jnp.asarray(lvl))


N_CMP_PAD = 128
N_SLC = 32


def _gelu_tanh(x):
    return 0.5 * x * (1.0 + jnp.tanh(math.sqrt(2.0 / math.pi) * (x + 0.044715 * (x * x * x))))


def _nsa_compress_body(tk_ref, tv_ref, pk_ref, pv_ref, w1k_ref, w1v_ref, w2k_ref, w2v_ref, kc_ref, vc_ref):
    for t_ref, p_ref, w1_ref, w2_ref, o_ref in ((tk_ref, pk_ref, w1k_ref, w2k_ref, kc_ref),
                                                (tv_ref, pv_ref, w1v_ref, w2v_ref, vc_ref)):
        lo = jnp.zeros((N_CMP_PAD, NSA_GROUPS * CMP_HIDDEN), F32)
        hi = jnp.zeros((N_CMP_PAD, NSA_GROUPS * CMP_HIDDEN), F32)
        for l in range(CMP_STRIDE):
            rows = t_ref[0, pl.ds(l, N_CMP_PAD, stride=CMP_STRIDE), :]
            lo = lo + _dot((rows + p_ref[l:l + 1]).astype(BF16), w1_ref[l])
            hi = hi + _dot((rows + p_ref[CMP_STRIDE + l:CMP_STRIDE + l + 1]).astype(BF16), w1_ref[CMP_STRIDE + l])
        pre = lo + pltpu.roll(hi, N_CMP_PAD - 1, 0)
        o_ref[0] = _dot(_gelu_tanh(pre).astype(BF16), w2_ref[...]).astype(o_ref.dtype)


def nsa_compress(hf, pk, pv, w1k, w1v, w2k, w2v):
    B, S, _ = hf.shape
    full = lambda a: pl.BlockSpec(a.shape, lambda b: (0,) * a.ndim)
    out = pl.BlockSpec((1, N_CMP_PAD, LANES), lambda b: (b, 0, 0))
    return pl.pallas_call(
        _nsa_compress_body,
        grid=(B,),
        in_specs=[pl.BlockSpec((1, S, LANES), lambda b: (b, 0, NSA_KC_BLK)),
                  pl.BlockSpec((1, S, LANES), lambda b: (b, 0, NSA_VC_BLK)),
                  full(pk), full(pv), full(w1k), full(w1v), full(w2k), full(w2v)],
        out_specs=[out, out],
        out_shape=[jax.ShapeDtypeStruct((B, N_CMP_PAD, LANES), BF16)] * 2,
        compiler_params=_cp("parallel"), name="nsa_compress")(hf, hf, pk, pv, w1k, w1v, w2k, w2v)


def _nsa_slope(g, j):
    return 2.0 ** (-(g * (NSA_HEADS // NSA_GROUPS) + j + 1))


def _nsa_cmp_body(q_ref, kc_ref, vc_ref, ov_ref, o_ref, selt_ref, sct_ref, *, tq):
    hg = NSA_HEADS // NSA_GROUPS
    qs = pl.program_id(1) * tq
    lane = _lane((1, LANES))
    tpos = qs + lax.broadcasted_iota(I32, (tq, LANES), 0)
    dist = tpos - (lane * CMP_STRIDE + (CMP_LEN - 1))
    valid = (dist >= 0) & (lane < N_CMP_PAD - 1)
    validf = valid.astype(F32)
    distf = dist.astype(F32)
    kc = kc_ref[0]
    vc = vc_ref[0]
    imp = jnp.zeros((tq, LANES), F32)
    blocks = [None] * hg
    for g in range(NSA_GROUPS):
        in_g = (lane // HALF) == g
        for j in range(hg):
            qb = q_ref[0, :, j * LANES:(j + 1) * LANES]
            s = _dot_nt(jnp.where(in_g, qb, jnp.zeros_like(qb)), kc) - _nsa_slope(g, j) * distf
            s = jnp.where(valid, s, NEG)
            p = jnp.exp(s - jnp.max(s, axis=-1, keepdims=True)) * validf
            p = (p / jnp.maximum(jnp.sum(p, axis=-1, keepdims=True), TINY)).astype(BF16)
            o = _dot(p, vc)
            blocks[j] = o if g == 0 else jnp.where(in_g, o, blocks[j])
            imp = imp + _dot(p, ov_ref[g])
    for j in range(hg):
        o_ref[0, :, j * LANES:(j + 1) * LANES] = blocks[j]
    blk = lane % N_SLC
    cur = tpos // SLC_LEN
    score = jnp.where((blk == cur) | (blk == 0), FORCE_SCORE, jnp.where(blk <= cur, imp, NEG))
    sct_ref[...] = score.T
    row8 = lax.broadcasted_iota(I32, (8, tq), 0)
    for g in range(NSA_GROUPS):
        base = g * N_SLC
        tiles = [sct_ref[base + 8 * v:base + 8 * (v + 1), :] for v in range(N_SLC // 8)]
        beaten = [jnp.zeros((8, tq), F32) for _ in tiles]
        for i in range(N_SLC):
            cand = jnp.broadcast_to(sct_ref[base + i:base + i + 1, :], (8, tq))
            for v, tile in enumerate(tiles):
                ge = jnp.where(cand >= tile, 1.0, 0.0)
                gt = jnp.where(cand > tile, 1.0, 0.0)
                if 8 * v > i:
                    wins = ge
                elif 8 * v + 7 <= i:
                    wins = gt
                else:
                    wins = jnp.where(row8 > i - 8 * v, ge, gt)
                beaten[v] = beaten[v] + wins
        for v in range(N_SLC // 8):
            selt_ref[0, base + 8 * v:base + 8 * (v + 1), :] = jnp.where(beaten[v] < N_SEL, 1.0, 0.0).astype(selt_ref.dtype)
    selt_ref[0, NSA_GROUPS * N_SLC:LANES, :] = jnp.zeros((LANES - NSA_GROUPS * N_SLC, tq), selt_ref.dtype)


def nsa_cmp(hb, kc, vc, ov, tq):
    B, S, _ = hb.shape
    qw = NSA_HEADS * NSA_DH
    return pl.pallas_call(
        functools.partial(_nsa_cmp_body, tq=tq),
        grid=(B, S // tq),
        in_specs=[pl.BlockSpec((1, tq, qw), lambda b, i: (b, i, NSA_Q_BLK)),
                  pl.BlockSpec((1, N_CMP_PAD, LANES), lambda b, i: (b, 0, 0)),
                  pl.BlockSpec((1, N_CMP_PAD, LANES), lambda b, i: (b, 0, 0)),
                  pl.BlockSpec(ov.shape, lambda b, i: (0, 0, 0))],
        out_specs=[pl.BlockSpec((1, tq, qw), lambda b, i: (b, i, 0)),
                   pl.BlockSpec((1, LANES, tq), lambda b, i: (b, 0, i))],
        out_shape=[jax.ShapeDtypeStruct((B, S, qw), F32), jax.ShapeDtypeStruct((B, LANES, S), BF16)],
        scratch_shapes=[pltpu.VMEM((LANES, tq), F32)],
        compiler_params=_cp("parallel", "parallel"), name="nsa_cmp")(hb, kc, vc, ov)


def _softmax_step(s, maskf, v, m, l, acc):
    m_new = jnp.maximum(m, jnp.max(s, axis=-1, keepdims=True))
    p = jnp.exp(s - m_new) * maskf
    alpha = jnp.exp(m - m_new)
    l = alpha * l + jnp.sum(p, axis=-1, keepdims=True)
    acc = alpha * acc + _dot(p.astype(BF16), v)
    return m_new, l, acc


FFN_TF = D_FF // 2
ROW_CHUNK = 32
LOG2E = math.log2(math.e)


FLASH_SLOTS = 2


def _flash_scratch(rows, tk):
    return ([pltpu.VMEM((rows, tk), F32)] * FLASH_SLOTS + [pltpu.VMEM((rows, tk), BF16)] * FLASH_SLOTS
            + [pltpu.VMEM((rows, LANES), F32)] * (FLASH_SLOTS + 3))


def _put_scores(dst, r0, raw, adjust):
    for i in range(raw.shape[0] // ROW_CHUNK):
        r = r0 + i * ROW_CHUNK
        dst[r:r + ROW_CHUNK, :] = adjust(r // ROW_CHUNK, raw[i * ROW_CHUNK:(i + 1) * ROW_CHUNK, :]) * LOG2E


def _softmax_pass1(s_ref, m_ref, a_ref):
    rows, tk = s_ref.shape
    m_in = m_out = m_ref
    for c in range(rows // ROW_CHUNK):
        r = slice(c * ROW_CHUNK, (c + 1) * ROW_CHUNK)
        s = s_ref[r, :]
        top = s[:, 0:LANES]
        for j in range(1, tk // LANES):
            top = jnp.maximum(top, s[:, j * LANES:(j + 1) * LANES])
        m_old = m_in[r, :]
        m_new = jnp.maximum(m_old, jnp.broadcast_to(jnp.max(top, axis=-1, keepdims=True), (ROW_CHUNK, LANES)))
        m_out[r, :] = m_new
        a_ref[r, :] = jnp.exp2(m_old - m_new)


def _softmax_pass2(s_ref, p_ref, m_ref, a_ref, l_ref):
    rows, tk = s_ref.shape
    for c in range(rows // ROW_CHUNK):
        r = slice(c * ROW_CHUNK, (c + 1) * ROW_CHUNK)
        m_new = m_ref[r, :]
        part = None
        for j in range(tk // LANES):
            p = jnp.exp2(s_ref[r, j * LANES:(j + 1) * LANES] - m_new)
            p_ref[r, j * LANES:(j + 1) * LANES] = p.astype(BF16)
            part = p if part is None else part + p
        l_ref[r, :] = a_ref[r, :] * l_ref[r, :] + part


def _flash_causal(n_full, tk, scores, pv_dot, scratch):
    n = FLASH_SLOTS
    s_bufs, p_bufs, a_bufs = (scratch[i * n:(i + 1) * n] for i in range(3))
    m_ref, l_ref, acc_ref = scratch[3 * n:]
    m_ref[...] = jnp.full(m_ref.shape, NEG, F32)
    a_bufs[n - 1][...] = jnp.zeros_like(a_bufs[0])
    p_bufs[n - 1][...] = jnp.zeros_like(p_bufs[0])
    l_ref[...] = jnp.zeros_like(l_ref)
    acc_ref[...] = jnp.zeros_like(acc_ref)

    def k_of(t):
        return pl.multiple_of(t * tk, tk)

    def pv(u, t):
        acc_ref[...] = a_bufs[u][...] * acc_ref[...] + pv_dot(p_bufs[u], k_of(t))

    NEXT_FULL, NEXT_LAST, FINAL = range(3)

    def stage(u, t, kind):
        pv((u - 1) % n, jnp.maximum(t - 1, 0))
        if kind != FINAL:
            scores(k_of(t + 1), s_bufs[(u + 1) % n], kind == NEXT_LAST)
        _softmax_pass1(s_bufs[u], m_ref, a_bufs[u])
        _softmax_pass2(s_bufs[u], p_bufs[u], m_ref, a_bufs[u], l_ref)
        if kind == FINAL:
            pv(u, t)

    def by_slot(t, kind):
        for u in range(n):
            @pl.when(t % n == u)
            def _():
                stage(u, t, kind)

    for first_is_last in (False, True):
        @pl.when((n_full == 0) == first_is_last)
        def _():
            scores(k_of(0), s_bufs[0], first_is_last)

    def body(t, carry):
        by_slot(t, NEXT_FULL)
        return carry

    lax.fori_loop(0, n_full - 1, body, 0)

    @pl.when(n_full >= 1)
    def _():
        by_slot(n_full - 1, NEXT_LAST)

    by_slot(n_full, FINAL)
    return acc_ref[...] / jnp.maximum(jnp.sum(l_ref[...], axis=-1, keepdims=True), TINY)


def _nsa_attn_body(q_ref, ks_ref, vs_ref, kw_ref, vw_ref, sel_ref, oc_ref, gt_ref, es_ref, o_ref,
                   selb_ref, ws_ref, wp_ref, wa_ref, wm_ref, wl_ref, *scratch, tq, tk):
    hg = NSA_HEADS // NSA_GROUPS
    span = WINDOW + tq
    qs = pl.program_id(1) * tq
    lane = _lane((1, LANES))
    first = lane < HALF
    zero = jnp.zeros((tq, LANES), BF16)
    q_all = jnp.concatenate(
        [jnp.where((lane // HALF) == g, q_ref[0, :, j * LANES:(j + 1) * LANES], zero)
         for g in range(NSA_GROUPS) for j in range(hg)], axis=0)
    for g in range(NSA_GROUPS):
        picked = lax.dot_general(sel_ref[0], es_ref[g], (((0,), (0,)), ((), ())), preferred_element_type=F32)
        selb_ref[g] = (picked - 1.0) * (-NEG)

    def t_of(c):
        head, r0 = divmod(c * ROW_CHUNK, tq)
        g, j = divmod(head, hg)
        return g, r0, qs + r0 + lax.broadcasted_iota(I32, (ROW_CHUNK, 1), 0), _nsa_slope(g, j)

    def slc_scores(k0, dst, last):
        kpos = k0 + _lane((1, tk))

        def adjust(c, s):
            g, r0, t, slope = t_of(c)
            s = s + slope * kpos.astype(F32) + selb_ref[g, r0:r0 + ROW_CHUNK, pl.ds(k0, tk)]
            return jnp.where(t >= kpos, s, NEG) if last else s

        _put_scores(dst, 0, _dot_nt(q_all, ks_ref[0, pl.ds(k0, tk), :]), adjust)

    o_slc = _flash_causal(qs // tk, tk, slc_scores,
                          lambda p_ref, k0: _dot(p_ref[...], vs_ref[0, pl.ds(k0, tk), :]), scratch)

    w0 = pl.multiple_of(jnp.maximum(qs - WINDOW, 0), tq)
    wpos = w0 + _lane((1, span))

    def win_adjust(c, s):
        _, _, t, slope = t_of(c)
        dist = t - wpos
        return jnp.where((dist >= 0) & (dist < WINDOW), s + slope * wpos.astype(F32), NEG)

    _put_scores(ws_ref, 0, _dot_nt(q_all, kw_ref[0, pl.ds(w0, span), :]), win_adjust)
    wm_ref[...] = jnp.full(wm_ref.shape, NEG, F32)
    wl_ref[...] = jnp.zeros_like(wl_ref)
    _softmax_pass1(ws_ref, wm_ref, wa_ref)
    _softmax_pass2(ws_ref, wp_ref, wm_ref, wa_ref, wl_ref)
    o_win = _dot(wp_ref[...], vw_ref[0, pl.ds(w0, span), :]) / jnp.maximum(
        jnp.sum(wl_ref[...], axis=-1, keepdims=True), TINY)

    gates = jax.nn.sigmoid(gt_ref[0])
    for j in range(hg):
        out = jnp.zeros((tq, LANES), F32)
        for br, branch in enumerate((None, o_slc, o_win)):
            gm = jnp.where(first, gates[:, j * 3 + br:j * 3 + br + 1],
                           gates[:, hg * 3 + j * 3 + br:hg * 3 + j * 3 + br + 1])
            if branch is None:
                val = oc_ref[0, :, j * LANES:(j + 1) * LANES]
            else:
                val = jnp.where(first, branch[j * tq:(j + 1) * tq], branch[(hg + j) * tq:(hg + j + 1) * tq])
            out = out + gm * val
        o_ref[0, :, j * LANES:(j + 1) * LANES] = out.astype(o_ref.dtype)


def nsa_attn(hb, hf, sel, ocmp, esel, tq, tk):
    B, S, _ = hb.shape
    qw = NSA_HEADS * NSA_DH
    rows = NSA_HEADS * tq
    span = WINDOW + tq
    kv = lambda blk: pl.BlockSpec((1, S, LANES), lambda b, i: (b, 0, blk))
    return pl.pallas_call(
        functools.partial(_nsa_attn_body, tq=tq, tk=tk),
        grid=(B, S // tq),
        in_specs=[pl.BlockSpec((1, tq, qw), lambda b, i: (b, i, NSA_Q_BLK)),
                  kv(NSA_KS_BLK), kv(NSA_VS_BLK), kv(NSA_KW_BLK), kv(NSA_VW_BLK),
                  pl.BlockSpec((1, LANES, tq), lambda b, i: (b, 0, i)),
                  pl.BlockSpec((1, tq, qw), lambda b, i: (b, i, 0)),
                  pl.BlockSpec((1, tq, LANES), lambda b, i: (b, i, NSA_GATE_BLK)),
                  pl.BlockSpec(esel.shape, lambda b, i: (0, 0, 0))],
        out_specs=pl.BlockSpec((1, tq, qw), lambda b, i: (b, i, 0)),
        out_shape=jax.ShapeDtypeStruct((B, S, qw), BF16),
        scratch_shapes=([pltpu.VMEM((NSA_GROUPS, tq, S), F32), pltpu.VMEM((rows, span), F32),
                         pltpu.VMEM((rows, span), BF16)]
                        + [pltpu.VMEM((rows, LANES), F32)] * 3 + _flash_scratch(rows, tk)),
        compiler_params=_cp("parallel", "parallel"), name="nsa_attn")(
            hb, hb, hb, hb, hb, sel, ocmp, hf, esel)


def _nsa_tables(S):
    n_cmp = (S - CMP_LEN) // CMP_STRIDE + 1
    c_start = np.arange(n_cmp) * CMP_STRIDE
    c_end = c_start + CMP_LEN - 1
    j_start = np.arange(S // SLC_LEN) * SLC_LEN
    overlap = (c_end[:, None] >= j_start[None]) & (c_start[:, None] <= j_start[None] + SLC_LEN - 1)
    ov = np.zeros((NSA_GROUPS, N_CMP_PAD, LANES), np.float32)
    es = np.zeros((NSA_GROUPS, LANES, S), np.float32)
    for g in range(NSA_GROUPS):
        ov[g, :n_cmp, g * N_SLC:(g + 1) * N_SLC] = overlap
        es[g, g * N_SLC + np.arange(S) // SLC_LEN, np.arange(S)] = 1.0
    return jnp.asarray(ov, BF16), jnp.asarray(es, BF16)


NSA_PERM = np.array([(half * (NSA_HEADS // NSA_GROUPS) + j) * NSA_DH + d
                     for j in range(NSA_HEADS // NSA_GROUPS) for half in range(2) for d in range(NSA_DH)])


def nsa_mixer(hb, hf, pos_k, w1_k, w2_k, pos_v, w1_v, w2_v):
    B, S, _ = hb.shape
    assert S // CMP_STRIDE == N_CMP_PAD

    def both_groups(w):
        z = jnp.zeros_like(w)
        return jnp.concatenate([jnp.concatenate([w, z], axis=-1), jnp.concatenate([z, w], axis=-1)], axis=-2)

    def w1_blocks(w1):
        return both_groups(w1.reshape(CMP_LEN, NSA_DH, CMP_HIDDEN)).astype(BF16)

    kc, vc = nsa_compress(hf, jnp.tile(pos_k, (1, NSA_GROUPS)), jnp.tile(pos_v, (1, NSA_GROUPS)),
                          w1_blocks(w1_k), w1_blocks(w1_v),
                          both_groups(w2_k).astype(BF16), both_groups(w2_v).astype(BF16))
    ov, es = _nsa_tables(S)
    ocmp, sel = nsa_cmp(hb, kc, vc, ov, 256)
    return nsa_attn(hb, hf, sel, ocmp, es, 128, 256)


def _diff_body(slope_ref, q1_ref, q2_ref, k1_ref, k2_ref, v_ref, lam_ref, gn_ref, o_ref, *scratch,
               tq, tk, lambda_init):
    pair = pl.program_id(1)
    qs = pl.program_id(2) * tq
    in_a = _lane((1, LANES)) < HALF
    zero = jnp.zeros((tq, LANES), BF16)
    q1, q2 = q1_ref[0], q2_ref[0]
    qa = jnp.concatenate([jnp.where(in_a, q1, zero), jnp.where(in_a, q2, zero)], axis=0)
    qb = jnp.concatenate([jnp.where(in_a, zero, q1), jnp.where(in_a, zero, q2)], axis=0)
    slopes = (slope_ref[2 * pair], slope_ref[2 * pair + 1])

    def scores(k0, dst, last):
        k1 = k1_ref[0, pl.ds(k0, tk), :]
        k2 = k2_ref[0, pl.ds(k0, tk), :]
        kpos = k0 + _lane((1, tk))

        def adjust(c, s):
            s = s + slopes[(c * ROW_CHUNK) // (2 * tq)] * kpos.astype(F32)
            if last:
                t = qs + (c * ROW_CHUNK) % tq + lax.broadcasted_iota(I32, (ROW_CHUNK, 1), 0)
                s = jnp.where(t >= kpos, s, NEG)
            return s

        for hh, q in enumerate((qa, qb)):
            _put_scores(dst, (2 * hh) * tq, _dot_nt(q[0:tq], k1), adjust)
            _put_scores(dst, (2 * hh + 1) * tq, _dot_nt(q[tq:2 * tq], k2), adjust)

    def pv_dot(p_ref, k0):
        return jnp.concatenate(
            [_dot(p_ref[2 * hh * tq:2 * (hh + 1) * tq, :], v_ref[0, pl.ds(k0, tk), hh * LANES:(hh + 1) * LANES])
             for hh in range(2)], axis=0)

    o = _flash_causal(qs // tk, tk, scores, pv_dot, scratch)
    lam_rows = lam_ref[...]
    lam = (jnp.exp(jnp.sum(lam_rows[0:1] * lam_rows[1:2], axis=-1, keepdims=True))
           - jnp.exp(jnp.sum(lam_rows[2:3] * lam_rows[3:4], axis=-1, keepdims=True)) + lambda_init)
    for hh in range(2):
        r0 = 2 * hh * tq
        y = _rms(o[r0:r0 + tq] - lam * o[r0 + tq:r0 + 2 * tq], gn_ref[...]) * (1.0 - lambda_init)
        o_ref[0, :, hh * LANES:(hh + 1) * LANES] = y.astype(o_ref.dtype)


def diff_mixer(hd, lam_rows, g_norm, lambda_init, tq, tk):
    B, S, _ = hd.shape
    nb = DIFF_HEADS // 2
    slopes = jnp.asarray(2.0 ** (-8.0 * np.arange(1, DIFF_HEADS + 1) / DIFF_HEADS), F32)
    grid_spec = pltpu.PrefetchScalarGridSpec(
        num_scalar_prefetch=1,
        grid=(B, nb, S // tq),
        in_specs=[pl.BlockSpec((1, tq, LANES), lambda b, p, i, s: (b, i, p)),
                  pl.BlockSpec((1, tq, LANES), lambda b, p, i, s: (b, i, nb + p)),
                  pl.BlockSpec((1, S, LANES), lambda b, p, i, s: (b, 0, 2 * nb + p)),
                  pl.BlockSpec((1, S, LANES), lambda b, p, i, s: (b, 0, 3 * nb + p)),
                  pl.BlockSpec((1, S, 2 * LANES), lambda b, p, i, s: (b, 0, 2 * nb + p)),
                  pl.BlockSpec((4, LANES), lambda b, p, i, s: (0, 0)),
                  pl.BlockSpec((1, LANES), lambda b, p, i, s: (0, 0))],
        out_specs=pl.BlockSpec((1, tq, 2 * LANES), lambda b, p, i, s: (b, i, p)),
        scratch_shapes=_flash_scratch(4 * tq, tk))
    return pl.pallas_call(
        functools.partial(_diff_body, tq=tq, tk=tk, lambda_init=lambda_init),
        grid_spec=grid_spec,
        out_shape=jax.ShapeDtypeStruct((B, S, DIFF_HEADS * DIFF_DV), BF16),
        compiler_params=_cp("parallel", "parallel", "parallel"), name="diff_attn")(
            slopes, hd, hd, hd, hd, hd, lam_rows, g_norm.reshape(1, LANES))


def _router_body(x_ref, g_ref, wr_ref, tri_ref, xn_ref, meta_ref, wts_ref, cnt_ref, run_ref):
    @pl.when(pl.program_id(0) == 0)
    def _():
        run_ref[...] = jnp.zeros_like(run_ref)

    xn = _rms(x_ref[...], g_ref[...])
    xn_ref[...] = xn
    lane = _lane((1, LANES))
    logits = jnp.dot(xn, wr_ref[...], precision=lax.Precision.HIGHEST, preferred_element_type=F32)
    logits = jnp.where(lane < N_EXPERTS, logits, LOWEST)
    m1 = jnp.max(logits, axis=-1, keepdims=True)
    i1 = jnp.min(jnp.where(logits == m1, lane, LANES), axis=-1, keepdims=True)
    rest = jnp.where(lane == i1, LOWEST, logits)
    m2 = jnp.max(rest, axis=-1, keepdims=True)
    i2 = jnp.min(jnp.where(rest == m2, lane, LANES), axis=-1, keepdims=True)
    e = jnp.exp(m2 - m1)
    w1 = 1.0 / (1.0 + e)
    w2 = e / (1.0 + e)
    onehot = ((lane == i1) | (lane == i2)).astype(F32)
    pos = _dot(tri_ref[...], onehot.astype(BF16)) + run_ref[...]
    p1 = jnp.sum(jnp.where(lane == i1, pos, 0.0), axis=-1, keepdims=True).astype(I32)
    p2 = jnp.sum(jnp.where(lane == i2, pos, 0.0), axis=-1, keepdims=True).astype(I32)
    run_ref[...] += jnp.sum(onehot, axis=0, keepdims=True)
    meta_ref[...] = jnp.where(lane == 0, i1, jnp.where(lane == 1, i2, jnp.where(lane == 2, p1, jnp.where(lane == 3, p2, 0))))
    wts_ref[...] = jnp.where(lane == 0, w1, jnp.where(lane == 1, w2, 0.0))
    cnt_ref[...] = run_ref[...]


def moe_router(x, g, wr, tm):
    T, D = x.shape
    tri = jnp.asarray(np.tril(np.ones((tm, tm), np.float32), -1), BF16)
    return pl.pallas_call(
        _router_body,
        grid=(T // tm,),
        in_specs=[pl.BlockSpec((tm, D), lambda i: (i, 0)),
                  pl.BlockSpec((1, D), lambda i: (0, 0)),
                  pl.BlockSpec((D, LANES), lambda i: (0, 0)),
                  pl.BlockSpec((tm, tm), lambda i: (0, 0))],
        out_specs=[pl.BlockSpec((tm, D), lambda i: (i, 0)),
                   pl.BlockSpec((tm, LANES), lambda i: (i, 0)),
                   pl.BlockSpec((tm, LANES), lambda i: (i, 0)),
                   pl.BlockSpec((1, LANES), lambda i: (0, 0))],
        out_shape=[jax.ShapeDtypeStruct((T, D), F32), jax.ShapeDtypeStruct((T, LANES), I32),
                   jax.ShapeDtypeStruct((T, LANES), F32), jax.ShapeDtypeStruct((1, LANES), F32)],
        scratch_shapes=[pltpu.VMEM((1, LANES), F32)],
        compiler_params=_cp("arbitrary"), name="moe_router")(x, g.reshape(1, D), wr, tri)


ROW_DMA_UNROLL = 8


def _row_copy(src_ref, src_row, dst_ref, dst_row, sem):
    return pltpu.make_async_copy(src_ref.at[pl.ds(src_row, 1)], dst_ref.at[pl.ds(dst_row, 1)], sem)


def _dispatch_body(dest_ref, xn_ref, zero_ref, xs_ref, sem, *, tm):
    del zero_ref
    base = pl.program_id(0) * tm

    def issue(t, c):
        for slot in range(TOP_K):
            _row_copy(xn_ref, t, xs_ref, dest_ref[TOP_K * (base + t) + slot], sem).start()
        return c

    lax.fori_loop(0, tm, issue, 0, unroll=ROW_DMA_UNROLL)

    def drain(t, c):
        for slot in range(TOP_K):
            _row_copy(xn_ref, 0, xs_ref, 0, sem).wait()
        return c

    lax.fori_loop(0, tm, drain, 0, unroll=ROW_DMA_UNROLL)


def moe_dispatch(dest, xn, rows, tm):
    T, D = xn.shape
    grid_spec = pltpu.PrefetchScalarGridSpec(
        num_scalar_prefetch=1,
        grid=(T // tm,),
        in_specs=[pl.BlockSpec((tm, D), lambda i, d: (i, 0)),
                  pl.BlockSpec(memory_space=pl.ANY)],
        out_specs=pl.BlockSpec(memory_space=pl.ANY),
        scratch_shapes=[pltpu.SemaphoreType.DMA])
    return pl.pallas_call(
        functools.partial(_dispatch_body, tm=tm),
        grid_spec=grid_spec,
        out_shape=jax.ShapeDtypeStruct((rows, D), F32),
        input_output_aliases={2: 0},
        compiler_params=_cp("arbitrary"), name="moe_dispatch")(dest, xn, jnp.zeros((rows, D), F32))


def _experts_body(te_ref, tv_ref, xs_ref, wg_ref, wu_ref, wd_ref, o_ref, xb_ref, acc_ref):
    i = pl.program_id(0)
    k = pl.program_id(1)

    @pl.when(k == 0)
    def _():
        xb_ref[...] = xs_ref[...].astype(BF16)
        acc_ref[...] = jnp.zeros_like(acc_ref)

    @pl.when(tv_ref[i] > 0)
    def _():
        xb = xb_ref[...]
        hg = _dot(xb, wg_ref[0])
        hu = _dot(xb, wu_ref[0])
        acc_ref[...] += _dot((hg * jax.nn.sigmoid(hg) * hu).astype(BF16), wd_ref[0])

    @pl.when(k == pl.num_programs(1) - 1)
    def _():
        o_ref[...] = acc_ref[...]


def moe_experts(tile_expert, tile_valid, xs, wg, wu, wd, tr, tf):
    R, D = xs.shape
    F = wg.shape[2]
    grid_spec = pltpu.PrefetchScalarGridSpec(
        num_scalar_prefetch=2,
        grid=(R // tr, F // tf),
        in_specs=[pl.BlockSpec((tr, D), lambda i, k, te, tv: (i, 0)),
                  pl.BlockSpec((1, D, tf), lambda i, k, te, tv: (te[i], 0, k)),
                  pl.BlockSpec((1, D, tf), lambda i, k, te, tv: (te[i], 0, k)),
                  pl.BlockSpec((1, tf, D), lambda i, k, te, tv: (te[i], k, 0))],
        out_specs=pl.BlockSpec((tr, D), lambda i, k, te, tv: (i, 0)),
        scratch_shapes=[pltpu.VMEM((tr, D), BF16), pltpu.VMEM((tr, D), F32)])
    return pl.pallas_call(
        _experts_body,
        grid_spec=grid_spec,
        out_shape=jax.ShapeDtypeStruct((R, D), F32),
        compiler_params=_cp("parallel", "arbitrary"), name="moe_experts")(tile_expert, tile_valid, xs, wg, wu, wd)


def _combine_body(dest_ref, x_ref, wts_ref, g_ref, ys_ref, o_ref, buf_ref, sem, *, tm):
    base = pl.program_id(0) * tm

    def issue(t, c):
        for slot in range(TOP_K):
            _row_copy(ys_ref, dest_ref[TOP_K * (base + t) + slot], buf_ref.at[slot], t, sem).start()
        return c

    lax.fori_loop(0, tm, issue, 0, unroll=ROW_DMA_UNROLL)

    def drain(t, c):
        for slot in range(TOP_K):
            _row_copy(ys_ref, 0, buf_ref.at[slot], 0, sem).wait()
        return c

    lax.fori_loop(0, tm, drain, 0, unroll=ROW_DMA_UNROLL)
    wts = wts_ref[...]
    y = x_ref[...] + (wts[:, 0:1] * buf_ref[0] + wts[:, 1:2] * buf_ref[1])
    o_ref[...] = _rms(y, g_ref[...])


def moe_combine(dest, x, wts, g_final, ys, tm):
    T, D = x.shape
    grid_spec = pltpu.PrefetchScalarGridSpec(
        num_scalar_prefetch=1,
        grid=(T // tm,),
        in_specs=[pl.BlockSpec((tm, D), lambda i, d: (i, 0)),
                  pl.BlockSpec((tm, LANES), lambda i, d: (i, 0)),
                  pl.BlockSpec((1, D), lambda i, d: (0, 0)),
                  pl.BlockSpec(memory_space=pl.ANY)],
        out_specs=pl.BlockSpec((tm, D), lambda i, d: (i, 0)),
        scratch_shapes=[pltpu.VMEM((TOP_K, tm, D), F32), pltpu.SemaphoreType.DMA])
    return pl.pallas_call(
        functools.partial(_combine_body, tm=tm),
        grid_spec=grid_spec,
        out_shape=jax.ShapeDtypeStruct((T, D), F32),
        compiler_params=_cp("arbitrary"), name="moe_combine")(dest, x, wts, g_final.reshape(1, D), ys)


def moe_block(x, g, w_router, wg, wu, wd, g_final, tr):
    T, D = x.shape
    wr = _pad_cols(w_router, LANES)
    xn, meta, wts, counts = moe_router(x, g, wr, 512)
    cnt = counts[0, :N_EXPERTS].astype(I32)
    gsz = ((cnt + tr - 1) // tr) * tr
    gend = jnp.cumsum(gsz)
    goff = gend - gsz
    dest = (jnp.take(goff, meta[:, 0:TOP_K]) + meta[:, TOP_K:2 * TOP_K]).reshape(-1)
    n_tiles = (TOP_K * T) // tr + N_EXPERTS
    tile_start = jnp.arange(n_tiles, dtype=I32) * tr
    tile_valid = (tile_start < gend[-1]).astype(I32)
    tile_expert = jnp.minimum(jnp.sum((tile_start[:, None] >= gend[None, :]).astype(I32), axis=1), N_EXPERTS - 1)
    last_valid = jnp.take(tile_expert, jnp.maximum(gend[-1] // tr - 1, 0))
    tile_expert = jnp.where(tile_valid > 0, tile_expert, last_valid)
    xs = moe_dispatch(dest, xn, n_tiles * tr, 256)
    ys = moe_experts(tile_expert, tile_valid, xs, wg, wu, wd, tr, FFN_TF)
    return moe_combine(dest, x, wts, g_final, ys, 256)


def _pad_cols(w, width):
    return jnp.pad(w, ((0, 0), (0, width - w.shape[1])))


def _mem_kv(mem2, g, wk, wv, batch):
    w = jnp.concatenate([wk, wv], axis=1).astype(BF16)
    width = wk.shape[1]
    k, v = norm_proj(mem2, g, w, [(width, BF16), (width, BF16)], 256, "mem_kv")
    return k.reshape(batch, -1, width), v.reshape(batch, -1, width)


def kernel(x, mem, mix_norm_0, w_in_0, gla_w_alpha_0, gla_b_alpha_0, gla_out_norm_0, nsa_cmp_pos_k_0, nsa_cmp_w1_k_0, nsa_cmp_w2_k_0, nsa_cmp_pos_v_0, nsa_cmp_w1_v_0, nsa_cmp_w2_v_0, w_out_0, xattn_norm_0, xattn_mem_norm_0, xattn_wq_0, xattn_wk_0, xattn_wv_0, xattn_wo_0, ffn_norm_0, ffn_w_gate_0, ffn_w_up_0, ffn_w_down_0, mix_norm_1, w_in_1, diff_lq1_1, diff_lk1_1, diff_lq2_1, diff_lk2_1, diff_out_norm_1, w_out_1, xattn_norm_1, xattn_mem_norm_1, xattn_wq_1, xattn_wk_1, xattn_wv_1, xattn_wo_1, ffn_norm_1, moe_router_1, moe_w_gate_1, moe_w_up_1, moe_w_down_1, final_norm):
    B, S, D = x.shape
    T = B * S
    x2 = x.reshape(T, D)
    mem2 = mem.reshape(-1, D)
    q_scale = NSA_DH ** -0.5

    sizes = [GLA_HEADS * GLA_DK, GLA_HEADS * GLA_DK, GLA_HEADS * GLA_DV, GLA_HEADS * GLA_DV, GLA_RANK,
             NSA_HEADS * NSA_DH] + [NSA_GROUPS * NSA_DH] * 6 + [3 * NSA_HEADS]
    (g_q, g_k, g_v, g_r, g_a, n_q, n_kc, n_vc, n_ks, n_vs, n_kw, n_vw, n_g) = jnp.split(
        w_in_0, np.cumsum(sizes)[:-1].tolist(), axis=1)
    w0 = jnp.concatenate([n_q[:, NSA_PERM] * q_scale, n_ks, n_vs, n_kw, n_vw, g_q * q_scale, g_k, g_v,
                          _pad_cols(g_a, LANES), g_r, _pad_cols(n_g, LANES), n_kc, n_vc], axis=1).astype(BF16)
    hb, hf = norm_proj(x2, mix_norm_0, w0, [(HB0_WIDTH, BF16), (HF0_WIDTH, F32)], 512, "in_proj0")
    hb = hb.reshape(B, S, HB0_WIDTH)
    hf = hf.reshape(B, S, HF0_WIDTH)
    w_alpha = jnp.pad(gla_w_alpha_0, ((0, LANES - GLA_RANK), (0, 0))).astype(BF16)
    o_a = gla_mixer(hb, hf, w_alpha, gla_b_alpha_0.reshape(1, -1), gla_out_norm_0, GLA_COLS, 8)
    o_b = nsa_mixer(hb, hf, nsa_cmp_pos_k_0, nsa_cmp_w1_k_0, nsa_cmp_w2_k_0,
                    nsa_cmp_pos_v_0, nsa_cmp_w1_v_0, nsa_cmp_w2_v_0)
    n_a = GLA_HEADS * GLA_DV
    k0, v0 = _mem_kv(mem2, xattn_mem_norm_0, xattn_wk_0, xattn_wv_0, B)
    x2 = proj_xattn_block(x2, [o_a.reshape(T, -1), o_b.reshape(T, -1)],
                          [w_out_0[:n_a].astype(BF16), w_out_0[n_a:][NSA_PERM].astype(BF16)],
                          xattn_norm_0, (xattn_wq_0 * q_scale).astype(BF16), k0, v0, xattn_wo_0.astype(BF16),
                          S, 512, "out_xattn0")
    x2 = ffn_block(x2, ffn_norm_0, ffn_w_gate_0.astype(BF16), ffn_w_up_0.astype(BF16), ffn_w_down_0.astype(BF16),
                   512, FFN_TF, "ffn0")

    lambda_init = 0.8 - 0.6 * math.exp(-0.3 * 1)
    n_q1 = 2 * DIFF_HEADS * DIFF_DH
    w1 = jnp.concatenate([w_in_1[:, :n_q1] * q_scale, w_in_1[:, n_q1:]], axis=1).astype(BF16)
    (hd,) = norm_proj(x2, mix_norm_1, w1, [(w1.shape[1], BF16)], 512, "in_proj1")
    lam_rows = _pad_cols(jnp.stack([diff_lq1_1, diff_lk1_1, diff_lq2_1, diff_lk2_1]), LANES)
    o_d = diff_mixer(hd.reshape(B, S, -1), lam_rows, diff_out_norm_1, lambda_init, 256, 256)
    k1, v1 = _mem_kv(mem2, xattn_mem_norm_1, xattn_wk_1, xattn_wv_1, B)
    x2 = proj_xattn_block(x2, [o_d.reshape(T, -1)], [w_out_1.astype(BF16)],
                          xattn_norm_1, (xattn_wq_1 * q_scale).astype(BF16), k1, v1, xattn_wo_1.astype(BF16),
                          S, 512, "out_xattn1")
    out = moe_block(x2, ffn_norm_1, moe_router_1, moe_w_gate_1.astype(BF16), moe_w_up_1.astype(BF16),
                    moe_w_down_1.astype(BF16), final_norm, 512)
    return out.reshape(B, S, D)
```

```python
import functools
import math

import numpy as np
import jax
import jax.numpy as jnp
from jax import lax
from jax.experimental import pallas as pl
from jax.experimental.pallas import tpu as pltpu

F32 = jnp.float32
BF16 = jnp.bfloat16
I32 = jnp.int32

D_MODEL = 1024
GLA_HEADS, GLA_DK, GLA_DV, GLA_RANK, GLA_TAU, GLA_CHUNK = 4, 64, 128, 16, 16.0, 64
NSA_HEADS, NSA_GROUPS, NSA_DH = 8, 2, 64
CMP_LEN, CMP_STRIDE, CMP_HIDDEN, SLC_LEN, N_SEL, WINDOW = 32, 16, 256, 64, 8, 512
DIFF_HEADS, DIFF_DH, DIFF_DV = 8, 64, 128
XATTN_HEADS, XATTN_DH = 4, 64
D_FF, N_EXPERTS, TOP_K = 2816, 8, 2
EPS, NEG, TINY, FORCE_SCORE = 1e-6, -1e30, 1e-30, 1e4

LANES = 128
HALF = LANES // 2
VMEM_LIMIT = 56 * 1024 * 1024
LOWEST = -3.0e38

HB0_WIDTH, HF0_WIDTH = 17 * LANES, 7 * LANES
NSA_Q_BLK, NSA_KS_BLK, NSA_VS_BLK, NSA_KW_BLK, NSA_VW_BLK = 0, 4, 5, 6, 7
GLA_COLS = (8, 10, 12, 16, 0)
NSA_GATE_BLK, NSA_KC_BLK, NSA_VC_BLK = 4, 5, 6


def _cp(*sem, flags=None):
    return pltpu.CompilerParams(dimension_semantics=sem, vmem_limit_bytes=VMEM_LIMIT, flags=flags)


def _rms(x, g):
    y = x * lax.rsqrt(jnp.mean(x * x, axis=-1, keepdims=True) + EPS)
    return y * g


def _dot(a, b):
    return jnp.dot(a, b, preferred_element_type=F32)


def _dot_nt(a, b):
    return lax.dot_general(a, b, (((1,), (1,)), ((), ())), preferred_element_type=F32)


def _lane(shape):
    return lax.broadcasted_iota(I32, shape, len(shape) - 1)


def _norm_proj_body(x_ref, g_ref, w_ref, *o_refs, widths):
    xn = _rms(x_ref[...], g_ref[...]).astype(BF16)
    off = 0
    for o_ref, wd in zip(o_refs, widths):
        for c0 in range(0, wd, 512):
            cw = min(512, wd - c0)
            o_ref[:, c0:c0 + cw] = _dot(xn, w_ref[:, off + c0:off + c0 + cw]).astype(o_ref.dtype)
        off += wd


def norm_proj(x, g, w, outs, tm, name):
    T, D = x.shape
    widths = tuple(o[0] for o in outs)
    return pl.pallas_call(
        functools.partial(_norm_proj_body, widths=widths),
        grid=(T // tm,),
        in_specs=[pl.BlockSpec((tm, D), lambda i: (i, 0)),
                  pl.BlockSpec((1, D), lambda i: (0, 0)),
                  pl.BlockSpec(w.shape, lambda i: (0, 0))],
        out_specs=[pl.BlockSpec((tm, wd), lambda i: (i, 0)) for wd, _ in outs],
        out_shape=[jax.ShapeDtypeStruct((T, wd), dt) for wd, dt in outs],
        compiler_params=_cp("parallel"), name=name)(x, g.reshape(1, D), w)


def _proj_xattn_body(*refs, n):
    res_ref = refs[0]
    a_refs = refs[1:1 + n]
    w_refs = refs[1 + n:1 + 2 * n]
    g_ref, wq_ref, k_ref, v_ref, wo_ref, o_ref = refs[1 + 2 * n:]
    x = res_ref[...]
    for a_ref, w_ref in zip(a_refs, w_refs):
        x = x + _dot(a_ref[...].astype(BF16), w_ref[...])
    q = _dot(_rms(x, g_ref[...]).astype(BF16), wq_ref[...]).astype(BF16)
    lane = _lane((1, LANES))
    blocks = []
    for blk in range(XATTN_HEADS // 2):
        qb = q[:, blk * LANES:(blk + 1) * LANES]
        kb = k_ref[0, :, blk * LANES:(blk + 1) * LANES]
        vb = v_ref[0, :, blk * LANES:(blk + 1) * LANES]
        pair = None
        for half in range(2):
            in_half = (lane // HALF) == half
            s = _dot_nt(jnp.where(in_half, qb, jnp.zeros_like(qb)), kb)
            p = jnp.exp(s - jnp.max(s, axis=-1, keepdims=True))
            p = p / jnp.sum(p, axis=-1, keepdims=True)
            o = _dot(p.astype(BF16), vb)
            pair = o if half == 0 else jnp.where(in_half, o, pair)
        blocks.append(pair)
    o = jnp.concatenate(blocks, axis=1).astype(BF16)
    o_ref[...] = x + _dot(o, wo_ref[...])


def proj_xattn_block(res, a_list, w_list, g, wq, k, v, wo, seq, tm, name):
    T, D = res.shape
    n = len(a_list)
    M, W = k.shape[1], k.shape[2]
    per = seq // tm
    return pl.pallas_call(
        functools.partial(_proj_xattn_body, n=n),
        grid=(T // tm,),
        in_specs=([pl.BlockSpec((tm, D), lambda i: (i, 0))]
                  + [pl.BlockSpec((tm, a.shape[1]), lambda i: (i, 0)) for a in a_list]
                  + [pl.BlockSpec(w.shape, lambda i: (0, 0)) for w in w_list]
                  + [pl.BlockSpec((1, D), lambda i: (0, 0)),
                     pl.BlockSpec(wq.shape, lambda i: (0, 0)),
                     pl.BlockSpec((1, M, W), lambda i: (i // per, 0, 0)),
                     pl.BlockSpec((1, M, W), lambda i: (i // per, 0, 0)),
                     pl.BlockSpec(wo.shape, lambda i: (0, 0))]),
        out_specs=pl.BlockSpec((tm, D), lambda i: (i, 0)),
        out_shape=jax.ShapeDtypeStruct((T, D), F32),
        compiler_params=_cp("parallel"), name=name)(res, *a_list, *w_list, g.reshape(1, D), wq, k, v, wo)


def _ffn_body(x_ref, g_ref, wg_ref, wu_ref, wd_ref, o_ref, xn_ref):
    @pl.when(pl.program_id(1) == 0)
    def _():
        xn_ref[...] = _rms(x_ref[...], g_ref[...]).astype(BF16)
        o_ref[...] = x_ref[...]

    xn = xn_ref[...]
    hg = _dot(xn, wg_ref[...])
    hu = _dot(xn, wu_ref[...])
    h = (hg * jax.nn.sigmoid(hg) * hu).astype(BF16)
    o_ref[...] += _dot(h, wd_ref[...])


def ffn_block(x, g, wg, wu, wd, tm, tf, name):
    T, D = x.shape
    F = wg.shape[1]
    return pl.pallas_call(
        _ffn_body,
        grid=(T // tm, F // tf),
        in_specs=[pl.BlockSpec((tm, D), lambda i, k: (i, 0)),
                  pl.BlockSpec((1, D), lambda i, k: (0, 0)),
                  pl.BlockSpec((D, tf), lambda i, k: (0, k)),
                  pl.BlockSpec((D, tf), lambda i, k: (0, k)),
                  pl.BlockSpec((tf, D), lambda i, k: (k, 0))],
        out_specs=pl.BlockSpec((tm, D), lambda i, k: (i, 0)),
        out_shape=jax.ShapeDtypeStruct((T, D), F32),
        scratch_shapes=[pltpu.VMEM((tm, D), BF16)],
        compiler_params=_cp("parallel", "arbitrary"), name=name)(x, g.reshape(1, D), wg, wu, wd)


GLA_TILE = 128
_GLA_LEVELS = tuple(GLA_TILE >> (i + 1) for i in range(GLA_TILE.bit_length() - 1))


def _gla_constants():
    C = GLA_TILE
    t = np.arange(C)
    r = t[None, :]
    mats = [r <= t[:, None], r > t[:, None]]
    for hs in _GLA_LEVELS:
        c = (t // (2 * hs)) * (2 * hs) + hs - 1
        right = (t % (2 * hs)) >= hs
        m_right = (r > c[:, None]) & (r <= t[:, None])
        m_left = (r > t[:, None]) & (r <= c[:, None])
        mats.append(np.where(right[:, None], m_right, m_left))
    cmat = np.concatenate(mats, 0).astype(np.float32)
    x = t[:, None] ^ t[None, :]
    lvl = np.full((C, C), -1, np.int32)
    for i, hs in enumerate(_GLA_LEVELS):
        lvl[(t[:, None] > t[None, :]) & (x >= hs) & (x < 2 * hs)] = i
    lvl[t[:, None] == t[None, :]] = len(_GLA_LEVELS)
    return cmat, lvl


def _gla_body(q_ref, k_ref, v_ref, a_ref, r_ref, wa_ref, ba_ref, gn_ref, cmat_ref, lvl_ref, o_ref, st_ref, *, nb):
    C = GLA_TILE

    @pl.when(pl.program_id(2) == 0)
    def _():
        st_ref[...] = jnp.zeros_like(st_ref)

    nlev = len(_GLA_LEVELS)
    in_a = _lane((1, LANES)) < HALF
    lvl = lvl_ref[...]
    cmat = cmat_ref[...]
    on_diag = (lax.broadcasted_iota(I32, (LANES, 2 * LANES), 0) // HALF
               == lax.broadcasted_iota(I32, (LANES, 2 * LANES), 1) // LANES)
    for bi in range(nb):
        q = q_ref[bi].astype(F32)
        k = k_ref[bi].astype(F32)
        z = _dot(a_ref[bi], wa_ref[...]) + ba_ref[...]
        la = (jnp.minimum(z, 0.0) - jnp.log1p(jnp.exp(-jnp.abs(z)))) / GLA_TAU
        hi = la.astype(BF16)
        rest = la - hi.astype(F32)
        mid = rest.astype(BF16)
        lo = (rest - mid.astype(F32)).astype(BF16)
        u3 = _dot(cmat, jnp.concatenate([hi, mid, lo], axis=1))
        e = jnp.exp(u3[:, 0:LANES] + u3[:, LANES:2 * LANES] + u3[:, 2 * LANES:3 * LANES])
        qhat = (q * e[0:C]).astype(BF16)
        kend = (k * e[C:2 * C]).astype(BF16)
        decay = jnp.broadcast_to(e[C - 1:C], (LANES, LANES)).T
        zero = jnp.zeros_like(q)

        def stacked(x):
            return jnp.concatenate([jnp.where(in_a, x, zero), jnp.where(in_a, zero, x)], axis=0).astype(BF16)

        sc = jnp.zeros((2 * C, C), F32)
        for i in range(nlev):
            ei = e[(2 + i) * C:(3 + i) * C]
            sc = jnp.where(lvl == i, _dot_nt(stacked(q * ei), (k * ei).astype(BF16)), sc)
        sc = jnp.where(lvl == nlev, _dot_nt(stacked(q), k.astype(BF16)), sc)
        v = v_ref[bi]
        state = st_ref[bi]
        o_inter = _dot(qhat, state.astype(BF16))
        o_intra = _dot(sc.astype(BF16), v)
        update = lax.dot_general(kend, v, (((0,), (0,)), ((), ())), preferred_element_type=F32)
        st_ref[bi] = jnp.where(on_diag, jnp.concatenate([decay, decay], axis=1) * state + update, 0.0)
        for half in range(2):
            cols = slice(half * LANES, (half + 1) * LANES)
            y = _rms(o_inter[:, cols] + o_intra[half * C:(half + 1) * C, cols], gn_ref[...])
            rr = r_ref[bi, :, cols]
            o_ref[bi, :, cols] = (y * (rr * jax.nn.sigmoid(rr))).astype(o_ref.dtype)


def gla_mixer(hb, hf, w_alpha, b_alpha, g_norm, cols, nb):
    B, S, _ = hb.shape
    C = GLA_TILE
    cmat, lvl = _gla_constants()
    lvl = np.concatenate([lvl, lvl], axis=0)
    qb, kb_, vb, ab, rb = cols
    return pl.pallas_call(
        functools.partial(_gla_body, nb=nb),
        grid=(B // nb, GLA_HEADS // 2, S // C),
        in_specs=[pl.BlockSpec((nb, C, LANES), lambda b, p, c: (b, c, qb + p)),
                  pl.BlockSpec((nb, C, LANES), lambda b, p, c: (b, c, kb_ + p)),
                  pl.BlockSpec((nb, C, 2 * LANES), lambda b, p, c: (b, c, vb // 2 + p)),
                  pl.BlockSpec((nb, C, LANES), lambda b, p, c: (b, c, ab)),
                  pl.BlockSpec((nb, C, 2 * LANES), lambda b, p, c: (b, c, rb // 2 + p)),
                  pl.BlockSpec((LANES, LANES), lambda b, p, c: (0, p)),
                  pl.BlockSpec((1, LANES), lambda b, p, c: (0, p)),
                  pl.BlockSpec((1, LANES), lambda b, p, c: (0, 0)),
                  pl.BlockSpec(cmat.shape, lambda b, p, c: (0, 0)),
                  pl.BlockSpec(lvl.shape, lambda b, p, c: (0, 0))],
        out_specs=pl.BlockSpec((nb, C, 2 * LANES), lambda b, p, c: (b, c, p)),
        out_shape=jax.ShapeDtypeStruct((B, S, GLA_HEADS * GLA_DV), BF16),
        scratch_shapes=[pltpu.VMEM((nb, LANES, 2 * LANES), F32)],
        compiler_params=_cp("parallel", "parallel", "arbitrary"), name="gla")(
            hb, hb, hb, hb, hf, w_alpha, b_alpha, g_norm.reshape(1, LANES), jnp.asarray(cmat, BF16), jnp.asarray(lvl))


N_CMP_PAD = 128
N_SLC = 32


def _gelu_tanh(x):
    return 0.5 * x * (1.0 + jnp.tanh(math.sqrt(2.0 / math.pi) * (x + 0.044715 * (x * x * x))))


def _nsa_compress_body(tk_ref, tv_ref, pk_ref, pv_ref, w1k_ref, w1v_ref, w2k_ref, w2v_ref, kc_ref, vc_ref):
    for t_ref, p_ref, w1_ref, w2_ref, o_ref in ((tk_ref, pk_ref, w1k_ref, w2k_ref, kc_ref),
                                                (tv_ref, pv_ref, w1v_ref, w2v_ref, vc_ref)):
        lo = jnp.zeros((N_CMP_PAD, NSA_GROUPS * CMP_HIDDEN), F32)
        hi = jnp.zeros((N_CMP_PAD, NSA_GROUPS * CMP_HIDDEN), F32)
        for l in range(CMP_STRIDE):
            rows = t_ref[0, pl.ds(l, N_CMP_PAD, stride=CMP_STRIDE), :]
            lo = lo + _dot((rows + p_ref[l:l + 1]).astype(BF16), w1_ref[l])
            hi = hi + _dot((rows + p_ref[CMP_STRIDE + l:CMP_STRIDE + l + 1]).astype(BF16), w1_ref[CMP_STRIDE + l])
        pre = lo + pltpu.roll(hi, N_CMP_PAD - 1, 0)
        o_ref[0] = _dot(_gelu_tanh(pre).astype(BF16), w2_ref[...]).astype(o_ref.dtype)


def nsa_compress(hf, pk, pv, w1k, w1v, w2k, w2v):
    B, S, _ = hf.shape
    full = lambda a: pl.BlockSpec(a.shape, lambda b: (0,) * a.ndim)
    out = pl.BlockSpec((1, N_CMP_PAD, LANES), lambda b: (b, 0, 0))
    return pl.pallas_call(
        _nsa_compress_body,
        grid=(B,),
        in_specs=[pl.BlockSpec((1, S, LANES), lambda b: (b, 0, NSA_KC_BLK)),
                  pl.BlockSpec((1, S, LANES), lambda b: (b, 0, NSA_VC_BLK)),
                  full(pk), full(pv), full(w1k), full(w1v), full(w2k), full(w2v)],
        out_specs=[out, out],
        out_shape=[jax.ShapeDtypeStruct((B, N_CMP_PAD, LANES), BF16)] * 2,
        compiler_params=_cp("parallel"), name="nsa_compress")(hf, hf, pk, pv, w1k, w1v, w2k, w2v)


def _nsa_slope(g, j):
    return 2.0 ** (-(g * (NSA_HEADS // NSA_GROUPS) + j + 1))


def _nsa_cmp_body(q_ref, kc_ref, vc_ref, ov_ref, o_ref, selt_ref, sct_ref, *, tq):
    hg = NSA_HEADS // NSA_GROUPS
    qs = pl.program_id(1) * tq
    lane = _lane((1, LANES))
    tpos = qs + lax.broadcasted_iota(I32, (tq, LANES), 0)
    dist = tpos - (lane * CMP_STRIDE + (CMP_LEN - 1))
    valid = (dist >= 0) & (lane < N_CMP_PAD - 1)
    validf = valid.astype(F32)
    distf = dist.astype(F32)
    kc = kc_ref[0]
    vc = vc_ref[0]
    imp = jnp.zeros((tq, LANES), F32)
    blocks = [None] * hg
    for g in range(NSA_GROUPS):
        in_g = (lane // HALF) == g
        for j in range(hg):
            qb = q_ref[0, :, j * LANES:(j + 1) * LANES]
            s = _dot_nt(jnp.where(in_g, qb, jnp.zeros_like(qb)), kc) - _nsa_slope(g, j) * distf
            s = jnp.where(valid, s, NEG)
            p = jnp.exp(s - jnp.max(s, axis=-1, keepdims=True)) * validf
            p = (p / jnp.maximum(jnp.sum(p, axis=-1, keepdims=True), TINY)).astype(BF16)
            o = _dot(p, vc)
            blocks[j] = o if g == 0 else jnp.where(in_g, o, blocks[j])
            imp = imp + _dot(p, ov_ref[g])
    for j in range(hg):
        o_ref[0, :, j * LANES:(j + 1) * LANES] = blocks[j]
    blk = lane % N_SLC
    cur = tpos // SLC_LEN
    score = jnp.where((blk == cur) | (blk == 0), FORCE_SCORE, jnp.where(blk <= cur, imp, NEG))
    sct_ref[...] = score.T
    row8 = lax.broadcasted_iota(I32, (8, tq), 0)
    for g in range(NSA_GROUPS):
        base = g * N_SLC
        tiles = [sct_ref[base + 8 * v:base + 8 * (v + 1), :] for v in range(N_SLC // 8)]
        beaten = [jnp.zeros((8, tq), F32) for _ in tiles]
        for i in range(N_SLC):
            cand = jnp.broadcast_to(sct_ref[base + i:base + i + 1, :], (8, tq))
            for v, tile in enumerate(tiles):
                ge = jnp.where(cand >= tile, 1.0, 0.0)
                gt = jnp.where(cand > tile, 1.0, 0.0)
                if 8 * v > i:
                    wins = ge
                elif 8 * v + 7 <= i:
                    wins = gt
                else:
                    wins = jnp.where(row8 > i - 8 * v, ge, gt)
                beaten[v] = beaten[v] + wins
        for v in range(N_SLC // 8):
            selt_ref[0, base + 8 * v:base + 8 * (v + 1), :] = jnp.where(beaten[v] < N_SEL, 1.0, 0.0).astype(selt_ref.dtype)
    selt_ref[0, NSA_GROUPS * N_SLC:LANES, :] = jnp.zeros((LANES - NSA_GROUPS * N_SLC, tq), selt_ref.dtype)


def nsa_cmp(hb, kc, vc, ov, tq):
    B, S, _ = hb.shape
    qw = NSA_HEADS * NSA_DH
    return pl.pallas_call(
        functools.partial(_nsa_cmp_body, tq=tq),
        grid=(B, S // tq),
        in_specs=[pl.BlockSpec((1, tq, qw), lambda b, i: (b, i, NSA_Q_BLK)),
                  pl.BlockSpec((1, N_CMP_PAD, LANES), lambda b, i: (b, 0, 0)),
                  pl.BlockSpec((1, N_CMP_PAD, LANES), lambda b, i: (b, 0, 0)),
                  pl.BlockSpec(ov.shape, lambda b, i: (0, 0, 0))],
        out_specs=[pl.BlockSpec((1, tq, qw), lambda b, i: (b, i, 0)),
                   pl.BlockSpec((1, LANES, tq), lambda b, i: (b, 0, i))],
        out_shape=[jax.ShapeDtypeStruct((B, S, qw), F32), jax.ShapeDtypeStruct((B, LANES, S), BF16)],
        scratch_shapes=[pltpu.VMEM((LANES, tq), F32)],
        compiler_params=_cp("parallel", "parallel"), name="nsa_cmp")(hb, kc, vc, ov)


def _softmax_step(s, maskf, v, m, l, acc):
    m_new = jnp.maximum(m, jnp.max(s, axis=-1, keepdims=True))
    p = jnp.exp(s - m_new) * maskf
    alpha = jnp.exp(m - m_new)
    l = alpha * l + jnp.sum(p, axis=-1, keepdims=True)
    acc = alpha * acc + _dot(p.astype(BF16), v)
    return m_new, l, acc


FFN_TF = D_FF // 2
ROW_CHUNK = 32
LOG2E = math.log2(math.e)


FLASH_SLOTS = 2


def _flash_scratch(rows, tk):
    return ([pltpu.VMEM((rows, tk), F32)] * FLASH_SLOTS + [pltpu.VMEM((rows, tk), BF16)] * FLASH_SLOTS
            + [pltpu.VMEM((rows, LANES), F32)] * (FLASH_SLOTS + 3))


def _put_scores(dst, r0, raw, adjust):
    for i in range(raw.shape[0] // ROW_CHUNK):
        r = r0 + i * ROW_CHUNK
        dst[r:r + ROW_CHUNK, :] = adjust(r // ROW_CHUNK, raw[i * ROW_CHUNK:(i + 1) * ROW_CHUNK, :]) * LOG2E


def _softmax_pass1(s_ref, m_ref, a_ref):
    rows, tk = s_ref.shape
    m_in = m_out = m_ref
    for c in range(rows // ROW_CHUNK):
        r = slice(c * ROW_CHUNK, (c + 1) * ROW_CHUNK)
        s = s_ref[r, :]
        top = s[:, 0:LANES]
        for j in range(1, tk // LANES):
            top = jnp.maximum(top, s[:, j * LANES:(j + 1) * LANES])
        m_old = m_in[r, :]
        m_new = jnp.maximum(m_old, jnp.broadcast_to(jnp.max(top, axis=-1, keepdims=True), (ROW_CHUNK, LANES)))
        m_out[r, :] = m_new
        a_ref[r, :] = jnp.exp2(m_old - m_new)


def _softmax_pass2(s_ref, p_ref, m_ref, a_ref, l_ref):
    rows, tk = s_ref.shape
    for c in range(rows // ROW_CHUNK):
        r = slice(c * ROW_CHUNK, (c + 1) * ROW_CHUNK)
        m_new = m_ref[r, :]
        part = None
        for j in range(tk // LANES):
            p = jnp.exp2(s_ref[r, j * LANES:(j + 1) * LANES] - m_new)
            p_ref[r, j * LANES:(j + 1) * LANES] = p.astype(BF16)
            part = p if part is None else part + p
        l_ref[r, :] = a_ref[r, :] * l_ref[r, :] + part


def _flash_causal(n_full, tk, scores, pv_dot, scratch):
    n = FLASH_SLOTS
    s_bufs, p_bufs, a_bufs = (scratch[i * n:(i + 1) * n] for i in range(3))
    m_ref, l_ref, acc_ref = scratch[3 * n:]
    m_ref[...] = jnp.full(m_ref.shape, NEG, F32)
    a_bufs[n - 1][...] = jnp.zeros_like(a_bufs[0])
    p_bufs[n - 1][...] = jnp.zeros_like(p_bufs[0])
    l_ref[...] = jnp.zeros_like(l_ref)
    acc_ref[...] = jnp.zeros_like(acc_ref)

    def k_of(t):
        return pl.multiple_of(t * tk, tk)

    def pv(u, t):
        acc_ref[...] = a_bufs[u][...] * acc_ref[...] + pv_dot(p_bufs[u], k_of(t))

    NEXT_FULL, NEXT_LAST, FINAL = range(3)

    def stage(u, t, kind):
        pv((u - 1) % n, jnp.maximum(t - 1, 0))
        if kind != FINAL:
            scores(k_of(t + 1), s_bufs[(u + 1) % n], kind == NEXT_LAST)
        _softmax_pass1(s_bufs[u], m_ref, a_bufs[u])
        _softmax_pass2(s_bufs[u], p_bufs[u], m_ref, a_bufs[u], l_ref)
        if kind == FINAL:
            pv(u, t)

    def by_slot(t, kind):
        for u in range(n):
            @pl.when(t % n == u)
            def _():
                stage(u, t, kind)

    for first_is_last in (False, True):
        @pl.when((n_full == 0) == first_is_last)
        def _():
            scores(k_of(0), s_bufs[0], first_is_last)

    def body(t, carry):
        by_slot(t, NEXT_FULL)
        return carry

    lax.fori_loop(0, n_full - 1, body, 0)

    @pl.when(n_full >= 1)
    def _():
        by_slot(n_full - 1, NEXT_LAST)

    by_slot(n_full, FINAL)
    return acc_ref[...] / jnp.maximum(jnp.sum(l_ref[...], axis=-1, keepdims=True), TINY)


def _nsa_attn_body(q_ref, ks_ref, vs_ref, kw_ref, vw_ref, sel_ref, oc_ref, gt_ref, es_ref, o_ref,
                   selb_ref, ws_ref, wp_ref, wa_ref, wm_ref, wl_ref, *scratch, tq, tk):
    hg = NSA_HEADS // NSA_GROUPS
    span = WINDOW + tq
    qs = pl.program_id(1) * tq
    lane = _lane((1, LANES))
    first = lane < HALF
    zero = jnp.zeros((tq, LANES), BF16)
    q_all = jnp.concatenate(
        [jnp.where((lane // HALF) == g, q_ref[0, :, j * LANES:(j + 1) * LANES], zero)
         for g in range(NSA_GROUPS) for j in range(hg)], axis=0)
    for g in range(NSA_GROUPS):
        picked = lax.dot_general(sel_ref[0], es_ref[g], (((0,), (0,)), ((), ())), preferred_element_type=F32)
        selb_ref[g] = (picked - 1.0) * (-NEG)

    def t_of(c):
        head, r0 = divmod(c * ROW_CHUNK, tq)
        g, j = divmod(head, hg)
        return g, r0, qs + r0 + lax.broadcasted_iota(I32, (ROW_CHUNK, 1), 0), _nsa_slope(g, j)

    def slc_scores(k0, dst, last):
        kpos = k0 + _lane((1, tk))

        def adjust(c, s):
            g, r0, t, slope = t_of(c)
            s = s + slope * kpos.astype(F32) + selb_ref[g, r0:r0 + ROW_CHUNK, pl.ds(k0, tk)]
            return jnp.where(t >= kpos, s, NEG) if last else s

        _put_scores(dst, 0, _dot_nt(q_all, ks_ref[0, pl.ds(k0, tk), :]), adjust)

    o_slc = _flash_causal(qs // tk, tk, slc_scores,
                          lambda p_ref, k0: _dot(p_ref[...], vs_ref[0, pl.ds(k0, tk), :]), scratch)

    w0 = pl.multiple_of(jnp.maximum(qs - WINDOW, 0), tq)
    wpos = w0 + _lane((1, span))

    def win_adjust(c, s):
        _, _, t, slope = t_of(c)
        dist = t - wpos
        return jnp.where((dist >= 0) & (dist < WINDOW), s + slope * wpos.astype(F32), NEG)

    _put_scores(ws_ref, 0, _dot_nt(q_all, kw_ref[0, pl.ds(w0, span), :]), win_adjust)
    wm_ref[...] = jnp.full(wm_ref.shape, NEG, F32)
    wl_ref[...] = jnp.zeros_like(wl_ref)
    _softmax_pass1(ws_ref, wm_ref, wa_ref)
    _softmax_pass2(ws_ref, wp_ref, wm_ref, wa_ref, wl_ref)
    o_win = _dot(wp_ref[...], vw_ref[0, pl.ds(w0, span), :]) / jnp.maximum(
        jnp.sum(wl_ref[...], axis=-1, keepdims=True), TINY)

    gates = jax.nn.sigmoid(gt_ref[0])
    for j in range(hg):
        out = jnp.zeros((tq, LANES), F32)
        for br, branch in enumerate((None, o_slc, o_win)):
            gm = jnp.where(first, gates[:, j * 3 + br:j * 3 + br + 1],
                           gates[:, hg * 3 + j * 3 + br:hg * 3 + j * 3 + br + 1])
            if branch is None:
                val = oc_ref[0, :, j * LANES:(j + 1) * LANES]
            else:
                val = jnp.where(first, branch[j * tq:(j + 1) * tq], branch[(hg + j) * tq:(hg + j + 1) * tq])
            out = out + gm * val
        o_ref[0, :, j * LANES:(j + 1) * LANES] = out.astype(o_ref.dtype)


def nsa_attn(hb, hf, sel, ocmp, esel, tq, tk):
    B, S, _ = hb.shape
    qw = NSA_HEADS * NSA_DH
    rows = NSA_HEADS * tq
    span = WINDOW + tq
    kv = lambda blk: pl.BlockSpec((1, S, LANES), lambda b, i: (b, 0, blk))
    return pl.pallas_call(
        functools.partial(_nsa_attn_body, tq=tq, tk=tk),
        grid=(B, S // tq),
        in_specs=[pl.BlockSpec((1, tq, qw), lambda b, i: (b, i, NSA_Q_BLK)),
                  kv(NSA_KS_BLK), kv(NSA_VS_BLK), kv(NSA_KW_BLK), kv(NSA_VW_BLK),
                  pl.BlockSpec((1, LANES, tq), lambda b, i: (b, 0, i)),
                  pl.BlockSpec((1, tq, qw), lambda b, i: (b, i, 0)),
                  pl.BlockSpec((1, tq, LANES), lambda b, i: (b, i, NSA_GATE_BLK)),
                  pl.BlockSpec(esel.shape, lambda b, i: (0, 0, 0))],
        out_specs=pl.BlockSpec((1, tq, qw), lambda b, i: (b, i, 0)),
        out_shape=jax.ShapeDtypeStruct((B, S, qw), BF16),
        scratch_shapes=([pltpu.VMEM((NSA_GROUPS, tq, S), F32), pltpu.VMEM((rows, span), F32),
                         pltpu.VMEM((rows, span), BF16)]
                        + [pltpu.VMEM((rows, LANES), F32)] * 3 + _flash_scratch(rows, tk)),
        compiler_params=_cp("parallel", "parallel"), name="nsa_attn")(
            hb, hb, hb, hb, hb, sel, ocmp, hf, esel)


def _nsa_tables(S):
    n_cmp = (S - CMP_LEN) // CMP_STRIDE + 1
    c_start = np.arange(n_cmp) * CMP_STRIDE
    c_end = c_start + CMP_LEN - 1
    j_start = np.arange(S // SLC_LEN) * SLC_LEN
    overlap = (c_end[:, None] >= j_start[None]) & (c_start[:, None] <= j_start[None] + SLC_LEN - 1)
    ov = np.zeros((NSA_GROUPS, N_CMP_PAD, LANES), np.float32)
    es = np.zeros((NSA_GROUPS, LANES, S), np.float32)
    for g in range(NSA_GROUPS):
        ov[g, :n_cmp, g * N_SLC:(g + 1) * N_SLC] = overlap
        es[g, g * N_SLC + np.arange(S) // SLC_LEN, np.arange(S)] = 1.0
    return jnp.asarray(ov, BF16), jnp.asarray(es, BF16)


NSA_PERM = np.array([(half * (NSA_HEADS // NSA_GROUPS) + j) * NSA_DH + d
                     for j in range(NSA_HEADS // NSA_GROUPS) for half in range(2) for d in range(NSA_DH)])


def nsa_mixer(hb, hf, pos_k, w1_k, w2_k, pos_v, w1_v, w2_v):
    B, S, _ = hb.shape
    assert S // CMP_STRIDE == N_CMP_PAD

    def both_groups(w):
        z = jnp.zeros_like(w)
        return jnp.concatenate([jnp.concatenate([w, z], axis=-1), jnp.concatenate([z, w], axis=-1)], axis=-2)

    def w1_blocks(w1):
        return both_groups(w1.reshape(CMP_LEN, NSA_DH, CMP_HIDDEN)).astype(BF16)

    kc, vc = nsa_compress(hf, jnp.tile(pos_k, (1, NSA_GROUPS)), jnp.tile(pos_v, (1, NSA_GROUPS)),
                          w1_blocks(w1_k), w1_blocks(w1_v),
                          both_groups(w2_k).astype(BF16), both_groups(w2_v).astype(BF16))
    ov, es = _nsa_tables(S)
    ocmp, sel = nsa_cmp(hb, kc, vc, ov, 256)
    return nsa_attn(hb, hf, sel, ocmp, es, 128, 256)


def _diff_body(slope_ref, q1_ref, q2_ref, k1_ref, k2_ref, v_ref, lam_ref, gn_ref, o_ref, *scratch,
               tq, tk, lambda_init):
    pair = pl.program_id(1)
    qs = pl.program_id(2) * tq
    in_a = _lane((1, LANES)) < HALF
    zero = jnp.zeros((tq, LANES), BF16)
    q1, q2 = q1_ref[0], q2_ref[0]
    qa = jnp.concatenate([jnp.where(in_a, q1, zero), jnp.where(in_a, q2, zero)], axis=0)
    qb = jnp.concatenate([jnp.where(in_a, zero, q1), jnp.where(in_a, zero, q2)], axis=0)
    slopes = (slope_ref[2 * pair], slope_ref[2 * pair + 1])

    def scores(k0, dst, last):
        k1 = k1_ref[0, pl.ds(k0, tk), :]
        k2 = k2_ref[0, pl.ds(k0, tk), :]
        kpos = k0 + _lane((1, tk))

        def adjust(c, s):
            s = s + slopes[(c * ROW_CHUNK) // (2 * tq)] * kpos.astype(F32)
            if last:
                t = qs + (c * ROW_CHUNK) % tq + lax.broadcasted_iota(I32, (ROW_CHUNK, 1), 0)
                s = jnp.where(t >= kpos, s, NEG)
            return s

        for hh, q in enumerate((qa, qb)):
            _put_scores(dst, (2 * hh) * tq, _dot_nt(q[0:tq], k1), adjust)
            _put_scores(dst, (2 * hh + 1) * tq, _dot_nt(q[tq:2 * tq], k2), adjust)

    def pv_dot(p_ref, k0):
        return jnp.concatenate(
            [_dot(p_ref[2 * hh * tq:2 * (hh + 1) * tq, :], v_ref[0, pl.ds(k0, tk), hh * LANES:(hh + 1) * LANES])
             for hh in range(2)], axis=0)

    o = _flash_causal(qs // tk, tk, scores, pv_dot, scratch)
    lam_rows = lam_ref[...]
    lam = (jnp.exp(jnp.sum(lam_rows[0:1] * lam_rows[1:2], axis=-1, keepdims=True))
           - jnp.exp(jnp.sum(lam_rows[2:3] * lam_rows[3:4], axis=-1, keepdims=True)) + lambda_init)
    for hh in range(2):
        r0 = 2 * hh * tq
        y = _rms(o[r0:r0 + tq] - lam * o[r0 + tq:r0 + 2 * tq], gn_ref[...]) * (1.0 - lambda_init)
        o_ref[0, :, hh * LANES:(hh + 1) * LANES] = y.astype(o_ref.dtype)


def diff_mixer(hd, lam_rows, g_norm, lambda_init, tq, tk):
    B, S, _ = hd.shape
    nb = DIFF_HEADS // 2
    slopes = jnp.asarray(2.0 ** (-8.0 * np.arange(1, DIFF_HEADS + 1) / DIFF_HEADS), F32)
    grid_spec = pltpu.PrefetchScalarGridSpec(
        num_scalar_prefetch=1,
        grid=(B, nb, S // tq),
        in_specs=[pl.BlockSpec((1, tq, LANES), lambda b, p, i, s: (b, i, p)),
                  pl.BlockSpec((1, tq, LANES), lambda b, p, i, s: (b, i, nb + p)),
                  pl.BlockSpec((1, S, LANES), lambda b, p, i, s: (b, 0, 2 * nb + p)),
                  pl.BlockSpec((1, S, LANES), lambda b, p, i, s: (b, 0, 3 * nb + p)),
                  pl.BlockSpec((1, S, 2 * LANES), lambda b, p, i, s: (b, 0, 2 * nb + p)),
                  pl.BlockSpec((4, LANES), lambda b, p, i, s: (0, 0)),
                  pl.BlockSpec((1, LANES), lambda b, p, i, s: (0, 0))],
        out_specs=pl.BlockSpec((1, tq, 2 * LANES), lambda b, p, i, s: (b, i, p)),
        scratch_shapes=_flash_scratch(4 * tq, tk))
    return pl.pallas_call(
        functools.partial(_diff_body, tq=tq, tk=tk, lambda_init=lambda_init),
        grid_spec=grid_spec,
        out_shape=jax.ShapeDtypeStruct((B, S, DIFF_HEADS * DIFF_DV), BF16),
        compiler_params=_cp("parallel", "parallel", "parallel"), name="diff_attn")(
            slopes, hd, hd, hd, hd, hd, lam_rows, g_norm.reshape(1, LANES))


def _router_body(x_ref, g_ref, wr_ref, tri_ref, xn_ref, meta_ref, wts_ref, cnt_ref, run_ref):
    @pl.when(pl.program_id(0) == 0)
    def _():
        run_ref[...] = jnp.zeros_like(run_ref)

    xn = _rms(x_ref[...], g_ref[...])
    xn_ref[...] = xn
    lane = _lane((1, LANES))
    logits = jnp.dot(xn, wr_ref[...], precision=lax.Precision.HIGHEST, preferred_element_type=F32)
    logits = jnp.where(lane < N_EXPERTS, logits, LOWEST)
    m1 = jnp.max(logits, axis=-1, keepdims=True)
    i1 = jnp.min(jnp.where(logits == m1, lane, LANES), axis=-1, keepdims=True)
    rest = jnp.where(lane == i1, LOWEST, logits)
    m2 = jnp.max(rest, axis=-1, keepdims=True)
    i2 = jnp.min(jnp.where(rest == m2, lane, LANES), axis=-1, keepdims=True)
    e = jnp.exp(m2 - m1)
    w1 = 1.0 / (1.0 + e)
    w2 = e / (1.0 + e)
    onehot = ((lane == i1) | (lane == i2)).astype(F32)
    pos = _dot(tri_ref[...], onehot.astype(BF16)) + run_ref[...]
    p1 = jnp.sum(jnp.where(lane == i1, pos, 0.0), axis=-1, keepdims=True).astype(I32)
    p2 = jnp.sum(jnp.where(lane == i2, pos, 0.0), axis=-1, keepdims=True).astype(I32)
    run_ref[...] += jnp.sum(onehot, axis=0, keepdims=True)
    meta_ref[...] = jnp.where(lane == 0, i1, jnp.where(lane == 1, i2, jnp.where(lane == 2, p1, jnp.where(lane == 3, p2, 0))))
    wts_ref[...] = jnp.where(lane == 0, w1, jnp.where(lane == 1, w2, 0.0))
    cnt_ref[...] = run_ref[...]


def moe_router(x, g, wr, tm):
    T, D = x.shape
    tri = jnp.asarray(np.tril(np.ones((tm, tm), np.float32), -1), BF16)
    return pl.pallas_call(
        _router_body,
        grid=(T // tm,),
        in_specs=[pl.BlockSpec((tm, D), lambda i: (i, 0)),
                  pl.BlockSpec((1, D), lambda i: (0, 0)),
                  pl.BlockSpec((D, LANES), lambda i: (0, 0)),
                  pl.BlockSpec((tm, tm), lambda i: (0, 0))],
        out_specs=[pl.BlockSpec((tm, D), lambda i: (i, 0)),
                   pl.BlockSpec((tm, LANES), lambda i: (i, 0)),
                   pl.BlockSpec((tm, LANES), lambda i: (i, 0)),
                   pl.BlockSpec((1, LANES), lambda i: (0, 0))],
        out_shape=[jax.ShapeDtypeStruct((T, D), F32), jax.ShapeDtypeStruct((T, LANES), I32),
                   jax.ShapeDtypeStruct((T, LANES), F32), jax.ShapeDtypeStruct((1, LANES), F32)],
        scratch_shapes=[pltpu.VMEM((1, LANES), F32)],
        compiler_params=_cp("arbitrary"), name="moe_router")(x, g.reshape(1, D), wr, tri)


ROW_DMA_UNROLL = 8


def _row_copy(src_ref, src_row, dst_ref, dst_row, sem):
    return pltpu.make_async_copy(src_ref.at[pl.ds(src_row, 1)], dst_ref.at[pl.ds(dst_row, 1)], sem)


def _dispatch_body(dest_ref, xn_ref, zero_ref, xs_ref, sem, *, tm):
    del zero_ref
    base = pl.program_id(0) * tm

    def issue(t, c):
        for slot in range(TOP_K):
            _row_copy(xn_ref, t, xs_ref, dest_ref[TOP_K * (base + t) + slot], sem).start()
        return c

    lax.fori_loop(0, tm, issue, 0, unroll=ROW_DMA_UNROLL)

    def drain(t, c):
        for slot in range(TOP_K):
            _row_copy(xn_ref, 0, xs_ref, 0, sem).wait()
        return c

    lax.fori_loop(0, tm, drain, 0, unroll=ROW_DMA_UNROLL)


def moe_dispatch(dest, xn, rows, tm):
    T, D = xn.shape
    grid_spec = pltpu.PrefetchScalarGridSpec(
        num_scalar_prefetch=1,
        grid=(T // tm,),
        in_specs=[pl.BlockSpec((tm, D), lambda i, d: (i, 0)),
                  pl.BlockSpec(memory_space=pl.ANY)],
        out_specs=pl.BlockSpec(memory_space=pl.ANY),
        scratch_shapes=[pltpu.SemaphoreType.DMA])
    return pl.pallas_call(
        functools.partial(_dispatch_body, tm=tm),
        grid_spec=grid_spec,
        out_shape=jax.ShapeDtypeStruct((rows, D), F32),
        input_output_aliases={2: 0},
        compiler_params=_cp("arbitrary"), name="moe_dispatch")(dest, xn, jnp.zeros((rows, D), F32))


def _experts_body(te_ref, tv_ref, xs_ref, wg_ref, wu_ref, wd_ref, o_ref, xb_ref):
    i = pl.program_id(0)

    @pl.when(pl.program_id(1) == 0)
    def _():
        xb_ref[...] = xs_ref[...].astype(BF16)
        o_ref[...] = jnp.zeros_like(o_ref)

    @pl.when(tv_ref[i] > 0)
    def _():
        xb = xb_ref[...]
        hg = _dot(xb, wg_ref[0])
        hu = _dot(xb, wu_ref[0])
        o_ref[...] += _dot((hg * jax.nn.sigmoid(hg) * hu).astype(BF16), wd_ref[0])


def moe_experts(tile_expert, tile_valid, xs, wg, wu, wd, tr, tf):
    R, D = xs.shape
    F = wg.shape[2]
    grid_spec = pltpu.PrefetchScalarGridSpec(
        num_scalar_prefetch=2,
        grid=(R // tr, F // tf),
        in_specs=[pl.BlockSpec((tr, D), lambda i, k, te, tv: (i, 0)),
                  pl.BlockSpec((1, D, tf), lambda i, k, te, tv: (te[i], 0, k)),
                  pl.BlockSpec((1, D, tf), lambda i, k, te, tv: (te[i], 0, k)),
                  pl.BlockSpec((1, tf, D), lambda i, k, te, tv: (te[i], k, 0))],
        out_specs=pl.BlockSpec((tr, D), lambda i, k, te, tv: (i, 0)),
        scratch_shapes=[pltpu.VMEM((tr, D), BF16)])
    return pl.pallas_call(
        _experts_body,
        grid_spec=grid_spec,
        out_shape=jax.ShapeDtypeStruct((R, D), F32),
        compiler_params=_cp("parallel", "arbitrary"), name="moe_experts")(tile_expert, tile_valid, xs, wg, wu, wd)


def _combine_body(dest_ref, x_ref, wts_ref, g_ref, ys_ref, o_ref, buf_ref, sem, *, tm):
    base = pl.program_id(0) * tm

    def issue(t, c):
        for slot in range(TOP_K):
            _row_copy(ys_ref, dest_ref[TOP_K * (base + t) + slot], buf_ref.at[slot], t, sem).start()
        return c

    lax.fori_loop(0, tm, issue, 0, unroll=ROW_DMA_UNROLL)

    def drain(t, c):
        for slot in range(TOP_K):
            _row_copy(ys_ref, 0, buf_ref.at[slot], 0, sem).wait()
        return c

    lax.fori_loop(0, tm, drain, 0, unroll=ROW_DMA_UNROLL)
    wts = wts_ref[...]
    y = x_ref[...] + (wts[:, 0:1] * buf_ref[0] + wts[:, 1:2] * buf_ref[1])
    o_ref[...] = _rms(y, g_ref[...])


def moe_combine(dest, x, wts, g_final, ys, tm):
    T, D = x.shape
    grid_spec = pltpu.PrefetchScalarGridSpec(
        num_scalar_prefetch=1,
        grid=(T // tm,),
        in_specs=[pl.BlockSpec((tm, D), lambda i, d: (i, 0)),
                  pl.BlockSpec((tm, LANES), lambda i, d: (i, 0)),
                  pl.BlockSpec((1, D), lambda i, d: (0, 0)),
                  pl.BlockSpec(memory_space=pl.ANY)],
        out_specs=pl.BlockSpec((tm, D), lambda i, d: (i, 0)),
        scratch_shapes=[pltpu.VMEM((TOP_K, tm, D), F32), pltpu.SemaphoreType.DMA])
    return pl.pallas_call(
        functools.partial(_combine_body, tm=tm),
        grid_spec=grid_spec,
        out_shape=jax.ShapeDtypeStruct((T, D), F32),
        compiler_params=_cp("arbitrary"), name="moe_combine")(dest, x, wts, g_final.reshape(1, D), ys)


def moe_block(x, g, w_router, wg, wu, wd, g_final, tr):
    T, D = x.shape
    wr = _pad_cols(w_router, LANES)
    xn, meta, wts, counts = moe_router(x, g, wr, 512)
    cnt = counts[0, :N_EXPERTS].astype(I32)
    gsz = ((cnt + tr - 1) // tr) * tr
    gend = jnp.cumsum(gsz)
    goff = gend - gsz
    dest = (jnp.take(goff, meta[:, 0:TOP_K]) + meta[:, TOP_K:2 * TOP_K]).reshape(-1)
    n_tiles = (TOP_K * T) // tr + N_EXPERTS
    tile_start = jnp.arange(n_tiles, dtype=I32) * tr
    tile_valid = (tile_start < gend[-1]).astype(I32)
    tile_expert = jnp.minimum(jnp.sum((tile_start[:, None] >= gend[None, :]).astype(I32), axis=1), N_EXPERTS - 1)
    last_valid = jnp.take(tile_expert, jnp.maximum(gend[-1] // tr - 1, 0))
    tile_expert = jnp.where(tile_valid > 0, tile_expert, last_valid)
    xs = moe_dispatch(dest, xn, n_tiles * tr, 256)
    ys = moe_experts(tile_expert, tile_valid, xs, wg, wu, wd, tr, FFN_TF)
    return moe_combine(dest, x, wts, g_final, ys, 256)


def _pad_cols(w, width):
    return jnp.pad(w, ((0, 0), (0, width - w.shape[1])))


def _mem_kv(mem2, g, wk, wv, batch):
    w = jnp.concatenate([wk, wv], axis=1).astype(BF16)
    width = wk.shape[1]
    k, v = norm_proj(mem2, g, w, [(width, BF16), (width, BF16)], 256, "mem_kv")
    return k.reshape(batch, -1, width), v.reshape(batch, -1, width)


def kernel(x, mem, mix_norm_0, w_in_0, gla_w_alpha_0, gla_b_alpha_0, gla_out_norm_0, nsa_cmp_pos_k_0, nsa_cmp_w1_k_0, nsa_cmp_w2_k_0, nsa_cmp_pos_v_0, nsa_cmp_w1_v_0, nsa_cmp_w2_v_0, w_out_0, xattn_norm_0, xattn_mem_norm_0, xattn_wq_0, xattn_wk_0, xattn_wv_0, xattn_wo_0, ffn_norm_0, ffn_w_gate_0, ffn_w_up_0, ffn_w_down_0, mix_norm_1, w_in_1, diff_lq1_1, diff_lk1_1, diff_lq2_1, diff_lk2_1, diff_out_norm_1, w_out_1, xattn_norm_1, xattn_mem_norm_1, xattn_wq_1, xattn_wk_1, xattn_wv_1, xattn_wo_1, ffn_norm_1, moe_router_1, moe_w_gate_1, moe_w_up_1, moe_w_down_1, final_norm):
    B, S, D = x.shape
    T = B * S
    x2 = x.reshape(T, D)
    mem2 = mem.reshape(-1, D)
    q_scale = NSA_DH ** -0.5

    sizes = [GLA_HEADS * GLA_DK, GLA_HEADS * GLA_DK, GLA_HEADS * GLA_DV, GLA_HEADS * GLA_DV, GLA_RANK,
             NSA_HEADS * NSA_DH] + [NSA_GROUPS * NSA_DH] * 6 + [3 * NSA_HEADS]
    (g_q, g_k, g_v, g_r, g_a, n_q, n_kc, n_vc, n_ks, n_vs, n_kw, n_vw, n_g) = jnp.split(
        w_in_0, np.cumsum(sizes)[:-1].tolist(), axis=1)
    w0 = jnp.concatenate([n_q[:, NSA_PERM] * q_scale, n_ks, n_vs, n_kw, n_vw, g_q * q_scale, g_k, g_v,
                          _pad_cols(g_a, LANES), g_r, _pad_cols(n_g, LANES), n_kc, n_vc], axis=1).astype(BF16)
    hb, hf = norm_proj(x2, mix_norm_0, w0, [(HB0_WIDTH, BF16), (HF0_WIDTH, F32)], 512, "in_proj0")
    hb = hb.reshape(B, S, HB0_WIDTH)
    hf = hf.reshape(B, S, HF0_WIDTH)
    w_alpha = jnp.pad(gla_w_alpha_0, ((0, LANES - GLA_RANK), (0, 0))).astype(BF16)
    o_a = gla_mixer(hb, hf, w_alpha, gla_b_alpha_0.reshape(1, -1), gla_out_norm_0, GLA_COLS, 8)
    o_b = nsa_mixer(hb, hf, nsa_cmp_pos_k_0, nsa_cmp_w1_k_0, nsa_cmp_w2_k_0,
                    nsa_cmp_pos_v_0, nsa_cmp_w1_v_0, nsa_cmp_w2_v_0)
    n_a = GLA_HEADS * GLA_DV
    k0, v0 = _mem_kv(mem2, xattn_mem_norm_0, xattn_wk_0, xattn_wv_0, B)
    x2 = proj_xattn_block(x2, [o_a.reshape(T, -1), o_b.reshape(T, -1)],
                          [w_out_0[:n_a].astype(BF16), w_out_0[n_a:][NSA_PERM].astype(BF16)],
                          xattn_norm_0, (xattn_wq_0 * q_scale).astype(BF16), k0, v0, xattn_wo_0.astype(BF16),
                          S, 512, "out_xattn0")
    x2 = ffn_block(x2, ffn_norm_0, ffn_w_gate_0.astype(BF16), ffn_w_up_0.astype(BF16), ffn_w_down_0.astype(BF16),
                   512, FFN_TF, "ffn0")

    lambda_init = 0.8 - 0.6 * math.exp(-0.3 * 1)
    n_q1 = 2 * DIFF_HEADS * DIFF_DH
    w1 = jnp.concatenate([w_in_1[:, :n_q1] * q_scale, w_in_1[:, n_q1:]], axis=1).astype(BF16)
    (hd,) = norm_proj(x2, mix_norm_1, w1, [(w1.shape[1], BF16)], 512, "in_proj1")
    lam_rows = _pad_cols(jnp.stack([diff_lq1_1, diff_lk1_1, diff_lq2_1, diff_lk2_1]), LANES)
    o_d = diff_mixer(hd.reshape(B, S, -1), lam_rows, diff_out_norm_1, lambda_init, 256, 256)
    k1, v1 = _mem_kv(mem2, xattn_mem_norm_1, xattn_wk_1, xattn_wv_1, B)
    x2 = proj_xattn_block(x2, [o_d.reshape(T, -1)], [w_out_1.astype(BF16)],
                          xattn_norm_1, (xattn_wq_1 * q_scale).astype(BF16), k1, v1, xattn_wo_1.astype(BF16),
                          S, 512, "out_xattn1")
    out = moe_block(x2, ffn_norm_1, moe_router_1, moe_w_gate_1.astype(BF16), moe_w_up_1.astype(BF16),
                    moe_w_down_1.astype(BF16), final_norm, 512)
    return out.reshape(B, S, D)
```

```python
import functools
import math

import numpy as np
import jax
import jax.numpy as jnp
from jax import lax
from jax.experimental import pallas as pl
from jax.experimental.pallas import tpu as pltpu

F32 = jnp.float32
BF16 = jnp.bfloat16
I32 = jnp.int32

D_MODEL = 1024
GLA_HEADS, GLA_DK, GLA_DV, GLA_RANK, GLA_TAU, GLA_CHUNK = 4, 64, 128, 16, 16.0, 64
NSA_HEADS, NSA_GROUPS, NSA_DH = 8, 2, 64
CMP_LEN, CMP_STRIDE, CMP_HIDDEN, SLC_LEN, N_SEL, WINDOW = 32, 16, 256, 64, 8, 512
DIFF_HEADS, DIFF_DH, DIFF_DV = 8, 64, 128
XATTN_HEADS, XATTN_DH = 4, 64
D_FF, N_EXPERTS, TOP_K = 2816, 8, 2
EPS, NEG, TINY, FORCE_SCORE = 1e-6, -1e30, 1e-30, 1e4

LANES = 128
HALF = LANES // 2
VMEM_LIMIT = 56 * 1024 * 1024
LOWEST = -3.0e38

HB0_WIDTH, HF0_WIDTH = 17 * LANES, 7 * LANES
NSA_Q_BLK, NSA_KS_BLK, NSA_VS_BLK, NSA_KW_BLK, NSA_VW_BLK = 0, 4, 5, 6, 7
GLA_COLS = (8, 10, 12, 16, 0)
NSA_GATE_BLK, NSA_KC_BLK, NSA_VC_BLK = 4, 5, 6


def _cp(*sem, flags=None):
    return pltpu.CompilerParams(dimension_semantics=sem, vmem_limit_bytes=VMEM_LIMIT, flags=flags)


def _rms(x, g):
    y = x * lax.rsqrt(jnp.mean(x * x, axis=-1, keepdims=True) + EPS)
    return y * g


def _dot(a, b):
    return jnp.dot(a, b, preferred_element_type=F32)


def _dot_nt(a, b):
    return lax.dot_general(a, b, (((1,), (1,)), ((), ())), preferred_element_type=F32)


def _lane(shape):
    return lax.broadcasted_iota(I32, shape, len(shape) - 1)


def _norm_proj_body(x_ref, g_ref, w_ref, *o_refs, widths):
    xn = _rms(x_ref[...], g_ref[...]).astype(BF16)
    off = 0
    for o_ref, wd in zip(o_refs, widths):
        for c0 in range(0, wd, 512):
            cw = min(512, wd - c0)
            o_ref[:, c0:c0 + cw] = _dot(xn, w_ref[:, off + c0:off + c0 + cw]).astype(o_ref.dtype)
        off += wd


def norm_proj(x, g, w, outs, tm, name):
    T, D = x.shape
    widths = tuple(o[0] for o in outs)
    return pl.pallas_call(
        functools.partial(_norm_proj_body, widths=widths),
        grid=(T // tm,),
        in_specs=[pl.BlockSpec((tm, D), lambda i: (i, 0)),
                  pl.BlockSpec((1, D), lambda i: (0, 0)),
                  pl.BlockSpec(w.shape, lambda i: (0, 0))],
        out_specs=[pl.BlockSpec((tm, wd), lambda i: (i, 0)) for wd, _ in outs],
        out_shape=[jax.ShapeDtypeStruct((T, wd), dt) for wd, dt in outs],
        compiler_params=_cp("parallel"), name=name)(x, g.reshape(1, D), w)


def _proj_xattn_body(*refs, n):
    res_ref = refs[0]
    a_refs = refs[1:1 + n]
    w_refs = refs[1 + n:1 + 2 * n]
    g_ref, wq_ref, k_ref, v_ref, wo_ref, o_ref = refs[1 + 2 * n:]
    x = res_ref[...]
    for a_ref, w_ref in zip(a_refs, w_refs):
        x = x + _dot(a_ref[...].astype(BF16), w_ref[...])
    q = _dot(_rms(x, g_ref[...]).astype(BF16), wq_ref[...]).astype(BF16)
    tm = q.shape[0]
    head_of_lane = _lane((1, q.shape[1])) // XATTN_DH
    zero = jnp.zeros_like(q)
    q_all = jnp.concatenate([jnp.where(head_of_lane == h, q, zero) for h in range(XATTN_HEADS)], axis=0)
    s = _dot_nt(q_all, k_ref[0])
    p = jnp.exp(s - jnp.max(s, axis=-1, keepdims=True))
    p = p / jnp.sum(p, axis=-1, keepdims=True)
    o_all = _dot(p.astype(BF16), v_ref[0])
    o = o_all[0:tm]
    for h in range(1, XATTN_HEADS):
        o = jnp.where(head_of_lane == h, o_all[h * tm:(h + 1) * tm], o)
    o_ref[...] = x + _dot(o.astype(BF16), wo_ref[...])


def proj_xattn_block(res, a_list, w_list, g, wq, k, v, wo, seq, tm, name):
    T, D = res.shape
    n = len(a_list)
    M, W = k.shape[1], k.shape[2]
    per = seq // tm
    return pl.pallas_call(
        functools.partial(_proj_xattn_body, n=n),
        grid=(T // tm,),
        in_specs=([pl.BlockSpec((tm, D), lambda i: (i, 0))]
                  + [pl.BlockSpec((tm, a.shape[1]), lambda i: (i, 0)) for a in a_list]
                  + [pl.BlockSpec(w.shape, lambda i: (0, 0)) for w in w_list]
                  + [pl.BlockSpec((1, D), lambda i: (0, 0)),
                     pl.BlockSpec(wq.shape, lambda i: (0, 0)),
                     pl.BlockSpec((1, M, W), lambda i: (i // per, 0, 0)),
                     pl.BlockSpec((1, M, W), lambda i: (i // per, 0, 0)),
                     pl.BlockSpec(wo.shape, lambda i: (0, 0))]),
        out_specs=pl.BlockSpec((tm, D), lambda i: (i, 0)),
        out_shape=jax.ShapeDtypeStruct((T, D), F32),
        compiler_params=_cp("parallel"), name=name)(res, *a_list, *w_list, g.reshape(1, D), wq, k, v, wo)


def _ffn_body(x_ref, g_ref, wg_ref, wu_ref, wd_ref, o_ref, xn_ref):
    @pl.when(pl.program_id(1) == 0)
    def _():
        xn_ref[...] = _rms(x_ref[...], g_ref[...]).astype(BF16)
        o_ref[...] = x_ref[...]

    xn = xn_ref[...]
    hg = _dot(xn, wg_ref[...])
    hu = _dot(xn, wu_ref[...])
    h = (hg * jax.nn.sigmoid(hg) * hu).astype(BF16)
    o_ref[...] += _dot(h, wd_ref[...])


def ffn_block(x, g, wg, wu, wd, tm, tf, name):
    T, D = x.shape
    F = wg.shape[1]
    return pl.pallas_call(
        _ffn_body,
        grid=(T // tm, F // tf),
        in_specs=[pl.BlockSpec((tm, D), lambda i, k: (i, 0)),
                  pl.BlockSpec((1, D), lambda i, k: (0, 0)),
                  pl.BlockSpec((D, tf), lambda i, k: (0, k)),
                  pl.BlockSpec((D, tf), lambda i, k: (0, k)),
                  pl.BlockSpec((tf, D), lambda i, k: (k, 0))],
        out_specs=pl.BlockSpec((tm, D), lambda i, k: (i, 0)),
        out_shape=jax.ShapeDtypeStruct((T, D), F32),
        scratch_shapes=[pltpu.VMEM((tm, D), BF16)],
        compiler_params=_cp("parallel", "arbitrary"), name=name)(x, g.reshape(1, D), wg, wu, wd)


GLA_TILE = 128
_GLA_LEVELS = tuple(GLA_TILE >> (i + 1) for i in range(GLA_TILE.bit_length() - 1))


def _gla_constants():
    C = GLA_TILE
    t = np.arange(C)
    r = t[None, :]
    mats = [r <= t[:, None], r > t[:, None]]
    for hs in _GLA_LEVELS:
        c = (t // (2 * hs)) * (2 * hs) + hs - 1
        right = (t % (2 * hs)) >= hs
        m_right = (r > c[:, None]) & (r <= t[:, None])
        m_left = (r > t[:, None]) & (r <= c[:, None])
        mats.append(np.where(right[:, None], m_right, m_left))
    cmat = np.concatenate(mats, 0).astype(np.float32)
    x = t[:, None] ^ t[None, :]
    lvl = np.full((C, C), -1, np.int32)
    for i, hs in enumerate(_GLA_LEVELS):
        lvl[(t[:, None] > t[None, :]) & (x >= hs) & (x < 2 * hs)] = i
    lvl[t[:, None] == t[None, :]] = len(_GLA_LEVELS)
    return cmat, lvl


def _gla_body(q_ref, k_ref, v_ref, a_ref, r_ref, wa_ref, ba_ref, gn_ref, cmat_ref, lvl_ref, o_ref, st_ref, *, nb):
    C = GLA_TILE

    @pl.when(pl.program_id(2) == 0)
    def _():
        st_ref[...] = jnp.zeros_like(st_ref)

    nlev = len(_GLA_LEVELS)
    in_a = _lane((1, LANES)) < HALF
    lvl = lvl_ref[...]
    cmat = cmat_ref[...]
    on_diag = (lax.broadcasted_iota(I32, (LANES, 2 * LANES), 0) // HALF
               == lax.broadcasted_iota(I32, (LANES, 2 * LANES), 1) // LANES)
    for bi in range(nb):
        q = q_ref[bi].astype(F32)
        k = k_ref[bi].astype(F32)
        z = _dot(a_ref[bi], wa_ref[...]) + ba_ref[...]
        la = (jnp.minimum(z, 0.0) - jnp.log1p(jnp.exp(-jnp.abs(z)))) / GLA_TAU
        hi = la.astype(BF16)
        rest = la - hi.astype(F32)
        mid = rest.astype(BF16)
        lo = (rest - mid.astype(F32)).astype(BF16)
        u3 = _dot(cmat, jnp.concatenate([hi, mid, lo], axis=1))
        e = jnp.exp(u3[:, 0:LANES] + u3[:, LANES:2 * LANES] + u3[:, 2 * LANES:3 * LANES])
        qhat = (q * e[0:C]).astype(BF16)
        kend = (k * e[C:2 * C]).astype(BF16)
        decay = jnp.broadcast_to(e[C - 1:C], (LANES, LANES)).T
        zero = jnp.zeros_like(q)

        def stacked(x):
            return jnp.concatenate([jnp.where(in_a, x, zero), jnp.where(in_a, zero, x)], axis=0).astype(BF16)

        sc = jnp.zeros((2 * C, C), F32)
        for i in range(nlev):
            ei = e[(2 + i) * C:(3 + i) * C]
            sc = jnp.where(lvl == i, _dot_nt(stacked(q * ei), (k * ei).astype(BF16)), sc)
        sc = jnp.where(lvl == nlev, _dot_nt(stacked(q), k.astype(BF16)), sc)
        v = v_ref[bi]
        state = st_ref[bi]
        o_inter = _dot(qhat, state.astype(BF16))
        o_intra = _dot(sc.astype(BF16), v)
        update = lax.dot_general(kend, v, (((0,), (0,)), ((), ())), preferred_element_type=F32)
        st_ref[bi] = jnp.where(on_diag, jnp.concatenate([decay, decay], axis=1) * state + update, 0.0)
        for half in range(2):
            cols = slice(half * LANES, (half + 1) * LANES)
            y = _rms(o_inter[:, cols] + o_intra[half * C:(half + 1) * C, cols], gn_ref[...])
            rr = r_ref[bi, :, cols]
            o_ref[bi, :, cols] = (y * (rr * jax.nn.sigmoid(rr))).astype(o_ref.dtype)


def gla_mixer(hb, hf, w_alpha, b_alpha, g_norm, cols, nb):
    B, S, _ = hb.shape
    C = GLA_TILE
    cmat, lvl = _gla_constants()
    lvl = np.concatenate([lvl, lvl], axis=0)
    qb, kb_, vb, ab, rb = cols
    return pl.pallas_call(
        functools.partial(_gla_body, nb=nb),
        grid=(B // nb, GLA_HEADS // 2, S // C),
        in_specs=[pl.BlockSpec((nb, C, LANES), lambda b, p, c: (b, c, qb + p)),
                  pl.BlockSpec((nb, C, LANES), lambda b, p, c: (b, c, kb_ + p)),
                  pl.BlockSpec((nb, C, 2 * LANES), lambda b, p, c: (b, c, vb // 2 + p)),
                  pl.BlockSpec((nb, C, LANES), lambda b, p, c: (b, c, ab)),
                  pl.BlockSpec((nb, C, 2 * LANES), lambda b, p, c: (b, c, rb // 2 + p)),
                  pl.BlockSpec((LANES, LANES), lambda b, p, c: (0, p)),
                  pl.BlockSpec((1, LANES), lambda b, p, c: (0, p)),
                  pl.BlockSpec((1, LANES), lambda b, p, c: (0, 0)),
                  pl.BlockSpec(cmat.shape, lambda b, p, c: (0, 0)),
                  pl.BlockSpec(lvl.shape, lambda b, p, c: (0, 0))],
        out_specs=pl.BlockSpec((nb, C, 2 * LANES), lambda b, p, c: (b, c, p)),
        out_shape=jax.ShapeDtypeStruct((B, S, GLA_HEADS * GLA_DV), BF16),
        scratch_shapes=[pltpu.VMEM((nb, LANES, 2 * LANES), F32)],
        compiler_params=_cp("parallel", "parallel", "arbitrary"), name="gla")(
            hb, hb, hb, hb, hf, w_alpha, b_alpha, g_norm.reshape(1, LANES), jnp.asarray(cmat, BF16), jnp.asarray(lvl))


N_CMP_PAD = 128
N_SLC = 32


def _gelu_tanh(x):
    return 0.5 * x * (1.0 + jnp.tanh(math.sqrt(2.0 / math.pi) * (x + 0.044715 * (x * x * x))))


def _nsa_compress_body(tk_ref, tv_ref, pk_ref, pv_ref, w1k_ref, w1v_ref, w2k_ref, w2v_ref, kc_ref, vc_ref):
    for t_ref, p_ref, w1_ref, w2_ref, o_ref in ((tk_ref, pk_ref, w1k_ref, w2k_ref, kc_ref),
                                                (tv_ref, pv_ref, w1v_ref, w2v_ref, vc_ref)):
        lo = jnp.zeros((N_CMP_PAD, NSA_GROUPS * CMP_HIDDEN), F32)
        hi = jnp.zeros((N_CMP_PAD, NSA_GROUPS * CMP_HIDDEN), F32)
        for l in range(CMP_STRIDE):
            rows = t_ref[0, pl.ds(l, N_CMP_PAD, stride=CMP_STRIDE), :]
            lo = lo + _dot((rows + p_ref[l:l + 1]).astype(BF16), w1_ref[l])
            hi = hi + _dot((rows + p_ref[CMP_STRIDE + l:CMP_STRIDE + l + 1]).astype(BF16), w1_ref[CMP_STRIDE + l])
        pre = lo + pltpu.roll(hi, N_CMP_PAD - 1, 0)
        o_ref[0] = _dot(_gelu_tanh(pre).astype(BF16), w2_ref[...]).astype(o_ref.dtype)


def nsa_compress(hf, pk, pv, w1k, w1v, w2k, w2v):
    B, S, _ = hf.shape
    full = lambda a: pl.BlockSpec(a.shape, lambda b: (0,) * a.ndim)
    out = pl.BlockSpec((1, N_CMP_PAD, LANES), lambda b: (b, 0, 0))
    return pl.pallas_call(
        _nsa_compress_body,
        grid=(B,),
        in_specs=[pl.BlockSpec((1, S, LANES), lambda b: (b, 0, NSA_KC_BLK)),
                  pl.BlockSpec((1, S, LANES), lambda b: (b, 0, NSA_VC_BLK)),
                  full(pk), full(pv), full(w1k), full(w1v), full(w2k), full(w2v)],
        out_specs=[out, out],
        out_shape=[jax.ShapeDtypeStruct((B, N_CMP_PAD, LANES), BF16)] * 2,
        compiler_params=_cp("parallel"), name="nsa_compress")(hf, hf, pk, pv, w1k, w1v, w2k, w2v)


def _nsa_slope(g, j):
    return 2.0 ** (-(g * (NSA_HEADS // NSA_GROUPS) + j + 1))


def _nsa_cmp_body(q_ref, kc_ref, vc_ref, ov_ref, o_ref, selt_ref, sct_ref, *, tq):
    hg = NSA_HEADS // NSA_GROUPS
    qs = pl.program_id(1) * tq
    lane = _lane((1, LANES))
    tpos = qs + lax.broadcasted_iota(I32, (tq, LANES), 0)
    dist = tpos - (lane * CMP_STRIDE + (CMP_LEN - 1))
    valid = (dist >= 0) & (lane < N_CMP_PAD - 1)
    validf = valid.astype(F32)
    distf = dist.astype(F32)
    zero = jnp.zeros((tq, LANES), BF16)
    q_all = jnp.concatenate(
        [jnp.where((lane // HALF) == g, q_ref[0, :, j * LANES:(j + 1) * LANES], zero)
         for g in range(NSA_GROUPS) for j in range(hg)], axis=0)
    s_all = _dot_nt(q_all, kc_ref[0])
    probs = []
    for h in range(NSA_HEADS):
        s = s_all[h * tq:(h + 1) * tq] - _nsa_slope(h // hg, h % hg) * distf
        s = jnp.where(valid, s, NEG)
        p = jnp.exp(s - jnp.max(s, axis=-1, keepdims=True)) * validf
        probs.append((p / jnp.maximum(jnp.sum(p, axis=-1, keepdims=True), TINY)).astype(BF16))
    o_all = _dot(jnp.concatenate(probs, axis=0), vc_ref[0])
    first = lane < HALF
    for j in range(hg):
        o_ref[0, :, j * LANES:(j + 1) * LANES] = jnp.where(first, o_all[j * tq:(j + 1) * tq],
                                                           o_all[(hg + j) * tq:(hg + j + 1) * tq])
    imp = jnp.zeros((tq, LANES), F32)
    for g in range(NSA_GROUPS):
        imp = imp + _dot(jnp.concatenate(probs[g * hg:(g + 1) * hg], axis=1), ov_ref[g])
    blk = lane % N_SLC
    cur = tpos // SLC_LEN
    score = jnp.where((blk == cur) | (blk == 0), FORCE_SCORE, jnp.where(blk <= cur, imp, NEG))
    sct_ref[...] = score.T
    row8 = lax.broadcasted_iota(I32, (8, tq), 0)
    for g in range(NSA_GROUPS):
        base = g * N_SLC
        tiles = [sct_ref[base + 8 * v:base + 8 * (v + 1), :] for v in range(N_SLC // 8)]
        beaten = [jnp.zeros((8, tq), F32) for _ in tiles]
        for i in range(N_SLC):
            cand = jnp.broadcast_to(sct_ref[base + i:base + i + 1, :], (8, tq))
            for v, tile in enumerate(tiles):
                ge = jnp.where(cand >= tile, 1.0, 0.0)
                gt = jnp.where(cand > tile, 1.0, 0.0)
                if 8 * v > i:
                    wins = ge
                elif 8 * v + 7 <= i:
                    wins = gt
                else:
                    wins = jnp.where(row8 > i - 8 * v, ge, gt)
                beaten[v] = beaten[v] + wins
        for v in range(N_SLC // 8):
            selt_ref[0, base + 8 * v:base + 8 * (v + 1), :] = jnp.where(beaten[v] < N_SEL, 1.0, 0.0).astype(selt_ref.dtype)
    selt_ref[0, NSA_GROUPS * N_SLC:LANES, :] = jnp.zeros((LANES - NSA_GROUPS * N_SLC, tq), selt_ref.dtype)


def nsa_cmp(hb, kc, vc, ov, tq):
    B, S, _ = hb.shape
    qw = NSA_HEADS * NSA_DH
    return pl.pallas_call(
        functools.partial(_nsa_cmp_body, tq=tq),
        grid=(B, S // tq),
        in_specs=[pl.BlockSpec((1, tq, qw), lambda b, i: (b, i, NSA_Q_BLK)),
                  pl.BlockSpec((1, N_CMP_PAD, LANES), lambda b, i: (b, 0, 0)),
                  pl.BlockSpec((1, N_CMP_PAD, LANES), lambda b, i: (b, 0, 0)),
                  pl.BlockSpec(ov.shape, lambda b, i: (0, 0, 0))],
        out_specs=[pl.BlockSpec((1, tq, qw), lambda b, i: (b, i, 0)),
                   pl.BlockSpec((1, LANES, tq), lambda b, i: (b, 0, i))],
        out_shape=[jax.ShapeDtypeStruct((B, S, qw), F32), jax.ShapeDtypeStruct((B, LANES, S), BF16)],
        scratch_shapes=[pltpu.VMEM((LANES, tq), F32)],
        compiler_params=_cp("parallel", "parallel"), name="nsa_cmp")(hb, kc, vc, ov)


def _softmax_step(s, maskf, v, m, l, acc):
    m_new = jnp.maximum(m, jnp.max(s, axis=-1, keepdims=True))
    p = jnp.exp(s - m_new) * maskf
    alpha = jnp.exp(m - m_new)
    l = alpha * l + jnp.sum(p, axis=-1, keepdims=True)
    acc = alpha * acc + _dot(p.astype(BF16), v)
    return m_new, l, acc


FFN_TF = D_FF // 2
ROW_CHUNK = 32
LOG2E = math.log2(math.e)


FLASH_SLOTS = 2


def _flash_scratch(rows, tk):
    return ([pltpu.VMEM((rows, tk), F32)] * FLASH_SLOTS + [pltpu.VMEM((rows, tk), BF16)] * FLASH_SLOTS
            + [pltpu.VMEM((rows, LANES), F32)] * (FLASH_SLOTS + 3))


def _put_scores(dst, r0, raw, adjust):
    for i in range(raw.shape[0] // ROW_CHUNK):
        r = r0 + i * ROW_CHUNK
        dst[r:r + ROW_CHUNK, :] = adjust(r // ROW_CHUNK, raw[i * ROW_CHUNK:(i + 1) * ROW_CHUNK, :]) * LOG2E


def _softmax_pass1(s_ref, m_ref, a_ref):
    rows, tk = s_ref.shape
    m_in = m_out = m_ref
    for c in range(rows // ROW_CHUNK):
        r = slice(c * ROW_CHUNK, (c + 1) * ROW_CHUNK)
        s = s_ref[r, :]
        top = s[:, 0:LANES]
        for j in range(1, tk // LANES):
            top = jnp.maximum(top, s[:, j * LANES:(j + 1) * LANES])
        m_old = m_in[r, :]
        m_new = jnp.maximum(m_old, jnp.broadcast_to(jnp.max(top, axis=-1, keepdims=True), (ROW_CHUNK, LANES)))
        m_out[r, :] = m_new
        a_ref[r, :] = jnp.exp2(m_old - m_new)


def _softmax_pass2(s_ref, p_ref, m_ref, a_ref, l_ref):
    rows, tk = s_ref.shape
    for c in range(rows // ROW_CHUNK):
        r = slice(c * ROW_CHUNK, (c + 1) * ROW_CHUNK)
        m_new = m_ref[r, :]
        part = None
        for j in range(tk // LANES):
            p = jnp.exp2(s_ref[r, j * LANES:(j + 1) * LANES] - m_new)
            p_ref[r, j * LANES:(j + 1) * LANES] = p.astype(BF16)
            part = p if part is None else part + p
        l_ref[r, :] = a_ref[r, :] * l_ref[r, :] + part


def _flash_causal(n_full, tk, scores, pv_dot, scratch):
    n = FLASH_SLOTS
    s_bufs, p_bufs, a_bufs = (scratch[i * n:(i + 1) * n] for i in range(3))
    m_ref, l_ref, acc_ref = scratch[3 * n:]
    m_ref[...] = jnp.full(m_ref.shape, NEG, F32)
    a_bufs[n - 1][...] = jnp.zeros_like(a_bufs[0])
    p_bufs[n - 1][...] = jnp.zeros_like(p_bufs[0])
    l_ref[...] = jnp.zeros_like(l_ref)
    acc_ref[...] = jnp.zeros_like(acc_ref)

    def k_of(t):
        return pl.multiple_of(t * tk, tk)

    def pv(u, t):
        acc_ref[...] = a_bufs[u][...] * acc_ref[...] + pv_dot(p_bufs[u], k_of(t))

    NEXT_FULL, NEXT_LAST, FINAL = range(3)

    def stage(u, t, kind):
        pv((u - 1) % n, jnp.maximum(t - 1, 0))
        if kind != FINAL:
            scores(k_of(t + 1), s_bufs[(u + 1) % n], kind == NEXT_LAST)
        _softmax_pass1(s_bufs[u], m_ref, a_bufs[u])
        _softmax_pass2(s_bufs[u], p_bufs[u], m_ref, a_bufs[u], l_ref)
        if kind == FINAL:
            pv(u, t)

    def by_slot(t, kind):
        for u in range(n):
            @pl.when(t % n == u)
            def _():
                stage(u, t, kind)

    for first_is_last in (False, True):
        @pl.when((n_full == 0) == first_is_last)
        def _():
            scores(k_of(0), s_bufs[0], first_is_last)

    def body(t, carry):
        by_slot(t, NEXT_FULL)
        return carry

    lax.fori_loop(0, n_full - 1, body, 0)

    @pl.when(n_full >= 1)
    def _():
        by_slot(n_full - 1, NEXT_LAST)

    by_slot(n_full, FINAL)
    return acc_ref[...] / jnp.maximum(jnp.sum(l_ref[...], axis=-1, keepdims=True), TINY)


def _nsa_attn_body(q_ref, ks_ref, vs_ref, kw_ref, vw_ref, sel_ref, oc_ref, gt_ref, es_ref, o_ref,
                   selb_ref, ws_ref, wp_ref, wa_ref, wm_ref, wl_ref, *scratch, tq, tk):
    hg = NSA_HEADS // NSA_GROUPS
    span = WINDOW + tq
    qs = pl.program_id(1) * tq
    lane = _lane((1, LANES))
    first = lane < HALF
    zero = jnp.zeros((tq, LANES), BF16)
    q_all = jnp.concatenate(
        [jnp.where((lane // HALF) == g, q_ref[0, :, j * LANES:(j + 1) * LANES], zero)
         for g in range(NSA_GROUPS) for j in range(hg)], axis=0)
    for g in range(NSA_GROUPS):
        picked = lax.dot_general(sel_ref[0], es_ref[g], (((0,), (0,)), ((), ())), preferred_element_type=F32)
        selb_ref[g] = (picked - 1.0) * (-NEG)

    def t_of(c):
        head, r0 = divmod(c * ROW_CHUNK, tq)
        g, j = divmod(head, hg)
        return g, r0, qs + r0 + lax.broadcasted_iota(I32, (ROW_CHUNK, 1), 0), _nsa_slope(g, j)

    def slc_scores(k0, dst, last):
        kpos = k0 + _lane((1, tk))

        def adjust(c, s):
            g, r0, t, slope = t_of(c)
            s = s + slope * kpos.astype(F32) + selb_ref[g, r0:r0 + ROW_CHUNK, pl.ds(k0, tk)]
            return jnp.where(t >= kpos, s, NEG) if last else s

        _put_scores(dst, 0, _dot_nt(q_all, ks_ref[0, pl.ds(k0, tk), :]), adjust)

    o_slc = _flash_causal(qs // tk, tk, slc_scores,
                          lambda p_ref, k0: _dot(p_ref[...], vs_ref[0, pl.ds(k0, tk), :]), scratch)

    w0 = pl.multiple_of(jnp.maximum(qs - WINDOW, 0), tq)
    wpos = w0 + _lane((1, span))

    def win_adjust(c, s):
        _, _, t, slope = t_of(c)
        dist = t - wpos
        return jnp.where((dist >= 0) & (dist < WINDOW), s + slope * wpos.astype(F32), NEG)

    _put_scores(ws_ref, 0, _dot_nt(q_all, kw_ref[0, pl.ds(w0, span), :]), win_adjust)
    wm_ref[...] = jnp.full(wm_ref.shape, NEG, F32)
    wl_ref[...] = jnp.zeros_like(wl_ref)
    _softmax_pass1(ws_ref, wm_ref, wa_ref)
    _softmax_pass2(ws_ref, wp_ref, wm_ref, wa_ref, wl_ref)
    o_win = _dot(wp_ref[...], vw_ref[0, pl.ds(w0, span), :]) / jnp.maximum(
        jnp.sum(wl_ref[...], axis=-1, keepdims=True), TINY)

    gates = jax.nn.sigmoid(gt_ref[0])
    for j in range(hg):
        out = jnp.zeros((tq, LANES), F32)
        for br, branch in enumerate((None, o_slc, o_win)):
            gm = jnp.where(first, gates[:, j * 3 + br:j * 3 + br + 1],
                           gates[:, hg * 3 + j * 3 + br:hg * 3 + j * 3 + br + 1])
            if branch is None:
                val = oc_ref[0, :, j * LANES:(j + 1) * LANES]
            else:
                val = jnp.where(first, branch[j * tq:(j + 1) * tq], branch[(hg + j) * tq:(hg + j + 1) * tq])
            out = out + gm * val
        o_ref[0, :, j * LANES:(j + 1) * LANES] = out.astype(o_ref.dtype)


def nsa_attn(hb, hf, sel, ocmp, esel, tq, tk):
    B, S, _ = hb.shape
    qw = NSA_HEADS * NSA_DH
    rows = NSA_HEADS * tq
    span = WINDOW + tq
    kv = lambda blk: pl.BlockSpec((1, S, LANES), lambda b, i: (b, 0, blk))
    return pl.pallas_call(
        functools.partial(_nsa_attn_body, tq=tq, tk=tk),
        grid=(B, S // tq),
        in_specs=[pl.BlockSpec((1, tq, qw), lambda b, i: (b, i, NSA_Q_BLK)),
                  kv(NSA_KS_BLK), kv(NSA_VS_BLK), kv(NSA_KW_BLK), kv(NSA_VW_BLK),
                  pl.BlockSpec((1, LANES, tq), lambda b, i: (b, 0, i)),
                  pl.BlockSpec((1, tq, qw), lambda b, i: (b, i, 0)),
                  pl.BlockSpec((1, tq, LANES), lambda b, i: (b, i, NSA_GATE_BLK)),
                  pl.BlockSpec(esel.shape, lambda b, i: (0, 0, 0))],
        out_specs=pl.BlockSpec((1, tq, qw), lambda b, i: (b, i, 0)),
        out_shape=jax.ShapeDtypeStruct((B, S, qw), BF16),
        scratch_shapes=([pltpu.VMEM((NSA_GROUPS, tq, S), F32), pltpu.VMEM((rows, span), F32),
                         pltpu.VMEM((rows, span), BF16)]
                        + [pltpu.VMEM((rows, LANES), F32)] * 3 + _flash_scratch(rows, tk)),
        compiler_params=_cp("parallel", "parallel"), name="nsa_attn")(
            hb, hb, hb, hb, hb, sel, ocmp, hf, esel)


def _nsa_tables(S):
    n_cmp = (S - CMP_LEN) // CMP_STRIDE + 1
    c_start = np.arange(n_cmp) * CMP_STRIDE
    c_end = c_start + CMP_LEN - 1
    j_start = np.arange(S // SLC_LEN) * SLC_LEN
    overlap = (c_end[:, None] >= j_start[None]) & (c_start[:, None] <= j_start[None] + SLC_LEN - 1)
    hg = NSA_HEADS // NSA_GROUPS
    ov = np.zeros((NSA_GROUPS, hg, N_CMP_PAD, LANES), np.float32)
    es = np.zeros((NSA_GROUPS, LANES, S), np.float32)
    for g in range(NSA_GROUPS):
        ov[g, :, :n_cmp, g * N_SLC:(g + 1) * N_SLC] = overlap
        es[g, g * N_SLC + np.arange(S) // SLC_LEN, np.arange(S)] = 1.0
    return jnp.asarray(ov.reshape(NSA_GROUPS, hg * N_CMP_PAD, LANES), BF16), jnp.asarray(es, BF16)


NSA_PERM = np.array([(half * (NSA_HEADS // NSA_GROUPS) + j) * NSA_DH + d
                     for j in range(NSA_HEADS // NSA_GROUPS) for half in range(2) for d in range(NSA_DH)])


def nsa_mixer(hb, hf, pos_k, w1_k, w2_k, pos_v, w1_v, w2_v):
    B, S, _ = hb.shape
    assert S // CMP_STRIDE == N_CMP_PAD

    def both_groups(w):
        z = jnp.zeros_like(w)
        return jnp.concatenate([jnp.concatenate([w, z], axis=-1), jnp.concatenate([z, w], axis=-1)], axis=-2)

    def w1_blocks(w1):
        return both_groups(w1.reshape(CMP_LEN, NSA_DH, CMP_HIDDEN)).astype(BF16)

    kc, vc = nsa_compress(hf, jnp.tile(pos_k, (1, NSA_GROUPS)), jnp.tile(pos_v, (1, NSA_GROUPS)),
                          w1_blocks(w1_k), w1_blocks(w1_v),
                          both_groups(w2_k).astype(BF16), both_groups(w2_v).astype(BF16))
    ov, es = _nsa_tables(S)
    ocmp, sel = nsa_cmp(hb, kc, vc, ov, 256)
    return nsa_attn(hb, hf, sel, ocmp, es, 128, 256)


def _diff_body(slope_ref, q1_ref, q2_ref, k1_ref, k2_ref, v_ref, lam_ref, gn_ref, o_ref, *scratch,
               tq, tk, hp, lambda_init):
    heads = 2 * hp
    head0 = pl.program_id(1) * heads
    qs = pl.program_id(2) * tq
    lane = _lane((1, LANES))
    zero = jnp.zeros((tq, LANES), BF16)

    def q_rows(q_ref, h):
        return jnp.where((lane // HALF) == h % 2, q_ref[0, :, (h // 2) * LANES:(h // 2 + 1) * LANES], zero)

    q1 = [q_rows(q1_ref, h) for h in range(heads)]
    q2 = [q_rows(q2_ref, h) for h in range(heads)]
    slopes = [slope_ref[head0 + h] for h in range(heads)]

    def scores(k0, dst, last):
        kpos = k0 + _lane((1, tk))

        def adjust(c, s):
            s = s + slopes[(c * ROW_CHUNK) // (2 * tq)] * kpos.astype(F32)
            if last:
                t = qs + (c * ROW_CHUNK) % tq + lax.broadcasted_iota(I32, (ROW_CHUNK, 1), 0)
                s = jnp.where(t >= kpos, s, NEG)
            return s

        for h in range(heads):
            blk = slice((h // 2) * LANES, (h // 2 + 1) * LANES)
            _put_scores(dst, (2 * h) * tq, _dot_nt(q1[h], k1_ref[0, pl.ds(k0, tk), blk]), adjust)
            _put_scores(dst, (2 * h + 1) * tq, _dot_nt(q2[h], k2_ref[0, pl.ds(k0, tk), blk]), adjust)

    def pv_dot(p_ref, k0):
        return jnp.concatenate(
            [_dot(p_ref[2 * h * tq:2 * (h + 1) * tq, :], v_ref[0, pl.ds(k0, tk), h * LANES:(h + 1) * LANES])
             for h in range(heads)], axis=0)

    o = _flash_causal(qs // tk, tk, scores, pv_dot, scratch)
    lam_rows = lam_ref[...]
    lam = (jnp.exp(jnp.sum(lam_rows[0:1] * lam_rows[1:2], axis=-1, keepdims=True))
           - jnp.exp(jnp.sum(lam_rows[2:3] * lam_rows[3:4], axis=-1, keepdims=True)) + lambda_init)
    for h in range(heads):
        r0 = 2 * h * tq
        y = _rms(o[r0:r0 + tq] - lam * o[r0 + tq:r0 + 2 * tq], gn_ref[...]) * (1.0 - lambda_init)
        o_ref[0, :, h * LANES:(h + 1) * LANES] = y.astype(o_ref.dtype)


def diff_mixer(hd, lam_rows, g_norm, lambda_init, tq, tk, hp):
    B, S, _ = hd.shape
    ng = DIFF_HEADS // (2 * hp)
    qk, vw = hp * LANES, 2 * hp * LANES
    slopes = jnp.asarray(2.0 ** (-8.0 * np.arange(1, DIFF_HEADS + 1) / DIFF_HEADS), F32)
    grid_spec = pltpu.PrefetchScalarGridSpec(
        num_scalar_prefetch=1,
        grid=(B, ng, S // tq),
        in_specs=[pl.BlockSpec((1, tq, qk), lambda b, p, i, s: (b, i, p)),
                  pl.BlockSpec((1, tq, qk), lambda b, p, i, s: (b, i, ng + p)),
                  pl.BlockSpec((1, S, qk), lambda b, p, i, s: (b, 0, 2 * ng + p)),
                  pl.BlockSpec((1, S, qk), lambda b, p, i, s: (b, 0, 3 * ng + p)),
                  pl.BlockSpec((1, S, vw), lambda b, p, i, s: (b, 0, 2 * ng + p)),
                  pl.BlockSpec((4, LANES), lambda b, p, i, s: (0, 0)),
                  pl.BlockSpec((1, LANES), lambda b, p, i, s: (0, 0))],
        out_specs=pl.BlockSpec((1, tq, vw), lambda b, p, i, s: (b, i, p)),
        scratch_shapes=_flash_scratch(4 * hp * tq, tk))
    return pl.pallas_call(
        functools.partial(_diff_body, tq=tq, tk=tk, hp=hp, lambda_init=lambda_init),
        grid_spec=grid_spec,
        out_shape=jax.ShapeDtypeStruct((B, S, DIFF_HEADS * DIFF_DV), BF16),
        compiler_params=_cp("parallel", "parallel", "parallel"), name="diff_attn")(
            slopes, hd, hd, hd, hd, hd, lam_rows, g_norm.reshape(1, LANES))


def _router_body(x_ref, g_ref, wr_ref, tri_ref, xn_ref, meta_ref, wts_ref, cnt_ref, run_ref):
    @pl.when(pl.program_id(0) == 0)
    def _():
        run_ref[...] = jnp.zeros_like(run_ref)

    xn = _rms(x_ref[...], g_ref[...])
    xn_ref[...] = xn
    lane = _lane((1, LANES))
    logits = jnp.dot(xn, wr_ref[...], precision=lax.Precision.HIGHEST, preferred_element_type=F32)
    logits = jnp.where(lane < N_EXPERTS, logits, LOWEST)
    m1 = jnp.max(logits, axis=-1, keepdims=True)
    i1 = jnp.min(jnp.where(logits == m1, lane, LANES), axis=-1, keepdims=True)
    rest = jnp.where(lane == i1, LOWEST, logits)
    m2 = jnp.max(rest, axis=-1, keepdims=True)
    i2 = jnp.min(jnp.where(rest == m2, lane, LANES), axis=-1, keepdims=True)
    e = jnp.exp(m2 - m1)
    w1 = 1.0 / (1.0 + e)
    w2 = e / (1.0 + e)
    onehot = ((lane == i1) | (lane == i2)).astype(F32)
    pos = _dot(tri_ref[...], onehot.astype(BF16)) + run_ref[...]
    p1 = jnp.sum(jnp.where(lane == i1, pos, 0.0), axis=-1, keepdims=True).astype(I32)
    p2 = jnp.sum(jnp.where(lane == i2, pos, 0.0), axis=-1, keepdims=True).astype(I32)
    run_ref[...] += jnp.sum(onehot, axis=0, keepdims=True)
    meta_ref[...] = jnp.where(lane == 0, i1, jnp.where(lane == 1, i2, jnp.where(lane == 2, p1, jnp.where(lane == 3, p2, 0))))
    wts_ref[...] = jnp.where(lane == 0, w1, jnp.where(lane == 1, w2, 0.0))
    cnt_ref[...] = run_ref[...]


def moe_router(x, g, wr, tm):
    T, D = x.shape
    tri = jnp.asarray(np.tril(np.ones((tm, tm), np.float32), -1), BF16)
    return pl.pallas_call(
        _router_body,
        grid=(T // tm,),
        in_specs=[pl.BlockSpec((tm, D), lambda i: (i, 0)),
                  pl.BlockSpec((1, D), lambda i: (0, 0)),
                  pl.BlockSpec((D, LANES), lambda i: (0, 0)),
                  pl.BlockSpec((tm, tm), lambda i: (0, 0))],
        out_specs=[pl.BlockSpec((tm, D), lambda i: (i, 0)),
                   pl.BlockSpec((tm, LANES), lambda i: (i, 0)),
                   pl.BlockSpec((tm, LANES), lambda i: (i, 0)),
                   pl.BlockSpec((1, LANES), lambda i: (0, 0))],
        out_shape=[jax.ShapeDtypeStruct((T, D), F32), jax.ShapeDtypeStruct((T, LANES), I32),
                   jax.ShapeDtypeStruct((T, LANES), F32), jax.ShapeDtypeStruct((1, LANES), F32)],
        scratch_shapes=[pltpu.VMEM((1, LANES), F32)],
        compiler_params=_cp("arbitrary"), name="moe_router")(x, g.reshape(1, D), wr, tri)


ROW_DMA_UNROLL = 8


def _row_copy(src_ref, src_row, dst_ref, dst_row, sem):
    return pltpu.make_async_copy(src_ref.at[pl.ds(src_row, 1)], dst_ref.at[pl.ds(dst_row, 1)], sem)


def _dispatch_body(dest_ref, xn_ref, zero_ref, xs_ref, sem, *, tm):
    del zero_ref
    base = pl.program_id(0) * tm

    def issue(t, c):
        for slot in range(TOP_K):
            _row_copy(xn_ref, t, xs_ref, dest_ref[TOP_K * (base + t) + slot], sem).start()
        return c

    lax.fori_loop(0, tm, issue, 0, unroll=ROW_DMA_UNROLL)

    def drain(t, c):
        for slot in range(TOP_K):
            _row_copy(xn_ref, 0, xs_ref, 0, sem).wait()
        return c

    lax.fori_loop(0, tm, drain, 0, unroll=ROW_DMA_UNROLL)


def moe_dispatch(dest, xn, rows, tm):
    T, D = xn.shape
    grid_spec = pltpu.PrefetchScalarGridSpec(
        num_scalar_prefetch=1,
        grid=(T // tm,),
        in_specs=[pl.BlockSpec((tm, D), lambda i, d: (i, 0)),
                  pl.BlockSpec(memory_space=pl.ANY)],
        out_specs=pl.BlockSpec(memory_space=pl.ANY),
        scratch_shapes=[pltpu.SemaphoreType.DMA])
    return pl.pallas_call(
        functools.partial(_dispatch_body, tm=tm),
        grid_spec=grid_spec,
        out_shape=jax.ShapeDtypeStruct((rows, D), F32),
        input_output_aliases={2: 0},
        compiler_params=_cp("arbitrary"), name="moe_dispatch")(dest, xn, jnp.zeros((rows, D), F32))


def _experts_body(te_ref, tv_ref, xs_ref, wg_ref, wu_ref, wd_ref, o_ref, xb_ref):
    i = pl.program_id(0)

    @pl.when(pl.program_id(1) == 0)
    def _():
        xb_ref[...] = xs_ref[...].astype(BF16)
        o_ref[...] = jnp.zeros_like(o_ref)

    @pl.when(tv_ref[i] > 0)
    def _():
        xb = xb_ref[...]
        hg = _dot(xb, wg_ref[0])
        hu = _dot(xb, wu_ref[0])
        o_ref[...] += _dot((hg * jax.nn.sigmoid(hg) * hu).astype(BF16), wd_ref[0])


def moe_experts(tile_expert, tile_valid, xs, wg, wu, wd, tr, tf):
    R, D = xs.shape
    F = wg.shape[2]
    grid_spec = pltpu.PrefetchScalarGridSpec(
        num_scalar_prefetch=2,
        grid=(R // tr, F // tf),
        in_specs=[pl.BlockSpec((tr, D), lambda i, k, te, tv: (i, 0)),
                  pl.BlockSpec((1, D, tf), lambda i, k, te, tv: (te[i], 0, k)),
                  pl.BlockSpec((1, D, tf), lambda i, k, te, tv: (te[i], 0, k)),
                  pl.BlockSpec((1, tf, D), lambda i, k, te, tv: (te[i], k, 0))],
        out_specs=pl.BlockSpec((tr, D), lambda i, k, te, tv: (i, 0)),
        scratch_shapes=[pltpu.VMEM((tr, D), BF16)])
    return pl.pallas_call(
        _experts_body,
        grid_spec=grid_spec,
        out_shape=jax.ShapeDtypeStruct((R, D), F32),
        compiler_params=_cp("parallel", "arbitrary"), name="moe_experts")(tile_expert, tile_valid, xs, wg, wu, wd)


def _combine_body(dest_ref, x_ref, wts_ref, g_ref, ys_ref, o_ref, buf_ref, sem, *, tm):
    base = pl.program_id(0) * tm

    def issue(t, c):
        for slot in range(TOP_K):
            _row_copy(ys_ref, dest_ref[TOP_K * (base + t) + slot], buf_ref.at[slot], t, sem).start()
        return c

    lax.fori_loop(0, tm, issue, 0, unroll=ROW_DMA_UNROLL)

    def drain(t, c):
        for slot in range(TOP_K):
            _row_copy(ys_ref, 0, buf_ref.at[slot], 0, sem).wait()
        return c

    lax.fori_loop(0, tm, drain, 0, unroll=ROW_DMA_UNROLL)
    wts = wts_ref[...]
    y = x_ref[...] + (wts[:, 0:1] * buf_ref[0] + wts[:, 1:2] * buf_ref[1])
    o_ref[...] = _rms(y, g_ref[...])


def moe_combine(dest, x, wts, g_final, ys, tm):
    T, D = x.shape
    grid_spec = pltpu.PrefetchScalarGridSpec(
        num_scalar_prefetch=1,
        grid=(T // tm,),
        in_specs=[pl.BlockSpec((tm, D), lambda i, d: (i, 0)),
                  pl.BlockSpec((tm, LANES), lambda i, d: (i, 0)),
                  pl.BlockSpec((1, D), lambda i, d: (0, 0)),
                  pl.BlockSpec(memory_space=pl.ANY)],
        out_specs=pl.BlockSpec((tm, D), lambda i, d: (i, 0)),
        scratch_shapes=[pltpu.VMEM((TOP_K, tm, D), F32), pltpu.SemaphoreType.DMA])
    return pl.pallas_call(
        functools.partial(_combine_body, tm=tm),
        grid_spec=grid_spec,
        out_shape=jax.ShapeDtypeStruct((T, D), F32),
        compiler_params=_cp("arbitrary"), name="moe_combine")(dest, x, wts, g_final.reshape(1, D), ys)


def moe_block(x, g, w_router, wg, wu, wd, g_final, tr):
    T, D = x.shape
    wr = _pad_cols(w_router, LANES)
    xn, meta, wts, counts = moe_router(x, g, wr, 512)
    cnt = counts[0, :N_EXPERTS].astype(I32)
    gsz = ((cnt + tr - 1) // tr) * tr
    gend = jnp.cumsum(gsz)
    goff = gend - gsz
    dest = (jnp.take(goff, meta[:, 0:TOP_K]) + meta[:, TOP_K:2 * TOP_K]).reshape(-1)
    n_tiles = (TOP_K * T) // tr + N_EXPERTS
    tile_start = jnp.arange(n_tiles, dtype=I32) * tr
    tile_valid = (tile_start < gend[-1]).astype(I32)
    tile_expert = jnp.minimum(jnp.sum((tile_start[:, None] >= gend[None, :]).astype(I32), axis=1), N_EXPERTS - 1)
    last_valid = jnp.take(tile_expert, jnp.maximum(gend[-1] // tr - 1, 0))
    tile_expert = jnp.where(tile_valid > 0, tile_expert, last_valid)
    xs = moe_dispatch(dest, xn, n_tiles * tr, 256)
    ys = moe_experts(tile_expert, tile_valid, xs, wg, wu, wd, tr, FFN_TF)
    return moe_combine(dest, x, wts, g_final, ys, 256)


def _pad_cols(w, width):
    return jnp.pad(w, ((0, 0), (0, width - w.shape[1])))


def _mem_kv(mem2, g, wk, wv, batch):
    w = jnp.concatenate([wk, wv], axis=1).astype(BF16)
    width = wk.shape[1]
    k, v = norm_proj(mem2, g, w, [(width, BF16), (width, BF16)], 256, "mem_kv")
    return k.reshape(batch, -1, width), v.reshape(batch, -1, width)


def kernel(x, mem, mix_norm_0, w_in_0, gla_w_alpha_0, gla_b_alpha_0, gla_out_norm_0, nsa_cmp_pos_k_0, nsa_cmp_w1_k_0, nsa_cmp_w2_k_0, nsa_cmp_pos_v_0, nsa_cmp_w1_v_0, nsa_cmp_w2_v_0, w_out_0, xattn_norm_0, xattn_mem_norm_0, xattn_wq_0, xattn_wk_0, xattn_wv_0, xattn_wo_0, ffn_norm_0, ffn_w_gate_0, ffn_w_up_0, ffn_w_down_0, mix_norm_1, w_in_1, diff_lq1_1, diff_lk1_1, diff_lq2_1, diff_lk2_1, diff_out_norm_1, w_out_1, xattn_norm_1, xattn_mem_norm_1, xattn_wq_1, xattn_wk_1, xattn_wv_1, xattn_wo_1, ffn_norm_1, moe_router_1, moe_w_gate_1, moe_w_up_1, moe_w_down_1, final_norm):
    B, S, D = x.shape
    T = B * S
    x2 = x.reshape(T, D)
    mem2 = mem.reshape(-1, D)
    q_scale = NSA_DH ** -0.5

    sizes = [GLA_HEADS * GLA_DK, GLA_HEADS * GLA_DK, GLA_HEADS * GLA_DV, GLA_HEADS * GLA_DV, GLA_RANK,
             NSA_HEADS * NSA_DH] + [NSA_GROUPS * NSA_DH] * 6 + [3 * NSA_HEADS]
    (g_q, g_k, g_v, g_r, g_a, n_q, n_kc, n_vc, n_ks, n_vs, n_kw, n_vw, n_g) = jnp.split(
        w_in_0, np.cumsum(sizes)[:-1].tolist(), axis=1)
    w0 = jnp.concatenate([n_q[:, NSA_PERM] * q_scale, n_ks, n_vs, n_kw, n_vw, g_q * q_scale, g_k, g_v,
                          _pad_cols(g_a, LANES), g_r, _pad_cols(n_g, LANES), n_kc, n_vc], axis=1).astype(BF16)
    hb, hf = norm_proj(x2, mix_norm_0, w0, [(HB0_WIDTH, BF16), (HF0_WIDTH, F32)], 512, "in_proj0")
    hb = hb.reshape(B, S, HB0_WIDTH)
    hf = hf.reshape(B, S, HF0_WIDTH)
    w_alpha = jnp.pad(gla_w_alpha_0, ((0, LANES - GLA_RANK), (0, 0))).astype(BF16)
    o_a = gla_mixer(hb, hf, w_alpha, gla_b_alpha_0.reshape(1, -1), gla_out_norm_0, GLA_COLS, 8)
    o_b = nsa_mixer(hb, hf, nsa_cmp_pos_k_0, nsa_cmp_w1_k_0, nsa_cmp_w2_k_0,
                    nsa_cmp_pos_v_0, nsa_cmp_w1_v_0, nsa_cmp_w2_v_0)
    n_a = GLA_HEADS * GLA_DV
    k0, v0 = _mem_kv(mem2, xattn_mem_norm_0, xattn_wk_0, xattn_wv_0, B)
    x2 = proj_xattn_block(x2, [o_a.reshape(T, -1), o_b.reshape(T, -1)],
                          [w_out_0[:n_a].astype(BF16), w_out_0[n_a:][NSA_PERM].astype(BF16)],
                          xattn_norm_0, (xattn_wq_0 * q_scale).astype(BF16), k0, v0, xattn_wo_0.astype(BF16),
                          S, 512, "out_xattn0")
    x2 = ffn_block(x2, ffn_norm_0, ffn_w_gate_0.astype(BF16), ffn_w_up_0.astype(BF16), ffn_w_down_0.astype(BF16),
                   512, FFN_TF, "ffn0")

    lambda_init = 0.8 - 0.6 * math.exp(-0.3 * 1)
    n_q1 = 2 * DIFF_HEADS * DIFF_DH
    w1 = jnp.concatenate([w_in_1[:, :n_q1] * q_scale, w_in_1[:, n_q1:]], axis=1).astype(BF16)
    (hd,) = norm_proj(x2, mix_norm_1, w1, [(w1.shape[1], BF16)], 512, "in_proj1")
    lam_rows = _pad_cols(jnp.stack([diff_lq1_1, diff_lk1_1, diff_lq2_1, diff_lk2_1]), LANES)
    o_d = diff_mixer(hd.reshape(B, S, -1), lam_rows, diff_out_norm_1, lambda_init, 256, 256, 2)
    k1, v1 = _mem_kv(mem2, xattn_mem_norm_1, xattn_wk_1, xattn_wv_1, B)
    x2 = proj_xattn_block(x2, [o_d.reshape(T, -1)], [w_out_1.astype(BF16)],
                          xattn_norm_1, (xattn_wq_1 * q_scale).astype(BF16), k1, v1, xattn_wo_1.astype(BF16),
                          S, 512, "out_xattn1")
    out = moe_block(x2, ffn_norm_1, moe_router_1, moe_w_gate_1.astype(BF16), moe_w_up_1.astype(BF16),
                    moe_w_down_1.astype(BF16), final_norm, 512)
    return out.reshape(B, S, D)
```

```python
import functools
import math

import numpy as np
import jax
import jax.numpy as jnp
from jax import lax
from jax.experimental import pallas as pl
from jax.experimental.pallas import tpu as pltpu

F32 = jnp.float32
BF16 = jnp.bfloat16
I32 = jnp.int32

D_MODEL = 1024
GLA_HEADS, GLA_DK, GLA_DV, GLA_RANK, GLA_TAU, GLA_CHUNK = 4, 64, 128, 16, 16.0, 64
NSA_HEADS, NSA_GROUPS, NSA_DH = 8, 2, 64
CMP_LEN, CMP_STRIDE, CMP_HIDDEN, SLC_LEN, N_SEL, WINDOW = 32, 16, 256, 64, 8, 512
DIFF_HEADS, DIFF_DH, DIFF_DV = 8, 64, 128
XATTN_HEADS, XATTN_DH = 4, 64
D_FF, N_EXPERTS, TOP_K = 2816, 8, 2
EPS, NEG, TINY, FORCE_SCORE = 1e-6, -1e30, 1e-30, 1e4

LANES = 128
HALF = LANES // 2
VMEM_LIMIT = 56 * 1024 * 1024
LOWEST = -3.0e38

HB0_WIDTH, HF0_WIDTH = 17 * LANES, 7 * LANES
NSA_Q_BLK, NSA_KS_BLK, NSA_VS_BLK, NSA_KW_BLK, NSA_VW_BLK = 0, 4, 5, 6, 7
GLA_COLS = (8, 10, 12, 16, 0)
NSA_GATE_BLK, NSA_KC_BLK, NSA_VC_BLK = 4, 5, 6


def _cp(*sem, flags=None):
    return pltpu.CompilerParams(dimension_semantics=sem, vmem_limit_bytes=VMEM_LIMIT, flags=flags)


def _rms(x, g):
    y = x * lax.rsqrt(jnp.mean(x * x, axis=-1, keepdims=True) + EPS)
    return y * g


def _dot(a, b):
    return jnp.dot(a, b, preferred_element_type=F32)


def _dot_nt(a, b):
    return lax.dot_general(a, b, (((1,), (1,)), ((), ())), preferred_element_type=F32)


def _lane(shape):
    return lax.broadcasted_iota(I32, shape, len(shape) - 1)


def _norm_proj_body(x_ref, g_ref, w_ref, *o_refs, widths):
    xn = _rms(x_ref[...], g_ref[...]).astype(BF16)
    off = 0
    for o_ref, wd in zip(o_refs, widths):
        for c0 in range(0, wd, 512):
            cw = min(512, wd - c0)
            o_ref[:, c0:c0 + cw] = _dot(xn, w_ref[:, off + c0:off + c0 + cw]).astype(o_ref.dtype)
        off += wd


def norm_proj(x, g, w, outs, tm, name):
    T, D = x.shape
    widths = tuple(o[0] for o in outs)
    return pl.pallas_call(
        functools.partial(_norm_proj_body, widths=widths),
        grid=(T // tm,),
        in_specs=[pl.BlockSpec((tm, D), lambda i: (i, 0)),
                  pl.BlockSpec((1, D), lambda i: (0, 0)),
                  pl.BlockSpec(w.shape, lambda i: (0, 0))],
        out_specs=[pl.BlockSpec((tm, wd), lambda i: (i, 0)) for wd, _ in outs],
        out_shape=[jax.ShapeDtypeStruct((T, wd), dt) for wd, dt in outs],
        compiler_params=_cp("parallel"), name=name)(x, g.reshape(1, D), w)


def _proj_xattn_body(*refs, n):
    res_ref = refs[0]
    a_refs = refs[1:1 + n]
    w_refs = refs[1 + n:1 + 2 * n]
    g_ref, wq_ref, k_ref, v_ref, wo_ref, o_ref = refs[1 + 2 * n:]
    x = res_ref[...]
    for a_ref, w_ref in zip(a_refs, w_refs):
        x = x + _dot(a_ref[...].astype(BF16), w_ref[...])
    q = _dot(_rms(x, g_ref[...]).astype(BF16), wq_ref[...]).astype(BF16)
    tm = q.shape[0]
    head_of_lane = _lane((1, q.shape[1])) // XATTN_DH
    zero = jnp.zeros_like(q)
    q_all = jnp.concatenate([jnp.where(head_of_lane == h, q, zero) for h in range(XATTN_HEADS)], axis=0)
    s = _dot_nt(q_all, k_ref[0])
    p = jnp.exp(s - jnp.max(s, axis=-1, keepdims=True))
    p = p / jnp.sum(p, axis=-1, keepdims=True)
    o_all = _dot(p.astype(BF16), v_ref[0])
    o = o_all[0:tm]
    for h in range(1, XATTN_HEADS):
        o = jnp.where(head_of_lane == h, o_all[h * tm:(h + 1) * tm], o)
    o_ref[...] = x + _dot(o.astype(BF16), wo_ref[...])


def proj_xattn_block(res, a_list, w_list, g, wq, k, v, wo, seq, tm, name):
    T, D = res.shape
    n = len(a_list)
    M, W = k.shape[1], k.shape[2]
    per = seq // tm
    return pl.pallas_call(
        functools.partial(_proj_xattn_body, n=n),
        grid=(T // tm,),
        in_specs=([pl.BlockSpec((tm, D), lambda i: (i, 0))]
                  + [pl.BlockSpec((tm, a.shape[1]), lambda i: (i, 0)) for a in a_list]
                  + [pl.BlockSpec(w.shape, lambda i: (0, 0)) for w in w_list]
                  + [pl.BlockSpec((1, D), lambda i: (0, 0)),
                     pl.BlockSpec(wq.shape, lambda i: (0, 0)),
                     pl.BlockSpec((1, M, W), lambda i: (i // per, 0, 0)),
                     pl.BlockSpec((1, M, W), lambda i: (i // per, 0, 0)),
                     pl.BlockSpec(wo.shape, lambda i: (0, 0))]),
        out_specs=pl.BlockSpec((tm, D), lambda i: (i, 0)),
        out_shape=jax.ShapeDtypeStruct((T, D), F32),
        compiler_params=_cp("parallel"), name=name)(res, *a_list, *w_list, g.reshape(1, D), wq, k, v, wo)


def _ffn_body(x_ref, g_ref, wg_ref, wu_ref, wd_ref, o_ref, xn_ref):
    @pl.when(pl.program_id(1) == 0)
    def _():
        xn_ref[...] = _rms(x_ref[...], g_ref[...]).astype(BF16)
        o_ref[...] = x_ref[...]

    xn = xn_ref[...]
    hg = _dot(xn, wg_ref[...])
    hu = _dot(xn, wu_ref[...])
    h = (hg * jax.nn.sigmoid(hg) * hu).astype(BF16)
    o_ref[...] += _dot(h, wd_ref[...])


def ffn_block(x, g, wg, wu, wd, tm, tf, name):
    T, D = x.shape
    F = wg.shape[1]
    return pl.pallas_call(
        _ffn_body,
        grid=(T // tm, F // tf),
        in_specs=[pl.BlockSpec((tm, D), lambda i, k: (i, 0)),
                  pl.BlockSpec((1, D), lambda i, k: (0, 0)),
                  pl.BlockSpec((D, tf), lambda i, k: (0, k)),
                  pl.BlockSpec((D, tf), lambda i, k: (0, k)),
                  pl.BlockSpec((tf, D), lambda i, k: (k, 0))],
        out_specs=pl.BlockSpec((tm, D), lambda i, k: (i, 0)),
        out_shape=jax.ShapeDtypeStruct((T, D), F32),
        scratch_shapes=[pltpu.VMEM((tm, D), BF16)],
        compiler_params=_cp("parallel", "arbitrary"), name=name)(x, g.reshape(1, D), wg, wu, wd)


GLA_TILE = 128
_GLA_LEVELS = tuple(GLA_TILE >> (i + 1) for i in range(GLA_TILE.bit_length() - 1))


def _gla_constants():
    C = GLA_TILE
    t = np.arange(C)
    r = t[None, :]
    mats = [r <= t[:, None], r > t[:, None]]
    for hs in _GLA_LEVELS:
        c = (t // (2 * hs)) * (2 * hs) + hs - 1
        right = (t % (2 * hs)) >= hs
        m_right = (r > c[:, None]) & (r <= t[:, None])
        m_left = (r > t[:, None]) & (r <= c[:, None])
        mats.append(np.where(right[:, None], m_right, m_left))
    cmat = np.concatenate(mats, 0).astype(np.float32)
    x = t[:, None] ^ t[None, :]
    lvl = np.full((C, C), -1, np.int32)
    for i, hs in enumerate(_GLA_LEVELS):
        lvl[(t[:, None] > t[None, :]) & (x >= hs) & (x < 2 * hs)] = i
    lvl[t[:, None] == t[None, :]] = len(_GLA_LEVELS)
    return cmat, lvl


def _gla_body(q_ref, k_ref, v_ref, a_ref, r_ref, wa_ref, ba_ref, gn_ref, cmat_ref, lvl_ref, o_ref, st_ref, *, nb):
    C = GLA_TILE

    @pl.when(pl.program_id(2) == 0)
    def _():
        st_ref[...] = jnp.zeros_like(st_ref)

    nlev = len(_GLA_LEVELS)
    in_a = _lane((1, LANES)) < HALF
    lvl = lvl_ref[...]
    cmat = cmat_ref[...]
    on_diag = (lax.broadcasted_iota(I32, (LANES, 2 * LANES), 0) // HALF
               == lax.broadcasted_iota(I32, (LANES, 2 * LANES), 1) // LANES)
    for bi in range(nb):
        q = q_ref[bi].astype(F32)
        k = k_ref[bi].astype(F32)
        z = _dot(a_ref[bi], wa_ref[...]) + ba_ref[...]
        la = (jnp.minimum(z, 0.0) - jnp.log1p(jnp.exp(-jnp.abs(z)))) / GLA_TAU
        hi = la.astype(BF16)
        rest = la - hi.astype(F32)
        mid = rest.astype(BF16)
        lo = (rest - mid.astype(F32)).astype(BF16)
        u3 = _dot(cmat, jnp.concatenate([hi, mid, lo], axis=1))
        e = jnp.exp(u3[:, 0:LANES] + u3[:, LANES:2 * LANES] + u3[:, 2 * LANES:3 * LANES])
        qhat = (q * e[0:C]).astype(BF16)
        kend = (k * e[C:2 * C]).astype(BF16)
        decay = jnp.broadcast_to(e[C - 1:C], (LANES, LANES)).T
        zero = jnp.zeros_like(q)

        def stacked(x):
            return jnp.concatenate([jnp.where(in_a, x, zero), jnp.where(in_a, zero, x)], axis=0).astype(BF16)

        sc = jnp.zeros((2 * C, C), F32)
        for i in range(nlev):
            ei = e[(2 + i) * C:(3 + i) * C]
            sc = jnp.where(lvl == i, _dot_nt(stacked(q * ei), (k * ei).astype(BF16)), sc)
        sc = jnp.where(lvl == nlev, _dot_nt(stacked(q), k.astype(BF16)), sc)
        v = v_ref[bi]
        state = st_ref[bi]
        o_inter = _dot(qhat, state.astype(BF16))
        o_intra = _dot(sc.astype(BF16), v)
        update = lax.dot_general(kend, v, (((0,), (0,)), ((), ())), preferred_element_type=F32)
        st_ref[bi] = jnp.where(on_diag, jnp.concatenate([decay, decay], axis=1) * state + update, 0.0)
        for half in range(2):
            cols = slice(half * LANES, (half + 1) * LANES)
            y = _rms(o_inter[:, cols] + o_intra[half * C:(half + 1) * C, cols], gn_ref[...])
            rr = r_ref[bi, :, cols]
            o_ref[bi, :, cols] = (y * (rr * jax.nn.sigmoid(rr))).astype(o_ref.dtype)


def gla_mixer(hb, hf, w_alpha, b_alpha, g_norm, cols, nb):
    B, S, _ = hb.shape
    C = GLA_TILE
    cmat, lvl = _gla_constants()
    lvl = np.concatenate([lvl, lvl], axis=0)
    qb, kb_, vb, ab, rb = cols
    return pl.pallas_call(
        functools.partial(_gla_body, nb=nb),
        grid=(B // nb, GLA_HEADS // 2, S // C),
        in_specs=[pl.BlockSpec((nb, C, LANES), lambda b, p, c: (b, c, qb + p)),
                  pl.BlockSpec((nb, C, LANES), lambda b, p, c: (b, c, kb_ + p)),
                  pl.BlockSpec((nb, C, 2 * LANES), lambda b, p, c: (b, c, vb // 2 + p)),
                  pl.BlockSpec((nb, C, LANES), lambda b, p, c: (b, c, ab)),
                  pl.BlockSpec((nb, C, 2 * LANES), lambda b, p, c: (b, c, rb // 2 + p)),
                  pl.BlockSpec((LANES, LANES), lambda b, p, c: (0, p)),
                  pl.BlockSpec((1, LANES), lambda b, p, c: (0, p)),
                  pl.BlockSpec((1, LANES), lambda b, p, c: (0, 0)),
                  pl.BlockSpec(cmat.shape, lambda b, p, c: (0, 0)),
                  pl.BlockSpec(lvl.shape, lambda b, p, c: (0, 0))],
        out_specs=pl.BlockSpec((nb, C, 2 * LANES), lambda b, p, c: (b, c, p)),
        out_shape=jax.ShapeDtypeStruct((B, S, GLA_HEADS * GLA_DV), BF16),
        scratch_shapes=[pltpu.VMEM((nb, LANES, 2 * LANES), F32)],
        compiler_params=_cp("parallel", "parallel", "arbitrary"), name="gla")(
            hb, hb, hb, hb, hf, w_alpha, b_alpha, g_norm.reshape(1, LANES), jnp.asarray(cmat, BF16), jnp.asarray(lvl))


N_CMP_PAD = 128
N_SLC = 32


def _gelu_tanh(x):
    return 0.5 * x * (1.0 + jnp.tanh(math.sqrt(2.0 / math.pi) * (x + 0.044715 * (x * x * x))))


def _nsa_compress_body(tk_ref, tv_ref, pk_ref, pv_ref, w1k_ref, w1v_ref, w2k_ref, w2v_ref, kc_ref, vc_ref):
    for t_ref, p_ref, w1_ref, w2_ref, o_ref in ((tk_ref, pk_ref, w1k_ref, w2k_ref, kc_ref),
                                                (tv_ref, pv_ref, w1v_ref, w2v_ref, vc_ref)):
        lo = jnp.zeros((N_CMP_PAD, NSA_GROUPS * CMP_HIDDEN), F32)
        hi = jnp.zeros((N_CMP_PAD, NSA_GROUPS * CMP_HIDDEN), F32)
        for l in range(CMP_STRIDE):
            rows = t_ref[0, pl.ds(l, N_CMP_PAD, stride=CMP_STRIDE), :]
            lo = lo + _dot((rows + p_ref[l:l + 1]).astype(BF16), w1_ref[l])
            hi = hi + _dot((rows + p_ref[CMP_STRIDE + l:CMP_STRIDE + l + 1]).astype(BF16), w1_ref[CMP_STRIDE + l])
        pre = lo + pltpu.roll(hi, N_CMP_PAD - 1, 0)
        o_ref[0] = _dot(_gelu_tanh(pre).astype(BF16), w2_ref[...]).astype(o_ref.dtype)


def nsa_compress(hf, pk, pv, w1k, w1v, w2k, w2v):
    B, S, _ = hf.shape
    full = lambda a: pl.BlockSpec(a.shape, lambda b: (0,) * a.ndim)
    out = pl.BlockSpec((1, N_CMP_PAD, LANES), lambda b: (b, 0, 0))
    return pl.pallas_call(
        _nsa_compress_body,
        grid=(B,),
        in_specs=[pl.BlockSpec((1, S, LANES), lambda b: (b, 0, NSA_KC_BLK)),
                  pl.BlockSpec((1, S, LANES), lambda b: (b, 0, NSA_VC_BLK)),
                  full(pk), full(pv), full(w1k), full(w1v), full(w2k), full(w2v)],
        out_specs=[out, out],
        out_shape=[jax.ShapeDtypeStruct((B, N_CMP_PAD, LANES), BF16)] * 2,
        compiler_params=_cp("parallel"), name="nsa_compress")(hf, hf, pk, pv, w1k, w1v, w2k, w2v)


def _nsa_slope(g, j):
    return 2.0 ** (-(g * (NSA_HEADS // NSA_GROUPS) + j + 1))


def _nsa_cmp_body(q_ref, kc_ref, vc_ref, ov_ref, o_ref, selt_ref, sct_ref, *, tq):
    hg = NSA_HEADS // NSA_GROUPS
    qs = pl.program_id(1) * tq
    lane = _lane((1, LANES))
    tpos = qs + lax.broadcasted_iota(I32, (tq, LANES), 0)
    dist = tpos - (lane * CMP_STRIDE + (CMP_LEN - 1))
    valid = (dist >= 0) & (lane < N_CMP_PAD - 1)
    validf = valid.astype(F32)
    distf = dist.astype(F32)
    zero = jnp.zeros((tq, LANES), BF16)
    q_all = jnp.concatenate(
        [jnp.where((lane // HALF) == g, q_ref[0, :, j * LANES:(j + 1) * LANES], zero)
         for g in range(NSA_GROUPS) for j in range(hg)], axis=0)
    s_all = _dot_nt(q_all, kc_ref[0])
    probs = []
    for h in range(NSA_HEADS):
        s = s_all[h * tq:(h + 1) * tq] - _nsa_slope(h // hg, h % hg) * distf
        s = jnp.where(valid, s, NEG)
        p = jnp.exp(s - jnp.max(s, axis=-1, keepdims=True)) * validf
        probs.append((p / jnp.maximum(jnp.sum(p, axis=-1, keepdims=True), TINY)).astype(BF16))
    o_all = _dot(jnp.concatenate(probs, axis=0), vc_ref[0])
    first = lane < HALF
    for j in range(hg):
        o_ref[0, :, j * LANES:(j + 1) * LANES] = jnp.where(first, o_all[j * tq:(j + 1) * tq],
                                                           o_all[(hg + j) * tq:(hg + j + 1) * tq])
    imp = jnp.zeros((tq, LANES), F32)
    for g in range(NSA_GROUPS):
        imp = imp + _dot(jnp.concatenate(probs[g * hg:(g + 1) * hg], axis=1), ov_ref[g])
    blk = lane % N_SLC
    cur = tpos // SLC_LEN
    score = jnp.where((blk == cur) | (blk == 0), FORCE_SCORE, jnp.where(blk <= cur, imp, NEG))
    sct_ref[...] = score.T
    row8 = lax.broadcasted_iota(I32, (8, tq), 0)
    for g in range(NSA_GROUPS):
        base = g * N_SLC
        tiles = [sct_ref[base + 8 * v:base + 8 * (v + 1), :] for v in range(N_SLC // 8)]
        beaten = [jnp.zeros((8, tq), F32) for _ in tiles]
        for i in range(N_SLC):
            cand = jnp.broadcast_to(sct_ref[base + i:base + i + 1, :], (8, tq))
            for v, tile in enumerate(tiles):
                ge = jnp.where(cand >= tile, 1.0, 0.0)
                gt = jnp.where(cand > tile, 1.0, 0.0)
                if 8 * v > i:
                    wins = ge
                elif 8 * v + 7 <= i:
                    wins = gt
                else:
                    wins = jnp.where(row8 > i - 8 * v, ge, gt)
                beaten[v] = beaten[v] + wins
        for v in range(N_SLC // 8):
            selt_ref[0, base + 8 * v:base + 8 * (v + 1), :] = jnp.where(beaten[v] < N_SEL, 1.0, 0.0).astype(selt_ref.dtype)
    selt_ref[0, NSA_GROUPS * N_SLC:LANES, :] = jnp.zeros((LANES - NSA_GROUPS * N_SLC, tq), selt_ref.dtype)


def nsa_cmp(hb, kc, vc, ov, tq):
    B, S, _ = hb.shape
    qw = NSA_HEADS * NSA_DH
    return pl.pallas_call(
        functools.partial(_nsa_cmp_body, tq=tq),
        grid=(B, S // tq),
        in_specs=[pl.BlockSpec((1, tq, qw), lambda b, i: (b, i, NSA_Q_BLK)),
                  pl.BlockSpec((1, N_CMP_PAD, LANES), lambda b, i: (b, 0, 0)),
                  pl.BlockSpec((1, N_CMP_PAD, LANES), lambda b, i: (b, 0, 0)),
                  pl.BlockSpec(ov.shape, lambda b, i: (0, 0, 0))],
        out_specs=[pl.BlockSpec((1, tq, qw), lambda b, i: (b, i, 0)),
                   pl.BlockSpec((1, LANES, tq), lambda b, i: (b, 0, i))],
        out_shape=[jax.ShapeDtypeStruct((B, S, qw), F32), jax.ShapeDtypeStruct((B, LANES, S), BF16)],
        scratch_shapes=[pltpu.VMEM((LANES, tq), F32)],
        compiler_params=_cp("parallel", "parallel"), name="nsa_cmp")(hb, kc, vc, ov)


def _softmax_step(s, maskf, v, m, l, acc):
    m_new = jnp.maximum(m, jnp.max(s, axis=-1, keepdims=True))
    p = jnp.exp(s - m_new) * maskf
    alpha = jnp.exp(m - m_new)
    l = alpha * l + jnp.sum(p, axis=-1, keepdims=True)
    acc = alpha * acc + _dot(p.astype(BF16), v)
    return m_new, l, acc


FFN_TF = D_FF // 2
ROW_CHUNK = 32
LOG2E = math.log2(math.e)


FLASH_SLOTS = 2


def _flash_scratch(rows, tk):
    return ([pltpu.VMEM((rows, tk), F32)] * FLASH_SLOTS + [pltpu.VMEM((rows, tk), BF16)] * FLASH_SLOTS
            + [pltpu.VMEM((rows, LANES), F32)] * (FLASH_SLOTS + 3))


def _put_scores(dst, r0, raw, adjust):
    for i in range(raw.shape[0] // ROW_CHUNK):
        r = r0 + i * ROW_CHUNK
        dst[r:r + ROW_CHUNK, :] = adjust(r // ROW_CHUNK, raw[i * ROW_CHUNK:(i + 1) * ROW_CHUNK, :]) * LOG2E


def _softmax_pass1(s_ref, m_ref, a_ref):
    rows, tk = s_ref.shape
    m_in = m_out = m_ref
    for c in range(rows // ROW_CHUNK):
        r = slice(c * ROW_CHUNK, (c + 1) * ROW_CHUNK)
        s = s_ref[r, :]
        top = s[:, 0:LANES]
        for j in range(1, tk // LANES):
            top = jnp.maximum(top, s[:, j * LANES:(j + 1) * LANES])
        m_old = m_in[r, :]
        m_new = jnp.maximum(m_old, jnp.broadcast_to(jnp.max(top, axis=-1, keepdims=True), (ROW_CHUNK, LANES)))
        m_out[r, :] = m_new
        a_ref[r, :] = jnp.exp2(m_old - m_new)


def _softmax_pass2(s_ref, p_ref, m_ref, a_ref, l_ref):
    rows, tk = s_ref.shape
    for c in range(rows // ROW_CHUNK):
        r = slice(c * ROW_CHUNK, (c + 1) * ROW_CHUNK)
        m_new = m_ref[r, :]
        part = None
        for j in range(tk // LANES):
            p = jnp.exp2(s_ref[r, j * LANES:(j + 1) * LANES] - m_new)
            p_ref[r, j * LANES:(j + 1) * LANES] = p.astype(BF16)
            part = p if part is None else part + p
        l_ref[r, :] = a_ref[r, :] * l_ref[r, :] + part


def _flash_causal(n_full, tk, scores, pv_dot, scratch):
    n = FLASH_SLOTS
    s_bufs, p_bufs, a_bufs = (scratch[i * n:(i + 1) * n] for i in range(3))
    m_ref, l_ref, acc_ref = scratch[3 * n:]
    m_ref[...] = jnp.full(m_ref.shape, NEG, F32)
    a_bufs[n - 1][...] = jnp.zeros_like(a_bufs[0])
    p_bufs[n - 1][...] = jnp.zeros_like(p_bufs[0])
    l_ref[...] = jnp.zeros_like(l_ref)
    acc_ref[...] = jnp.zeros_like(acc_ref)

    def k_of(t):
        return pl.multiple_of(t * tk, tk)

    def pv(u, t):
        acc_ref[...] = a_bufs[u][...] * acc_ref[...] + pv_dot(p_bufs[u], k_of(t))

    NEXT_FULL, NEXT_LAST, FINAL = range(3)

    def stage(u, t, kind):
        pv((u - 1) % n, jnp.maximum(t - 1, 0))
        if kind != FINAL:
            scores(k_of(t + 1), s_bufs[(u + 1) % n], kind == NEXT_LAST)
        _softmax_pass1(s_bufs[u], m_ref, a_bufs[u])
        _softmax_pass2(s_bufs[u], p_bufs[u], m_ref, a_bufs[u], l_ref)
        if kind == FINAL:
            pv(u, t)

    def by_slot(t, kind):
        for u in range(n):
            @pl.when(t % n == u)
            def _():
                stage(u, t, kind)

    for first_is_last in (False, True):
        @pl.when((n_full == 0) == first_is_last)
        def _():
            scores(k_of(0), s_bufs[0], first_is_last)

    def body(t, carry):
        by_slot(t, NEXT_FULL)
        return carry

    lax.fori_loop(0, n_full - 1, body, 0)

    @pl.when(n_full >= 1)
    def _():
        by_slot(n_full - 1, NEXT_LAST)

    by_slot(n_full, FINAL)
    return acc_ref[...] / jnp.maximum(jnp.sum(l_ref[...], axis=-1, keepdims=True), TINY)


def _nsa_attn_body(q_ref, ks_ref, vs_ref, kw_ref, vw_ref, sel_ref, oc_ref, gt_ref, es_ref, o_ref,
                   selb_ref, ws_ref, wp_ref, wa_ref, wm_ref, wl_ref, *scratch, tq, tk):
    hg = NSA_HEADS // NSA_GROUPS
    span = WINDOW + tq
    qs = pl.program_id(1) * tq
    lane = _lane((1, LANES))
    first = lane < HALF
    zero = jnp.zeros((tq, LANES), BF16)
    q_all = jnp.concatenate(
        [jnp.where((lane // HALF) == g, q_ref[0, :, j * LANES:(j + 1) * LANES], zero)
         for g in range(NSA_GROUPS) for j in range(hg)], axis=0)
    for g in range(NSA_GROUPS):
        picked = lax.dot_general(sel_ref[0], es_ref[g], (((0,), (0,)), ((), ())), preferred_element_type=F32)
        selb_ref[g] = (picked - 1.0) * (-NEG)

    def t_of(c):
        head, r0 = divmod(c * ROW_CHUNK, tq)
        g, j = divmod(head, hg)
        return g, r0, qs + r0 + lax.broadcasted_iota(I32, (ROW_CHUNK, 1), 0), _nsa_slope(g, j)

    def slc_scores(k0, dst, last):
        kpos = k0 + _lane((1, tk))

        def adjust(c, s):
            g, r0, t, slope = t_of(c)
            s = s + slope * kpos.astype(F32) + selb_ref[g, r0:r0 + ROW_CHUNK, pl.ds(k0, tk)]
            return jnp.where(t >= kpos, s, NEG) if last else s

        _put_scores(dst, 0, _dot_nt(q_all, ks_ref[0, pl.ds(k0, tk), :]), adjust)

    o_slc = _flash_causal(qs // tk, tk, slc_scores,
                          lambda p_ref, k0: _dot(p_ref[...], vs_ref[0, pl.ds(k0, tk), :]), scratch)

    w0 = pl.multiple_of(jnp.maximum(qs - WINDOW, 0), tq)
    wpos = w0 + _lane((1, span))

    def win_adjust(c, s):
        _, _, t, slope = t_of(c)
        dist = t - wpos
        return jnp.where((dist >= 0) & (dist < WINDOW), s + slope * wpos.astype(F32), NEG)

    _put_scores(ws_ref, 0, _dot_nt(q_all, kw_ref[0, pl.ds(w0, span), :]), win_adjust)
    wm_ref[...] = jnp.full(wm_ref.shape, NEG, F32)
    wl_ref[...] = jnp.zeros_like(wl_ref)
    _softmax_pass1(ws_ref, wm_ref, wa_ref)
    _softmax_pass2(ws_ref, wp_ref, wm_ref, wa_ref, wl_ref)
    o_win = _dot(wp_ref[...], vw_ref[0, pl.ds(w0, span), :]) / jnp.maximum(
        jnp.sum(wl_ref[...], axis=-1, keepdims=True), TINY)

    gates = jax.nn.sigmoid(gt_ref[0])
    for j in range(hg):
        out = jnp.zeros((tq, LANES), F32)
        for br, branch in enumerate((None, o_slc, o_win)):
            gm = jnp.where(first, gates[:, j * 3 + br:j * 3 + br + 1],
                           gates[:, hg * 3 + j * 3 + br:hg * 3 + j * 3 + br + 1])
            if branch is None:
                val = oc_ref[0, :, j * LANES:(j + 1) * LANES]
            else:
                val = jnp.where(first, branch[j * tq:(j + 1) * tq], branch[(hg + j) * tq:(hg + j + 1) * tq])
            out = out + gm * val
        o_ref[0, :, j * LANES:(j + 1) * LANES] = out.astype(o_ref.dtype)


def nsa_attn(hb, hf, sel, ocmp, esel, tq, tk):
    B, S, _ = hb.shape
    qw = NSA_HEADS * NSA_DH
    rows = NSA_HEADS * tq
    span = WINDOW + tq
    kv = lambda blk: pl.BlockSpec((1, S, LANES), lambda b, i: (b, 0, blk))
    return pl.pallas_call(
        functools.partial(_nsa_attn_body, tq=tq, tk=tk),
        grid=(B, S // tq),
        in_specs=[pl.BlockSpec((1, tq, qw), lambda b, i: (b, i, NSA_Q_BLK)),
                  kv(NSA_KS_BLK), kv(NSA_VS_BLK), kv(NSA_KW_BLK), kv(NSA_VW_BLK),
                  pl.BlockSpec((1, LANES, tq), lambda b, i: (b, 0, i)),
                  pl.BlockSpec((1, tq, qw), lambda b, i: (b, i, 0)),
                  pl.BlockSpec((1, tq, LANES), lambda b, i: (b, i, NSA_GATE_BLK)),
                  pl.BlockSpec(esel.shape, lambda b, i: (0, 0, 0))],
        out_specs=pl.BlockSpec((1, tq, qw), lambda b, i: (b, i, 0)),
        out_shape=jax.ShapeDtypeStruct((B, S, qw), BF16),
        scratch_shapes=([pltpu.VMEM((NSA_GROUPS, tq, S), F32), pltpu.VMEM((rows, span), F32),
                         pltpu.VMEM((rows, span), BF16)]
                        + [pltpu.VMEM((rows, LANES), F32)] * 3 + _flash_scratch(rows, tk)),
        compiler_params=_cp("parallel", "parallel"), name="nsa_attn")(
            hb, hb, hb, hb, hb, sel, ocmp, hf, esel)


def _nsa_tables(S):
    n_cmp = (S - CMP_LEN) // CMP_STRIDE + 1
    c_start = np.arange(n_cmp) * CMP_STRIDE
    c_end = c_start + CMP_LEN - 1
    j_start = np.arange(S // SLC_LEN) * SLC_LEN
    overlap = (c_end[:, None] >= j_start[None]) & (c_start[:, None] <= j_start[None] + SLC_LEN - 1)
    hg = NSA_HEADS // NSA_GROUPS
    ov = np.zeros((NSA_GROUPS, hg, N_CMP_PAD, LANES), np.float32)
    es = np.zeros((NSA_GROUPS, LANES, S), np.float32)
    for g in range(NSA_GROUPS):
        ov[g, :, :n_cmp, g * N_SLC:(g + 1) * N_SLC] = overlap
        es[g, g * N_SLC + np.arange(S) // SLC_LEN, np.arange(S)] = 1.0
    return jnp.asarray(ov.reshape(NSA_GROUPS, hg * N_CMP_PAD, LANES), BF16), jnp.asarray(es, BF16)


NSA_PERM = np.array([(half * (NSA_HEADS // NSA_GROUPS) + j) * NSA_DH + d
                     for j in range(NSA_HEADS // NSA_GROUPS) for half in range(2) for d in range(NSA_DH)])


def nsa_mixer(hb, hf, pos_k, w1_k, w2_k, pos_v, w1_v, w2_v):
    B, S, _ = hb.shape
    assert S // CMP_STRIDE == N_CMP_PAD

    def both_groups(w):
        z = jnp.zeros_like(w)
        return jnp.concatenate([jnp.concatenate([w, z], axis=-1), jnp.concatenate([z, w], axis=-1)], axis=-2)

    def w1_blocks(w1):
        return both_groups(w1.reshape(CMP_LEN, NSA_DH, CMP_HIDDEN)).astype(BF16)

    kc, vc = nsa_compress(hf, jnp.tile(pos_k, (1, NSA_GROUPS)), jnp.tile(pos_v, (1, NSA_GROUPS)),
                          w1_blocks(w1_k), w1_blocks(w1_v),
                          both_groups(w2_k).astype(BF16), both_groups(w2_v).astype(BF16))
    ov, es = _nsa_tables(S)
    ocmp, sel = nsa_cmp(hb, kc, vc, ov, 256)
    return nsa_attn(hb, hf, sel, ocmp, es, 128, 256)


def _diff_body(slope_ref, q1_ref, q2_ref, k1_ref, k2_ref, v_ref, lam_ref, gn_ref, o_ref, *scratch,
               tq, tk, hp, lambda_init):
    heads = 2 * hp
    head0 = pl.program_id(1) * heads
    qs = pl.program_id(2) * tq
    lane = _lane((1, LANES))
    zero = jnp.zeros((tq, LANES), BF16)

    def q_rows(q_ref, h):
        return jnp.where((lane // HALF) == h % 2, q_ref[0, :, (h // 2) * LANES:(h // 2 + 1) * LANES], zero)

    q1 = [q_rows(q1_ref, h) for h in range(heads)]
    q2 = [q_rows(q2_ref, h) for h in range(heads)]
    slopes = [slope_ref[head0 + h] for h in range(heads)]

    def scores(k0, dst, last):
        kpos = k0 + _lane((1, tk))

        def adjust(c, s):
            s = s + slopes[(c * ROW_CHUNK) // (2 * tq)] * kpos.astype(F32)
            if last:
                t = qs + (c * ROW_CHUNK) % tq + lax.broadcasted_iota(I32, (ROW_CHUNK, 1), 0)
                s = jnp.where(t >= kpos, s, NEG)
            return s

        for h in range(heads):
            blk = slice((h // 2) * LANES, (h // 2 + 1) * LANES)
            _put_scores(dst, (2 * h) * tq, _dot_nt(q1[h], k1_ref[0, pl.ds(k0, tk), blk]), adjust)
            _put_scores(dst, (2 * h + 1) * tq, _dot_nt(q2[h], k2_ref[0, pl.ds(k0, tk), blk]), adjust)

    def pv_dot(p_ref, k0):
        return jnp.concatenate(
            [_dot(p_ref[2 * h * tq:2 * (h + 1) * tq, :], v_ref[0, pl.ds(k0, tk), h * LANES:(h + 1) * LANES])
             for h in range(heads)], axis=0)

    o = _flash_causal(qs // tk, tk, scores, pv_dot, scratch)
    lam_rows = lam_ref[...]
    lam = (jnp.exp(jnp.sum(lam_rows[0:1] * lam_rows[1:2], axis=-1, keepdims=True))
           - jnp.exp(jnp.sum(lam_rows[2:3] * lam_rows[3:4], axis=-1, keepdims=True)) + lambda_init)
    for h in range(heads):
        r0 = 2 * h * tq
        y = _rms(o[r0:r0 + tq] - lam * o[r0 + tq:r0 + 2 * tq], gn_ref[...]) * (1.0 - lambda_init)
        o_ref[0, :, h * LANES:(h + 1) * LANES] = y.astype(o_ref.dtype)


def diff_mixer(hd, lam_rows, g_norm, lambda_init, tq, tk, hp):
    B, S, _ = hd.shape
    ng = DIFF_HEADS // (2 * hp)
    qk, vw = hp * LANES, 2 * hp * LANES
    slopes = jnp.asarray(2.0 ** (-8.0 * np.arange(1, DIFF_HEADS + 1) / DIFF_HEADS), F32)
    grid_spec = pltpu.PrefetchScalarGridSpec(
        num_scalar_prefetch=1,
        grid=(B, ng, S // tq),
        in_specs=[pl.BlockSpec((1, tq, qk), lambda b, p, i, s: (b, i, p)),
                  pl.BlockSpec((1, tq, qk), lambda b, p, i, s: (b, i, ng + p)),
                  pl.BlockSpec((1, S, qk), lambda b, p, i, s: (b, 0, 2 * ng + p)),
                  pl.BlockSpec((1, S, qk), lambda b, p, i, s: (b, 0, 3 * ng + p)),
                  pl.BlockSpec((1, S, vw), lambda b, p, i, s: (b, 0, 2 * ng + p)),
                  pl.BlockSpec((4, LANES), lambda b, p, i, s: (0, 0)),
                  pl.BlockSpec((1, LANES), lambda b, p, i, s: (0, 0))],
        out_specs=pl.BlockSpec((1, tq, vw), lambda b, p, i, s: (b, i, p)),
        scratch_shapes=_flash_scratch(4 * hp * tq, tk))
    return pl.pallas_call(
        functools.partial(_diff_body, tq=tq, tk=tk, hp=hp, lambda_init=lambda_init),
        grid_spec=grid_spec,
        out_shape=jax.ShapeDtypeStruct((B, S, DIFF_HEADS * DIFF_DV), BF16),
        compiler_params=_cp("parallel", "parallel", "parallel"), name="diff_attn")(
            slopes, hd, hd, hd, hd, hd, lam_rows, g_norm.reshape(1, LANES))


def _router_body(x_ref, g_ref, wr_ref, tri_ref, xn_ref, meta_ref, wts_ref, cnt_ref, run_ref):
    @pl.when(pl.program_id(0) == 0)
    def _():
        run_ref[...] = jnp.zeros_like(run_ref)

    xn = _rms(x_ref[...], g_ref[...])
    xn_ref[...] = xn.reshape(xn_ref.shape)
    lane = _lane((1, LANES))
    logits = jnp.dot(xn, wr_ref[...], precision=lax.Precision.HIGHEST, preferred_element_type=F32)
    logits = jnp.where(lane < N_EXPERTS, logits, LOWEST)
    m1 = jnp.max(logits, axis=-1, keepdims=True)
    i1 = jnp.min(jnp.where(logits == m1, lane, LANES), axis=-1, keepdims=True)
    rest = jnp.where(lane == i1, LOWEST, logits)
    m2 = jnp.max(rest, axis=-1, keepdims=True)
    i2 = jnp.min(jnp.where(rest == m2, lane, LANES), axis=-1, keepdims=True)
    e = jnp.exp(m2 - m1)
    w1 = 1.0 / (1.0 + e)
    w2 = e / (1.0 + e)
    onehot = ((lane == i1) | (lane == i2)).astype(F32)
    pos = _dot(tri_ref[...], onehot.astype(BF16)) + run_ref[...]
    p1 = jnp.sum(jnp.where(lane == i1, pos, 0.0), axis=-1, keepdims=True).astype(I32)
    p2 = jnp.sum(jnp.where(lane == i2, pos, 0.0), axis=-1, keepdims=True).astype(I32)
    run_ref[...] += jnp.sum(onehot, axis=0, keepdims=True)
    meta_ref[...] = jnp.where(lane == 0, i1, jnp.where(lane == 1, i2, jnp.where(lane == 2, p1, jnp.where(lane == 3, p2, 0))))
    wts_ref[...] = jnp.where(lane == 0, w1, jnp.where(lane == 1, w2, 0.0))
    cnt_ref[...] = run_ref[...]


def moe_router(x, g, wr, tm):
    T, D = x.shape
    tri = jnp.asarray(np.tril(np.ones((tm, tm), np.float32), -1), BF16)
    return pl.pallas_call(
        _router_body,
        grid=(T // tm,),
        in_specs=[pl.BlockSpec((tm, D), lambda i: (i, 0)),
                  pl.BlockSpec((1, D), lambda i: (0, 0)),
                  pl.BlockSpec((D, LANES), lambda i: (0, 0)),
                  pl.BlockSpec((tm, tm), lambda i: (0, 0))],
        out_specs=[pl.BlockSpec((tm, D // LANES, LANES), lambda i: (i, 0, 0)),
                   pl.BlockSpec((tm, LANES), lambda i: (i, 0)),
                   pl.BlockSpec((tm, LANES), lambda i: (i, 0)),
                   pl.BlockSpec((1, LANES), lambda i: (0, 0))],
        out_shape=[jax.ShapeDtypeStruct((T, D // LANES, LANES), F32), jax.ShapeDtypeStruct((T, LANES), I32),
                   jax.ShapeDtypeStruct((T, LANES), F32), jax.ShapeDtypeStruct((1, LANES), F32)],
        scratch_shapes=[pltpu.VMEM((1, LANES), F32)],
        compiler_params=_cp("arbitrary"), name="moe_router")(x, g.reshape(1, D), wr, tri)


ROW_DMA_UNROLL = 8


def _row_copy(src_ref, src_row, dst_ref, dst_row, sem):
    return pltpu.make_async_copy(src_ref.at[pl.ds(src_row, 1)], dst_ref.at[pl.ds(dst_row, 1)], sem)


def _dispatch_body(dest_ref, xn_ref, zero_ref, xs_ref, sem, *, tm):
    del zero_ref
    base = pl.program_id(0) * tm

    def issue(t, c):
        for slot in range(TOP_K):
            _row_copy(xn_ref, t, xs_ref, dest_ref[TOP_K * (base + t) + slot], sem).start()
        return c

    lax.fori_loop(0, tm, issue, 0, unroll=ROW_DMA_UNROLL)

    def drain(t, c):
        for slot in range(TOP_K):
            _row_copy(xn_ref, 0, xs_ref, 0, sem).wait()
        return c

    lax.fori_loop(0, tm, drain, 0, unroll=ROW_DMA_UNROLL)


def moe_dispatch(dest, xn, rows, tm):
    T = xn.shape[0]
    row = xn.shape[1:]
    grid_spec = pltpu.PrefetchScalarGridSpec(
        num_scalar_prefetch=1,
        grid=(T // tm,),
        in_specs=[pl.BlockSpec((tm,) + row, lambda i, d: (i, 0, 0)),
                  pl.BlockSpec(memory_space=pl.ANY)],
        out_specs=pl.BlockSpec(memory_space=pl.ANY),
        scratch_shapes=[pltpu.SemaphoreType.DMA])
    return pl.pallas_call(
        functools.partial(_dispatch_body, tm=tm),
        grid_spec=grid_spec,
        out_shape=jax.ShapeDtypeStruct((rows,) + row, F32),
        input_output_aliases={2: 0},
        compiler_params=_cp("arbitrary"), name="moe_dispatch")(dest, xn, jnp.zeros((rows,) + row, F32))


def _experts_body(te_ref, tv_ref, xs_ref, wg_ref, wu_ref, wd_ref, o_ref, xb_ref, acc_ref):
    i = pl.program_id(0)
    k = pl.program_id(1)

    @pl.when(k == 0)
    def _():
        xb_ref[...] = xs_ref[...].reshape(xb_ref.shape).astype(BF16)
        acc_ref[...] = jnp.zeros_like(acc_ref)

    @pl.when(tv_ref[i] > 0)
    def _():
        xb = xb_ref[...]
        hg = _dot(xb, wg_ref[0])
        hu = _dot(xb, wu_ref[0])
        acc_ref[...] += _dot((hg * jax.nn.sigmoid(hg) * hu).astype(BF16), wd_ref[0])

    @pl.when(k == pl.num_programs(1) - 1)
    def _():
        o_ref[...] = acc_ref[...].reshape(o_ref.shape)


def moe_experts(tile_expert, tile_valid, xs, wg, wu, wd, tr, tf):
    R = xs.shape[0]
    row = xs.shape[1:]
    D, F = wg.shape[1], wg.shape[2]
    grid_spec = pltpu.PrefetchScalarGridSpec(
        num_scalar_prefetch=2,
        grid=(R // tr, F // tf),
        in_specs=[pl.BlockSpec((tr,) + row, lambda i, k, te, tv: (i, 0, 0)),
                  pl.BlockSpec((1, D, tf), lambda i, k, te, tv: (te[i], 0, k)),
                  pl.BlockSpec((1, D, tf), lambda i, k, te, tv: (te[i], 0, k)),
                  pl.BlockSpec((1, tf, D), lambda i, k, te, tv: (te[i], k, 0))],
        out_specs=pl.BlockSpec((tr,) + row, lambda i, k, te, tv: (i, 0, 0)),
        scratch_shapes=[pltpu.VMEM((tr, D), BF16), pltpu.VMEM((tr, D), F32)])
    return pl.pallas_call(
        _experts_body,
        grid_spec=grid_spec,
        out_shape=jax.ShapeDtypeStruct((R,) + row, F32),
        compiler_params=_cp("parallel", "arbitrary"), name="moe_experts")(tile_expert, tile_valid, xs, wg, wu, wd)


def _combine_body(dest_ref, x_ref, wts_ref, g_ref, ys_ref, o_ref, buf_ref, sem, *, tm):
    base = pl.program_id(0) * tm

    def issue(t, c):
        for slot in range(TOP_K):
            _row_copy(ys_ref, dest_ref[TOP_K * (base + t) + slot], buf_ref.at[slot], t, sem).start()
        return c

    lax.fori_loop(0, tm, issue, 0, unroll=ROW_DMA_UNROLL)

    def drain(t, c):
        for slot in range(TOP_K):
            _row_copy(ys_ref, 0, buf_ref.at[slot], 0, sem).wait()
        return c

    lax.fori_loop(0, tm, drain, 0, unroll=ROW_DMA_UNROLL)
    wts = wts_ref[...]
    y = x_ref[...] + (wts[:, 0:1] * buf_ref[0].reshape(x_ref.shape) + wts[:, 1:2] * buf_ref[1].reshape(x_ref.shape))
    o_ref[...] = _rms(y, g_ref[...])


def moe_combine(dest, x, wts, g_final, ys, tm):
    T, D = x.shape
    grid_spec = pltpu.PrefetchScalarGridSpec(
        num_scalar_prefetch=1,
        grid=(T // tm,),
        in_specs=[pl.BlockSpec((tm, D), lambda i, d: (i, 0)),
                  pl.BlockSpec((tm, LANES), lambda i, d: (i, 0)),
                  pl.BlockSpec((1, D), lambda i, d: (0, 0)),
                  pl.BlockSpec(memory_space=pl.ANY)],
        out_specs=pl.BlockSpec((tm, D), lambda i, d: (i, 0)),
        scratch_shapes=[pltpu.VMEM((TOP_K, tm) + ys.shape[1:], F32), pltpu.SemaphoreType.DMA])
    return pl.pallas_call(
        functools.partial(_combine_body, tm=tm),
        grid_spec=grid_spec,
        out_shape=jax.ShapeDtypeStruct((T, D), F32),
        compiler_params=_cp("arbitrary"), name="moe_combine")(dest, x, wts, g_final.reshape(1, D), ys)


def moe_block(x, g, w_router, wg, wu, wd, g_final, tr):
    T, D = x.shape
    wr = _pad_cols(w_router, LANES)
    xn, meta, wts, counts = moe_router(x, g, wr, 512)
    cnt = counts[0, :N_EXPERTS].astype(I32)
    gsz = ((cnt + tr - 1) // tr) * tr
    gend = jnp.cumsum(gsz)
    goff = gend - gsz
    dest = (jnp.take(goff, meta[:, 0:TOP_K]) + meta[:, TOP_K:2 * TOP_K]).reshape(-1)
    n_tiles = (TOP_K * T) // tr + N_EXPERTS
    tile_start = jnp.arange(n_tiles, dtype=I32) * tr
    tile_valid = (tile_start < gend[-1]).astype(I32)
    tile_expert = jnp.minimum(jnp.sum((tile_start[:, None] >= gend[None, :]).astype(I32), axis=1), N_EXPERTS - 1)
    last_valid = jnp.take(tile_expert, jnp.maximum(gend[-1] // tr - 1, 0))
    tile_expert = jnp.where(tile_valid > 0, tile_expert, last_valid)
    xs = moe_dispatch(dest, xn, n_tiles * tr, 256)
    ys = moe_experts(tile_expert, tile_valid, xs, wg, wu, wd, tr, FFN_TF)
    return moe_combine(dest, x, wts, g_final, ys, 256)


def _pad_cols(w, width):
    return jnp.pad(w, ((0, 0), (0, width - w.shape[1])))


def _mem_kv(mem2, g, wk, wv, batch):
    w = jnp.concatenate([wk, wv], axis=1).astype(BF16)
    width = wk.shape[1]
    k, v = norm_proj(mem2, g, w, [(width, BF16), (width, BF16)], 256, "mem_kv")
    return k.reshape(batch, -1, width), v.reshape(batch, -1, width)


def kernel(x, mem, mix_norm_0, w_in_0, gla_w_alpha_0, gla_b_alpha_0, gla_out_norm_0, nsa_cmp_pos_k_0, nsa_cmp_w1_k_0, nsa_cmp_w2_k_0, nsa_cmp_pos_v_0, nsa_cmp_w1_v_0, nsa_cmp_w2_v_0, w_out_0, xattn_norm_0, xattn_mem_norm_0, xattn_wq_0, xattn_wk_0, xattn_wv_0, xattn_wo_0, ffn_norm_0, ffn_w_gate_0, ffn_w_up_0, ffn_w_down_0, mix_norm_1, w_in_1, diff_lq1_1, diff_lk1_1, diff_lq2_1, diff_lk2_1, diff_out_norm_1, w_out_1, xattn_norm_1, xattn_mem_norm_1, xattn_wq_1, xattn_wk_1, xattn_wv_1, xattn_wo_1, ffn_norm_1, moe_router_1, moe_w_gate_1, moe_w_up_1, moe_w_down_1, final_norm):
    B, S, D = x.shape
    T = B * S
    x2 = x.reshape(T, D)
    mem2 = mem.reshape(-1, D)
    q_scale = NSA_DH ** -0.5

    sizes = [GLA_HEADS * GLA_DK, GLA_HEADS * GLA_DK, GLA_HEADS * GLA_DV, GLA_HEADS * GLA_DV, GLA_RANK,
             NSA_HEADS * NSA_DH] + [NSA_GROUPS * NSA_DH] * 6 + [3 * NSA_HEADS]
    (g_q, g_k, g_v, g_r, g_a, n_q, n_kc, n_vc, n_ks, n_vs, n_kw, n_vw, n_g) = jnp.split(
        w_in_0, np.cumsum(sizes)[:-1].tolist(), axis=1)
    w0 = jnp.concatenate([n_q[:, NSA_PERM] * q_scale, n_ks, n_vs, n_kw, n_vw, g_q * q_scale, g_k, g_v,
                          _pad_cols(g_a, LANES), g_r, _pad_cols(n_g, LANES), n_kc, n_vc], axis=1).astype(BF16)
    hb, hf = norm_proj(x2, mix_norm_0, w0, [(HB0_WIDTH, BF16), (HF0_WIDTH, F32)], 512, "in_proj0")
    hb = hb.reshape(B, S, HB0_WIDTH)
    hf = hf.reshape(B, S, HF0_WIDTH)
    w_alpha = jnp.pad(gla_w_alpha_0, ((0, LANES - GLA_RANK), (0, 0))).astype(BF16)
    o_a = gla_mixer(hb, hf, w_alpha, gla_b_alpha_0.reshape(1, -1), gla_out_norm_0, GLA_COLS, 8)
    o_b = nsa_mixer(hb, hf, nsa_cmp_pos_k_0, nsa_cmp_w1_k_0, nsa_cmp_w2_k_0,
                    nsa_cmp_pos_v_0, nsa_cmp_w1_v_0, nsa_cmp_w2_v_0)
    n_a = GLA_HEADS * GLA_DV
    k0, v0 = _mem_kv(mem2, xattn_mem_norm_0, xattn_wk_0, xattn_wv_0, B)
    x2 = proj_xattn_block(x2, [o_a.reshape(T, -1), o_b.reshape(T, -1)],
                          [w_out_0[:n_a].astype(BF16), w_out_0[n_a:][NSA_PERM].astype(BF16)],
                          xattn_norm_0, (xattn_wq_0 * q_scale).astype(BF16), k0, v0, xattn_wo_0.astype(BF16),
                          S, 1024, "out_xattn0")
    x2 = ffn_block(x2, ffn_norm_0, ffn_w_gate_0.astype(BF16), ffn_w_up_0.astype(BF16), ffn_w_down_0.astype(BF16),
                   512, FFN_TF, "ffn0")

    lambda_init = 0.8 - 0.6 * math.exp(-0.3 * 1)
    n_q1 = 2 * DIFF_HEADS * DIFF_DH
    w1 = jnp.concatenate([w_in_1[:, :n_q1] * q_scale, w_in_1[:, n_q1:]], axis=1).astype(BF16)
    (hd,) = norm_proj(x2, mix_norm_1, w1, [(w1.shape[1], BF16)], 512, "in_proj1")
    lam_rows = _pad_cols(jnp.stack([diff_lq1_1, diff_lk1_1, diff_lq2_1, diff_lk2_1]), LANES)
    o_d = diff_mixer(hd.reshape(B, S, -1), lam_rows, diff_out_norm_1, lambda_init, 256, 256, 2)
    k1, v1 = _mem_kv(mem2, xattn_mem_norm_1, xattn_wk_1, xattn_wv_1, B)
    x2 = proj_xattn_block(x2, [o_d.reshape(T, -1)], [w_out_1.astype(BF16)],
                          xattn_norm_1, (xattn_wq_1 * q_scale).astype(BF16), k1, v1, xattn_wo_1.astype(BF16),
                          S, 1024, "out_xattn1")
    out = moe_block(x2, ffn_norm_1, moe_router_1, moe_w_gate_1.astype(BF16), moe_w_up_1.astype(BF16),
                    moe_w_down_1.astype(BF16), final_norm, 512)
    return out.reshape(B, S, D)
```

```python
import functools
import math

import numpy as np
import jax
import jax.numpy as jnp
from jax import lax
from jax.experimental import pallas as pl
from jax.experimental.pallas import tpu as pltpu

F32 = jnp.float32
BF16 = jnp.bfloat16
I32 = jnp.int32

D_MODEL = 1024
GLA_HEADS, GLA_DK, GLA_DV, GLA_RANK, GLA_TAU, GLA_CHUNK = 4, 64, 128, 16, 16.0, 64
NSA_HEADS, NSA_GROUPS, NSA_DH = 8, 2, 64
CMP_LEN, CMP_STRIDE, CMP_HIDDEN, SLC_LEN, N_SEL, WINDOW = 32, 16, 256, 64, 8, 512
DIFF_HEADS, DIFF_DH, DIFF_DV = 8, 64, 128
XATTN_HEADS, XATTN_DH = 4, 64
D_FF, N_EXPERTS, TOP_K = 2816, 8, 2
EPS, NEG, TINY, FORCE_SCORE = 1e-6, -1e30, 1e-30, 1e4

LANES = 128
HALF = LANES // 2
VMEM_LIMIT = 56 * 1024 * 1024
LOWEST = -3.0e38

HB0_WIDTH, HF0_WIDTH = 17 * LANES, 7 * LANES
NSA_Q_BLK, NSA_KS_BLK, NSA_VS_BLK, NSA_KW_BLK, NSA_VW_BLK = 0, 4, 5, 6, 7
GLA_COLS = (8, 10, 12, 16, 0)
NSA_GATE_BLK, NSA_KC_BLK, NSA_VC_BLK = 4, 5, 6


def _cp(*sem, flags=None):
    return pltpu.CompilerParams(dimension_semantics=sem, vmem_limit_bytes=VMEM_LIMIT, flags=flags)


def _rms(x, g):
    y = x * lax.rsqrt(jnp.mean(x * x, axis=-1, keepdims=True) + EPS)
    return y * g


def _dot(a, b):
    return jnp.dot(a, b, preferred_element_type=F32)


def _dot_nt(a, b):
    return lax.dot_general(a, b, (((1,), (1,)), ((), ())), preferred_element_type=F32)


def _lane(shape):
    return lax.broadcasted_iota(I32, shape, len(shape) - 1)


def _norm_proj_body(x_ref, g_ref, w_ref, *o_refs, widths):
    xn = _rms(x_ref[...], g_ref[...]).astype(BF16)
    off = 0
    for o_ref, wd in zip(o_refs, widths):
        for c0 in range(0, wd, 512):
            cw = min(512, wd - c0)
            o_ref[:, c0:c0 + cw] = _dot(xn, w_ref[:, off + c0:off + c0 + cw]).astype(o_ref.dtype)
        off += wd


def norm_proj(x, g, w, outs, tm, name):
    T, D = x.shape
    widths = tuple(o[0] for o in outs)
    return pl.pallas_call(
        functools.partial(_norm_proj_body, widths=widths),
        grid=(T // tm,),
        in_specs=[pl.BlockSpec((tm, D), lambda i: (i, 0)),
                  pl.BlockSpec((1, D), lambda i: (0, 0)),
                  pl.BlockSpec(w.shape, lambda i: (0, 0))],
        out_specs=[pl.BlockSpec((tm, wd), lambda i: (i, 0)) for wd, _ in outs],
        out_shape=[jax.ShapeDtypeStruct((T, wd), dt) for wd, dt in outs],
        compiler_params=_cp("parallel"), name=name)(x, g.reshape(1, D), w)


def _proj_xattn_body(*refs, n):
    res_ref = refs[0]
    a_refs = refs[1:1 + n]
    w_refs = refs[1 + n:1 + 2 * n]
    g_ref, wq_ref, k_ref, v_ref, wo_ref, o_ref = refs[1 + 2 * n:]
    x = res_ref[...]
    for a_ref, w_ref in zip(a_refs, w_refs):
        x = x + _dot(a_ref[...].astype(BF16), w_ref[...])
    q = _dot(_rms(x, g_ref[...]).astype(BF16), wq_ref[...]).astype(BF16)
    tm = q.shape[0]
    head_of_lane = _lane((1, q.shape[1])) // XATTN_DH
    zero = jnp.zeros_like(q)
    q_all = jnp.concatenate([jnp.where(head_of_lane == h, q, zero) for h in range(XATTN_HEADS)], axis=0)
    s = _dot_nt(q_all, k_ref[0])
    p = jnp.exp(s - jnp.max(s, axis=-1, keepdims=True))
    p = p / jnp.sum(p, axis=-1, keepdims=True)
    o_all = _dot(p.astype(BF16), v_ref[0])
    o = o_all[0:tm]
    for h in range(1, XATTN_HEADS):
        o = jnp.where(head_of_lane == h, o_all[h * tm:(h + 1) * tm], o)
    o_ref[...] = x + _dot(o.astype(BF16), wo_ref[...])


def proj_xattn_block(res, a_list, w_list, g, wq, k, v, wo, seq, tm, name):
    T, D = res.shape
    n = len(a_list)
    M, W = k.shape[1], k.shape[2]
    per = seq // tm
    return pl.pallas_call(
        functools.partial(_proj_xattn_body, n=n),
        grid=(T // tm,),
        in_specs=([pl.BlockSpec((tm, D), lambda i: (i, 0))]
                  + [pl.BlockSpec((tm, a.shape[1]), lambda i: (i, 0)) for a in a_list]
                  + [pl.BlockSpec(w.shape, lambda i: (0, 0)) for w in w_list]
                  + [pl.BlockSpec((1, D), lambda i: (0, 0)),
                     pl.BlockSpec(wq.shape, lambda i: (0, 0)),
                     pl.BlockSpec((1, M, W), lambda i: (i // per, 0, 0)),
                     pl.BlockSpec((1, M, W), lambda i: (i // per, 0, 0)),
                     pl.BlockSpec(wo.shape, lambda i: (0, 0))]),
        out_specs=pl.BlockSpec((tm, D), lambda i: (i, 0)),
        out_shape=jax.ShapeDtypeStruct((T, D), F32),
        compiler_params=_cp("parallel"), name=name)(res, *a_list, *w_list, g.reshape(1, D), wq, k, v, wo)


def _ffn_body(x_ref, g_ref, wg_ref, wu_ref, wd_ref, o_ref, xn_ref):
    @pl.when(pl.program_id(1) == 0)
    def _():
        xn_ref[...] = _rms(x_ref[...], g_ref[...]).astype(BF16)
        o_ref[...] = x_ref[...]

    xn = xn_ref[...]
    hg = _dot(xn, wg_ref[...])
    hu = _dot(xn, wu_ref[...])
    h = (hg * jax.nn.sigmoid(hg) * hu).astype(BF16)
    o_ref[...] += _dot(h, wd_ref[...])


def ffn_block(x, g, wg, wu, wd, tm, tf, name):
    T, D = x.shape
    F = wg.shape[1]
    return pl.pallas_call(
        _ffn_body,
        grid=(T // tm, F // tf),
        in_specs=[pl.BlockSpec((tm, D), lambda i, k: (i, 0)),
                  pl.BlockSpec((1, D), lambda i, k: (0, 0)),
                  pl.BlockSpec((D, tf), lambda i, k: (0, k)),
                  pl.BlockSpec((D, tf), lambda i, k: (0, k)),
                  pl.BlockSpec((tf, D), lambda i, k: (k, 0))],
        out_specs=pl.BlockSpec((tm, D), lambda i, k: (i, 0)),
        out_shape=jax.ShapeDtypeStruct((T, D), F32),
        scratch_shapes=[pltpu.VMEM((tm, D), BF16)],
        compiler_params=_cp("parallel", "arbitrary"), name=name)(x, g.reshape(1, D), wg, wu, wd)


GLA_TILE = 128
_GLA_LEVELS = tuple(GLA_TILE >> (i + 1) for i in range(GLA_TILE.bit_length() - 1))


def _gla_constants():
    C = GLA_TILE
    t = np.arange(C)
    r = t[None, :]
    mats = [r <= t[:, None], r > t[:, None]]
    for hs in _GLA_LEVELS:
        c = (t // (2 * hs)) * (2 * hs) + hs - 1
        right = (t % (2 * hs)) >= hs
        m_right = (r > c[:, None]) & (r <= t[:, None])
        m_left = (r > t[:, None]) & (r <= c[:, None])
        mats.append(np.where(right[:, None], m_right, m_left))
    cmat = np.concatenate(mats, 0).astype(np.float32)
    x = t[:, None] ^ t[None, :]
    lvl = np.full((C, C), -1, np.int32)
    for i, hs in enumerate(_GLA_LEVELS):
        lvl[(t[:, None] > t[None, :]) & (x >= hs) & (x < 2 * hs)] = i
    lvl[t[:, None] == t[None, :]] = len(_GLA_LEVELS)
    return cmat, lvl


def _gla_body(q_ref, k_ref, v_ref, a_ref, r_ref, wa_ref, ba_ref, gn_ref, cmat_ref, lvl_ref, o_ref, st_ref, *, nb):
    C = GLA_TILE

    @pl.when(pl.program_id(2) == 0)
    def _():
        st_ref[...] = jnp.zeros_like(st_ref)

    nlev = len(_GLA_LEVELS)
    in_a = _lane((1, LANES)) < HALF
    lvl = lvl_ref[...]
    cmat = cmat_ref[...]
    on_diag = (lax.broadcasted_iota(I32, (LANES, 2 * LANES), 0) // HALF
               == lax.broadcasted_iota(I32, (LANES, 2 * LANES), 1) // LANES)
    for bi in range(nb):
        q = q_ref[bi].astype(F32)
        k = k_ref[bi].astype(F32)
        z = _dot(a_ref[bi], wa_ref[...]) + ba_ref[...]
        la = (jnp.minimum(z, 0.0) - jnp.log1p(jnp.exp(-jnp.abs(z)))) / GLA_TAU
        hi = la.astype(BF16)
        rest = la - hi.astype(F32)
        mid = rest.astype(BF16)
        lo = (rest - mid.astype(F32)).astype(BF16)
        u3 = _dot(cmat, jnp.concatenate([hi, mid, lo], axis=1))
        e = jnp.exp(u3[:, 0:LANES] + u3[:, LANES:2 * LANES] + u3[:, 2 * LANES:3 * LANES])
        qhat = (q * e[0:C]).astype(BF16)
        kend = (k * e[C:2 * C]).astype(BF16)
        decay = jnp.broadcast_to(e[C - 1:C], (LANES, LANES)).T
        zero = jnp.zeros_like(q)

        def stacked(x):
            return jnp.concatenate([jnp.where(in_a, x, zero), jnp.where(in_a, zero, x)], axis=0).astype(BF16)

        sc = jnp.zeros((2 * C, C), F32)
        for i in range(nlev):
            ei = e[(2 + i) * C:(3 + i) * C]
            sc = jnp.where(lvl == i, _dot_nt(stacked(q * ei), (k * ei).astype(BF16)), sc)
        sc = jnp.where(lvl == nlev, _dot_nt(stacked(q), k.astype(BF16)), sc)
        v = v_ref[bi]
        state = st_ref[bi]
        o_inter = _dot(qhat, state.astype(BF16))
        o_intra = _dot(sc.astype(BF16), v)
        update = lax.dot_general(kend, v, (((0,), (0,)), ((), ())), preferred_element_type=F32)
        st_ref[bi] = jnp.where(on_diag, jnp.concatenate([decay, decay], axis=1) * state + update, 0.0)
        for half in range(2):
            cols = slice(half * LANES, (half + 1) * LANES)
            y = _rms(o_inter[:, cols] + o_intra[half * C:(half + 1) * C, cols], gn_ref[...])
            rr = r_ref[bi, :, cols]
            o_ref[bi, :, cols] = (y * (rr * jax.nn.sigmoid(rr))).astype(o_ref.dtype)


def gla_mixer(hb, hf, w_alpha, b_alpha, g_norm, cols, nb):
    B, S, _ = hb.shape
    C = GLA_TILE
    cmat, lvl = _gla_constants()
    lvl = np.concatenate([lvl, lvl], axis=0)
    qb, kb_, vb, ab, rb = cols
    return pl.pallas_call(
        functools.partial(_gla_body, nb=nb),
        grid=(B // nb, GLA_HEADS // 2, S // C),
        in_specs=[pl.BlockSpec((nb, C, LANES), lambda b, p, c: (b, c, qb + p)),
                  pl.BlockSpec((nb, C, LANES), lambda b, p, c: (b, c, kb_ + p)),
                  pl.BlockSpec((nb, C, 2 * LANES), lambda b, p, c: (b, c, vb // 2 + p)),
                  pl.BlockSpec((nb, C, LANES), lambda b, p, c: (b, c, ab)),
                  pl.BlockSpec((nb, C, 2 * LANES), lambda b, p, c: (b, c, rb // 2 + p)),
                  pl.BlockSpec((LANES, LANES), lambda b, p, c: (0, p)),
                  pl.BlockSpec((1, LANES), lambda b, p, c: (0, p)),
                  pl.BlockSpec((1, LANES), lambda b, p, c: (0, 0)),
                  pl.BlockSpec(cmat.shape, lambda b, p, c: (0, 0)),
                  pl.BlockSpec(lvl.shape, lambda b, p, c: (0, 0))],
        out_specs=pl.BlockSpec((nb, C, 2 * LANES), lambda b, p, c: (b, c, p)),
        out_shape=jax.ShapeDtypeStruct((B, S, GLA_HEADS * GLA_DV), BF16),
        scratch_shapes=[pltpu.VMEM((nb, LANES, 2 * LANES), F32)],
        compiler_params=_cp("parallel", "parallel", "arbitrary"), name="gla")(
            hb, hb, hb, hb, hf, w_alpha, b_alpha, g_norm.reshape(1, LANES), jnp.asarray(cmat, BF16), jnp.asarray(lvl))


N_CMP_PAD = 128
N_SLC = 32


def _gelu_tanh(x):
    return 0.5 * x * (1.0 + jnp.tanh(math.sqrt(2.0 / math.pi) * (x + 0.044715 * (x * x * x))))


def _nsa_compress_body(tk_ref, tv_ref, pk_ref, pv_ref, w1k_ref, w1v_ref, w2k_ref, w2v_ref, kc_ref, vc_ref):
    for t_ref, p_ref, w1_ref, w2_ref, o_ref in ((tk_ref, pk_ref, w1k_ref, w2k_ref, kc_ref),
                                                (tv_ref, pv_ref, w1v_ref, w2v_ref, vc_ref)):
        lo = jnp.zeros((N_CMP_PAD, NSA_GROUPS * CMP_HIDDEN), F32)
        hi = jnp.zeros((N_CMP_PAD, NSA_GROUPS * CMP_HIDDEN), F32)
        for l in range(CMP_STRIDE):
            rows = t_ref[0, pl.ds(l, N_CMP_PAD, stride=CMP_STRIDE), :]
            lo = lo + _dot((rows + p_ref[l:l + 1]).astype(BF16), w1_ref[l])
            hi = hi + _dot((rows + p_ref[CMP_STRIDE + l:CMP_STRIDE + l + 1]).astype(BF16), w1_ref[CMP_STRIDE + l])
        pre = lo + pltpu.roll(hi, N_CMP_PAD - 1, 0)
        o_ref[0] = _dot(_gelu_tanh(pre).astype(BF16), w2_ref[...]).astype(o_ref.dtype)


def nsa_compress(hf, pk, pv, w1k, w1v, w2k, w2v):
    B, S, _ = hf.shape
    full = lambda a: pl.BlockSpec(a.shape, lambda b: (0,) * a.ndim)
    out = pl.BlockSpec((1, N_CMP_PAD, LANES), lambda b: (b, 0, 0))
    return pl.pallas_call(
        _nsa_compress_body,
        grid=(B,),
        in_specs=[pl.BlockSpec((1, S, LANES), lambda b: (b, 0, NSA_KC_BLK)),
                  pl.BlockSpec((1, S, LANES), lambda b: (b, 0, NSA_VC_BLK)),
                  full(pk), full(pv), full(w1k), full(w1v), full(w2k), full(w2v)],
        out_specs=[out, out],
        out_shape=[jax.ShapeDtypeStruct((B, N_CMP_PAD, LANES), BF16)] * 2,
        compiler_params=_cp("parallel"), name="nsa_compress")(hf, hf, pk, pv, w1k, w1v, w2k, w2v)


def _nsa_slope(g, j):
    return 2.0 ** (-(g * (NSA_HEADS // NSA_GROUPS) + j + 1))


def _nsa_cmp_body(q_ref, kc_ref, vc_ref, ov_ref, o_ref, selt_ref, sct_ref, *, tq):
    hg = NSA_HEADS // NSA_GROUPS
    qs = pl.program_id(1) * tq
    lane = _lane((1, LANES))
    tpos = qs + lax.broadcasted_iota(I32, (tq, LANES), 0)
    dist = tpos - (lane * CMP_STRIDE + (CMP_LEN - 1))
    valid = (dist >= 0) & (lane < N_CMP_PAD - 1)
    validf = valid.astype(F32)
    distf = dist.astype(F32)
    zero = jnp.zeros((tq, LANES), BF16)
    q_all = jnp.concatenate(
        [jnp.where((lane // HALF) == g, q_ref[0, :, j * LANES:(j + 1) * LANES], zero)
         for g in range(NSA_GROUPS) for j in range(hg)], axis=0)
    s_all = _dot_nt(q_all, kc_ref[0])
    probs = []
    for h in range(NSA_HEADS):
        s = s_all[h * tq:(h + 1) * tq] - _nsa_slope(h // hg, h % hg) * distf
        s = jnp.where(valid, s, NEG)
        p = jnp.exp(s - jnp.max(s, axis=-1, keepdims=True)) * validf
        probs.append((p / jnp.maximum(jnp.sum(p, axis=-1, keepdims=True), TINY)).astype(BF16))
    o_all = _dot(jnp.concatenate(probs, axis=0), vc_ref[0])
    first = lane < HALF
    for j in range(hg):
        o_ref[0, :, j * LANES:(j + 1) * LANES] = jnp.where(first, o_all[j * tq:(j + 1) * tq],
                                                           o_all[(hg + j) * tq:(hg + j + 1) * tq])
    imp = jnp.zeros((tq, LANES), F32)
    for g in range(NSA_GROUPS):
        imp = imp + _dot(jnp.concatenate(probs[g * hg:(g + 1) * hg], axis=1), ov_ref[g])
    blk = lane % N_SLC
    cur = tpos // SLC_LEN
    score = jnp.where((blk == cur) | (blk == 0), FORCE_SCORE, jnp.where(blk <= cur, imp, NEG))
    sct_ref[...] = score.T
    row8 = lax.broadcasted_iota(I32, (8, tq), 0)
    for g in range(NSA_GROUPS):
        base = g * N_SLC
        tiles = [sct_ref[base + 8 * v:base + 8 * (v + 1), :] for v in range(N_SLC // 8)]
        beaten = [jnp.zeros((8, tq), F32) for _ in tiles]
        for i in range(N_SLC):
            cand = jnp.broadcast_to(sct_ref[base + i:base + i + 1, :], (8, tq))
            for v, tile in enumerate(tiles):
                ge = jnp.where(cand >= tile, 1.0, 0.0)
                gt = jnp.where(cand > tile, 1.0, 0.0)
                if 8 * v > i:
                    wins = ge
                elif 8 * v + 7 <= i:
                    wins = gt
                else:
                    wins = jnp.where(row8 > i - 8 * v, ge, gt)
                beaten[v] = beaten[v] + wins
        for v in range(N_SLC // 8):
            selt_ref[0, base + 8 * v:base + 8 * (v + 1), :] = jnp.where(beaten[v] < N_SEL, 1.0, 0.0).astype(selt_ref.dtype)
    selt_ref[0, NSA_GROUPS * N_SLC:LANES, :] = jnp.zeros((LANES - NSA_GROUPS * N_SLC, tq), selt_ref.dtype)


def nsa_cmp(hb, kc, vc, ov, tq):
    B, S, _ = hb.shape
    qw = NSA_HEADS * NSA_DH
    return pl.pallas_call(
        functools.partial(_nsa_cmp_body, tq=tq),
        grid=(B, S // tq),
        in_specs=[pl.BlockSpec((1, tq, qw), lambda b, i: (b, i, NSA_Q_BLK)),
                  pl.BlockSpec((1, N_CMP_PAD, LANES), lambda b, i: (b, 0, 0)),
                  pl.BlockSpec((1, N_CMP_PAD, LANES), lambda b, i: (b, 0, 0)),
                  pl.BlockSpec(ov.shape, lambda b, i: (0, 0, 0))],
        out_specs=[pl.BlockSpec((1, tq, qw), lambda b, i: (b, i, 0)),
                   pl.BlockSpec((1, LANES, tq), lambda b, i: (b, 0, i))],
        out_shape=[jax.ShapeDtypeStruct((B, S, qw), F32), jax.ShapeDtypeStruct((B, LANES, S), BF16)],
        scratch_shapes=[pltpu.VMEM((LANES, tq), F32)],
        compiler_params=_cp("parallel", "parallel"), name="nsa_cmp")(hb, kc, vc, ov)


def _softmax_step(s, maskf, v, m, l, acc):
    m_new = jnp.maximum(m, jnp.max(s, axis=-1, keepdims=True))
    p = jnp.exp(s - m_new) * maskf
    alpha = jnp.exp(m - m_new)
    l = alpha * l + jnp.sum(p, axis=-1, keepdims=True)
    acc = alpha * acc + _dot(p.astype(BF16), v)
    return m_new, l, acc


FFN_TF = D_FF // 2
ROW_CHUNK = 32
LOG2E = math.log2(math.e)


FLASH_SLOTS = 2


def _flash_scratch(rows, tk):
    return ([pltpu.VMEM((rows, tk), F32)] * FLASH_SLOTS + [pltpu.VMEM((rows, tk), BF16)] * FLASH_SLOTS
            + [pltpu.VMEM((rows, LANES), F32)] * (FLASH_SLOTS + 3))


def _put_scores(dst, r0, raw, adjust):
    for i in range(raw.shape[0] // ROW_CHUNK):
        r = r0 + i * ROW_CHUNK
        dst[r:r + ROW_CHUNK, :] = adjust(r // ROW_CHUNK, raw[i * ROW_CHUNK:(i + 1) * ROW_CHUNK, :]) * LOG2E


def _softmax_pass1(s_ref, m_ref, a_ref):
    rows, tk = s_ref.shape
    m_in = m_out = m_ref
    for c in range(rows // ROW_CHUNK):
        r = slice(c * ROW_CHUNK, (c + 1) * ROW_CHUNK)
        s = s_ref[r, :]
        top = s[:, 0:LANES]
        for j in range(1, tk // LANES):
            top = jnp.maximum(top, s[:, j * LANES:(j + 1) * LANES])
        m_old = m_in[r, :]
        m_new = jnp.maximum(m_old, jnp.broadcast_to(jnp.max(top, axis=-1, keepdims=True), (ROW_CHUNK, LANES)))
        m_out[r, :] = m_new
        a_ref[r, :] = jnp.exp2(m_old - m_new)


def _softmax_pass2(s_ref, p_ref, m_ref, a_ref, l_ref):
    rows, tk = s_ref.shape
    for c in range(rows // ROW_CHUNK):
        r = slice(c * ROW_CHUNK, (c + 1) * ROW_CHUNK)
        m_new = m_ref[r, :]
        part = None
        for j in range(tk // LANES):
            p = jnp.exp2(s_ref[r, j * LANES:(j + 1) * LANES] - m_new)
            p_ref[r, j * LANES:(j + 1) * LANES] = p.astype(BF16)
            part = p if part is None else part + p
        l_ref[r, :] = a_ref[r, :] * l_ref[r, :] + part


def _flash_causal(n_full, tk, scores, pv_dot, scratch):
    n = FLASH_SLOTS
    s_bufs, p_bufs, a_bufs = (scratch[i * n:(i + 1) * n] for i in range(3))
    m_ref, l_ref, acc_ref = scratch[3 * n:]
    m_ref[...] = jnp.full(m_ref.shape, NEG, F32)
    a_bufs[n - 1][...] = jnp.zeros_like(a_bufs[0])
    p_bufs[n - 1][...] = jnp.zeros_like(p_bufs[0])
    l_ref[...] = jnp.zeros_like(l_ref)
    acc_ref[...] = jnp.zeros_like(acc_ref)

    def k_of(t):
        return pl.multiple_of(t * tk, tk)

    def pv(u, t):
        acc_ref[...] = a_bufs[u][...] * acc_ref[...] + pv_dot(p_bufs[u], k_of(t))

    NEXT_FULL, NEXT_LAST, FINAL = range(3)

    def stage(u, t, kind):
        pv((u - 1) % n, jnp.maximum(t - 1, 0))
        if kind != FINAL:
            scores(k_of(t + 1), s_bufs[(u + 1) % n], kind == NEXT_LAST)
        _softmax_pass1(s_bufs[u], m_ref, a_bufs[u])
        _softmax_pass2(s_bufs[u], p_bufs[u], m_ref, a_bufs[u], l_ref)
        if kind == FINAL:
            pv(u, t)

    def by_slot(t, kind):
        for u in range(n):
            @pl.when(t % n == u)
            def _():
                stage(u, t, kind)

    for first_is_last in (False, True):
        @pl.when((n_full == 0) == first_is_last)
        def _():
            scores(k_of(0), s_bufs[0], first_is_last)

    def body(t, carry):
        by_slot(t, NEXT_FULL)
        return carry

    lax.fori_loop(0, n_full - 1, body, 0)

    @pl.when(n_full >= 1)
    def _():
        by_slot(n_full - 1, NEXT_LAST)

    by_slot(n_full, FINAL)
    return acc_ref[...] / jnp.maximum(jnp.sum(l_ref[...], axis=-1, keepdims=True), TINY)


def _nsa_attn_body(q_ref, ks_ref, vs_ref, kw_ref, vw_ref, sel_ref, oc_ref, gt_ref, es_ref, o_ref,
                   selb_ref, ws_ref, wp_ref, wa_ref, wm_ref, wl_ref, *scratch, tq, tk):
    hg = NSA_HEADS // NSA_GROUPS
    span = WINDOW + tq
    qs = pl.program_id(1) * tq
    lane = _lane((1, LANES))
    first = lane < HALF
    zero = jnp.zeros((tq, LANES), BF16)
    q_all = jnp.concatenate(
        [jnp.where((lane // HALF) == g, q_ref[0, :, j * LANES:(j + 1) * LANES], zero)
         for g in range(NSA_GROUPS) for j in range(hg)], axis=0)
    for g in range(NSA_GROUPS):
        picked = lax.dot_general(sel_ref[0], es_ref[g], (((0,), (0,)), ((), ())), preferred_element_type=F32)
        selb_ref[g] = (picked - 1.0) * (-NEG)

    def t_of(c):
        head, r0 = divmod(c * ROW_CHUNK, tq)
        g, j = divmod(head, hg)
        return g, r0, qs + r0 + lax.broadcasted_iota(I32, (ROW_CHUNK, 1), 0), _nsa_slope(g, j)

    def slc_scores(k0, dst, last):
        kpos = k0 + _lane((1, tk))

        def adjust(c, s):
            g, r0, t, slope = t_of(c)
            s = s + slope * kpos.astype(F32) + selb_ref[g, r0:r0 + ROW_CHUNK, pl.ds(k0, tk)]
            return jnp.where(t >= kpos, s, NEG) if last else s

        _put_scores(dst, 0, _dot_nt(q_all, ks_ref[0, pl.ds(k0, tk), :]), adjust)

    o_slc = _flash_causal(qs // tk, tk, slc_scores,
                          lambda p_ref, k0: _dot(p_ref[...], vs_ref[0, pl.ds(k0, tk), :]), scratch)

    w0 = pl.multiple_of(jnp.maximum(qs - WINDOW, 0), tq)
    wpos = w0 + _lane((1, span))

    def win_adjust(c, s):
        _, _, t, slope = t_of(c)
        dist = t - wpos
        return jnp.where((dist >= 0) & (dist < WINDOW), s + slope * wpos.astype(F32), NEG)

    _put_scores(ws_ref, 0, _dot_nt(q_all, kw_ref[0, pl.ds(w0, span), :]), win_adjust)
    wm_ref[...] = jnp.full(wm_ref.shape, NEG, F32)
    wl_ref[...] = jnp.zeros_like(wl_ref)
    _softmax_pass1(ws_ref, wm_ref, wa_ref)
    _softmax_pass2(ws_ref, wp_ref, wm_ref, wa_ref, wl_ref)
    o_win = _dot(wp_ref[...], vw_ref[0, pl.ds(w0, span), :]) / jnp.maximum(
        jnp.sum(wl_ref[...], axis=-1, keepdims=True), TINY)

    gates = jax.nn.sigmoid(gt_ref[0])
    for j in range(hg):
        out = jnp.zeros((tq, LANES), F32)
        for br, branch in enumerate((None, o_slc, o_win)):
            gm = jnp.where(first, gates[:, j * 3 + br:j * 3 + br + 1],
                           gates[:, hg * 3 + j * 3 + br:hg * 3 + j * 3 + br + 1])
            if branch is None:
                val = oc_ref[0, :, j * LANES:(j + 1) * LANES]
            else:
                val = jnp.where(first, branch[j * tq:(j + 1) * tq], branch[(hg + j) * tq:(hg + j + 1) * tq])
            out = out + gm * val
        o_ref[0, :, j * LANES:(j + 1) * LANES] = out.astype(o_ref.dtype)


def nsa_attn(hb, hf, sel, ocmp, esel, tq, tk):
    B, S, _ = hb.shape
    qw = NSA_HEADS * NSA_DH
    rows = NSA_HEADS * tq
    span = WINDOW + tq
    kv = lambda blk: pl.BlockSpec((1, S, LANES), lambda b, i: (b, 0, blk))
    return pl.pallas_call(
        functools.partial(_nsa_attn_body, tq=tq, tk=tk),
        grid=(B, S // tq),
        in_specs=[pl.BlockSpec((1, tq, qw), lambda b, i: (b, i, NSA_Q_BLK)),
                  kv(NSA_KS_BLK), kv(NSA_VS_BLK), kv(NSA_KW_BLK), kv(NSA_VW_BLK),
                  pl.BlockSpec((1, LANES, tq), lambda b, i: (b, 0, i)),
                  pl.BlockSpec((1, tq, qw), lambda b, i: (b, i, 0)),
                  pl.BlockSpec((1, tq, LANES), lambda b, i: (b, i, NSA_GATE_BLK)),
                  pl.BlockSpec(esel.shape, lambda b, i: (0, 0, 0))],
        out_specs=pl.BlockSpec((1, tq, qw), lambda b, i: (b, i, 0)),
        out_shape=jax.ShapeDtypeStruct((B, S, qw), BF16),
        scratch_shapes=([pltpu.VMEM((NSA_GROUPS, tq, S), F32), pltpu.VMEM((rows, span), F32),
                         pltpu.VMEM((rows, span), BF16)]
                        + [pltpu.VMEM((rows, LANES), F32)] * 3 + _flash_scratch(rows, tk)),
        compiler_params=_cp("parallel", "parallel"), name="nsa_attn")(
            hb, hb, hb, hb, hb, sel, ocmp, hf, esel)


def _nsa_tables(S):
    n_cmp = (S - CMP_LEN) // CMP_STRIDE + 1
    c_start = np.arange(n_cmp) * CMP_STRIDE
    c_end = c_start + CMP_LEN - 1
    j_start = np.arange(S // SLC_LEN) * SLC_LEN
    overlap = (c_end[:, None] >= j_start[None]) & (c_start[:, None] <= j_start[None] + SLC_LEN - 1)
    hg = NSA_HEADS // NSA_GROUPS
    ov = np.zeros((NSA_GROUPS, hg, N_CMP_PAD, LANES), np.float32)
    es = np.zeros((NSA_GROUPS, LANES, S), np.float32)
    for g in range(NSA_GROUPS):
        ov[g, :, :n_cmp, g * N_SLC:(g + 1) * N_SLC] = overlap
        es[g, g * N_SLC + np.arange(S) // SLC_LEN, np.arange(S)] = 1.0
    return jnp.asarray(ov.reshape(NSA_GROUPS, hg * N_CMP_PAD, LANES), BF16), jnp.asarray(es, BF16)


NSA_PERM = np.array([(half * (NSA_HEADS // NSA_GROUPS) + j) * NSA_DH + d
                     for j in range(NSA_HEADS // NSA_GROUPS) for half in range(2) for d in range(NSA_DH)])


def nsa_mixer(hb, hf, pos_k, w1_k, w2_k, pos_v, w1_v, w2_v):
    B, S, _ = hb.shape
    assert S // CMP_STRIDE == N_CMP_PAD

    def both_groups(w):
        z = jnp.zeros_like(w)
        return jnp.concatenate([jnp.concatenate([w, z], axis=-1), jnp.concatenate([z, w], axis=-1)], axis=-2)

    def w1_blocks(w1):
        return both_groups(w1.reshape(CMP_LEN, NSA_DH, CMP_HIDDEN)).astype(BF16)

    kc, vc = nsa_compress(hf, jnp.tile(pos_k, (1, NSA_GROUPS)), jnp.tile(pos_v, (1, NSA_GROUPS)),
                          w1_blocks(w1_k), w1_blocks(w1_v),
                          both_groups(w2_k).astype(BF16), both_groups(w2_v).astype(BF16))
    ov, es = _nsa_tables(S)
    ocmp, sel = nsa_cmp(hb, kc, vc, ov, 256)
    return nsa_attn(hb, hf, sel, ocmp, es, 128, 256)


def _diff_body(slope_ref, q1_ref, q2_ref, k1_ref, k2_ref, v_ref, lam_ref, gn_ref, o_ref, *scratch,
               tq, tk, hp, lambda_init):
    heads = 2 * hp
    head0 = pl.program_id(1) * heads
    qs = pl.program_id(2) * tq
    lane = _lane((1, LANES))
    zero = jnp.zeros((tq, LANES), BF16)

    def q_rows(q_ref, h):
        return jnp.where((lane // HALF) == h % 2, q_ref[0, :, (h // 2) * LANES:(h // 2 + 1) * LANES], zero)

    q1 = [q_rows(q1_ref, h) for h in range(heads)]
    q2 = [q_rows(q2_ref, h) for h in range(heads)]
    slopes = [slope_ref[head0 + h] for h in range(heads)]

    def scores(k0, dst, last):
        kpos = k0 + _lane((1, tk))

        def adjust(c, s):
            s = s + slopes[(c * ROW_CHUNK) // (2 * tq)] * kpos.astype(F32)
            if last:
                t = qs + (c * ROW_CHUNK) % tq + lax.broadcasted_iota(I32, (ROW_CHUNK, 1), 0)
                s = jnp.where(t >= kpos, s, NEG)
            return s

        for h in range(heads):
            blk = slice((h // 2) * LANES, (h // 2 + 1) * LANES)
            _put_scores(dst, (2 * h) * tq, _dot_nt(q1[h], k1_ref[0, pl.ds(k0, tk), blk]), adjust)
            _put_scores(dst, (2 * h + 1) * tq, _dot_nt(q2[h], k2_ref[0, pl.ds(k0, tk), blk]), adjust)

    def pv_dot(p_ref, k0):
        return jnp.concatenate(
            [_dot(p_ref[2 * h * tq:2 * (h + 1) * tq, :], v_ref[0, pl.ds(k0, tk), h * LANES:(h + 1) * LANES])
             for h in range(heads)], axis=0)

    o = _flash_causal(qs // tk, tk, scores, pv_dot, scratch)
    lam_rows = lam_ref[...]
    lam = (jnp.exp(jnp.sum(lam_rows[0:1] * lam_rows[1:2], axis=-1, keepdims=True))
           - jnp.exp(jnp.sum(lam_rows[2:3] * lam_rows[3:4], axis=-1, keepdims=True)) + lambda_init)
    for h in range(heads):
        r0 = 2 * h * tq
        y = _rms(o[r0:r0 + tq] - lam * o[r0 + tq:r0 + 2 * tq], gn_ref[...]) * (1.0 - lambda_init)
        o_ref[0, :, h * LANES:(h + 1) * LANES] = y.astype(o_ref.dtype)


def diff_mixer(hd, lam_rows, g_norm, lambda_init, tq, tk, hp):
    B, S, _ = hd.shape
    ng = DIFF_HEADS // (2 * hp)
    qk, vw = hp * LANES, 2 * hp * LANES
    slopes = jnp.asarray(2.0 ** (-8.0 * np.arange(1, DIFF_HEADS + 1) / DIFF_HEADS), F32)
    grid_spec = pltpu.PrefetchScalarGridSpec(
        num_scalar_prefetch=1,
        grid=(B, ng, S // tq),
        in_specs=[pl.BlockSpec((1, tq, qk), lambda b, p, i, s: (b, i, p)),
                  pl.BlockSpec((1, tq, qk), lambda b, p, i, s: (b, i, ng + p)),
                  pl.BlockSpec((1, S, qk), lambda b, p, i, s: (b, 0, 2 * ng + p)),
                  pl.BlockSpec((1, S, qk), lambda b, p, i, s: (b, 0, 3 * ng + p)),
                  pl.BlockSpec((1, S, vw), lambda b, p, i, s: (b, 0, 2 * ng + p)),
                  pl.BlockSpec((4, LANES), lambda b, p, i, s: (0, 0)),
                  pl.BlockSpec((1, LANES), lambda b, p, i, s: (0, 0))],
        out_specs=pl.BlockSpec((1, tq, vw), lambda b, p, i, s: (b, i, p)),
        scratch_shapes=_flash_scratch(4 * hp * tq, tk))
    return pl.pallas_call(
        functools.partial(_diff_body, tq=tq, tk=tk, hp=hp, lambda_init=lambda_init),
        grid_spec=grid_spec,
        out_shape=jax.ShapeDtypeStruct((B, S, DIFF_HEADS * DIFF_DV), BF16),
        compiler_params=_cp("parallel", "parallel", "parallel"), name="diff_attn")(
            slopes, hd, hd, hd, hd, hd, lam_rows, g_norm.reshape(1, LANES))


def _router_body(x_ref, g_ref, wr_ref, tri_ref, xn_ref, meta_ref, wts_ref, cnt_ref, run_ref):
    @pl.when(pl.program_id(0) == 0)
    def _():
        run_ref[...] = jnp.zeros_like(run_ref)

    xn = _rms(x_ref[...], g_ref[...])
    xn_ref[...] = xn
    lane = _lane((1, LANES))
    logits = jnp.dot(xn, wr_ref[...], precision=lax.Precision.HIGHEST, preferred_element_type=F32)
    logits = jnp.where(lane < N_EXPERTS, logits, LOWEST)
    m1 = jnp.max(logits, axis=-1, keepdims=True)
    i1 = jnp.min(jnp.where(logits == m1, lane, LANES), axis=-1, keepdims=True)
    rest = jnp.where(lane == i1, LOWEST, logits)
    m2 = jnp.max(rest, axis=-1, keepdims=True)
    i2 = jnp.min(jnp.where(rest == m2, lane, LANES), axis=-1, keepdims=True)
    e = jnp.exp(m2 - m1)
    w1 = 1.0 / (1.0 + e)
    w2 = e / (1.0 + e)
    onehot = ((lane == i1) | (lane == i2)).astype(F32)
    pos = _dot(tri_ref[...], onehot.astype(BF16)) + run_ref[...]
    p1 = jnp.sum(jnp.where(lane == i1, pos, 0.0), axis=-1, keepdims=True).astype(I32)
    p2 = jnp.sum(jnp.where(lane == i2, pos, 0.0), axis=-1, keepdims=True).astype(I32)
    run_ref[...] += jnp.sum(onehot, axis=0, keepdims=True)
    meta_ref[...] = jnp.where(lane == 0, i1, jnp.where(lane == 1, i2, jnp.where(lane == 2, p1, jnp.where(lane == 3, p2, 0))))
    wts_ref[...] = jnp.where(lane == 0, w1, jnp.where(lane == 1, w2, 0.0))
    cnt_ref[...] = run_ref[...]


def moe_router(x, g, wr, tm):
    T, D = x.shape
    tri = jnp.asarray(np.tril(np.ones((tm, tm), np.float32), -1), BF16)
    return pl.pallas_call(
        _router_body,
        grid=(T // tm,),
        in_specs=[pl.BlockSpec((tm, D), lambda i: (i, 0)),
                  pl.BlockSpec((1, D), lambda i: (0, 0)),
                  pl.BlockSpec((D, LANES), lambda i: (0, 0)),
                  pl.BlockSpec((tm, tm), lambda i: (0, 0))],
        out_specs=[pl.BlockSpec((tm, D), lambda i: (i, 0)),
                   pl.BlockSpec((tm, LANES), lambda i: (i, 0)),
                   pl.BlockSpec((tm, LANES), lambda i: (i, 0)),
                   pl.BlockSpec((1, LANES), lambda i: (0, 0))],
        out_shape=[jax.ShapeDtypeStruct((T, D), F32), jax.ShapeDtypeStruct((T, LANES), I32),
                   jax.ShapeDtypeStruct((T, LANES), F32), jax.ShapeDtypeStruct((1, LANES), F32)],
        scratch_shapes=[pltpu.VMEM((1, LANES), F32)],
        compiler_params=_cp("arbitrary"), name="moe_router")(x, g.reshape(1, D), wr, tri)


ROW_DMA_UNROLL = 8


def _row_copy(src_ref, src_row, dst_ref, dst_row, sem):
    return pltpu.make_async_copy(src_ref.at[pl.ds(src_row, 1)], dst_ref.at[pl.ds(dst_row, 1)], sem)


def _dispatch_body(dest_ref, xn_ref, zero_ref, xs_ref, sem, *, tm):
    del zero_ref
    base = pl.program_id(0) * tm

    def issue(t, c):
        for slot in range(TOP_K):
            _row_copy(xn_ref, t, xs_ref, dest_ref[TOP_K * (base + t) + slot], sem).start()
        return c

    lax.fori_loop(0, tm, issue, 0, unroll=ROW_DMA_UNROLL)

    def drain(t, c):
        for slot in range(TOP_K):
            _row_copy(xn_ref, 0, xs_ref, 0, sem).wait()
        return c

    lax.fori_loop(0, tm, drain, 0, unroll=ROW_DMA_UNROLL)


def moe_dispatch(dest, xn, rows, tm):
    T, D = xn.shape
    grid_spec = pltpu.PrefetchScalarGridSpec(
        num_scalar_prefetch=1,
        grid=(T // tm,),
        in_specs=[pl.BlockSpec((tm, D), lambda i, d: (i, 0)),
                  pl.BlockSpec(memory_space=pl.ANY)],
        out_specs=pl.BlockSpec(memory_space=pl.ANY),
        scratch_shapes=[pltpu.SemaphoreType.DMA])
    return pl.pallas_call(
        functools.partial(_dispatch_body, tm=tm),
        grid_spec=grid_spec,
        out_shape=jax.ShapeDtypeStruct((rows, D), F32),
        input_output_aliases={2: 0},
        compiler_params=_cp("arbitrary"), name="moe_dispatch")(dest, xn, jnp.zeros((rows, D), F32))


def _experts_body(te_ref, tv_ref, xs_ref, wg_ref, wu_ref, wd_ref, o_ref, xb_ref):
    i = pl.program_id(0)

    @pl.when(pl.program_id(1) == 0)
    def _():
        xb_ref[...] = xs_ref[...].astype(BF16)
        o_ref[...] = jnp.zeros_like(o_ref)

    @pl.when(tv_ref[i] > 0)
    def _():
        xb = xb_ref[...]
        hg = _dot(xb, wg_ref[0])
        hu = _dot(xb, wu_ref[0])
        o_ref[...] += _dot((hg * jax.nn.sigmoid(hg) * hu).astype(BF16), wd_ref[0])


def moe_experts(tile_expert, tile_valid, xs, wg, wu, wd, tr, tf):
    R, D = xs.shape
    F = wg.shape[2]
    grid_spec = pltpu.PrefetchScalarGridSpec(
        num_scalar_prefetch=2,
        grid=(R // tr, F // tf),
        in_specs=[pl.BlockSpec((tr, D), lambda i, k, te, tv: (i, 0)),
                  pl.BlockSpec((1, D, tf), lambda i, k, te, tv: (te[i], 0, k)),
                  pl.BlockSpec((1, D, tf), lambda i, k, te, tv: (te[i], 0, k)),
                  pl.BlockSpec((1, tf, D), lambda i, k, te, tv: (te[i], k, 0))],
        out_specs=pl.BlockSpec((tr, D), lambda i, k, te, tv: (i, 0)),
        scratch_shapes=[pltpu.VMEM((tr, D), BF16)])
    return pl.pallas_call(
        _experts_body,
        grid_spec=grid_spec,
        out_shape=jax.ShapeDtypeStruct((R, D), F32),
        compiler_params=_cp("parallel", "arbitrary"), name="moe_experts")(tile_expert, tile_valid, xs, wg, wu, wd)


def _combine_body(dest_ref, x_ref, wts_ref, g_ref, ys_ref, o_ref, buf_ref, sem, *, tm):
    base = pl.program_id(0) * tm

    def issue(t, c):
        for slot in range(TOP_K):
            _row_copy(ys_ref, dest_ref[TOP_K * (base + t) + slot], buf_ref.at[slot], t, sem).start()
        return c

    lax.fori_loop(0, tm, issue, 0, unroll=ROW_DMA_UNROLL)

    def drain(t, c):
        for slot in range(TOP_K):
            _row_copy(ys_ref, 0, buf_ref.at[slot], 0, sem).wait()
        return c

    lax.fori_loop(0, tm, drain, 0, unroll=ROW_DMA_UNROLL)
    wts = wts_ref[...]
    y = x_ref[...] + (wts[:, 0:1] * buf_ref[0] + wts[:, 1:2] * buf_ref[1])
    o_ref[...] = _rms(y, g_ref[...])


def moe_combine(dest, x, wts, g_final, ys, tm):
    T, D = x.shape
    grid_spec = pltpu.PrefetchScalarGridSpec(
        num_scalar_prefetch=1,
        grid=(T // tm,),
        in_specs=[pl.BlockSpec((tm, D), lambda i, d: (i, 0)),
                  pl.BlockSpec((tm, LANES), lambda i, d: (i, 0)),
                  pl.BlockSpec((1, D), lambda i, d: (0, 0)),
                  pl.BlockSpec(memory_space=pl.ANY)],
        out_specs=pl.BlockSpec((tm, D), lambda i, d: (i, 0)),
        scratch_shapes=[pltpu.VMEM((TOP_K, tm, D), F32), pltpu.SemaphoreType.DMA])
    return pl.pallas_call(
        functools.partial(_combine_body, tm=tm),
        grid_spec=grid_spec,
        out_shape=jax.ShapeDtypeStruct((T, D), F32),
        compiler_params=_cp("arbitrary"), name="moe_combine")(dest, x, wts, g_final.reshape(1, D), ys)


def moe_block(x, g, w_router, wg, wu, wd, g_final, tr):
    T, D = x.shape
    wr = _pad_cols(w_router, LANES)
    xn, meta, wts, counts = moe_router(x, g, wr, 512)
    cnt = counts[0, :N_EXPERTS].astype(I32)
    gsz = ((cnt + tr - 1) // tr) * tr
    gend = jnp.cumsum(gsz)
    goff = gend - gsz
    dest = (jnp.take(goff, meta[:, 0:TOP_K]) + meta[:, TOP_K:2 * TOP_K]).reshape(-1)
    n_tiles = (TOP_K * T) // tr + N_EXPERTS
    tile_start = jnp.arange(n_tiles, dtype=I32) * tr
    tile_valid = (tile_start < gend[-1]).astype(I32)
    tile_expert = jnp.minimum(jnp.sum((tile_start[:, None] >= gend[None, :]).astype(I32), axis=1), N_EXPERTS - 1)
    last_valid = jnp.take(tile_expert, jnp.maximum(gend[-1] // tr - 1, 0))
    tile_expert = jnp.where(tile_valid > 0, tile_expert, last_valid)
    xs = moe_dispatch(dest, xn, n_tiles * tr, 256)
    ys = moe_experts(tile_expert, tile_valid, xs, wg, wu, wd, tr, FFN_TF)
    return moe_combine(dest, x, wts, g_final, ys, 256)


def _pad_cols(w, width):
    return jnp.pad(w, ((0, 0), (0, width - w.shape[1])))


def _mem_kv(mem2, g, wk, wv, batch):
    w = jnp.concatenate([wk, wv], axis=1).astype(BF16)
    width = wk.shape[1]
    k, v = norm_proj(mem2, g, w, [(width, BF16), (width, BF16)], 256, "mem_kv")
    return k.reshape(batch, -1, width), v.reshape(batch, -1, width)


def kernel(x, mem, mix_norm_0, w_in_0, gla_w_alpha_0, gla_b_alpha_0, gla_out_norm_0, nsa_cmp_pos_k_0, nsa_cmp_w1_k_0, nsa_cmp_w2_k_0, nsa_cmp_pos_v_0, nsa_cmp_w1_v_0, nsa_cmp_w2_v_0, w_out_0, xattn_norm_0, xattn_mem_norm_0, xattn_wq_0, xattn_wk_0, xattn_wv_0, xattn_wo_0, ffn_norm_0, ffn_w_gate_0, ffn_w_up_0, ffn_w_down_0, mix_norm_1, w_in_1, diff_lq1_1, diff_lk1_1, diff_lq2_1, diff_lk2_1, diff_out_norm_1, w_out_1, xattn_norm_1, xattn_mem_norm_1, xattn_wq_1, xattn_wk_1, xattn_wv_1, xattn_wo_1, ffn_norm_1, moe_router_1, moe_w_gate_1, moe_w_up_1, moe_w_down_1, final_norm):
    B, S, D = x.shape
    T = B * S
    x2 = x.reshape(T, D)
    mem2 = mem.reshape(-1, D)
    q_scale = NSA_DH ** -0.5

    sizes = [GLA_HEADS * GLA_DK, GLA_HEADS * GLA_DK, GLA_HEADS * GLA_DV, GLA_HEADS * GLA_DV, GLA_RANK,
             NSA_HEADS * NSA_DH] + [NSA_GROUPS * NSA_DH] * 6 + [3 * NSA_HEADS]
    (g_q, g_k, g_v, g_r, g_a, n_q, n_kc, n_vc, n_ks, n_vs, n_kw, n_vw, n_g) = jnp.split(
        w_in_0, np.cumsum(sizes)[:-1].tolist(), axis=1)
    w0 = jnp.concatenate([n_q[:, NSA_PERM] * q_scale, n_ks, n_vs, n_kw, n_vw, g_q * q_scale, g_k, g_v,
                          _pad_cols(g_a, LANES), g_r, _pad_cols(n_g, LANES), n_kc, n_vc], axis=1).astype(BF16)
    hb, hf = norm_proj(x2, mix_norm_0, w0, [(HB0_WIDTH, BF16), (HF0_WIDTH, F32)], 512, "in_proj0")
    hb = hb.reshape(B, S, HB0_WIDTH)
    hf = hf.reshape(B, S, HF0_WIDTH)
    w_alpha = jnp.pad(gla_w_alpha_0, ((0, LANES - GLA_RANK), (0, 0))).astype(BF16)
    o_a = gla_mixer(hb, hf, w_alpha, gla_b_alpha_0.reshape(1, -1), gla_out_norm_0, GLA_COLS, 8)
    o_b = nsa_mixer(hb, hf, nsa_cmp_pos_k_0, nsa_cmp_w1_k_0, nsa_cmp_w2_k_0,
                    nsa_cmp_pos_v_0, nsa_cmp_w1_v_0, nsa_cmp_w2_v_0)
    n_a = GLA_HEADS * GLA_DV
    k0, v0 = _mem_kv(mem2, xattn_mem_norm_0, xattn_wk_0, xattn_wv_0, B)
    x2 = proj_xattn_block(x2, [o_a.reshape(T, -1), o_b.reshape(T, -1)],
                          [w_out_0[:n_a].astype(BF16), w_out_0[n_a:][NSA_PERM].astype(BF16)],
                          xattn_norm_0, (xattn_wq_0 * q_scale).astype(BF16), k0, v0, xattn_wo_0.astype(BF16),
                          S, 1024, "out_xattn0")
    x2 = ffn_block(x2, ffn_norm_0, ffn_w_gate_0.astype(BF16), ffn_w_up_0.astype(BF16), ffn_w_down_0.astype(BF16),
                   512, FFN_TF, "ffn0")

    lambda_init = 0.8 - 0.6 * math.exp(-0.3 * 1)
    n_q1 = 2 * DIFF_HEADS * DIFF_DH
    w1 = jnp.concatenate([w_in_1[:, :n_q1] * q_scale, w_in_1[:, n_q1:]], axis=1).astype(BF16)
    (hd,) = norm_proj(x2, mix_norm_1, w1, [(w1.shape[1], BF16)], 512, "in_proj1")
    lam_rows = _pad_cols(jnp.stack([diff_lq1_1, diff_lk1_1, diff_lq2_1, diff_lk2_1]), LANES)
    o_d = diff_mixer(hd.reshape(B, S, -1), lam_rows, diff_out_norm_1, lambda_init, 256, 256, 4)
    k1, v1 = _mem_kv(mem2, xattn_mem_norm_1, xattn_wk_1, xattn_wv_1, B)
    x2 = proj_xattn_block(x2, [o_d.reshape(T, -1)], [w_out_1.astype(BF16)],
                          xattn_norm_1, (xattn_wq_1 * q_scale).astype(BF16), k1, v1, xattn_wo_1.astype(BF16),
                          S, 1024, "out_xattn1")
    out = moe_block(x2, ffn_norm_1, moe_router_1, moe_w_gate_1.astype(BF16), moe_w_up_1.astype(BF16),
                    moe_w_down_1.astype(BF16), final_norm, 512)
    return out.reshape(B, S, D)
```

```python
import functools
import math

import numpy as np
import jax
import jax.numpy as jnp
from jax import lax
from jax.experimental import pallas as pl
from jax.experimental.pallas import tpu as pltpu

F32 = jnp.float32
BF16 = jnp.bfloat16
I32 = jnp.int32

D_MODEL = 1024
GLA_HEADS, GLA_DK, GLA_DV, GLA_RANK, GLA_TAU, GLA_CHUNK = 4, 64, 128, 16, 16.0, 64
NSA_HEADS, NSA_GROUPS, NSA_DH = 8, 2, 64
CMP_LEN, CMP_STRIDE, CMP_HIDDEN, SLC_LEN, N_SEL, WINDOW = 32, 16, 256, 64, 8, 512
DIFF_HEADS, DIFF_DH, DIFF_DV = 8, 64, 128
XATTN_HEADS, XATTN_DH = 4, 64
D_FF, N_EXPERTS, TOP_K = 2816, 8, 2
EPS, NEG, TINY, FORCE_SCORE = 1e-6, -1e30, 1e-30, 1e4

LANES = 128
HALF = LANES // 2
VMEM_LIMIT = 56 * 1024 * 1024
LOWEST = -3.0e38

HB0_WIDTH, HF0_WIDTH = 17 * LANES, 7 * LANES
NSA_Q_BLK, NSA_KS_BLK, NSA_VS_BLK, NSA_KW_BLK, NSA_VW_BLK = 0, 4, 5, 6, 7
GLA_COLS = (8, 10, 12, 16, 0)
NSA_GATE_BLK, NSA_KC_BLK, NSA_VC_BLK = 4, 5, 6


def _cp(*sem, flags=None):
    return pltpu.CompilerParams(dimension_semantics=sem, vmem_limit_bytes=VMEM_LIMIT, flags=flags)


def _rms(x, g):
    y = x * lax.rsqrt(jnp.mean(x * x, axis=-1, keepdims=True) + EPS)
    return y * g


def _dot(a, b):
    return jnp.dot(a, b, preferred_element_type=F32)


def _dot_nt(a, b):
    return lax.dot_general(a, b, (((1,), (1,)), ((), ())), preferred_element_type=F32)


def _lane(shape):
    return lax.broadcasted_iota(I32, shape, len(shape) - 1)


def _norm_proj_body(x_ref, g_ref, w_ref, *o_refs, widths):
    xn = _rms(x_ref[...], g_ref[...]).astype(BF16)
    off = 0
    for o_ref, wd in zip(o_refs, widths):
        for c0 in range(0, wd, 512):
            cw = min(512, wd - c0)
            o_ref[:, c0:c0 + cw] = _dot(xn, w_ref[:, off + c0:off + c0 + cw]).astype(o_ref.dtype)
        off += wd


def norm_proj(x, g, w, outs, tm, name):
    T, D = x.shape
    widths = tuple(o[0] for o in outs)
    return pl.pallas_call(
        functools.partial(_norm_proj_body, widths=widths),
        grid=(T // tm,),
        in_specs=[pl.BlockSpec((tm, D), lambda i: (i, 0)),
                  pl.BlockSpec((1, D), lambda i: (0, 0)),
                  pl.BlockSpec(w.shape, lambda i: (0, 0))],
        out_specs=[pl.BlockSpec((tm, wd), lambda i: (i, 0)) for wd, _ in outs],
        out_shape=[jax.ShapeDtypeStruct((T, wd), dt) for wd, dt in outs],
        compiler_params=_cp("parallel"), name=name)(x, g.reshape(1, D), w)


def _proj_xattn_body(*refs, n):
    res_ref = refs[0]
    a_refs = refs[1:1 + n]
    w_refs = refs[1 + n:1 + 2 * n]
    g_ref, wq_ref, k_ref, v_ref, wo_ref, o_ref = refs[1 + 2 * n:]
    x = res_ref[...]
    for a_ref, w_ref in zip(a_refs, w_refs):
        x = x + _dot(a_ref[...].astype(BF16), w_ref[...])
    q = _dot(_rms(x, g_ref[...]).astype(BF16), wq_ref[...]).astype(BF16)
    tm = q.shape[0]
    head_of_lane = _lane((1, q.shape[1])) // XATTN_DH
    zero = jnp.zeros_like(q)
    q_all = jnp.concatenate([jnp.where(head_of_lane == h, q, zero) for h in range(XATTN_HEADS)], axis=0)
    s = _dot_nt(q_all, k_ref[0])
    p = jnp.exp(s - jnp.max(s, axis=-1, keepdims=True))
    p = p / jnp.sum(p, axis=-1, keepdims=True)
    o_all = _dot(p.astype(BF16), v_ref[0])
    o = o_all[0:tm]
    for h in range(1, XATTN_HEADS):
        o = jnp.where(head_of_lane == h, o_all[h * tm:(h + 1) * tm], o)
    o_ref[...] = x + _dot(o.astype(BF16), wo_ref[...])


def proj_xattn_block(res, a_list, w_list, g, wq, k, v, wo, seq, tm, name):
    T, D = res.shape
    n = len(a_list)
    M, W = k.shape[1], k.shape[2]
    per = seq // tm
    return pl.pallas_call(
        functools.partial(_proj_xattn_body, n=n),
        grid=(T // tm,),
        in_specs=([pl.BlockSpec((tm, D), lambda i: (i, 0))]
                  + [pl.BlockSpec((tm, a.shape[1]), lambda i: (i, 0)) for a in a_list]
                  + [pl.BlockSpec(w.shape, lambda i: (0, 0)) for w in w_list]
                  + [pl.BlockSpec((1, D), lambda i: (0, 0)),
                     pl.BlockSpec(wq.shape, lambda i: (0, 0)),
                     pl.BlockSpec((1, M, W), lambda i: (i // per, 0, 0)),
                     pl.BlockSpec((1, M, W), lambda i: (i // per, 0, 0)),
                     pl.BlockSpec(wo.shape, lambda i: (0, 0))]),
        out_specs=pl.BlockSpec((tm, D), lambda i: (i, 0)),
        out_shape=jax.ShapeDtypeStruct((T, D), F32),
        compiler_params=_cp("parallel"), name=name)(res, *a_list, *w_list, g.reshape(1, D), wq, k, v, wo)


def _ffn_body(x_ref, g_ref, wg_ref, wu_ref, wd_ref, o_ref, xn_ref):
    @pl.when(pl.program_id(1) == 0)
    def _():
        xn_ref[...] = _rms(x_ref[...], g_ref[...]).astype(BF16)
        o_ref[...] = x_ref[...]

    xn = xn_ref[...]
    hg = _dot(xn, wg_ref[...])
    hu = _dot(xn, wu_ref[...])
    h = (hg * jax.nn.sigmoid(hg) * hu).astype(BF16)
    o_ref[...] += _dot(h, wd_ref[...])


def ffn_block(x, g, wg, wu, wd, tm, tf, name):
    T, D = x.shape
    F = wg.shape[1]
    return pl.pallas_call(
        _ffn_body,
        grid=(T // tm, F // tf),
        in_specs=[pl.BlockSpec((tm, D), lambda i, k: (i, 0)),
                  pl.BlockSpec((1, D), lambda i, k: (0, 0)),
                  pl.BlockSpec((D, tf), lambda i, k: (0, k)),
                  pl.BlockSpec((D, tf), lambda i, k: (0, k)),
                  pl.BlockSpec((tf, D), lambda i, k: (k, 0))],
        out_specs=pl.BlockSpec((tm, D), lambda i, k: (i, 0)),
        out_shape=jax.ShapeDtypeStruct((T, D), F32),
        scratch_shapes=[pltpu.VMEM((tm, D), BF16)],
        compiler_params=_cp("parallel", "arbitrary"), name=name)(x, g.reshape(1, D), wg, wu, wd)


GLA_TILE = 128
_GLA_LEVELS = tuple(GLA_TILE >> (i + 1) for i in range(GLA_TILE.bit_length() - 1))


def _gla_constants():
    C = GLA_TILE
    t = np.arange(C)
    r = t[None, :]
    mats = [r <= t[:, None], r > t[:, None]]
    for hs in _GLA_LEVELS:
        c = (t // (2 * hs)) * (2 * hs) + hs - 1
        right = (t % (2 * hs)) >= hs
        m_right = (r > c[:, None]) & (r <= t[:, None])
        m_left = (r > t[:, None]) & (r <= c[:, None])
        mats.append(np.where(right[:, None], m_right, m_left))
    cmat = np.concatenate(mats, 0).astype(np.float32)
    x = t[:, None] ^ t[None, :]
    lvl = np.full((C, C), -1, np.int32)
    for i, hs in enumerate(_GLA_LEVELS):
        lvl[(t[:, None] > t[None, :]) & (x >= hs) & (x < 2 * hs)] = i
    lvl[t[:, None] == t[None, :]] = len(_GLA_LEVELS)
    return cmat, lvl


def _gla_body(q_ref, k_ref, v_ref, a_ref, r_ref, wa_ref, ba_ref, gn_ref, cmat_ref, lvl_ref, o_ref, st_ref, *, nb):
    C = GLA_TILE

    @pl.when(pl.program_id(2) == 0)
    def _():
        st_ref[...] = jnp.zeros_like(st_ref)

    nlev = len(_GLA_LEVELS)
    in_a = _lane((1, LANES)) < HALF
    lvl = lvl_ref[...]
    cmat = cmat_ref[...]
    on_diag = (lax.broadcasted_iota(I32, (LANES, 2 * LANES), 0) // HALF
               == lax.broadcasted_iota(I32, (LANES, 2 * LANES), 1) // LANES)
    for bi in range(nb):
        q = q_ref[bi].astype(F32)
        k = k_ref[bi].astype(F32)
        z = _dot(a_ref[bi], wa_ref[...]) + ba_ref[...]
        la = (jnp.minimum(z, 0.0) - jnp.log1p(jnp.exp(-jnp.abs(z)))) / GLA_TAU
        hi = la.astype(BF16)
        rest = la - hi.astype(F32)
        mid = rest.astype(BF16)
        lo = (rest - mid.astype(F32)).astype(BF16)
        u3 = _dot(cmat, jnp.concatenate([hi, mid, lo], axis=1))
        e = jnp.exp(u3[:, 0:LANES] + u3[:, LANES:2 * LANES] + u3[:, 2 * LANES:3 * LANES])
        qhat = (q * e[0:C]).astype(BF16)
        kend = (k * e[C:2 * C]).astype(BF16)
        decay = jnp.broadcast_to(e[C - 1:C], (LANES, LANES)).T
        zero = jnp.zeros_like(q)

        def stacked(x):
            return jnp.concatenate([jnp.where(in_a, x, zero), jnp.where(in_a, zero, x)], axis=0).astype(BF16)

        sc = jnp.zeros((2 * C, C), F32)
        for i in range(nlev):
            ei = e[(2 + i) * C:(3 + i) * C]
            sc = jnp.where(lvl == i, _dot_nt(stacked(q * ei), (k * ei).astype(BF16)), sc)
        sc = jnp.where(lvl == nlev, _dot_nt(stacked(q), k.astype(BF16)), sc)
        v = v_ref[bi]
        state = st_ref[bi]
        o_inter = _dot(qhat, state.astype(BF16))
        o_intra = _dot(sc.astype(BF16), v)
        update = lax.dot_general(kend, v, (((0,), (0,)), ((), ())), preferred_element_type=F32)
        st_ref[bi] = jnp.where(on_diag, jnp.concatenate([decay, decay], axis=1) * state + update, 0.0)
        for half in range(2):
            cols = slice(half * LANES, (half + 1) * LANES)
            y = _rms(o_inter[:, cols] + o_intra[half * C:(half + 1) * C, cols], gn_ref[...])
            rr = r_ref[bi, :, cols]
            o_ref[bi, :, cols] = (y * (rr * jax.nn.sigmoid(rr))).astype(o_ref.dtype)


def gla_mixer(hb, hf, w_alpha, b_alpha, g_norm, cols, nb):
    B, S, _ = hb.shape
    C = GLA_TILE
    cmat, lvl = _gla_constants()
    lvl = np.concatenate([lvl, lvl], axis=0)
    qb, kb_, vb, ab, rb = cols
    return pl.pallas_call(
        functools.partial(_gla_body, nb=nb),
        grid=(B // nb, GLA_HEADS // 2, S // C),
        in_specs=[pl.BlockSpec((nb, C, LANES), lambda b, p, c: (b, c, qb + p)),
                  pl.BlockSpec((nb, C, LANES), lambda b, p, c: (b, c, kb_ + p)),
                  pl.BlockSpec((nb, C, 2 * LANES), lambda b, p, c: (b, c, vb // 2 + p)),
                  pl.BlockSpec((nb, C, LANES), lambda b, p, c: (b, c, ab)),
                  pl.BlockSpec((nb, C, 2 * LANES), lambda b, p, c: (b, c, rb // 2 + p)),
                  pl.BlockSpec((LANES, LANES), lambda b, p, c: (0, p)),
                  pl.BlockSpec((1, LANES), lambda b, p, c: (0, p)),
                  pl.BlockSpec((1, LANES), lambda b, p, c: (0, 0)),
                  pl.BlockSpec(cmat.shape, lambda b, p, c: (0, 0)),
                  pl.BlockSpec(lvl.shape, lambda b, p, c: (0, 0))],
        out_specs=pl.BlockSpec((nb, C, 2 * LANES), lambda b, p, c: (b, c, p)),
        out_shape=jax.ShapeDtypeStruct((B, S, GLA_HEADS * GLA_DV), BF16),
        scratch_shapes=[pltpu.VMEM((nb, LANES, 2 * LANES), F32)],
        compiler_params=_cp("parallel", "parallel", "arbitrary"), name="gla")(
            hb, hb, hb, hb, hf, w_alpha, b_alpha, g_norm.reshape(1, LANES), jnp.asarray(cmat, BF16), jnp.asarray(lvl))


N_CMP_PAD = 128
N_SLC = 32


def _gelu_tanh(x):
    return 0.5 * x * (1.0 + jnp.tanh(math.sqrt(2.0 / math.pi) * (x + 0.044715 * (x * x * x))))


def _nsa_compress_body(tk_ref, tv_ref, pk_ref, pv_ref, w1k_ref, w1v_ref, w2k_ref, w2v_ref, kc_ref, vc_ref):
    for t_ref, p_ref, w1_ref, w2_ref, o_ref in ((tk_ref, pk_ref, w1k_ref, w2k_ref, kc_ref),
                                                (tv_ref, pv_ref, w1v_ref, w2v_ref, vc_ref)):
        lo = jnp.zeros((N_CMP_PAD, NSA_GROUPS * CMP_HIDDEN), F32)
        hi = jnp.zeros((N_CMP_PAD, NSA_GROUPS * CMP_HIDDEN), F32)
        for l in range(CMP_STRIDE):
            rows = t_ref[0, pl.ds(l, N_CMP_PAD, stride=CMP_STRIDE), :]
            lo = lo + _dot((rows + p_ref[l:l + 1]).astype(BF16), w1_ref[l])
            hi = hi + _dot((rows + p_ref[CMP_STRIDE + l:CMP_STRIDE + l + 1]).astype(BF16), w1_ref[CMP_STRIDE + l])
        pre = lo + pltpu.roll(hi, N_CMP_PAD - 1, 0)
        o_ref[0] = _dot(_gelu_tanh(pre).astype(BF16), w2_ref[...]).astype(o_ref.dtype)


def nsa_compress(hf, pk, pv, w1k, w1v, w2k, w2v):
    B, S, _ = hf.shape
    full = lambda a: pl.BlockSpec(a.shape, lambda b: (0,) * a.ndim)
    out = pl.BlockSpec((1, N_CMP_PAD, LANES), lambda b: (b, 0, 0))
    return pl.pallas_call(
        _nsa_compress_body,
        grid=(B,),
        in_specs=[pl.BlockSpec((1, S, LANES), lambda b: (b, 0, NSA_KC_BLK)),
                  pl.BlockSpec((1, S, LANES), lambda b: (b, 0, NSA_VC_BLK)),
                  full(pk), full(pv), full(w1k), full(w1v), full(w2k), full(w2v)],
        out_specs=[out, out],
        out_shape=[jax.ShapeDtypeStruct((B, N_CMP_PAD, LANES), BF16)] * 2,
        compiler_params=_cp("parallel"), name="nsa_compress")(hf, hf, pk, pv, w1k, w1v, w2k, w2v)


def _nsa_slope(g, j):
    return 2.0 ** (-(g * (NSA_HEADS // NSA_GROUPS) + j + 1))


def _nsa_cmp_body(q_ref, kc_ref, vc_ref, ov_ref, o_ref, selt_ref, sct_ref, *, tq):
    hg = NSA_HEADS // NSA_GROUPS
    qs = pl.program_id(1) * tq
    lane = _lane((1, LANES))
    tpos = qs + lax.broadcasted_iota(I32, (tq, LANES), 0)
    dist = tpos - (lane * CMP_STRIDE + (CMP_LEN - 1))
    valid = (dist >= 0) & (lane < N_CMP_PAD - 1)
    validf = valid.astype(F32)
    distf = dist.astype(F32)
    zero = jnp.zeros((tq, LANES), BF16)
    q_all = jnp.concatenate(
        [jnp.where((lane // HALF) == g, q_ref[0, :, j * LANES:(j + 1) * LANES], zero)
         for g in range(NSA_GROUPS) for j in range(hg)], axis=0)
    s_all = _dot_nt(q_all, kc_ref[0])
    probs = []
    for h in range(NSA_HEADS):
        s = s_all[h * tq:(h + 1) * tq] - _nsa_slope(h // hg, h % hg) * distf
        s = jnp.where(valid, s, NEG)
        p = jnp.exp(s - jnp.max(s, axis=-1, keepdims=True)) * validf
        probs.append((p / jnp.maximum(jnp.sum(p, axis=-1, keepdims=True), TINY)).astype(BF16))
    o_all = _dot(jnp.concatenate(probs, axis=0), vc_ref[0])
    first = lane < HALF
    for j in range(hg):
        o_ref[0, :, j * LANES:(j + 1) * LANES] = jnp.where(first, o_all[j * tq:(j + 1) * tq],
                                                           o_all[(hg + j) * tq:(hg + j + 1) * tq])
    imp = jnp.zeros((tq, LANES), F32)
    for g in range(NSA_GROUPS):
        imp = imp + _dot(jnp.concatenate(probs[g * hg:(g + 1) * hg], axis=1), ov_ref[g])
    blk = lane % N_SLC
    cur = tpos // SLC_LEN
    score = jnp.where((blk == cur) | (blk == 0), FORCE_SCORE, jnp.where(blk <= cur, imp, NEG))
    sct_ref[...] = score.T
    row8 = lax.broadcasted_iota(I32, (8, tq), 0)
    for g in range(NSA_GROUPS):
        base = g * N_SLC
        tiles = [sct_ref[base + 8 * v:base + 8 * (v + 1), :] for v in range(N_SLC // 8)]
        beaten = [jnp.zeros((8, tq), F32) for _ in tiles]
        for i in range(N_SLC):
            cand = jnp.broadcast_to(sct_ref[base + i:base + i + 1, :], (8, tq))
            for v, tile in enumerate(tiles):
                ge = jnp.where(cand >= tile, 1.0, 0.0)
                gt = jnp.where(cand > tile, 1.0, 0.0)
                if 8 * v > i:
                    wins = ge
                elif 8 * v + 7 <= i:
                    wins = gt
                else:
                    wins = jnp.where(row8 > i - 8 * v, ge, gt)
                beaten[v] = beaten[v] + wins
        for v in range(N_SLC // 8):
            selt_ref[0, base + 8 * v:base + 8 * (v + 1), :] = jnp.where(beaten[v] < N_SEL, 1.0, 0.0).astype(selt_ref.dtype)
    selt_ref[0, NSA_GROUPS * N_SLC:LANES, :] = jnp.zeros((LANES - NSA_GROUPS * N_SLC, tq), selt_ref.dtype)


def nsa_cmp(hb, kc, vc, ov, tq):
    B, S, _ = hb.shape
    qw = NSA_HEADS * NSA_DH
    return pl.pallas_call(
        functools.partial(_nsa_cmp_body, tq=tq),
        grid=(B, S // tq),
        in_specs=[pl.BlockSpec((1, tq, qw), lambda b, i: (b, i, NSA_Q_BLK)),
                  pl.BlockSpec((1, N_CMP_PAD, LANES), lambda b, i: (b, 0, 0)),
                  pl.BlockSpec((1, N_CMP_PAD, LANES), lambda b, i: (b, 0, 0)),
                  pl.BlockSpec(ov.shape, lambda b, i: (0, 0, 0))],
        out_specs=[pl.BlockSpec((1, tq, qw), lambda b, i: (b, i, 0)),
                   pl.BlockSpec((1, LANES, tq), lambda b, i: (b, 0, i))],
        out_shape=[jax.ShapeDtypeStruct((B, S, qw), F32), jax.ShapeDtypeStruct((B, LANES, S), BF16)],
        scratch_shapes=[pltpu.VMEM((LANES, tq), F32)],
        compiler_params=_cp("parallel", "parallel"), name="nsa_cmp")(hb, kc, vc, ov)


def _softmax_step(s, maskf, v, m, l, acc):
    m_new = jnp.maximum(m, jnp.max(s, axis=-1, keepdims=True))
    p = jnp.exp(s - m_new) * maskf
    alpha = jnp.exp(m - m_new)
    l = alpha * l + jnp.sum(p, axis=-1, keepdims=True)
    acc = alpha * acc + _dot(p.astype(BF16), v)
    return m_new, l, acc


FFN_TF = D_FF // 2
ROW_CHUNK = 32
LOG2E = math.log2(math.e)


FLASH_SLOTS = 2


def _flash_scratch(rows, tk):
    return ([pltpu.VMEM((rows, tk), F32)] * FLASH_SLOTS + [pltpu.VMEM((rows, tk), BF16)] * FLASH_SLOTS
            + [pltpu.VMEM((rows, LANES), F32)] * (FLASH_SLOTS + 3))


def _put_scores(dst, r0, raw, adjust):
    for i in range(raw.shape[0] // ROW_CHUNK):
        r = r0 + i * ROW_CHUNK
        dst[r:r + ROW_CHUNK, :] = adjust(r // ROW_CHUNK, raw[i * ROW_CHUNK:(i + 1) * ROW_CHUNK, :]) * LOG2E


def _softmax_pass1(s_ref, m_ref, a_ref):
    rows, tk = s_ref.shape
    m_in = m_out = m_ref
    for c in range(rows // ROW_CHUNK):
        r = slice(c * ROW_CHUNK, (c + 1) * ROW_CHUNK)
        s = s_ref[r, :]
        top = s[:, 0:LANES]
        for j in range(1, tk // LANES):
            top = jnp.maximum(top, s[:, j * LANES:(j + 1) * LANES])
        m_old = m_in[r, :]
        m_new = jnp.maximum(m_old, jnp.broadcast_to(jnp.max(top, axis=-1, keepdims=True), (ROW_CHUNK, LANES)))
        m_out[r, :] = m_new
        a_ref[r, :] = jnp.exp2(m_old - m_new)


def _softmax_pass2(s_ref, p_ref, m_ref, a_ref, l_ref):
    rows, tk = s_ref.shape
    for c in range(rows // ROW_CHUNK):
        r = slice(c * ROW_CHUNK, (c + 1) * ROW_CHUNK)
        m_new = m_ref[r, :]
        part = None
        for j in range(tk // LANES):
            p = jnp.exp2(s_ref[r, j * LANES:(j + 1) * LANES] - m_new)
            p_ref[r, j * LANES:(j + 1) * LANES] = p.astype(BF16)
            part = p if part is None else part + p
        l_ref[r, :] = a_ref[r, :] * l_ref[r, :] + part


def _flash_causal(n_full, tk, scores, pv_dot, scratch):
    n = FLASH_SLOTS
    s_bufs, p_bufs, a_bufs = (scratch[i * n:(i + 1) * n] for i in range(3))
    m_ref, l_ref, acc_ref = scratch[3 * n:]
    m_ref[...] = jnp.full(m_ref.shape, NEG, F32)
    a_bufs[n - 1][...] = jnp.zeros_like(a_bufs[0])
    p_bufs[n - 1][...] = jnp.zeros_like(p_bufs[0])
    l_ref[...] = jnp.zeros_like(l_ref)
    acc_ref[...] = jnp.zeros_like(acc_ref)

    def k_of(t):
        return pl.multiple_of(t * tk, tk)

    def pv(u, t):
        acc_ref[...] = a_bufs[u][...] * acc_ref[...] + pv_dot(p_bufs[u], k_of(t))

    NEXT_FULL, NEXT_LAST, FINAL = range(3)

    def stage(u, t, kind):
        pv((u - 1) % n, jnp.maximum(t - 1, 0))
        if kind != FINAL:
            scores(k_of(t + 1), s_bufs[(u + 1) % n], kind == NEXT_LAST)
        _softmax_pass1(s_bufs[u], m_ref, a_bufs[u])
        _softmax_pass2(s_bufs[u], p_bufs[u], m_ref, a_bufs[u], l_ref)
        if kind == FINAL:
            pv(u, t)

    def by_slot(t, kind):
        for u in range(n):
            @pl.when(t % n == u)
            def _():
                stage(u, t, kind)

    for first_is_last in (False, True):
        @pl.when((n_full == 0) == first_is_last)
        def _():
            scores(k_of(0), s_bufs[0], first_is_last)

    def body(t, carry):
        by_slot(t, NEXT_FULL)
        return carry

    lax.fori_loop(0, n_full - 1, body, 0)

    @pl.when(n_full >= 1)
    def _():
        by_slot(n_full - 1, NEXT_LAST)

    by_slot(n_full, FINAL)
    return acc_ref[...] / jnp.maximum(jnp.sum(l_ref[...], axis=-1, keepdims=True), TINY)


def _nsa_attn_body(q_ref, ks_ref, vs_ref, kw_ref, vw_ref, sel_ref, oc_ref, gt_ref, es_ref, o_ref,
                   selb_ref, ws_ref, wp_ref, wa_ref, wm_ref, wl_ref, *scratch, tq, tk):
    hg = NSA_HEADS // NSA_GROUPS
    span = WINDOW + tq
    qs = pl.program_id(1) * tq
    lane = _lane((1, LANES))
    first = lane < HALF
    zero = jnp.zeros((tq, LANES), BF16)
    q_all = jnp.concatenate(
        [jnp.where((lane // HALF) == g, q_ref[0, :, j * LANES:(j + 1) * LANES], zero)
         for g in range(NSA_GROUPS) for j in range(hg)], axis=0)
    for g in range(NSA_GROUPS):
        picked = lax.dot_general(sel_ref[0], es_ref[g], (((0,), (0,)), ((), ())), preferred_element_type=F32)
        selb_ref[g] = (picked - 1.0) * (-NEG)

    def t_of(c):
        head, r0 = divmod(c * ROW_CHUNK, tq)
        g, j = divmod(head, hg)
        return g, r0, qs + r0 + lax.broadcasted_iota(I32, (ROW_CHUNK, 1), 0), _nsa_slope(g, j)

    def slc_scores(k0, dst, last):
        kpos = k0 + _lane((1, tk))

        def adjust(c, s):
            g, r0, t, slope = t_of(c)
            s = s + slope * kpos.astype(F32) + selb_ref[g, r0:r0 + ROW_CHUNK, pl.ds(k0, tk)]
            return jnp.where(t >= kpos, s, NEG) if last else s

        _put_scores(dst, 0, _dot_nt(q_all, ks_ref[0, pl.ds(k0, tk), :]), adjust)

    o_slc = _flash_causal(qs // tk, tk, slc_scores,
                          lambda p_ref, k0: _dot(p_ref[...], vs_ref[0, pl.ds(k0, tk), :]), scratch)

    w0 = pl.multiple_of(jnp.maximum(qs - WINDOW, 0), tq)
    wpos = w0 + _lane((1, span))

    def win_adjust(c, s):
        _, _, t, slope = t_of(c)
        dist = t - wpos
        return jnp.where((dist >= 0) & (dist < WINDOW), s + slope * wpos.astype(F32), NEG)

    _put_scores(ws_ref, 0, _dot_nt(q_all, kw_ref[0, pl.ds(w0, span), :]), win_adjust)
    wm_ref[...] = jnp.full(wm_ref.shape, NEG, F32)
    wl_ref[...] = jnp.zeros_like(wl_ref)
    _softmax_pass1(ws_ref, wm_ref, wa_ref)
    _softmax_pass2(ws_ref, wp_ref, wm_ref, wa_ref, wl_ref)
    o_win = _dot(wp_ref[...], vw_ref[0, pl.ds(w0, span), :]) / jnp.maximum(
        jnp.sum(wl_ref[...], axis=-1, keepdims=True), TINY)

    gates = jax.nn.sigmoid(gt_ref[0])
    for j in range(hg):
        out = jnp.zeros((tq, LANES), F32)
        for br, branch in enumerate((None, o_slc, o_win)):
            gm = jnp.where(first, gates[:, j * 3 + br:j * 3 + br + 1],
                           gates[:, hg * 3 + j * 3 + br:hg * 3 + j * 3 + br + 1])
            if branch is None:
                val = oc_ref[0, :, j * LANES:(j + 1) * LANES]
            else:
                val = jnp.where(first, branch[j * tq:(j + 1) * tq], branch[(hg + j) * tq:(hg + j + 1) * tq])
            out = out + gm * val
        o_ref[0, :, j * LANES:(j + 1) * LANES] = out.astype(o_ref.dtype)


def nsa_attn(hb, hf, sel, ocmp, esel, tq, tk):
    B, S, _ = hb.shape
    qw = NSA_HEADS * NSA_DH
    rows = NSA_HEADS * tq
    span = WINDOW + tq
    kv = lambda blk: pl.BlockSpec((1, S, LANES), lambda b, i: (b, 0, blk))
    return pl.pallas_call(
        functools.partial(_nsa_attn_body, tq=tq, tk=tk),
        grid=(B, S // tq),
        in_specs=[pl.BlockSpec((1, tq, qw), lambda b, i: (b, i, NSA_Q_BLK)),
                  kv(NSA_KS_BLK), kv(NSA_VS_BLK), kv(NSA_KW_BLK), kv(NSA_VW_BLK),
                  pl.BlockSpec((1, LANES, tq), lambda b, i: (b, 0, i)),
                  pl.BlockSpec((1, tq, qw), lambda b, i: (b, i, 0)),
                  pl.BlockSpec((1, tq, LANES), lambda b, i: (b, i, NSA_GATE_BLK)),
                  pl.BlockSpec(esel.shape, lambda b, i: (0, 0, 0))],
        out_specs=pl.BlockSpec((1, tq, qw), lambda b, i: (b, i, 0)),
        out_shape=jax.ShapeDtypeStruct((B, S, qw), BF16),
        scratch_shapes=([pltpu.VMEM((NSA_GROUPS, tq, S), F32), pltpu.VMEM((rows, span), F32),
                         pltpu.VMEM((rows, span), BF16)]
                        + [pltpu.VMEM((rows, LANES), F32)] * 3 + _flash_scratch(rows, tk)),
        compiler_params=_cp("parallel", "parallel"), name="nsa_attn")(
            hb, hb, hb, hb, hb, sel, ocmp, hf, esel)


def _nsa_tables(S):
    n_cmp = (S - CMP_LEN) // CMP_STRIDE + 1
    c_start = np.arange(n_cmp) * CMP_STRIDE
    c_end = c_start + CMP_LEN - 1
    j_start = np.arange(S // SLC_LEN) * SLC_LEN
    overlap = (c_end[:, None] >= j_start[None]) & (c_start[:, None] <= j_start[None] + SLC_LEN - 1)
    hg = NSA_HEADS // NSA_GROUPS
    ov = np.zeros((NSA_GROUPS, hg, N_CMP_PAD, LANES), np.float32)
    es = np.zeros((NSA_GROUPS, LANES, S), np.float32)
    for g in range(NSA_GROUPS):
        ov[g, :, :n_cmp, g * N_SLC:(g + 1) * N_SLC] = overlap
        es[g, g * N_SLC + np.arange(S) // SLC_LEN, np.arange(S)] = 1.0
    return jnp.asarray(ov.reshape(NSA_GROUPS, hg * N_CMP_PAD, LANES), BF16), jnp.asarray(es, BF16)


NSA_PERM = np.array([(half * (NSA_HEADS // NSA_GROUPS) + j) * NSA_DH + d
                     for j in range(NSA_HEADS // NSA_GROUPS) for half in range(2) for d in range(NSA_DH)])


def nsa_mixer(hb, hf, pos_k, w1_k, w2_k, pos_v, w1_v, w2_v):
    B, S, _ = hb.shape
    assert S // CMP_STRIDE == N_CMP_PAD

    def both_groups(w):
        z = jnp.zeros_like(w)
        return jnp.concatenate([jnp.concatenate([w, z], axis=-1), jnp.concatenate([z, w], axis=-1)], axis=-2)

    def w1_blocks(w1):
        return both_groups(w1.reshape(CMP_LEN, NSA_DH, CMP_HIDDEN)).astype(BF16)

    kc, vc = nsa_compress(hf, jnp.tile(pos_k, (1, NSA_GROUPS)), jnp.tile(pos_v, (1, NSA_GROUPS)),
                          w1_blocks(w1_k), w1_blocks(w1_v),
                          both_groups(w2_k).astype(BF16), both_groups(w2_v).astype(BF16))
    ov, es = _nsa_tables(S)
    ocmp, sel = nsa_cmp(hb, kc, vc, ov, 256)
    return nsa_attn(hb, hf, sel, ocmp, es, 256, 256)


def _diff_body(slope_ref, q1_ref, q2_ref, k1_ref, k2_ref, v_ref, lam_ref, gn_ref, o_ref, *scratch,
               tq, tk, hp, lambda_init):
    heads = 2 * hp
    head0 = pl.program_id(1) * heads
    qs = pl.program_id(2) * tq
    lane = _lane((1, LANES))
    zero = jnp.zeros((tq, LANES), BF16)

    def q_rows(q_ref, h):
        return jnp.where((lane // HALF) == h % 2, q_ref[0, :, (h // 2) * LANES:(h // 2 + 1) * LANES], zero)

    q1 = [q_rows(q1_ref, h) for h in range(heads)]
    q2 = [q_rows(q2_ref, h) for h in range(heads)]
    slopes = [slope_ref[head0 + h] for h in range(heads)]

    def scores(k0, dst, last):
        kpos = k0 + _lane((1, tk))

        def adjust(c, s):
            s = s + slopes[(c * ROW_CHUNK) // (2 * tq)] * kpos.astype(F32)
            if last:
                t = qs + (c * ROW_CHUNK) % tq + lax.broadcasted_iota(I32, (ROW_CHUNK, 1), 0)
                s = jnp.where(t >= kpos, s, NEG)
            return s

        for h in range(heads):
            blk = slice((h // 2) * LANES, (h // 2 + 1) * LANES)
            _put_scores(dst, (2 * h) * tq, _dot_nt(q1[h], k1_ref[0, pl.ds(k0, tk), blk]), adjust)
            _put_scores(dst, (2 * h + 1) * tq, _dot_nt(q2[h], k2_ref[0, pl.ds(k0, tk), blk]), adjust)

    def pv_dot(p_ref, k0):
        return jnp.concatenate(
            [_dot(p_ref[2 * h * tq:2 * (h + 1) * tq, :], v_ref[0, pl.ds(k0, tk), h * LANES:(h + 1) * LANES])
             for h in range(heads)], axis=0)

    o = _flash_causal(qs // tk, tk, scores, pv_dot, scratch)
    lam_rows = lam_ref[...]
    lam = (jnp.exp(jnp.sum(lam_rows[0:1] * lam_rows[1:2], axis=-1, keepdims=True))
           - jnp.exp(jnp.sum(lam_rows[2:3] * lam_rows[3:4], axis=-1, keepdims=True)) + lambda_init)
    for h in range(heads):
        r0 = 2 * h * tq
        y = _rms(o[r0:r0 + tq] - lam * o[r0 + tq:r0 + 2 * tq], gn_ref[...]) * (1.0 - lambda_init)
        o_ref[0, :, h * LANES:(h + 1) * LANES] = y.astype(o_ref.dtype)


def diff_mixer(hd, lam_rows, g_norm, lambda_init, tq, tk, hp):
    B, S, _ = hd.shape
    ng = DIFF_HEADS // (2 * hp)
    qk, vw = hp * LANES, 2 * hp * LANES
    slopes = jnp.asarray(2.0 ** (-8.0 * np.arange(1, DIFF_HEADS + 1) / DIFF_HEADS), F32)
    grid_spec = pltpu.PrefetchScalarGridSpec(
        num_scalar_prefetch=1,
        grid=(B, ng, S // tq),
        in_specs=[pl.BlockSpec((1, tq, qk), lambda b, p, i, s: (b, i, p)),
                  pl.BlockSpec((1, tq, qk), lambda b, p, i, s: (b, i, ng + p)),
                  pl.BlockSpec((1, S, qk), lambda b, p, i, s: (b, 0, 2 * ng + p)),
                  pl.BlockSpec((1, S, qk), lambda b, p, i, s: (b, 0, 3 * ng + p)),
                  pl.BlockSpec((1, S, vw), lambda b, p, i, s: (b, 0, 2 * ng + p)),
                  pl.BlockSpec((4, LANES), lambda b, p, i, s: (0, 0)),
                  pl.BlockSpec((1, LANES), lambda b, p, i, s: (0, 0))],
        out_specs=pl.BlockSpec((1, tq, vw), lambda b, p, i, s: (b, i, p)),
        scratch_shapes=_flash_scratch(4 * hp * tq, tk))
    return pl.pallas_call(
        functools.partial(_diff_body, tq=tq, tk=tk, hp=hp, lambda_init=lambda_init),
        grid_spec=grid_spec,
        out_shape=jax.ShapeDtypeStruct((B, S, DIFF_HEADS * DIFF_DV), BF16),
        compiler_params=_cp("parallel", "parallel", "parallel"), name="diff_attn")(
            slopes, hd, hd, hd, hd, hd, lam_rows, g_norm.reshape(1, LANES))


def _router_body(x_ref, g_ref, wr_ref, tri_ref, xn_ref, meta_ref, wts_ref, cnt_ref, run_ref):
    @pl.when(pl.program_id(0) == 0)
    def _():
        run_ref[...] = jnp.zeros_like(run_ref)

    xn = _rms(x_ref[...], g_ref[...])
    xn_ref[...] = xn
    lane = _lane((1, LANES))
    logits = jnp.dot(xn, wr_ref[...], precision=lax.Precision.HIGHEST, preferred_element_type=F32)
    logits = jnp.where(lane < N_EXPERTS, logits, LOWEST)
    m1 = jnp.max(logits, axis=-1, keepdims=True)
    i1 = jnp.min(jnp.where(logits == m1, lane, LANES), axis=-1, keepdims=True)
    rest = jnp.where(lane == i1, LOWEST, logits)
    m2 = jnp.max(rest, axis=-1, keepdims=True)
    i2 = jnp.min(jnp.where(rest == m2, lane, LANES), axis=-1, keepdims=True)
    e = jnp.exp(m2 - m1)
    w1 = 1.0 / (1.0 + e)
    w2 = e / (1.0 + e)
    onehot = ((lane == i1) | (lane == i2)).astype(F32)
    pos = _dot(tri_ref[...], onehot.astype(BF16)) + run_ref[...]
    p1 = jnp.sum(jnp.where(lane == i1, pos, 0.0), axis=-1, keepdims=True).astype(I32)
    p2 = jnp.sum(jnp.where(lane == i2, pos, 0.0), axis=-1, keepdims=True).astype(I32)
    run_ref[...] += jnp.sum(onehot, axis=0, keepdims=True)
    meta_ref[...] = jnp.where(lane == 0, i1, jnp.where(lane == 1, i2, jnp.where(lane == 2, p1, jnp.where(lane == 3, p2, 0))))
    wts_ref[...] = jnp.where(lane == 0, w1, jnp.where(lane == 1, w2, 0.0))
    cnt_ref[...] = run_ref[...]


def moe_router(x, g, wr, tm):
    T, D = x.shape
    tri = jnp.asarray(np.tril(np.ones((tm, tm), np.float32), -1), BF16)
    return pl.pallas_call(
        _router_body,
        grid=(T // tm,),
        in_specs=[pl.BlockSpec((tm, D), lambda i: (i, 0)),
                  pl.BlockSpec((1, D), lambda i: (0, 0)),
                  pl.BlockSpec((D, LANES), lambda i: (0, 0)),
                  pl.BlockSpec((tm, tm), lambda i: (0, 0))],
        out_specs=[pl.BlockSpec((tm, D), lambda i: (i, 0)),
                   pl.BlockSpec((tm, LANES), lambda i: (i, 0)),
                   pl.BlockSpec((tm, LANES), lambda i: (i, 0)),
                   pl.BlockSpec((1, LANES), lambda i: (0, 0))],
        out_shape=[jax.ShapeDtypeStruct((T, D), F32), jax.ShapeDtypeStruct((T, LANES), I32),
                   jax.ShapeDtypeStruct((T, LANES), F32), jax.ShapeDtypeStruct((1, LANES), F32)],
        scratch_shapes=[pltpu.VMEM((1, LANES), F32)],
        compiler_params=_cp("arbitrary"), name="moe_router")(x, g.reshape(1, D), wr, tri)


ROW_DMA_UNROLL = 8


def _row_copy(src_ref, src_row, dst_ref, dst_row, sem):
    return pltpu.make_async_copy(src_ref.at[pl.ds(src_row, 1)], dst_ref.at[pl.ds(dst_row, 1)], sem)


def _dispatch_body(dest_ref, xn_ref, zero_ref, xs_ref, sem, *, tm):
    del zero_ref
    base = pl.program_id(0) * tm

    def issue(t, c):
        for slot in range(TOP_K):
            _row_copy(xn_ref, t, xs_ref, dest_ref[TOP_K * (base + t) + slot], sem).start()
        return c

    lax.fori_loop(0, tm, issue, 0, unroll=ROW_DMA_UNROLL)

    def drain(t, c):
        for slot in range(TOP_K):
            _row_copy(xn_ref, 0, xs_ref, 0, sem).wait()
        return c

    lax.fori_loop(0, tm, drain, 0, unroll=ROW_DMA_UNROLL)


def moe_dispatch(dest, xn, rows, tm):
    T, D = xn.shape
    grid_spec = pltpu.PrefetchScalarGridSpec(
        num_scalar_prefetch=1,
        grid=(T // tm,),
        in_specs=[pl.BlockSpec((tm, D), lambda i, d: (i, 0)),
                  pl.BlockSpec(memory_space=pl.ANY)],
        out_specs=pl.BlockSpec(memory_space=pl.ANY),
        scratch_shapes=[pltpu.SemaphoreType.DMA])
    return pl.pallas_call(
        functools.partial(_dispatch_body, tm=tm),
        grid_spec=grid_spec,
        out_shape=jax.ShapeDtypeStruct((rows, D), F32),
        input_output_aliases={2: 0},
        compiler_params=_cp("arbitrary"), name="moe_dispatch")(dest, xn, jnp.zeros((rows, D), F32))


def _experts_body(te_ref, tv_ref, xs_ref, wg_ref, wu_ref, wd_ref, o_ref, xb_ref):
    i = pl.program_id(0)

    @pl.when(pl.program_id(1) == 0)
    def _():
        xb_ref[...] = xs_ref[...].astype(BF16)
        o_ref[...] = jnp.zeros_like(o_ref)

    @pl.when(tv_ref[i] > 0)
    def _():
        xb = xb_ref[...]
        hg = _dot(xb, wg_ref[0])
        hu = _dot(xb, wu_ref[0])
        o_ref[...] += _dot((hg * jax.nn.sigmoid(hg) * hu).astype(BF16), wd_ref[0])


def moe_experts(tile_expert, tile_valid, xs, wg, wu, wd, tr, tf):
    R, D = xs.shape
    F = wg.shape[2]
    grid_spec = pltpu.PrefetchScalarGridSpec(
        num_scalar_prefetch=2,
        grid=(R // tr, F // tf),
        in_specs=[pl.BlockSpec((tr, D), lambda i, k, te, tv: (i, 0)),
                  pl.BlockSpec((1, D, tf), lambda i, k, te, tv: (te[i], 0, k)),
                  pl.BlockSpec((1, D, tf), lambda i, k, te, tv: (te[i], 0, k)),
                  pl.BlockSpec((1, tf, D), lambda i, k, te, tv: (te[i], k, 0))],
        out_specs=pl.BlockSpec((tr, D), lambda i, k, te, tv: (i, 0)),
        scratch_shapes=[pltpu.VMEM((tr, D), BF16)])
    return pl.pallas_call(
        _experts_body,
        grid_spec=grid_spec,
        out_shape=jax.ShapeDtypeStruct((R, D), F32),
        compiler_params=_cp("parallel", "arbitrary"), name="moe_experts")(tile_expert, tile_valid, xs, wg, wu, wd)


def _combine_body(dest_ref, x_ref, wts_ref, g_ref, ys_ref, o_ref, buf_ref, sem, *, tm):
    base = pl.program_id(0) * tm

    def issue(t, c):
        for slot in range(TOP_K):
            _row_copy(ys_ref, dest_ref[TOP_K * (base + t) + slot], buf_ref.at[slot], t, sem).start()
        return c

    lax.fori_loop(0, tm, issue, 0, unroll=ROW_DMA_UNROLL)

    def drain(t, c):
        for slot in range(TOP_K):
            _row_copy(ys_ref, 0, buf_ref.at[slot], 0, sem).wait()
        return c

    lax.fori_loop(0, tm, drain, 0, unroll=ROW_DMA_UNROLL)
    wts = wts_ref[...]
    y = x_ref[...] + (wts[:, 0:1] * buf_ref[0] + wts[:, 1:2] * buf_ref[1])
    o_ref[...] = _rms(y, g_ref[...])


def moe_combine(dest, x, wts, g_final, ys, tm):
    T, D = x.shape
    grid_spec = pltpu.PrefetchScalarGridSpec(
        num_scalar_prefetch=1,
        grid=(T // tm,),
        in_specs=[pl.BlockSpec((tm, D), lambda i, d: (i, 0)),
                  pl.BlockSpec((tm, LANES), lambda i, d: (i, 0)),
                  pl.BlockSpec((1, D), lambda i, d: (0, 0)),
                  pl.BlockSpec(memory_space=pl.ANY)],
        out_specs=pl.BlockSpec((tm, D), lambda i, d: (i, 0)),
        scratch_shapes=[pltpu.VMEM((TOP_K, tm, D), F32), pltpu.SemaphoreType.DMA])
    return pl.pallas_call(
        functools.partial(_combine_body, tm=tm),
        grid_spec=grid_spec,
        out_shape=jax.ShapeDtypeStruct((T, D), F32),
        compiler_params=_cp("arbitrary"), name="moe_combine")(dest, x, wts, g_final.reshape(1, D), ys)


def moe_block(x, g, w_router, wg, wu, wd, g_final, tr):
    T, D = x.shape
    wr = _pad_cols(w_router, LANES)
    xn, meta, wts, counts = moe_router(x, g, wr, 512)
    cnt = counts[0, :N_EXPERTS].astype(I32)
    gsz = ((cnt + tr - 1) // tr) * tr
    gend = jnp.cumsum(gsz)
    goff = gend - gsz
    dest = (jnp.take(goff, meta[:, 0:TOP_K]) + meta[:, TOP_K:2 * TOP_K]).reshape(-1)
    n_tiles = (TOP_K * T) // tr + N_EXPERTS
    tile_start = jnp.arange(n_tiles, dtype=I32) * tr
    tile_valid = (tile_start < gend[-1]).astype(I32)
    tile_expert = jnp.minimum(jnp.sum((tile_start[:, None] >= gend[None, :]).astype(I32), axis=1), N_EXPERTS - 1)
    last_valid = jnp.take(tile_expert, jnp.maximum(gend[-1] // tr - 1, 0))
    tile_expert = jnp.where(tile_valid > 0, tile_expert, last_valid)
    xs = moe_dispatch(dest, xn, n_tiles * tr, 256)
    ys = moe_experts(tile_expert, tile_valid, xs, wg, wu, wd, tr, FFN_TF)
    return moe_combine(dest, x, wts, g_final, ys, 256)


def _pad_cols(w, width):
    return jnp.pad(w, ((0, 0), (0, width - w.shape[1])))


def _mem_kv(mem2, g, wk, wv, batch):
    w = jnp.concatenate([wk, wv], axis=1).astype(BF16)
    width = wk.shape[1]
    k, v = norm_proj(mem2, g, w, [(width, BF16), (width, BF16)], 256, "mem_kv")
    return k.reshape(batch, -1, width), v.reshape(batch, -1, width)


def kernel(x, mem, mix_norm_0, w_in_0, gla_w_alpha_0, gla_b_alpha_0, gla_out_norm_0, nsa_cmp_pos_k_0, nsa_cmp_w1_k_0, nsa_cmp_w2_k_0, nsa_cmp_pos_v_0, nsa_cmp_w1_v_0, nsa_cmp_w2_v_0, w_out_0, xattn_norm_0, xattn_mem_norm_0, xattn_wq_0, xattn_wk_0, xattn_wv_0, xattn_wo_0, ffn_norm_0, ffn_w_gate_0, ffn_w_up_0, ffn_w_down_0, mix_norm_1, w_in_1, diff_lq1_1, diff_lk1_1, diff_lq2_1, diff_lk2_1, diff_out_norm_1, w_out_1, xattn_norm_1, xattn_mem_norm_1, xattn_wq_1, xattn_wk_1, xattn_wv_1, xattn_wo_1, ffn_norm_1, moe_router_1, moe_w_gate_1, moe_w_up_1, moe_w_down_1, final_norm):
    B, S, D = x.shape
    T = B * S
    x2 = x.reshape(T, D)
    mem2 = mem.reshape(-1, D)
    q_scale = NSA_DH ** -0.5

    sizes = [GLA_HEADS * GLA_DK, GLA_HEADS * GLA_DK, GLA_HEADS * GLA_DV, GLA_HEADS * GLA_DV, GLA_RANK,
             NSA_HEADS * NSA_DH] + [NSA_GROUPS * NSA_DH] * 6 + [3 * NSA_HEADS]
    (g_q, g_k, g_v, g_r, g_a, n_q, n_kc, n_vc, n_ks, n_vs, n_kw, n_vw, n_g) = jnp.split(
        w_in_0, np.cumsum(sizes)[:-1].tolist(), axis=1)
    w0 = jnp.concatenate([n_q[:, NSA_PERM] * q_scale, n_ks, n_vs, n_kw, n_vw, g_q * q_scale, g_k, g_v,
                          _pad_cols(g_a, LANES), g_r, _pad_cols(n_g, LANES), n_kc, n_vc], axis=1).astype(BF16)
    hb, hf = norm_proj(x2, mix_norm_0, w0, [(HB0_WIDTH, BF16), (HF0_WIDTH, F32)], 512, "in_proj0")
    hb = hb.reshape(B, S, HB0_WIDTH)
    hf = hf.reshape(B, S, HF0_WIDTH)
    w_alpha = jnp.pad(gla_w_alpha_0, ((0, LANES - GLA_RANK), (0, 0))).astype(BF16)
    o_a = gla_mixer(hb, hf, w_alpha, gla_b_alpha_0.reshape(1, -1), gla_out_norm_0, GLA_COLS, 8)
    o_b = nsa_mixer(hb, hf, nsa_cmp_pos_k_0, nsa_cmp_w1_k_0, nsa_cmp_w2_k_0,
                    nsa_cmp_pos_v_0, nsa_cmp_w1_v_0, nsa_cmp_w2_v_0)
    n_a = GLA_HEADS * GLA_DV
    k0, v0 = _mem_kv(mem2, xattn_mem_norm_0, xattn_wk_0, xattn_wv_0, B)
    x2 = proj_xattn_block(x2, [o_a.reshape(T, -1), o_b.reshape(T, -1)],
                          [w_out_0[:n_a].astype(BF16), w_out_0[n_a:][NSA_PERM].astype(BF16)],
                          xattn_norm_0, (xattn_wq_0 * q_scale).astype(BF16), k0, v0, xattn_wo_0.astype(BF16),
                          S, 1024, "out_xattn0")
    x2 = ffn_block(x2, ffn_norm_0, ffn_w_gate_0.astype(BF16), ffn_w_up_0.astype(BF16), ffn_w_down_0.astype(BF16),
                   512, FFN_TF, "ffn0")

    lambda_init = 0.8 - 0.6 * math.exp(-0.3 * 1)
    n_q1 = 2 * DIFF_HEADS * DIFF_DH
    w1 = jnp.concatenate([w_in_1[:, :n_q1] * q_scale, w_in_1[:, n_q1:]], axis=1).astype(BF16)
    (hd,) = norm_proj(x2, mix_norm_1, w1, [(w1.shape[1], BF16)], 512, "in_proj1")
    lam_rows = _pad_cols(jnp.stack([diff_lq1_1, diff_lk1_1, diff_lq2_1, diff_lk2_1]), LANES)
    o_d = diff_mixer(hd.reshape(B, S, -1), lam_rows, diff_out_norm_1, lambda_init, 256, 256, 4)
    k1, v1 = _mem_kv(mem2, xattn_mem_norm_1, xattn_wk_1, xattn_wv_1, B)
    x2 = proj_xattn_block(x2, [o_d.reshape(T, -1)], [w_out_1.astype(BF16)],
                          xattn_norm_1, (xattn_wq_1 * q_scale).astype(BF16), k1, v1, xattn_wo_1.astype(BF16),
                          S, 1024, "out_xattn1")
    out = moe_block(x2, ffn_norm_1, moe_router_1, moe_w_gate_1.astype(BF16), moe_w_up_1.astype(BF16),
                    moe_w_down_1.astype(BF16), final_norm, 512)
    return out.reshape(B, S, D)
```

```python
import functools
import math

import numpy as np
import jax
import jax.numpy as jnp
from jax import lax
from jax.experimental import pallas as pl
from jax.experimental.pallas import tpu as pltpu

F32 = jnp.float32
BF16 = jnp.bfloat16
I32 = jnp.int32

D_MODEL = 1024
GLA_HEADS, GLA_DK, GLA_DV, GLA_RANK, GLA_TAU, GLA_CHUNK = 4, 64, 128, 16, 16.0, 64
NSA_HEADS, NSA_GROUPS, NSA_DH = 8, 2, 64
CMP_LEN, CMP_STRIDE, CMP_HIDDEN, SLC_LEN, N_SEL, WINDOW = 32, 16, 256, 64, 8, 512
DIFF_HEADS, DIFF_DH, DIFF_DV = 8, 64, 128
XATTN_HEADS, XATTN_DH = 4, 64
D_FF, N_EXPERTS, TOP_K = 2816, 8, 2
EPS, NEG, TINY, FORCE_SCORE = 1e-6, -1e30, 1e-30, 1e4

LANES = 128
HALF = LANES // 2
VMEM_LIMIT = 56 * 1024 * 1024
LOWEST = -3.0e38

HB0_WIDTH, HF0_WIDTH = 17 * LANES, 7 * LANES
NSA_Q_BLK, NSA_KS_BLK, NSA_VS_BLK, NSA_KW_BLK, NSA_VW_BLK = 0, 4, 5, 6, 7
GLA_COLS = (8, 10, 12, 16, 0)
NSA_GATE_BLK, NSA_KC_BLK, NSA_VC_BLK = 4, 5, 6


def _cp(*sem, flags=None):
    return pltpu.CompilerParams(dimension_semantics=sem, vmem_limit_bytes=VMEM_LIMIT, flags=flags)


def _rms(x, g):
    y = x * lax.rsqrt(jnp.mean(x * x, axis=-1, keepdims=True) + EPS)
    return y * g


def _dot(a, b):
    return jnp.dot(a, b, preferred_element_type=F32)


def _dot_nt(a, b):
    return lax.dot_general(a, b, (((1,), (1,)), ((), ())), preferred_element_type=F32)


def _lane(shape):
    return lax.broadcasted_iota(I32, shape, len(shape) - 1)


def _norm_proj_body(x_ref, g_ref, w_ref, *o_refs, widths):
    xn = _rms(x_ref[...], g_ref[...]).astype(BF16)
    off = 0
    for o_ref, wd in zip(o_refs, widths):
        for c0 in range(0, wd, 512):
            cw = min(512, wd - c0)
            o_ref[:, c0:c0 + cw] = _dot(xn, w_ref[:, off + c0:off + c0 + cw]).astype(o_ref.dtype)
        off += wd


def norm_proj(x, g, w, outs, tm, name):
    T, D = x.shape
    widths = tuple(o[0] for o in outs)
    return pl.pallas_call(
        functools.partial(_norm_proj_body, widths=widths),
        grid=(T // tm,),
        in_specs=[pl.BlockSpec((tm, D), lambda i: (i, 0)),
                  pl.BlockSpec((1, D), lambda i: (0, 0)),
                  pl.BlockSpec(w.shape, lambda i: (0, 0))],
        out_specs=[pl.BlockSpec((tm, wd), lambda i: (i, 0)) for wd, _ in outs],
        out_shape=[jax.ShapeDtypeStruct((T, wd), dt) for wd, dt in outs],
        compiler_params=_cp("parallel"), name=name)(x, g.reshape(1, D), w)


def _proj_xattn_body(*refs, n):
    res_ref = refs[0]
    a_refs = refs[1:1 + n]
    w_refs = refs[1 + n:1 + 2 * n]
    g_ref, wq_ref, k_ref, v_ref, wo_ref, o_ref = refs[1 + 2 * n:]
    x = res_ref[...]
    for a_ref, w_ref in zip(a_refs, w_refs):
        x = x + _dot(a_ref[...].astype(BF16), w_ref[...])
    q = _dot(_rms(x, g_ref[...]).astype(BF16), wq_ref[...]).astype(BF16)
    tm = q.shape[0]
    head_of_lane = _lane((1, q.shape[1])) // XATTN_DH
    zero = jnp.zeros_like(q)
    q_all = jnp.concatenate([jnp.where(head_of_lane == h, q, zero) for h in range(XATTN_HEADS)], axis=0)
    s = _dot_nt(q_all, k_ref[0])
    p = jnp.exp(s - jnp.max(s, axis=-1, keepdims=True))
    p = p / jnp.sum(p, axis=-1, keepdims=True)
    o_all = _dot(p.astype(BF16), v_ref[0])
    o = o_all[0:tm]
    for h in range(1, XATTN_HEADS):
        o = jnp.where(head_of_lane == h, o_all[h * tm:(h + 1) * tm], o)
    o_ref[...] = x + _dot(o.astype(BF16), wo_ref[...])


def proj_xattn_block(res, a_list, w_list, g, wq, k, v, wo, seq, tm, name):
    T, D = res.shape
    n = len(a_list)
    M, W = k.shape[1], k.shape[2]
    per = seq // tm
    return pl.pallas_call(
        functools.partial(_proj_xattn_body, n=n),
        grid=(T // tm,),
        in_specs=([pl.BlockSpec((tm, D), lambda i: (i, 0))]
                  + [pl.BlockSpec((tm, a.shape[1]), lambda i: (i, 0)) for a in a_list]
                  + [pl.BlockSpec(w.shape, lambda i: (0, 0)) for w in w_list]
                  + [pl.BlockSpec((1, D), lambda i: (0, 0)),
                     pl.BlockSpec(wq.shape, lambda i: (0, 0)),
                     pl.BlockSpec((1, M, W), lambda i: (i // per, 0, 0)),
                     pl.BlockSpec((1, M, W), lambda i: (i // per, 0, 0)),
                     pl.BlockSpec(wo.shape, lambda i: (0, 0))]),
        out_specs=pl.BlockSpec((tm, D), lambda i: (i, 0)),
        out_shape=jax.ShapeDtypeStruct((T, D), F32),
        compiler_params=_cp("parallel"), name=name)(res, *a_list, *w_list, g.reshape(1, D), wq, k, v, wo)


def _ffn_body(x_ref, g_ref, wg_ref, wu_ref, wd_ref, o_ref, xn_ref):
    @pl.when(pl.program_id(1) == 0)
    def _():
        xn_ref[...] = _rms(x_ref[...], g_ref[...]).astype(BF16)
        o_ref[...] = x_ref[...]

    xn = xn_ref[...]
    hg = _dot(xn, wg_ref[...])
    hu = _dot(xn, wu_ref[...])
    h = (hg * jax.nn.sigmoid(hg) * hu).astype(BF16)
    o_ref[...] += _dot(h, wd_ref[...])


def ffn_block(x, g, wg, wu, wd, tm, tf, name):
    T, D = x.shape
    F = wg.shape[1]
    return pl.pallas_call(
        _ffn_body,
        grid=(T // tm, F // tf),
        in_specs=[pl.BlockSpec((tm, D), lambda i, k: (i, 0)),
                  pl.BlockSpec((1, D), lambda i, k: (0, 0)),
                  pl.BlockSpec((D, tf), lambda i, k: (0, k)),
                  pl.BlockSpec((D, tf), lambda i, k: (0, k)),
                  pl.BlockSpec((tf, D), lambda i, k: (k, 0))],
        out_specs=pl.BlockSpec((tm, D), lambda i, k: (i, 0)),
        out_shape=jax.ShapeDtypeStruct((T, D), F32),
        scratch_shapes=[pltpu.VMEM((tm, D), BF16)],
        compiler_params=_cp("parallel", "arbitrary"), name=name)(x, g.reshape(1, D), wg, wu, wd)


GLA_TILE = 128
_GLA_LEVELS = tuple(GLA_TILE >> (i + 1) for i in range(GLA_TILE.bit_length() - 1))


def _gla_constants():
    C = GLA_TILE
    t = np.arange(C)
    r = t[None, :]
    mats = [r <= t[:, None], r > t[:, None]]
    for hs in _GLA_LEVELS:
        c = (t // (2 * hs)) * (2 * hs) + hs - 1
        right = (t % (2 * hs)) >= hs
        m_right = (r > c[:, None]) & (r <= t[:, None])
        m_left = (r > t[:, None]) & (r <= c[:, None])
        mats.append(np.where(right[:, None], m_right, m_left))
    cmat = np.concatenate(mats, 0).astype(np.float32)
    x = t[:, None] ^ t[None, :]
    lvl = np.full((C, C), -1, np.int32)
    for i, hs in enumerate(_GLA_LEVELS):
        lvl[(t[:, None] > t[None, :]) & (x >= hs) & (x < 2 * hs)] = i
    lvl[t[:, None] == t[None, :]] = len(_GLA_LEVELS)
    return cmat, lvl


def _gla_body(q_ref, k_ref, v_ref, a_ref, r_ref, wa_ref, ba_ref, gn_ref, cmat_ref, lvl_ref, o_ref, st_ref, *, nb):
    C = GLA_TILE

    @pl.when(pl.program_id(2) == 0)
    def _():
        st_ref[...] = jnp.zeros_like(st_ref)

    nlev = len(_GLA_LEVELS)
    in_a = _lane((1, LANES)) < HALF
    lvl = lvl_ref[...]
    cmat = cmat_ref[...]
    on_diag = (lax.broadcasted_iota(I32, (LANES, 2 * LANES), 0) // HALF
               == lax.broadcasted_iota(I32, (LANES, 2 * LANES), 1) // LANES)
    for bi in range(nb):
        q = q_ref[bi].astype(F32)
        k = k_ref[bi].astype(F32)
        z = _dot(a_ref[bi], wa_ref[...]) + ba_ref[...]
        la = (jnp.minimum(z, 0.0) - jnp.log1p(jnp.exp(-jnp.abs(z)))) / GLA_TAU
        hi = la.astype(BF16)
        rest = la - hi.astype(F32)
        mid = rest.astype(BF16)
        lo = (rest - mid.astype(F32)).astype(BF16)
        u3 = _dot(cmat, jnp.concatenate([hi, mid, lo], axis=1))
        e = jnp.exp(u3[:, 0:LANES] + u3[:, LANES:2 * LANES] + u3[:, 2 * LANES:3 * LANES])
        qhat = (q * e[0:C]).astype(BF16)
        kend = (k * e[C:2 * C]).astype(BF16)
        decay = jnp.broadcast_to(e[C - 1:C], (LANES, LANES)).T
        zero = jnp.zeros_like(q)

        def stacked(x):
            return jnp.concatenate([jnp.where(in_a, x, zero), jnp.where(in_a, zero, x)], axis=0).astype(BF16)

        sc = jnp.zeros((2 * C, C), F32)
        for i in range(nlev):
            ei = e[(2 + i) * C:(3 + i) * C]
            sc = jnp.where(lvl == i, _dot_nt(stacked(q * ei), (k * ei).astype(BF16)), sc)
        sc = jnp.where(lvl == nlev, _dot_nt(stacked(q), k.astype(BF16)), sc)
        v = v_ref[bi]
        state = st_ref[bi]
        o_inter = _dot(qhat, state.astype(BF16))
        o_intra = _dot(sc.astype(BF16), v)
        update = lax.dot_general(kend, v, (((0,), (0,)), ((), ())), preferred_element_type=F32)
        st_ref[bi] = jnp.where(on_diag, jnp.concatenate([decay, decay], axis=1) * state + update, 0.0)
        for half in range(2):
            cols = slice(half * LANES, (half + 1) * LANES)
            y = _rms(o_inter[:, cols] + o_intra[half * C:(half + 1) * C, cols], gn_ref[...])
            rr = r_ref[bi, :, cols]
            o_ref[bi, :, cols] = (y * (rr * jax.nn.sigmoid(rr))).astype(o_ref.dtype)


def gla_mixer(hb, hf, w_alpha, b_alpha, g_norm, cols, nb):
    B, S, _ = hb.shape
    C = GLA_TILE
    cmat, lvl = _gla_constants()
    lvl = np.concatenate([lvl, lvl], axis=0)
    qb, kb_, vb, ab, rb = cols
    return pl.pallas_call(
        functools.partial(_gla_body, nb=nb),
        grid=(B // nb, GLA_HEADS // 2, S // C),
        in_specs=[pl.BlockSpec((nb, C, LANES), lambda b, p, c: (b, c, qb + p)),
                  pl.BlockSpec((nb, C, LANES), lambda b, p, c: (b, c, kb_ + p)),
                  pl.BlockSpec((nb, C, 2 * LANES), lambda b, p, c: (b, c, vb // 2 + p)),
                  pl.BlockSpec((nb, C, LANES), lambda b, p, c: (b, c, ab)),
                  pl.BlockSpec((nb, C, 2 * LANES), lambda b, p, c: (b, c, rb // 2 + p)),
                  pl.BlockSpec((LANES, LANES), lambda b, p, c: (0, p)),
                  pl.BlockSpec((1, LANES), lambda b, p, c: (0, p)),
                  pl.BlockSpec((1, LANES), lambda b, p, c: (0, 0)),
                  pl.BlockSpec(cmat.shape, lambda b, p, c: (0, 0)),
                  pl.BlockSpec(lvl.shape, lambda b, p, c: (0, 0))],
        out_specs=pl.BlockSpec((nb, C, 2 * LANES), lambda b, p, c: (b, c, p)),
        out_shape=jax.ShapeDtypeStruct((B, S, GLA_HEADS * GLA_DV), BF16),
        scratch_shapes=[pltpu.VMEM((nb, LANES, 2 * LANES), F32)],
        compiler_params=_cp("parallel", "parallel", "arbitrary"), name="gla")(
            hb, hb, hb, hb, hf, w_alpha, b_alpha, g_norm.reshape(1, LANES), jnp.asarray(cmat, BF16), jnp.asarray(lvl))


N_CMP_PAD = 128
N_SLC = 32


def _gelu_tanh(x):
    return 0.5 * x * (1.0 + jnp.tanh(math.sqrt(2.0 / math.pi) * (x + 0.044715 * (x * x * x))))


def _nsa_compress_body(tk_ref, tv_ref, pk_ref, pv_ref, w1k_ref, w1v_ref, w2k_ref, w2v_ref, kc_ref, vc_ref):
    for t_ref, p_ref, w1_ref, w2_ref, o_ref in ((tk_ref, pk_ref, w1k_ref, w2k_ref, kc_ref),
                                                (tv_ref, pv_ref, w1v_ref, w2v_ref, vc_ref)):
        lo = jnp.zeros((N_CMP_PAD, NSA_GROUPS * CMP_HIDDEN), F32)
        hi = jnp.zeros((N_CMP_PAD, NSA_GROUPS * CMP_HIDDEN), F32)
        for l in range(CMP_STRIDE):
            rows = t_ref[0, pl.ds(l, N_CMP_PAD, stride=CMP_STRIDE), :]
            lo = lo + _dot((rows + p_ref[l:l + 1]).astype(BF16), w1_ref[l])
            hi = hi + _dot((rows + p_ref[CMP_STRIDE + l:CMP_STRIDE + l + 1]).astype(BF16), w1_ref[CMP_STRIDE + l])
        pre = lo + pltpu.roll(hi, N_CMP_PAD - 1, 0)
        o_ref[0] = _dot(_gelu_tanh(pre).astype(BF16), w2_ref[...]).astype(o_ref.dtype)


def nsa_compress(hf, pk, pv, w1k, w1v, w2k, w2v):
    B, S, _ = hf.shape
    full = lambda a: pl.BlockSpec(a.shape, lambda b: (0,) * a.ndim)
    out = pl.BlockSpec((1, N_CMP_PAD, LANES), lambda b: (b, 0, 0))
    return pl.pallas_call(
        _nsa_compress_body,
        grid=(B,),
        in_specs=[pl.BlockSpec((1, S, LANES), lambda b: (b, 0, NSA_KC_BLK)),
                  pl.BlockSpec((1, S, LANES), lambda b: (b, 0, NSA_VC_BLK)),
                  full(pk), full(pv), full(w1k), full(w1v), full(w2k), full(w2v)],
        out_specs=[out, out],
        out_shape=[jax.ShapeDtypeStruct((B, N_CMP_PAD, LANES), BF16)] * 2,
        compiler_params=_cp("parallel"), name="nsa_compress")(hf, hf, pk, pv, w1k, w1v, w2k, w2v)


def _nsa_slope(g, j):
    return 2.0 ** (-(g * (NSA_HEADS // NSA_GROUPS) + j + 1))


def _nsa_cmp_body(q_ref, kc_ref, vc_ref, ov_ref, o_ref, selt_ref, sct_ref, *, tq):
    hg = NSA_HEADS // NSA_GROUPS
    qs = pl.program_id(1) * tq
    lane = _lane((1, LANES))
    tpos = qs + lax.broadcasted_iota(I32, (tq, LANES), 0)
    dist = tpos - (lane * CMP_STRIDE + (CMP_LEN - 1))
    valid = (dist >= 0) & (lane < N_CMP_PAD - 1)
    validf = valid.astype(F32)
    distf = dist.astype(F32)
    zero = jnp.zeros((tq, LANES), BF16)
    q_all = jnp.concatenate(
        [jnp.where((lane // HALF) == g, q_ref[0, :, j * LANES:(j + 1) * LANES], zero)
         for g in range(NSA_GROUPS) for j in range(hg)], axis=0)
    s_all = _dot_nt(q_all, kc_ref[0])
    probs = []
    for h in range(NSA_HEADS):
        s = s_all[h * tq:(h + 1) * tq] - _nsa_slope(h // hg, h % hg) * distf
        s = jnp.where(valid, s, NEG)
        p = jnp.exp(s - jnp.max(s, axis=-1, keepdims=True)) * validf
        probs.append((p / jnp.maximum(jnp.sum(p, axis=-1, keepdims=True), TINY)).astype(BF16))
    o_all = _dot(jnp.concatenate(probs, axis=0), vc_ref[0])
    first = lane < HALF
    for j in range(hg):
        o_ref[0, :, j * LANES:(j + 1) * LANES] = jnp.where(first, o_all[j * tq:(j + 1) * tq],
                                                           o_all[(hg + j) * tq:(hg + j + 1) * tq])
    imp = jnp.zeros((tq, LANES), F32)
    for g in range(NSA_GROUPS):
        imp = imp + _dot(jnp.concatenate(probs[g * hg:(g + 1) * hg], axis=1), ov_ref[g])
    blk = lane % N_SLC
    cur = tpos // SLC_LEN
    score = jnp.where((blk == cur) | (blk == 0), FORCE_SCORE, jnp.where(blk <= cur, imp, NEG))
    sct_ref[...] = score.T
    row8 = lax.broadcasted_iota(I32, (8, tq), 0)
    for g in range(NSA_GROUPS):
        base = g * N_SLC
        tiles = [sct_ref[base + 8 * v:base + 8 * (v + 1), :] for v in range(N_SLC // 8)]
        beaten = [jnp.zeros((8, tq), F32) for _ in tiles]
        for i in range(N_SLC):
            cand = jnp.broadcast_to(sct_ref[base + i:base + i + 1, :], (8, tq))
            for v, tile in enumerate(tiles):
                ge = jnp.where(cand >= tile, 1.0, 0.0)
                gt = jnp.where(cand > tile, 1.0, 0.0)
                if 8 * v > i:
                    wins = ge
                elif 8 * v + 7 <= i:
                    wins = gt
                else:
                    wins = jnp.where(row8 > i - 8 * v, ge, gt)
                beaten[v] = beaten[v] + wins
        for v in range(N_SLC // 8):
            selt_ref[0, base + 8 * v:base + 8 * (v + 1), :] = jnp.where(beaten[v] < N_SEL, 1.0, 0.0).astype(selt_ref.dtype)
    selt_ref[0, NSA_GROUPS * N_SLC:LANES, :] = jnp.zeros((LANES - NSA_GROUPS * N_SLC, tq), selt_ref.dtype)


def nsa_cmp(hb, kc, vc, ov, tq):
    B, S, _ = hb.shape
    qw = NSA_HEADS * NSA_DH
    return pl.pallas_call(
        functools.partial(_nsa_cmp_body, tq=tq),
        grid=(B, S // tq),
        in_specs=[pl.BlockSpec((1, tq, qw), lambda b, i: (b, i, NSA_Q_BLK)),
                  pl.BlockSpec((1, N_CMP_PAD, LANES), lambda b, i: (b, 0, 0)),
                  pl.BlockSpec((1, N_CMP_PAD, LANES), lambda b, i: (b, 0, 0)),
                  pl.BlockSpec(ov.shape, lambda b, i: (0, 0, 0))],
        out_specs=[pl.BlockSpec((1, tq, qw), lambda b, i: (b, i, 0)),
                   pl.BlockSpec((1, LANES, tq), lambda b, i: (b, 0, i))],
        out_shape=[jax.ShapeDtypeStruct((B, S, qw), F32), jax.ShapeDtypeStruct((B, LANES, S), BF16)],
        scratch_shapes=[pltpu.VMEM((LANES, tq), F32)],
        compiler_params=_cp("parallel", "parallel"), name="nsa_cmp")(hb, kc, vc, ov)


def _softmax_step(s, maskf, v, m, l, acc):
    m_new = jnp.maximum(m, jnp.max(s, axis=-1, keepdims=True))
    p = jnp.exp(s - m_new) * maskf
    alpha = jnp.exp(m - m_new)
    l = alpha * l + jnp.sum(p, axis=-1, keepdims=True)
    acc = alpha * acc + _dot(p.astype(BF16), v)
    return m_new, l, acc


FFN_TF = D_FF // 2
ROW_CHUNK = 32
LOG2E = math.log2(math.e)


FLASH_SLOTS = 2


def _flash_scratch(rows, tk):
    return ([pltpu.VMEM((rows, tk), F32)] * FLASH_SLOTS + [pltpu.VMEM((rows, tk), BF16)] * FLASH_SLOTS
            + [pltpu.VMEM((rows, LANES), F32)] * (FLASH_SLOTS + 3))


def _put_scores(dst, r0, raw, adjust):
    for i in range(raw.shape[0] // ROW_CHUNK):
        r = r0 + i * ROW_CHUNK
        dst[r:r + ROW_CHUNK, :] = adjust(r // ROW_CHUNK, raw[i * ROW_CHUNK:(i + 1) * ROW_CHUNK, :]) * LOG2E


def _softmax_pass1(s_ref, m_ref, a_ref):
    rows, tk = s_ref.shape
    m_in = m_out = m_ref
    for c in range(rows // ROW_CHUNK):
        r = slice(c * ROW_CHUNK, (c + 1) * ROW_CHUNK)
        s = s_ref[r, :]
        top = s[:, 0:LANES]
        for j in range(1, tk // LANES):
            top = jnp.maximum(top, s[:, j * LANES:(j + 1) * LANES])
        m_old = m_in[r, :]
        m_new = jnp.maximum(m_old, jnp.broadcast_to(jnp.max(top, axis=-1, keepdims=True), (ROW_CHUNK, LANES)))
        m_out[r, :] = m_new
        a_ref[r, :] = jnp.exp2(m_old - m_new)


def _softmax_pass2(s_ref, p_ref, m_ref, a_ref, l_ref):
    rows, tk = s_ref.shape
    for c in range(rows // ROW_CHUNK):
        r = slice(c * ROW_CHUNK, (c + 1) * ROW_CHUNK)
        m_new = m_ref[r, :]
        part = None
        for j in range(tk // LANES):
            p = jnp.exp2(s_ref[r, j * LANES:(j + 1) * LANES] - m_new)
            p_ref[r, j * LANES:(j + 1) * LANES] = p.astype(BF16)
            part = p if part is None else part + p
        l_ref[r, :] = a_ref[r, :] * l_ref[r, :] + part


def _flash_causal(n_full, tk, scores, pv_dot, scratch):
    n = FLASH_SLOTS
    s_bufs, p_bufs, a_bufs = (scratch[i * n:(i + 1) * n] for i in range(3))
    m_ref, l_ref, acc_ref = scratch[3 * n:]
    m_ref[...] = jnp.full(m_ref.shape, NEG, F32)
    a_bufs[n - 1][...] = jnp.zeros_like(a_bufs[0])
    p_bufs[n - 1][...] = jnp.zeros_like(p_bufs[0])
    l_ref[...] = jnp.zeros_like(l_ref)
    acc_ref[...] = jnp.zeros_like(acc_ref)

    def k_of(t):
        return pl.multiple_of(t * tk, tk)

    def pv(u, t):
        acc_ref[...] = a_bufs[u][...] * acc_ref[...] + pv_dot(p_bufs[u], k_of(t))

    NEXT_FULL, NEXT_LAST, FINAL = range(3)

    def stage(u, t, kind):
        pv((u - 1) % n, jnp.maximum(t - 1, 0))
        if kind != FINAL:
            scores(k_of(t + 1), s_bufs[(u + 1) % n], kind == NEXT_LAST)
        _softmax_pass1(s_bufs[u], m_ref, a_bufs[u])
        _softmax_pass2(s_bufs[u], p_bufs[u], m_ref, a_bufs[u], l_ref)
        if kind == FINAL:
            pv(u, t)

    def by_slot(t, kind):
        for u in range(n):
            @pl.when(t % n == u)
            def _():
                stage(u, t, kind)

    for first_is_last in (False, True):
        @pl.when((n_full == 0) == first_is_last)
        def _():
            scores(k_of(0), s_bufs[0], first_is_last)

    def body(t, carry):
        by_slot(t, NEXT_FULL)
        return carry

    lax.fori_loop(0, n_full - 1, body, 0)

    @pl.when(n_full >= 1)
    def _():
        by_slot(n_full - 1, NEXT_LAST)

    by_slot(n_full, FINAL)
    return acc_ref[...] / jnp.maximum(jnp.sum(l_ref[...], axis=-1, keepdims=True), TINY)


def _nsa_attn_body(q_ref, ks_ref, vs_ref, kw_ref, vw_ref, sel_ref, oc_ref, gt_ref, es_ref, o_ref,
                   selb_ref, ws_ref, wp_ref, wa_ref, wm_ref, wl_ref, *scratch, tq, tk):
    hg = NSA_HEADS // NSA_GROUPS
    span = WINDOW + tq
    qs = pl.program_id(1) * tq
    lane = _lane((1, LANES))
    first = lane < HALF
    zero = jnp.zeros((tq, LANES), BF16)
    q_all = jnp.concatenate(
        [jnp.where((lane // HALF) == g, q_ref[0, :, j * LANES:(j + 1) * LANES], zero)
         for g in range(NSA_GROUPS) for j in range(hg)], axis=0)
    for g in range(NSA_GROUPS):
        picked = lax.dot_general(sel_ref[0], es_ref[g], (((0,), (0,)), ((), ())), preferred_element_type=F32)
        selb_ref[g] = (picked - 1.0) * (-NEG)

    def t_of(c):
        head, r0 = divmod(c * ROW_CHUNK, tq)
        g, j = divmod(head, hg)
        return g, r0, qs + r0 + lax.broadcasted_iota(I32, (ROW_CHUNK, 1), 0), _nsa_slope(g, j)

    def slc_scores(k0, dst, last):
        kpos = k0 + _lane((1, tk))

        def adjust(c, s):
            g, r0, t, slope = t_of(c)
            s = s + slope * kpos.astype(F32) + selb_ref[g, r0:r0 + ROW_CHUNK, pl.ds(k0, tk)]
            return jnp.where(t >= kpos, s, NEG) if last else s

        _put_scores(dst, 0, _dot_nt(q_all, ks_ref[0, pl.ds(k0, tk), :]), adjust)

    o_slc = _flash_causal(qs // tk, tk, slc_scores,
                          lambda p_ref, k0: _dot(p_ref[...], vs_ref[0, pl.ds(k0, tk), :]), scratch)

    w0 = pl.multiple_of(jnp.maximum(qs - WINDOW, 0), tq)
    wpos = w0 + _lane((1, span))

    def win_adjust(c, s):
        _, _, t, slope = t_of(c)
        dist = t - wpos
        return jnp.where((dist >= 0) & (dist < WINDOW), s + slope * wpos.astype(F32), NEG)

    _put_scores(ws_ref, 0, _dot_nt(q_all, kw_ref[0, pl.ds(w0, span), :]), win_adjust)
    wm_ref[...] = jnp.full(wm_ref.shape, NEG, F32)
    wl_ref[...] = jnp.zeros_like(wl_ref)
    _softmax_pass1(ws_ref, wm_ref, wa_ref)
    _softmax_pass2(ws_ref, wp_ref, wm_ref, wa_ref, wl_ref)
    o_win = _dot(wp_ref[...], vw_ref[0, pl.ds(w0, span), :]) / jnp.maximum(
        jnp.sum(wl_ref[...], axis=-1, keepdims=True), TINY)

    gates = jax.nn.sigmoid(gt_ref[0])
    for j in range(hg):
        out = jnp.zeros((tq, LANES), F32)
        for br, branch in enumerate((None, o_slc, o_win)):
            gm = jnp.where(first, gates[:, j * 3 + br:j * 3 + br + 1],
                           gates[:, hg * 3 + j * 3 + br:hg * 3 + j * 3 + br + 1])
            if branch is None:
                val = oc_ref[0, :, j * LANES:(j + 1) * LANES]
            else:
                val = jnp.where(first, branch[j * tq:(j + 1) * tq], branch[(hg + j) * tq:(hg + j + 1) * tq])
            out = out + gm * val
        o_ref[0, :, j * LANES:(j + 1) * LANES] = out.astype(o_ref.dtype)


def nsa_attn(hb, hf, sel, ocmp, esel, tq, tk):
    B, S, _ = hb.shape
    qw = NSA_HEADS * NSA_DH
    rows = NSA_HEADS * tq
    span = WINDOW + tq
    kv = lambda blk: pl.BlockSpec((1, S, LANES), lambda b, i: (b, 0, blk))
    return pl.pallas_call(
        functools.partial(_nsa_attn_body, tq=tq, tk=tk),
        grid=(B, S // tq),
        in_specs=[pl.BlockSpec((1, tq, qw), lambda b, i: (b, i, NSA_Q_BLK)),
                  kv(NSA_KS_BLK), kv(NSA_VS_BLK), kv(NSA_KW_BLK), kv(NSA_VW_BLK),
                  pl.BlockSpec((1, LANES, tq), lambda b, i: (b, 0, i)),
                  pl.BlockSpec((1, tq, qw), lambda b, i: (b, i, 0)),
                  pl.BlockSpec((1, tq, LANES), lambda b, i: (b, i, NSA_GATE_BLK)),
                  pl.BlockSpec(esel.shape, lambda b, i: (0, 0, 0))],
        out_specs=pl.BlockSpec((1, tq, qw), lambda b, i: (b, i, 0)),
        out_shape=jax.ShapeDtypeStruct((B, S, qw), BF16),
        scratch_shapes=([pltpu.VMEM((NSA_GROUPS, tq, S), F32), pltpu.VMEM((rows, span), F32),
                         pltpu.VMEM((rows, span), BF16)]
                        + [pltpu.VMEM((rows, LANES), F32)] * 3 + _flash_scratch(rows, tk)),
        compiler_params=_cp("parallel", "parallel"), name="nsa_attn")(
            hb, hb, hb, hb, hb, sel, ocmp, hf, esel)


def _nsa_tables(S):
    n_cmp = (S - CMP_LEN) // CMP_STRIDE + 1
    c_start = np.arange(n_cmp) * CMP_STRIDE
    c_end = c_start + CMP_LEN - 1
    j_start = np.arange(S // SLC_LEN) * SLC_LEN
    overlap = (c_end[:, None] >= j_start[None]) & (c_start[:, None] <= j_start[None] + SLC_LEN - 1)
    hg = NSA_HEADS // NSA_GROUPS
    ov = np.zeros((NSA_GROUPS, hg, N_CMP_PAD, LANES), np.float32)
    es = np.zeros((NSA_GROUPS, LANES, S), np.float32)
    for g in range(NSA_GROUPS):
        ov[g, :, :n_cmp, g * N_SLC:(g + 1) * N_SLC] = overlap
        es[g, g * N_SLC + np.arange(S) // SLC_LEN, np.arange(S)] = 1.0
    return jnp.asarray(ov.reshape(NSA_GROUPS, hg * N_CMP_PAD, LANES), BF16), jnp.asarray(es, BF16)


NSA_PERM = np.array([(half * (NSA_HEADS // NSA_GROUPS) + j) * NSA_DH + d
                     for j in range(NSA_HEADS // NSA_GROUPS) for half in range(2) for d in range(NSA_DH)])


def nsa_mixer(hb, hf, pos_k, w1_k, w2_k, pos_v, w1_v, w2_v):
    B, S, _ = hb.shape
    assert S // CMP_STRIDE == N_CMP_PAD

    def both_groups(w):
        z = jnp.zeros_like(w)
        return jnp.concatenate([jnp.concatenate([w, z], axis=-1), jnp.concatenate([z, w], axis=-1)], axis=-2)

    def w1_blocks(w1):
        return both_groups(w1.reshape(CMP_LEN, NSA_DH, CMP_HIDDEN)).astype(BF16)

    kc, vc = nsa_compress(hf, jnp.tile(pos_k, (1, NSA_GROUPS)), jnp.tile(pos_v, (1, NSA_GROUPS)),
                          w1_blocks(w1_k), w1_blocks(w1_v),
                          both_groups(w2_k).astype(BF16), both_groups(w2_v).astype(BF16))
    ov, es = _nsa_tables(S)
    ocmp, sel = nsa_cmp(hb, kc, vc, ov, 512)
    return nsa_attn(hb, hf, sel, ocmp, es, 256, 256)


def _diff_body(slope_ref, q1_ref, q2_ref, k1_ref, k2_ref, v_ref, lam_ref, gn_ref, o_ref, *scratch,
               tq, tk, hp, lambda_init):
    heads = 2 * hp
    head0 = pl.program_id(1) * heads
    qs = pl.program_id(2) * tq
    lane = _lane((1, LANES))
    zero = jnp.zeros((tq, LANES), BF16)

    def q_rows(q_ref, h):
        return jnp.where((lane // HALF) == h % 2, q_ref[0, :, (h // 2) * LANES:(h // 2 + 1) * LANES], zero)

    q1 = [q_rows(q1_ref, h) for h in range(heads)]
    q2 = [q_rows(q2_ref, h) for h in range(heads)]
    slopes = [slope_ref[head0 + h] for h in range(heads)]

    def scores(k0, dst, last):
        kpos = k0 + _lane((1, tk))

        def adjust(c, s):
            s = s + slopes[(c * ROW_CHUNK) // (2 * tq)] * kpos.astype(F32)
            if last:
                t = qs + (c * ROW_CHUNK) % tq + lax.broadcasted_iota(I32, (ROW_CHUNK, 1), 0)
                s = jnp.where(t >= kpos, s, NEG)
            return s

        for h in range(heads):
            blk = slice((h // 2) * LANES, (h // 2 + 1) * LANES)
            _put_scores(dst, (2 * h) * tq, _dot_nt(q1[h], k1_ref[0, pl.ds(k0, tk), blk]), adjust)
            _put_scores(dst, (2 * h + 1) * tq, _dot_nt(q2[h], k2_ref[0, pl.ds(k0, tk), blk]), adjust)

    def pv_dot(p_ref, k0):
        return jnp.concatenate(
            [_dot(p_ref[2 * h * tq:2 * (h + 1) * tq, :], v_ref[0, pl.ds(k0, tk), h * LANES:(h + 1) * LANES])
             for h in range(heads)], axis=0)

    o = _flash_causal(qs // tk, tk, scores, pv_dot, scratch)
    lam_rows = lam_ref[...]
    lam = (jnp.exp(jnp.sum(lam_rows[0:1] * lam_rows[1:2], axis=-1, keepdims=True))
           - jnp.exp(jnp.sum(lam_rows[2:3] * lam_rows[3:4], axis=-1, keepdims=True)) + lambda_init)
    for h in range(heads):
        r0 = 2 * h * tq
        y = _rms(o[r0:r0 + tq] - lam * o[r0 + tq:r0 + 2 * tq], gn_ref[...]) * (1.0 - lambda_init)
        o_ref[0, :, h * LANES:(h + 1) * LANES] = y.astype(o_ref.dtype)


def diff_mixer(hd, lam_rows, g_norm, lambda_init, tq, tk, hp):
    B, S, _ = hd.shape
    ng = DIFF_HEADS // (2 * hp)
    qk, vw = hp * LANES, 2 * hp * LANES
    slopes = jnp.asarray(2.0 ** (-8.0 * np.arange(1, DIFF_HEADS + 1) / DIFF_HEADS), F32)
    grid_spec = pltpu.PrefetchScalarGridSpec(
        num_scalar_prefetch=1,
        grid=(B, ng, S // tq),
        in_specs=[pl.BlockSpec((1, tq, qk), lambda b, p, i, s: (b, i, p)),
                  pl.BlockSpec((1, tq, qk), lambda b, p, i, s: (b, i, ng + p)),
                  pl.BlockSpec((1, S, qk), lambda b, p, i, s: (b, 0, 2 * ng + p)),
                  pl.BlockSpec((1, S, qk), lambda b, p, i, s: (b, 0, 3 * ng + p)),
                  pl.BlockSpec((1, S, vw), lambda b, p, i, s: (b, 0, 2 * ng + p)),
                  pl.BlockSpec((4, LANES), lambda b, p, i, s: (0, 0)),
                  pl.BlockSpec((1, LANES), lambda b, p, i, s: (0, 0))],
        out_specs=pl.BlockSpec((1, tq, vw), lambda b, p, i, s: (b, i, p)),
        scratch_shapes=_flash_scratch(4 * hp * tq, tk))
    return pl.pallas_call(
        functools.partial(_diff_body, tq=tq, tk=tk, hp=hp, lambda_init=lambda_init),
        grid_spec=grid_spec,
        out_shape=jax.ShapeDtypeStruct((B, S, DIFF_HEADS * DIFF_DV), BF16),
        compiler_params=_cp("parallel", "parallel", "parallel"), name="diff_attn")(
            slopes, hd, hd, hd, hd, hd, lam_rows, g_norm.reshape(1, LANES))


def _router_body(x_ref, g_ref, wr_ref, tri_ref, xn_ref, meta_ref, wts_ref, cnt_ref, run_ref):
    @pl.when(pl.program_id(0) == 0)
    def _():
        run_ref[...] = jnp.zeros_like(run_ref)

    xn = _rms(x_ref[...], g_ref[...])
    xn_ref[...] = xn
    lane = _lane((1, LANES))
    logits = jnp.dot(xn, wr_ref[...], precision=lax.Precision.HIGHEST, preferred_element_type=F32)
    logits = jnp.where(lane < N_EXPERTS, logits, LOWEST)
    m1 = jnp.max(logits, axis=-1, keepdims=True)
    i1 = jnp.min(jnp.where(logits == m1, lane, LANES), axis=-1, keepdims=True)
    rest = jnp.where(lane == i1, LOWEST, logits)
    m2 = jnp.max(rest, axis=-1, keepdims=True)
    i2 = jnp.min(jnp.where(rest == m2, lane, LANES), axis=-1, keepdims=True)
    e = jnp.exp(m2 - m1)
    w1 = 1.0 / (1.0 + e)
    w2 = e / (1.0 + e)
    onehot = ((lane == i1) | (lane == i2)).astype(F32)
    pos = _dot(tri_ref[...], onehot.astype(BF16)) + run_ref[...]
    p1 = jnp.sum(jnp.where(lane == i1, pos, 0.0), axis=-1, keepdims=True).astype(I32)
    p2 = jnp.sum(jnp.where(lane == i2, pos, 0.0), axis=-1, keepdims=True).astype(I32)
    run_ref[...] += jnp.sum(onehot, axis=0, keepdims=True)
    meta_ref[...] = jnp.where(lane == 0, i1, jnp.where(lane == 1, i2, jnp.where(lane == 2, p1, jnp.where(lane == 3, p2, 0))))
    wts_ref[...] = jnp.where(lane == 0, w1, jnp.where(lane == 1, w2, 0.0))
    cnt_ref[...] = run_ref[...]


def moe_router(x, g, wr, tm):
    T, D = x.shape
    tri = jnp.asarray(np.tril(np.ones((tm, tm), np.float32), -1), BF16)
    return pl.pallas_call(
        _router_body,
        grid=(T // tm,),
        in_specs=[pl.BlockSpec((tm, D), lambda i: (i, 0)),
                  pl.BlockSpec((1, D), lambda i: (0, 0)),
                  pl.BlockSpec((D, LANES), lambda i: (0, 0)),
                  pl.BlockSpec((tm, tm), lambda i: (0, 0))],
        out_specs=[pl.BlockSpec((tm, D), lambda i: (i, 0)),
                   pl.BlockSpec((tm, LANES), lambda i: (i, 0)),
                   pl.BlockSpec((tm, LANES), lambda i: (i, 0)),
                   pl.BlockSpec((1, LANES), lambda i: (0, 0))],
        out_shape=[jax.ShapeDtypeStruct((T, D), F32), jax.ShapeDtypeStruct((T, LANES), I32),
                   jax.ShapeDtypeStruct((T, LANES), F32), jax.ShapeDtypeStruct((1, LANES), F32)],
        scratch_shapes=[pltpu.VMEM((1, LANES), F32)],
        compiler_params=_cp("arbitrary"), name="moe_router")(x, g.reshape(1, D), wr, tri)


ROW_DMA_UNROLL = 8


def _row_copy(src_ref, src_row, dst_ref, dst_row, sem):
    return pltpu.make_async_copy(src_ref.at[pl.ds(src_row, 1)], dst_ref.at[pl.ds(dst_row, 1)], sem)


def _dispatch_body(dest_ref, xn_ref, zero_ref, xs_ref, sem, *, tm):
    del zero_ref
    base = pl.program_id(0) * tm

    def issue(t, c):
        for slot in range(TOP_K):
            _row_copy(xn_ref, t, xs_ref, dest_ref[TOP_K * (base + t) + slot], sem).start()
        return c

    lax.fori_loop(0, tm, issue, 0, unroll=ROW_DMA_UNROLL)

    def drain(t, c):
        for slot in range(TOP_K):
            _row_copy(xn_ref, 0, xs_ref, 0, sem).wait()
        return c

    lax.fori_loop(0, tm, drain, 0, unroll=ROW_DMA_UNROLL)


def moe_dispatch(dest, xn, rows, tm):
    T, D = xn.shape
    grid_spec = pltpu.PrefetchScalarGridSpec(
        num_scalar_prefetch=1,
        grid=(T // tm,),
        in_specs=[pl.BlockSpec((tm, D), lambda i, d: (i, 0)),
                  pl.BlockSpec(memory_space=pl.ANY)],
        out_specs=pl.BlockSpec(memory_space=pl.ANY),
        scratch_shapes=[pltpu.SemaphoreType.DMA])
    return pl.pallas_call(
        functools.partial(_dispatch_body, tm=tm),
        grid_spec=grid_spec,
        out_shape=jax.ShapeDtypeStruct((rows, D), F32),
        input_output_aliases={2: 0},
        compiler_params=_cp("arbitrary"), name="moe_dispatch")(dest, xn, jnp.zeros((rows, D), F32))


def _experts_body(te_ref, tv_ref, xs_ref, wg_ref, wu_ref, wd_ref, o_ref, xb_ref):
    i = pl.program_id(0)

    @pl.when(pl.program_id(1) == 0)
    def _():
        xb_ref[...] = xs_ref[...].astype(BF16)
        o_ref[...] = jnp.zeros_like(o_ref)

    @pl.when(tv_ref[i] > 0)
    def _():
        xb = xb_ref[...]
        hg = _dot(xb, wg_ref[0])
        hu = _dot(xb, wu_ref[0])
        o_ref[...] += _dot((hg * jax.nn.sigmoid(hg) * hu).astype(BF16), wd_ref[0])


def moe_experts(tile_expert, tile_valid, xs, wg, wu, wd, tr, tf):
    R, D = xs.shape
    F = wg.shape[2]
    grid_spec = pltpu.PrefetchScalarGridSpec(
        num_scalar_prefetch=2,
        grid=(R // tr, F // tf),
        in_specs=[pl.BlockSpec((tr, D), lambda i, k, te, tv: (i, 0)),
                  pl.BlockSpec((1, D, tf), lambda i, k, te, tv: (te[i], 0, k)),
                  pl.BlockSpec((1, D, tf), lambda i, k, te, tv: (te[i], 0, k)),
                  pl.BlockSpec((1, tf, D), lambda i, k, te, tv: (te[i], k, 0))],
        out_specs=pl.BlockSpec((tr, D), lambda i, k, te, tv: (i, 0)),
        scratch_shapes=[pltpu.VMEM((tr, D), BF16)])
    return pl.pallas_call(
        _experts_body,
        grid_spec=grid_spec,
        out_shape=jax.ShapeDtypeStruct((R, D), F32),
        compiler_params=_cp("parallel", "arbitrary"), name="moe_experts")(tile_expert, tile_valid, xs, wg, wu, wd)


def _combine_body(dest_ref, x_ref, wts_ref, g_ref, ys_ref, o_ref, buf_ref, sem, *, tm):
    base = pl.program_id(0) * tm

    def issue(t, c):
        for slot in range(TOP_K):
            _row_copy(ys_ref, dest_ref[TOP_K * (base + t) + slot], buf_ref.at[slot], t, sem).start()
        return c

    lax.fori_loop(0, tm, issue, 0, unroll=ROW_DMA_UNROLL)

    def drain(t, c):
        for slot in range(TOP_K):
            _row_copy(ys_ref, 0, buf_ref.at[slot], 0, sem).wait()
        return c

    lax.fori_loop(0, tm, drain, 0, unroll=ROW_DMA_UNROLL)
    wts = wts_ref[...]
    y = x_ref[...] + (wts[:, 0:1] * buf_ref[0] + wts[:, 1:2] * buf_ref[1])
    o_ref[...] = _rms(y, g_ref[...])


def moe_combine(dest, x, wts, g_final, ys, tm):
    T, D = x.shape
    grid_spec = pltpu.PrefetchScalarGridSpec(
        num_scalar_prefetch=1,
        grid=(T // tm,),
        in_specs=[pl.BlockSpec((tm, D), lambda i, d: (i, 0)),
                  pl.BlockSpec((tm, LANES), lambda i, d: (i, 0)),
                  pl.BlockSpec((1, D), lambda i, d: (0, 0)),
                  pl.BlockSpec(memory_space=pl.ANY)],
        out_specs=pl.BlockSpec((tm, D), lambda i, d: (i, 0)),
        scratch_shapes=[pltpu.VMEM((TOP_K, tm, D), F32), pltpu.SemaphoreType.DMA])
    return pl.pallas_call(
        functools.partial(_combine_body, tm=tm),
        grid_spec=grid_spec,
        out_shape=jax.ShapeDtypeStruct((T, D), F32),
        compiler_params=_cp("arbitrary"), name="moe_combine")(dest, x, wts, g_final.reshape(1, D), ys)


def moe_block(x, g, w_router, wg, wu, wd, g_final, tr):
    T, D = x.shape
    wr = _pad_cols(w_router, LANES)
    xn, meta, wts, counts = moe_router(x, g, wr, 512)
    cnt = counts[0, :N_EXPERTS].astype(I32)
    gsz = ((cnt + tr - 1) // tr) * tr
    gend = jnp.cumsum(gsz)
    goff = gend - gsz
    dest = (jnp.take(goff, meta[:, 0:TOP_K]) + meta[:, TOP_K:2 * TOP_K]).reshape(-1)
    n_tiles = (TOP_K * T) // tr + N_EXPERTS
    tile_start = jnp.arange(n_tiles, dtype=I32) * tr
    tile_valid = (tile_start < gend[-1]).astype(I32)
    tile_expert = jnp.minimum(jnp.sum((tile_start[:, None] >= gend[None, :]).astype(I32), axis=1), N_EXPERTS - 1)
    last_valid = jnp.take(tile_expert, jnp.maximum(gend[-1] // tr - 1, 0))
    tile_expert = jnp.where(tile_valid > 0, tile_expert, last_valid)
    xs = moe_dispatch(dest, xn, n_tiles * tr, 256)
    ys = moe_experts(tile_expert, tile_valid, xs, wg, wu, wd, tr, FFN_TF)
    return moe_combine(dest, x, wts, g_final, ys, 256)


def _pad_cols(w, width):
    return jnp.pad(w, ((0, 0), (0, width - w.shape[1])))


def _mem_kv(mem2, g, wk, wv, batch):
    w = jnp.concatenate([wk, wv], axis=1).astype(BF16)
    width = wk.shape[1]
    k, v = norm_proj(mem2, g, w, [(width, BF16), (width, BF16)], 256, "mem_kv")
    return k.reshape(batch, -1, width), v.reshape(batch, -1, width)


def kernel(x, mem, mix_norm_0, w_in_0, gla_w_alpha_0, gla_b_alpha_0, gla_out_norm_0, nsa_cmp_pos_k_0, nsa_cmp_w1_k_0, nsa_cmp_w2_k_0, nsa_cmp_pos_v_0, nsa_cmp_w1_v_0, nsa_cmp_w2_v_0, w_out_0, xattn_norm_0, xattn_mem_norm_0, xattn_wq_0, xattn_wk_0, xattn_wv_0, xattn_wo_0, ffn_norm_0, ffn_w_gate_0, ffn_w_up_0, ffn_w_down_0, mix_norm_1, w_in_1, diff_lq1_1, diff_lk1_1, diff_lq2_1, diff_lk2_1, diff_out_norm_1, w_out_1, xattn_norm_1, xattn_mem_norm_1, xattn_wq_1, xattn_wk_1, xattn_wv_1, xattn_wo_1, ffn_norm_1, moe_router_1, moe_w_gate_1, moe_w_up_1, moe_w_down_1, final_norm):
    B, S, D = x.shape
    T = B * S
    x2 = x.reshape(T, D)
    mem2 = mem.reshape(-1, D)
    q_scale = NSA_DH ** -0.5

    sizes = [GLA_HEADS * GLA_DK, GLA_HEADS * GLA_DK, GLA_HEADS * GLA_DV, GLA_HEADS * GLA_DV, GLA_RANK,
             NSA_HEADS * NSA_DH] + [NSA_GROUPS * NSA_DH] * 6 + [3 * NSA_HEADS]
    (g_q, g_k, g_v, g_r, g_a, n_q, n_kc, n_vc, n_ks, n_vs, n_kw, n_vw, n_g) = jnp.split(
        w_in_0, np.cumsum(sizes)[:-1].tolist(), axis=1)
    w0 = jnp.concatenate([n_q[:, NSA_PERM] * q_scale, n_ks, n_vs, n_kw, n_vw, g_q * q_scale, g_k, g_v,
                          _pad_cols(g_a, LANES), g_r, _pad_cols(n_g, LANES), n_kc, n_vc], axis=1).astype(BF16)
    hb, hf = norm_proj(x2, mix_norm_0, w0, [(HB0_WIDTH, BF16), (HF0_WIDTH, F32)], 512, "in_proj0")
    hb = hb.reshape(B, S, HB0_WIDTH)
    hf = hf.reshape(B, S, HF0_WIDTH)
    w_alpha = jnp.pad(gla_w_alpha_0, ((0, LANES - GLA_RANK), (0, 0))).astype(BF16)
    o_a = gla_mixer(hb, hf, w_alpha, gla_b_alpha_0.reshape(1, -1), gla_out_norm_0, GLA_COLS, 8)
    o_b = nsa_mixer(hb, hf, nsa_cmp_pos_k_0, nsa_cmp_w1_k_0, nsa_cmp_w2_k_0,
                    nsa_cmp_pos_v_0, nsa_cmp_w1_v_0, nsa_cmp_w2_v_0)
    n_a = GLA_HEADS * GLA_DV
    k0, v0 = _mem_kv(mem2, xattn_mem_norm_0, xattn_wk_0, xattn_wv_0, B)
    x2 = proj_xattn_block(x2, [o_a.reshape(T, -1), o_b.reshape(T, -1)],
                          [w_out_0[:n_a].astype(BF16), w_out_0[n_a:][NSA_PERM].astype(BF16)],
                          xattn_norm_0, (xattn_wq_0 * q_scale).astype(BF16), k0, v0, xattn_wo_0.astype(BF16),
                          S, 1024, "out_xattn0")
    x2 = ffn_block(x2, ffn_norm_0, ffn_w_gate_0.astype(BF16), ffn_w_up_0.astype(BF16), ffn_w_down_0.astype(BF16),
                   1024, FFN_TF, "ffn0")

    lambda_init = 0.8 - 0.6 * math.exp(-0.3 * 1)
    n_q1 = 2 * DIFF_HEADS * DIFF_DH
    w1 = jnp.concatenate([w_in_1[:, :n_q1] * q_scale, w_in_1[:, n_q1:]], axis=1).astype(BF16)
    (hd,) = norm_proj(x2, mix_norm_1, w1, [(w1.shape[1], BF16)], 512, "in_proj1")
    lam_rows = _pad_cols(jnp.stack([diff_lq1_1, diff_lk1_1, diff_lq2_1, diff_lk2_1]), LANES)
    o_d = diff_mixer(hd.reshape(B, S, -1), lam_rows, diff_out_norm_1, lambda_init, 256, 256, 4)
    k1, v1 = _mem_kv(mem2, xattn_mem_norm_1, xattn_wk_1, xattn_wv_1, B)
    x2 = proj_xattn_block(x2, [o_d.reshape(T, -1)], [w_out_1.astype(BF16)],
                          xattn_norm_1, (xattn_wq_1 * q_scale).astype(BF16), k1, v1, xattn_wo_1.astype(BF16),
                          S, 1024, "out_xattn1")
    out = moe_block(x2, ffn_norm_1, moe_router_1, moe_w_gate_1.astype(BF16), moe_w_up_1.astype(BF16),
                    moe_w_down_1.astype(BF16), final_norm, 512)
    return out.reshape(B, S, D)
```

```python
import functools
import math

import numpy as np
import jax
import jax.numpy as jnp
from jax import lax
from jax.experimental import pallas as pl
from jax.experimental.pallas import tpu as pltpu

F32 = jnp.float32
BF16 = jnp.bfloat16
I32 = jnp.int32

D_MODEL = 1024
GLA_HEADS, GLA_DK, GLA_DV, GLA_RANK, GLA_TAU, GLA_CHUNK = 4, 64, 128, 16, 16.0, 64
NSA_HEADS, NSA_GROUPS, NSA_DH = 8, 2, 64
CMP_LEN, CMP_STRIDE, CMP_HIDDEN, SLC_LEN, N_SEL, WINDOW = 32, 16, 256, 64, 8, 512
DIFF_HEADS, DIFF_DH, DIFF_DV = 8, 64, 128
XATTN_HEADS, XATTN_DH = 4, 64
D_FF, N_EXPERTS, TOP_K = 2816, 8, 2
EPS, NEG, TINY, FORCE_SCORE = 1e-6, -1e30, 1e-30, 1e4

LANES = 128
HALF = LANES // 2
VMEM_LIMIT = 56 * 1024 * 1024
LOWEST = -3.0e38

HB0_WIDTH, HF0_WIDTH = 17 * LANES, 7 * LANES
NSA_Q_BLK, NSA_KS_BLK, NSA_VS_BLK, NSA_KW_BLK, NSA_VW_BLK = 0, 4, 5, 6, 7
GLA_COLS = (8, 10, 12, 16, 0)
NSA_GATE_BLK, NSA_KC_BLK, NSA_VC_BLK = 4, 5, 6


def _cp(*sem, flags=None):
    return pltpu.CompilerParams(dimension_semantics=sem, vmem_limit_bytes=VMEM_LIMIT, flags=flags)


def _rms(x, g):
    y = x * lax.rsqrt(jnp.mean(x * x, axis=-1, keepdims=True) + EPS)
    return y * g


def _dot(a, b):
    return jnp.dot(a, b, preferred_element_type=F32)


def _dot_nt(a, b):
    return lax.dot_general(a, b, (((1,), (1,)), ((), ())), preferred_element_type=F32)


def _lane(shape):
    return lax.broadcasted_iota(I32, shape, len(shape) - 1)


def _norm_proj_body(x_ref, g_ref, w_ref, *o_refs, widths):
    xn = _rms(x_ref[...], g_ref[...]).astype(BF16)
    off = 0
    for o_ref, wd in zip(o_refs, widths):
        for c0 in range(0, wd, 512):
            cw = min(512, wd - c0)
            o_ref[:, c0:c0 + cw] = _dot(xn, w_ref[:, off + c0:off + c0 + cw]).astype(o_ref.dtype)
        off += wd


def norm_proj(x, g, w, outs, tm, name):
    T, D = x.shape
    widths = tuple(o[0] for o in outs)
    return pl.pallas_call(
        functools.partial(_norm_proj_body, widths=widths),
        grid=(T // tm,),
        in_specs=[pl.BlockSpec((tm, D), lambda i: (i, 0)),
                  pl.BlockSpec((1, D), lambda i: (0, 0)),
                  pl.BlockSpec(w.shape, lambda i: (0, 0))],
        out_specs=[pl.BlockSpec((tm, wd), lambda i: (i, 0)) for wd, _ in outs],
        out_shape=[jax.ShapeDtypeStruct((T, wd), dt) for wd, dt in outs],
        compiler_params=_cp("parallel"), name=name)(x, g.reshape(1, D), w)


def _proj_xattn_body(*refs, n):
    res_ref = refs[0]
    a_refs = refs[1:1 + n]
    w_refs = refs[1 + n:1 + 2 * n]
    g_ref, wq_ref, k_ref, v_ref, wo_ref, o_ref = refs[1 + 2 * n:]
    x = res_ref[...]
    for a_ref, w_ref in zip(a_refs, w_refs):
        x = x + _dot(a_ref[...].astype(BF16), w_ref[...])
    q = _dot(_rms(x, g_ref[...]).astype(BF16), wq_ref[...]).astype(BF16)
    tm = q.shape[0]
    head_of_lane = _lane((1, q.shape[1])) // XATTN_DH
    zero = jnp.zeros_like(q)
    q_all = jnp.concatenate([jnp.where(head_of_lane == h, q, zero) for h in range(XATTN_HEADS)], axis=0)
    s = _dot_nt(q_all, k_ref[0])
    p = jnp.exp(s - jnp.max(s, axis=-1, keepdims=True))
    p = p / jnp.sum(p, axis=-1, keepdims=True)
    o_all = _dot(p.astype(BF16), v_ref[0])
    o = o_all[0:tm]
    for h in range(1, XATTN_HEADS):
        o = jnp.where(head_of_lane == h, o_all[h * tm:(h + 1) * tm], o)
    o_ref[...] = x + _dot(o.astype(BF16), wo_ref[...])


def proj_xattn_block(res, a_list, w_list, g, wq, k, v, wo, seq, tm, name):
    T, D = res.shape
    n = len(a_list)
    M, W = k.shape[1], k.shape[2]
    per = seq // tm
    return pl.pallas_call(
        functools.partial(_proj_xattn_body, n=n),
        grid=(T // tm,),
        in_specs=([pl.BlockSpec((tm, D), lambda i: (i, 0))]
                  + [pl.BlockSpec((tm, a.shape[1]), lambda i: (i, 0)) for a in a_list]
                  + [pl.BlockSpec(w.shape, lambda i: (0, 0)) for w in w_list]
                  + [pl.BlockSpec((1, D), lambda i: (0, 0)),
                     pl.BlockSpec(wq.shape, lambda i: (0, 0)),
                     pl.BlockSpec((1, M, W), lambda i: (i // per, 0, 0)),
                     pl.BlockSpec((1, M, W), lambda i: (i // per, 0, 0)),
                     pl.BlockSpec(wo.shape, lambda i: (0, 0))]),
        out_specs=pl.BlockSpec((tm, D), lambda i: (i, 0)),
        out_shape=jax.ShapeDtypeStruct((T, D), F32),
        compiler_params=_cp("parallel"), name=name)(res, *a_list, *w_list, g.reshape(1, D), wq, k, v, wo)


def _ffn_body(x_ref, g_ref, wg_ref, wu_ref, wd_ref, o_ref, xn_ref):
    @pl.when(pl.program_id(1) == 0)
    def _():
        xn_ref[...] = _rms(x_ref[...], g_ref[...]).astype(BF16)
        o_ref[...] = x_ref[...]

    xn = xn_ref[...]
    hg = _dot(xn, wg_ref[...])
    hu = _dot(xn, wu_ref[...])
    h = (hg * jax.nn.sigmoid(hg) * hu).astype(BF16)
    o_ref[...] += _dot(h, wd_ref[...])


def ffn_block(x, g, wg, wu, wd, tm, tf, name):
    T, D = x.shape
    F = wg.shape[1]
    return pl.pallas_call(
        _ffn_body,
        grid=(T // tm, F // tf),
        in_specs=[pl.BlockSpec((tm, D), lambda i, k: (i, 0)),
                  pl.BlockSpec((1, D), lambda i, k: (0, 0)),
                  pl.BlockSpec((D, tf), lambda i, k: (0, k)),
                  pl.BlockSpec((D, tf), lambda i, k: (0, k)),
                  pl.BlockSpec((tf, D), lambda i, k: (k, 0))],
        out_specs=pl.BlockSpec((tm, D), lambda i, k: (i, 0)),
        out_shape=jax.ShapeDtypeStruct((T, D), F32),
        scratch_shapes=[pltpu.VMEM((tm, D), BF16)],
        compiler_params=_cp("parallel", "arbitrary"), name=name)(x, g.reshape(1, D), wg, wu, wd)


GLA_TILE = 128
_GLA_LEVELS = tuple(GLA_TILE >> (i + 1) for i in range(GLA_TILE.bit_length() - 1))


def _gla_constants():
    C = GLA_TILE
    t = np.arange(C)
    r = t[None, :]
    mats = [r <= t[:, None], r > t[:, None]]
    for hs in _GLA_LEVELS:
        c = (t // (2 * hs)) * (2 * hs) + hs - 1
        right = (t % (2 * hs)) >= hs
        m_right = (r > c[:, None]) & (r <= t[:, None])
        m_left = (r > t[:, None]) & (r <= c[:, None])
        mats.append(np.where(right[:, None], m_right, m_left))
    cmat = np.concatenate(mats, 0).astype(np.float32)
    x = t[:, None] ^ t[None, :]
    lvl = np.full((C, C), -1, np.int32)
    for i, hs in enumerate(_GLA_LEVELS):
        lvl[(t[:, None] > t[None, :]) & (x >= hs) & (x < 2 * hs)] = i
    lvl[t[:, None] == t[None, :]] = len(_GLA_LEVELS)
    return cmat, lvl


def _gla_body(q_ref, k_ref, v_ref, a_ref, r_ref, wa_ref, ba_ref, gn_ref, cmat_ref, lvl_ref, o_ref, st_ref, *, nb):
    C = GLA_TILE

    @pl.when(pl.program_id(2) == 0)
    def _():
        st_ref[...] = jnp.zeros_like(st_ref)

    nlev = len(_GLA_LEVELS)
    in_a = _lane((1, LANES)) < HALF
    lvl = lvl_ref[...]
    cmat = cmat_ref[...]
    on_diag = (lax.broadcasted_iota(I32, (LANES, 2 * LANES), 0) // HALF
               == lax.broadcasted_iota(I32, (LANES, 2 * LANES), 1) // LANES)
    for bi in range(nb):
        q = q_ref[bi].astype(F32)
        k = k_ref[bi].astype(F32)
        z = _dot(a_ref[bi], wa_ref[...]) + ba_ref[...]
        la = (jnp.minimum(z, 0.0) - jnp.log1p(jnp.exp(-jnp.abs(z)))) / GLA_TAU
        hi = la.astype(BF16)
        rest = la - hi.astype(F32)
        mid = rest.astype(BF16)
        lo = (rest - mid.astype(F32)).astype(BF16)
        u3 = _dot(cmat, jnp.concatenate([hi, mid, lo], axis=1))
        e = jnp.exp(u3[:, 0:LANES] + u3[:, LANES:2 * LANES] + u3[:, 2 * LANES:3 * LANES])
        qhat = (q * e[0:C]).astype(BF16)
        kend = (k * e[C:2 * C]).astype(BF16)
        decay = jnp.broadcast_to(e[C - 1:C], (LANES, LANES)).T
        zero = jnp.zeros_like(q)

        def stacked(x):
            return jnp.concatenate([jnp.where(in_a, x, zero), jnp.where(in_a, zero, x)], axis=0).astype(BF16)

        sc = jnp.zeros((2 * C, C), F32)
        for i in range(nlev):
            ei = e[(2 + i) * C:(3 + i) * C]
            sc = jnp.where(lvl == i, _dot_nt(stacked(q * ei), (k * ei).astype(BF16)), sc)
        sc = jnp.where(lvl == nlev, _dot_nt(stacked(q), k.astype(BF16)), sc)
        v = v_ref[bi]
        state = st_ref[bi]
        o_inter = _dot(qhat, state.astype(BF16))
        o_intra = _dot(sc.astype(BF16), v)
        update = lax.dot_general(kend, v, (((0,), (0,)), ((), ())), preferred_element_type=F32)
        st_ref[bi] = jnp.where(on_diag, jnp.concatenate([decay, decay], axis=1) * state + update, 0.0)
        for half in range(2):
            cols = slice(half * LANES, (half + 1) * LANES)
            y = _rms(o_inter[:, cols] + o_intra[half * C:(half + 1) * C, cols], gn_ref[...])
            rr = r_ref[bi, :, cols]
            o_ref[bi, :, cols] = (y * (rr * jax.nn.sigmoid(rr))).astype(o_ref.dtype)


def gla_mixer(hb, hf, w_alpha, b_alpha, g_norm, cols, nb):
    B, S, _ = hb.shape
    C = GLA_TILE
    cmat, lvl = _gla_constants()
    lvl = np.concatenate([lvl, lvl], axis=0)
    qb, kb_, vb, ab, rb = cols
    return pl.pallas_call(
        functools.partial(_gla_body, nb=nb),
        grid=(B // nb, GLA_HEADS // 2, S // C),
        in_specs=[pl.BlockSpec((nb, C, LANES), lambda b, p, c: (b, c, qb + p)),
                  pl.BlockSpec((nb, C, LANES), lambda b, p, c: (b, c, kb_ + p)),
                  pl.BlockSpec((nb, C, 2 * LANES), lambda b, p, c: (b, c, vb // 2 + p)),
                  pl.BlockSpec((nb, C, LANES), lambda b, p, c: (b, c, ab)),
                  pl.BlockSpec((nb, C, 2 * LANES), lambda b, p, c: (b, c, rb // 2 + p)),
                  pl.BlockSpec((LANES, LANES), lambda b, p, c: (0, p)),
                  pl.BlockSpec((1, LANES), lambda b, p, c: (0, p)),
                  pl.BlockSpec((1, LANES), lambda b, p, c: (0, 0)),
                  pl.BlockSpec(cmat.shape, lambda b, p, c: (0, 0)),
                  pl.BlockSpec(lvl.shape, lambda b, p, c: (0, 0))],
        out_specs=pl.BlockSpec((nb, C, 2 * LANES), lambda b, p, c: (b, c, p)),
        out_shape=jax.ShapeDtypeStruct((B, S, GLA_HEADS * GLA_DV), BF16),
        scratch_shapes=[pltpu.VMEM((nb, LANES, 2 * LANES), F32)],
        compiler_params=_cp("parallel", "parallel", "arbitrary"), name="gla")(
            hb, hb, hb, hb, hf, w_alpha, b_alpha, g_norm.reshape(1, LANES), jnp.asarray(cmat, BF16), jnp.asarray(lvl))


N_CMP_PAD = 128
N_SLC = 32


def _gelu_tanh(x):
    return 0.5 * x * (1.0 + jnp.tanh(math.sqrt(2.0 / math.pi) * (x + 0.044715 * (x * x * x))))


def _nsa_compress_body(tk_ref, tv_ref, pk_ref, pv_ref, w1k_ref, w1v_ref, w2k_ref, w2v_ref, kc_ref, vc_ref):
    for t_ref, p_ref, w1_ref, w2_ref, o_ref in ((tk_ref, pk_ref, w1k_ref, w2k_ref, kc_ref),
                                                (tv_ref, pv_ref, w1v_ref, w2v_ref, vc_ref)):
        lo = jnp.zeros((N_CMP_PAD, NSA_GROUPS * CMP_HIDDEN), F32)
        hi = jnp.zeros((N_CMP_PAD, NSA_GROUPS * CMP_HIDDEN), F32)
        for l in range(CMP_STRIDE):
            rows = t_ref[0, pl.ds(l, N_CMP_PAD, stride=CMP_STRIDE), :]
            lo = lo + _dot((rows + p_ref[l:l + 1]).astype(BF16), w1_ref[l])
            hi = hi + _dot((rows + p_ref[CMP_STRIDE + l:CMP_STRIDE + l + 1]).astype(BF16), w1_ref[CMP_STRIDE + l])
        pre = lo + pltpu.roll(hi, N_CMP_PAD - 1, 0)
        o_ref[0] = _dot(_gelu_tanh(pre).astype(BF16), w2_ref[...]).astype(o_ref.dtype)


def nsa_compress(hf, pk, pv, w1k, w1v, w2k, w2v):
    B, S, _ = hf.shape
    full = lambda a: pl.BlockSpec(a.shape, lambda b: (0,) * a.ndim)
    out = pl.BlockSpec((1, N_CMP_PAD, LANES), lambda b: (b, 0, 0))
    return pl.pallas_call(
        _nsa_compress_body,
        grid=(B,),
        in_specs=[pl.BlockSpec((1, S, LANES), lambda b: (b, 0, NSA_KC_BLK)),
                  pl.BlockSpec((1, S, LANES), lambda b: (b, 0, NSA_VC_BLK)),
                  full(pk), full(pv), full(w1k), full(w1v), full(w2k), full(w2v)],
        out_specs=[out, out],
        out_shape=[jax.ShapeDtypeStruct((B, N_CMP_PAD, LANES), BF16)] * 2,
        compiler_params=_cp("parallel"), name="nsa_compress")(hf, hf, pk, pv, w1k, w1v, w2k, w2v)


def _nsa_slope(g, j):
    return 2.0 ** (-(g * (NSA_HEADS // NSA_GROUPS) + j + 1))


def _nsa_cmp_body(q_ref, kc_ref, vc_ref, ov_ref, o_ref, selt_ref, sct_ref, *, tq):
    hg = NSA_HEADS // NSA_GROUPS
    qs = pl.program_id(1) * tq
    lane = _lane((1, LANES))
    tpos = qs + lax.broadcasted_iota(I32, (tq, LANES), 0)
    dist = tpos - (lane * CMP_STRIDE + (CMP_LEN - 1))
    valid = (dist >= 0) & (lane < N_CMP_PAD - 1)
    validf = valid.astype(F32)
    distf = dist.astype(F32)
    zero = jnp.zeros((tq, LANES), BF16)
    q_all = jnp.concatenate(
        [jnp.where((lane // HALF) == g, q_ref[0, :, j * LANES:(j + 1) * LANES], zero)
         for g in range(NSA_GROUPS) for j in range(hg)], axis=0)
    s_all = _dot_nt(q_all, kc_ref[0])
    probs = []
    for h in range(NSA_HEADS):
        s = s_all[h * tq:(h + 1) * tq] - _nsa_slope(h // hg, h % hg) * distf
        s = jnp.where(valid, s, NEG)
        p = jnp.exp(s - jnp.max(s, axis=-1, keepdims=True)) * validf
        probs.append((p / jnp.maximum(jnp.sum(p, axis=-1, keepdims=True), TINY)).astype(BF16))
    o_all = _dot(jnp.concatenate(probs, axis=0), vc_ref[0])
    first = lane < HALF
    for j in range(hg):
        o_ref[0, :, j * LANES:(j + 1) * LANES] = jnp.where(first, o_all[j * tq:(j + 1) * tq],
                                                           o_all[(hg + j) * tq:(hg + j + 1) * tq])
    imp = jnp.zeros((tq, LANES), F32)
    for g in range(NSA_GROUPS):
        imp = imp + _dot(jnp.concatenate(probs[g * hg:(g + 1) * hg], axis=1), ov_ref[g])
    blk = lane % N_SLC
    cur = tpos // SLC_LEN
    score = jnp.where((blk == cur) | (blk == 0), FORCE_SCORE, jnp.where(blk <= cur, imp, NEG))
    sct_ref[...] = score.T
    row8 = lax.broadcasted_iota(I32, (8, tq), 0)
    for g in range(NSA_GROUPS):
        base = g * N_SLC
        tiles = [sct_ref[base + 8 * v:base + 8 * (v + 1), :] for v in range(N_SLC // 8)]
        beaten = [jnp.zeros((8, tq), F32) for _ in tiles]
        for i in range(N_SLC):
            cand = jnp.broadcast_to(sct_ref[base + i:base + i + 1, :], (8, tq))
            for v, tile in enumerate(tiles):
                ge = jnp.where(cand >= tile, 1.0, 0.0)
                gt = jnp.where(cand > tile, 1.0, 0.0)
                if 8 * v > i:
                    wins = ge
                elif 8 * v + 7 <= i:
                    wins = gt
                else:
                    wins = jnp.where(row8 > i - 8 * v, ge, gt)
                beaten[v] = beaten[v] + wins
        for v in range(N_SLC // 8):
            selt_ref[0, base + 8 * v:base + 8 * (v + 1), :] = jnp.where(beaten[v] < N_SEL, 1.0, 0.0).astype(selt_ref.dtype)
    selt_ref[0, NSA_GROUPS * N_SLC:LANES, :] = jnp.zeros((LANES - NSA_GROUPS * N_SLC, tq), selt_ref.dtype)


def nsa_cmp(hb, kc, vc, ov, tq):
    B, S, _ = hb.shape
    qw = NSA_HEADS * NSA_DH
    return pl.pallas_call(
        functools.partial(_nsa_cmp_body, tq=tq),
        grid=(B, S // tq),
        in_specs=[pl.BlockSpec((1, tq, qw), lambda b, i: (b, i, NSA_Q_BLK)),
                  pl.BlockSpec((1, N_CMP_PAD, LANES), lambda b, i: (b, 0, 0)),
                  pl.BlockSpec((1, N_CMP_PAD, LANES), lambda b, i: (b, 0, 0)),
                  pl.BlockSpec(ov.shape, lambda b, i: (0, 0, 0))],
        out_specs=[pl.BlockSpec((1, tq, qw), lambda b, i: (b, i, 0)),
                   pl.BlockSpec((1, LANES, tq), lambda b, i: (b, 0, i))],
        out_shape=[jax.ShapeDtypeStruct((B, S, qw), F32), jax.ShapeDtypeStruct((B, LANES, S), BF16)],
        scratch_shapes=[pltpu.VMEM((LANES, tq), F32)],
        compiler_params=_cp("parallel", "parallel"), name="nsa_cmp")(hb, kc, vc, ov)


def _softmax_step(s, maskf, v, m, l, acc):
    m_new = jnp.maximum(m, jnp.max(s, axis=-1, keepdims=True))
    p = jnp.exp(s - m_new) * maskf
    alpha = jnp.exp(m - m_new)
    l = alpha * l + jnp.sum(p, axis=-1, keepdims=True)
    acc = alpha * acc + _dot(p.astype(BF16), v)
    return m_new, l, acc


FFN_TF = D_FF // 2
ROW_CHUNK = 32
LOG2E = math.log2(math.e)


FLASH_SLOTS = 2


def _flash_scratch(rows, tk):
    return ([pltpu.VMEM((rows, tk), F32)] * FLASH_SLOTS + [pltpu.VMEM((rows, tk), BF16)] * FLASH_SLOTS
            + [pltpu.VMEM((rows, LANES), F32)] * (FLASH_SLOTS + 3))


def _put_scores(dst, r0, raw, adjust):
    for i in range(raw.shape[0] // ROW_CHUNK):
        r = r0 + i * ROW_CHUNK
        dst[r:r + ROW_CHUNK, :] = adjust(r // ROW_CHUNK, raw[i * ROW_CHUNK:(i + 1) * ROW_CHUNK, :]) * LOG2E


def _softmax_pass1(s_ref, m_ref, a_ref):
    rows, tk = s_ref.shape
    m_in = m_out = m_ref
    for c in range(rows // ROW_CHUNK):
        r = slice(c * ROW_CHUNK, (c + 1) * ROW_CHUNK)
        s = s_ref[r, :]
        top = s[:, 0:LANES]
        for j in range(1, tk // LANES):
            top = jnp.maximum(top, s[:, j * LANES:(j + 1) * LANES])
        m_old = m_in[r, :]
        m_new = jnp.maximum(m_old, jnp.broadcast_to(jnp.max(top, axis=-1, keepdims=True), (ROW_CHUNK, LANES)))
        m_out[r, :] = m_new
        a_ref[r, :] = jnp.exp2(m_old - m_new)


def _softmax_pass2(s_ref, p_ref, m_ref, a_ref, l_ref):
    rows, tk = s_ref.shape
    for c in range(rows // ROW_CHUNK):
        r = slice(c * ROW_CHUNK, (c + 1) * ROW_CHUNK)
        m_new = m_ref[r, :]
        part = None
        for j in range(tk // LANES):
            p = jnp.exp2(s_ref[r, j * LANES:(j + 1) * LANES] - m_new)
            p_ref[r, j * LANES:(j + 1) * LANES] = p.astype(BF16)
            part = p if part is None else part + p
        l_ref[r, :] = a_ref[r, :] * l_ref[r, :] + part


def _flash_causal(n_full, tk, scores, pv_dot, scratch):
    n = FLASH_SLOTS
    s_bufs, p_bufs, a_bufs = (scratch[i * n:(i + 1) * n] for i in range(3))
    m_ref, l_ref, acc_ref = scratch[3 * n:]
    m_ref[...] = jnp.full(m_ref.shape, NEG, F32)
    a_bufs[n - 1][...] = jnp.zeros_like(a_bufs[0])
    p_bufs[n - 1][...] = jnp.zeros_like(p_bufs[0])
    l_ref[...] = jnp.zeros_like(l_ref)
    acc_ref[...] = jnp.zeros_like(acc_ref)

    def k_of(t):
        return pl.multiple_of(t * tk, tk)

    def pv(u, t):
        acc_ref[...] = a_bufs[u][...] * acc_ref[...] + pv_dot(p_bufs[u], k_of(t))

    NEXT_FULL, NEXT_LAST, FINAL = range(3)

    def stage(u, t, kind):
        pv((u - 1) % n, jnp.maximum(t - 1, 0))
        if kind != FINAL:
            scores(k_of(t + 1), s_bufs[(u + 1) % n], kind == NEXT_LAST)
        _softmax_pass1(s_bufs[u], m_ref, a_bufs[u])
        _softmax_pass2(s_bufs[u], p_bufs[u], m_ref, a_bufs[u], l_ref)
        if kind == FINAL:
            pv(u, t)

    def by_slot(t, kind):
        for u in range(n):
            @pl.when(t % n == u)
            def _():
                stage(u, t, kind)

    for first_is_last in (False, True):
        @pl.when((n_full == 0) == first_is_last)
        def _():
            scores(k_of(0), s_bufs[0], first_is_last)

    def body(t, carry):
        by_slot(t, NEXT_FULL)
        return carry

    lax.fori_loop(0, n_full - 1, body, 0)

    @pl.when(n_full >= 1)
    def _():
        by_slot(n_full - 1, NEXT_LAST)

    by_slot(n_full, FINAL)
    return acc_ref[...] / jnp.maximum(jnp.sum(l_ref[...], axis=-1, keepdims=True), TINY)


def _nsa_attn_body(q_ref, ks_ref, vs_ref, kw_ref, vw_ref, sel_ref, oc_ref, gt_ref, es_ref, o_ref,
                   selb_ref, ws_ref, wp_ref, wa_ref, wm_ref, wl_ref, *scratch, tq, tk):
    hg = NSA_HEADS // NSA_GROUPS
    span = WINDOW + tq
    qs = pl.program_id(1) * tq
    lane = _lane((1, LANES))
    first = lane < HALF
    zero = jnp.zeros((tq, LANES), BF16)
    q_all = jnp.concatenate(
        [jnp.where((lane // HALF) == g, q_ref[0, :, j * LANES:(j + 1) * LANES], zero)
         for g in range(NSA_GROUPS) for j in range(hg)], axis=0)
    for g in range(NSA_GROUPS):
        picked = lax.dot_general(sel_ref[0], es_ref[g], (((0,), (0,)), ((), ())), preferred_element_type=F32)
        selb_ref[g] = (picked - 1.0) * (-NEG)

    def t_of(c):
        head, r0 = divmod(c * ROW_CHUNK, tq)
        g, j = divmod(head, hg)
        return g, r0, qs + r0 + lax.broadcasted_iota(I32, (ROW_CHUNK, 1), 0), _nsa_slope(g, j)

    def slc_scores(k0, dst, last):
        kpos = k0 + _lane((1, tk))

        def adjust(c, s):
            g, r0, t, slope = t_of(c)
            s = s + slope * kpos.astype(F32) + selb_ref[g, r0:r0 + ROW_CHUNK, pl.ds(k0, tk)]
            return jnp.where(t >= kpos, s, NEG) if last else s

        _put_scores(dst, 0, _dot_nt(q_all, ks_ref[0, pl.ds(k0, tk), :]), adjust)

    o_slc = _flash_causal(qs // tk, tk, slc_scores,
                          lambda p_ref, k0: _dot(p_ref[...], vs_ref[0, pl.ds(k0, tk), :]), scratch)

    w0 = pl.multiple_of(jnp.maximum(qs - WINDOW, 0), tq)
    wpos = w0 + _lane((1, span))

    def win_adjust(c, s):
        _, _, t, slope = t_of(c)
        dist = t - wpos
        return jnp.where((dist >= 0) & (dist < WINDOW), s + slope * wpos.astype(F32), NEG)

    _put_scores(ws_ref, 0, _dot_nt(q_all, kw_ref[0, pl.ds(w0, span), :]), win_adjust)
    wm_ref[...] = jnp.full(wm_ref.shape, NEG, F32)
    wl_ref[...] = jnp.zeros_like(wl_ref)
    _softmax_pass1(ws_ref, wm_ref, wa_ref)
    _softmax_pass2(ws_ref, wp_ref, wm_ref, wa_ref, wl_ref)
    o_win = _dot(wp_ref[...], vw_ref[0, pl.ds(w0, span), :]) / jnp.maximum(
        jnp.sum(wl_ref[...], axis=-1, keepdims=True), TINY)

    gates = jax.nn.sigmoid(gt_ref[0])
    for j in range(hg):
        out = jnp.zeros((tq, LANES), F32)
        for br, branch in enumerate((None, o_slc, o_win)):
            gm = jnp.where(first, gates[:, j * 3 + br:j * 3 + br + 1],
                           gates[:, hg * 3 + j * 3 + br:hg * 3 + j * 3 + br + 1])
            if branch is None:
                val = oc_ref[0, :, j * LANES:(j + 1) * LANES]
            else:
                val = jnp.where(first, branch[j * tq:(j + 1) * tq], branch[(hg + j) * tq:(hg + j + 1) * tq])
            out = out + gm * val
        o_ref[0, :, j * LANES:(j + 1) * LANES] = out.astype(o_ref.dtype)


def nsa_attn(hb, hf, sel, ocmp, esel, tq, tk):
    B, S, _ = hb.shape
    qw = NSA_HEADS * NSA_DH
    rows = NSA_HEADS * tq
    span = WINDOW + tq
    kv = lambda blk: pl.BlockSpec((1, S, LANES), lambda b, i: (b, 0, blk))
    return pl.pallas_call(
        functools.partial(_nsa_attn_body, tq=tq, tk=tk),
        grid=(B, S // tq),
        in_specs=[pl.BlockSpec((1, tq, qw), lambda b, i: (b, i, NSA_Q_BLK)),
                  kv(NSA_KS_BLK), kv(NSA_VS_BLK), kv(NSA_KW_BLK), kv(NSA_VW_BLK),
                  pl.BlockSpec((1, LANES, tq), lambda b, i: (b, 0, i)),
                  pl.BlockSpec((1, tq, qw), lambda b, i: (b, i, 0)),
                  pl.BlockSpec((1, tq, LANES), lambda b, i: (b, i, NSA_GATE_BLK)),
                  pl.BlockSpec(esel.shape, lambda b, i: (0, 0, 0))],
        out_specs=pl.BlockSpec((1, tq, qw), lambda b, i: (b, i, 0)),
        out_shape=jax.ShapeDtypeStruct((B, S, qw), BF16),
        scratch_shapes=([pltpu.VMEM((NSA_GROUPS, tq, S), F32), pltpu.VMEM((rows, span), F32),
                         pltpu.VMEM((rows, span), BF16)]
                        + [pltpu.VMEM((rows, LANES), F32)] * 3 + _flash_scratch(rows, tk)),
        compiler_params=_cp("parallel", "parallel"), name="nsa_attn")(
            hb, hb, hb, hb, hb, sel, ocmp, hf, esel)


def _nsa_tables(S):
    n_cmp = (S - CMP_LEN) // CMP_STRIDE + 1
    c_start = np.arange(n_cmp) * CMP_STRIDE
    c_end = c_start + CMP_LEN - 1
    j_start = np.arange(S // SLC_LEN) * SLC_LEN
    overlap = (c_end[:, None] >= j_start[None]) & (c_start[:, None] <= j_start[None] + SLC_LEN - 1)
    hg = NSA_HEADS // NSA_GROUPS
    ov = np.zeros((NSA_GROUPS, hg, N_CMP_PAD, LANES), np.float32)
    es = np.zeros((NSA_GROUPS, LANES, S), np.float32)
    for g in range(NSA_GROUPS):
        ov[g, :, :n_cmp, g * N_SLC:(g + 1) * N_SLC] = overlap
        es[g, g * N_SLC + np.arange(S) // SLC_LEN, np.arange(S)] = 1.0
    return jnp.asarray(ov.reshape(NSA_GROUPS, hg * N_CMP_PAD, LANES), BF16), jnp.asarray(es, BF16)


NSA_PERM = np.array([(half * (NSA_HEADS // NSA_GROUPS) + j) * NSA_DH + d
                     for j in range(NSA_HEADS // NSA_GROUPS) for half in range(2) for d in range(NSA_DH)])


def nsa_mixer(hb, hf, pos_k, w1_k, w2_k, pos_v, w1_v, w2_v):
    B, S, _ = hb.shape
    assert S // CMP_STRIDE == N_CMP_PAD

    def both_groups(w):
        z = jnp.zeros_like(w)
        return jnp.concatenate([jnp.concatenate([w, z], axis=-1), jnp.concatenate([z, w], axis=-1)], axis=-2)

    def w1_blocks(w1):
        return both_groups(w1.reshape(CMP_LEN, NSA_DH, CMP_HIDDEN)).astype(BF16)

    kc, vc = nsa_compress(hf, jnp.tile(pos_k, (1, NSA_GROUPS)), jnp.tile(pos_v, (1, NSA_GROUPS)),
                          w1_blocks(w1_k), w1_blocks(w1_v),
                          both_groups(w2_k).astype(BF16), both_groups(w2_v).astype(BF16))
    ov, es = _nsa_tables(S)
    ocmp, sel = nsa_cmp(hb, kc, vc, ov, 512)
    return nsa_attn(hb, hf, sel, ocmp, es, 256, 256)


def _diff_body(slope_ref, q1_ref, q2_ref, k1_ref, k2_ref, v_ref, lam_ref, gn_ref, o_ref, *scratch,
               tq, tk, hp, lambda_init):
    heads = 2 * hp
    head0 = pl.program_id(1) * heads
    qs = pl.program_id(2) * tq
    lane = _lane((1, LANES))
    zero = jnp.zeros((tq, LANES), BF16)

    def q_rows(q_ref, h):
        return jnp.where((lane // HALF) == h % 2, q_ref[0, :, (h // 2) * LANES:(h // 2 + 1) * LANES], zero)

    q1 = [q_rows(q1_ref, h) for h in range(heads)]
    q2 = [q_rows(q2_ref, h) for h in range(heads)]
    slopes = [slope_ref[head0 + h] for h in range(heads)]

    def scores(k0, dst, last):
        kpos = k0 + _lane((1, tk))

        def adjust(c, s):
            s = s + slopes[(c * ROW_CHUNK) // (2 * tq)] * kpos.astype(F32)
            if last:
                t = qs + (c * ROW_CHUNK) % tq + lax.broadcasted_iota(I32, (ROW_CHUNK, 1), 0)
                s = jnp.where(t >= kpos, s, NEG)
            return s

        for h in range(heads):
            blk = slice((h // 2) * LANES, (h // 2 + 1) * LANES)
            _put_scores(dst, (2 * h) * tq, _dot_nt(q1[h], k1_ref[0, pl.ds(k0, tk), blk]), adjust)
            _put_scores(dst, (2 * h + 1) * tq, _dot_nt(q2[h], k2_ref[0, pl.ds(k0, tk), blk]), adjust)

    def pv_dot(p_ref, k0):
        return jnp.concatenate(
            [_dot(p_ref[2 * h * tq:2 * (h + 1) * tq, :], v_ref[0, pl.ds(k0, tk), h * LANES:(h + 1) * LANES])
             for h in range(heads)], axis=0)

    o = _flash_causal(qs // tk, tk, scores, pv_dot, scratch)
    lam_rows = lam_ref[...]
    lam = (jnp.exp(jnp.sum(lam_rows[0:1] * lam_rows[1:2], axis=-1, keepdims=True))
           - jnp.exp(jnp.sum(lam_rows[2:3] * lam_rows[3:4], axis=-1, keepdims=True)) + lambda_init)
    for h in range(heads):
        r0 = 2 * h * tq
        y = _rms(o[r0:r0 + tq] - lam * o[r0 + tq:r0 + 2 * tq], gn_ref[...]) * (1.0 - lambda_init)
        o_ref[0, :, h * LANES:(h + 1) * LANES] = y.astype(o_ref.dtype)


def diff_mixer(hd, lam_rows, g_norm, lambda_init, tq, tk, hp):
    B, S, _ = hd.shape
    ng = DIFF_HEADS // (2 * hp)
    qk, vw = hp * LANES, 2 * hp * LANES
    slopes = jnp.asarray(2.0 ** (-8.0 * np.arange(1, DIFF_HEADS + 1) / DIFF_HEADS), F32)
    grid_spec = pltpu.PrefetchScalarGridSpec(
        num_scalar_prefetch=1,
        grid=(B, ng, S // tq),
        in_specs=[pl.BlockSpec((1, tq, qk), lambda b, p, i, s: (b, i, p)),
                  pl.BlockSpec((1, tq, qk), lambda b, p, i, s: (b, i, ng + p)),
                  pl.BlockSpec((1, S, qk), lambda b, p, i, s: (b, 0, 2 * ng + p)),
                  pl.BlockSpec((1, S, qk), lambda b, p, i, s: (b, 0, 3 * ng + p)),
                  pl.BlockSpec((1, S, vw), lambda b, p, i, s: (b, 0, 2 * ng + p)),
                  pl.BlockSpec((4, LANES), lambda b, p, i, s: (0, 0)),
                  pl.BlockSpec((1, LANES), lambda b, p, i, s: (0, 0))],
        out_specs=pl.BlockSpec((1, tq, vw), lambda b, p, i, s: (b, i, p)),
        scratch_shapes=_flash_scratch(4 * hp * tq, tk))
    return pl.pallas_call(
        functools.partial(_diff_body, tq=tq, tk=tk, hp=hp, lambda_init=lambda_init),
        grid_spec=grid_spec,
        out_shape=jax.ShapeDtypeStruct((B, S, DIFF_HEADS * DIFF_DV), BF16),
        compiler_params=_cp("parallel", "parallel", "parallel"), name="diff_attn")(
            slopes, hd, hd, hd, hd, hd, lam_rows, g_norm.reshape(1, LANES))


def _router_body(x_ref, g_ref, wr_ref, tri_ref, xn_ref, meta_ref, wts_ref, cnt_ref, run_ref):
    @pl.when(pl.program_id(0) == 0)
    def _():
        run_ref[...] = jnp.zeros_like(run_ref)

    xn = _rms(x_ref[...], g_ref[...])
    xn_ref[...] = xn
    lane = _lane((1, LANES))
    logits = jnp.dot(xn, wr_ref[...], precision=lax.Precision.HIGHEST, preferred_element_type=F32)
    logits = jnp.where(lane < N_EXPERTS, logits, LOWEST)
    m1 = jnp.max(logits, axis=-1, keepdims=True)
    i1 = jnp.min(jnp.where(logits == m1, lane, LANES), axis=-1, keepdims=True)
    rest = jnp.where(lane == i1, LOWEST, logits)
    m2 = jnp.max(rest, axis=-1, keepdims=True)
    i2 = jnp.min(jnp.where(rest == m2, lane, LANES), axis=-1, keepdims=True)
    e = jnp.exp(m2 - m1)
    w1 = 1.0 / (1.0 + e)
    w2 = e / (1.0 + e)
    onehot = ((lane == i1) | (lane == i2)).astype(F32)
    pos = _dot(tri_ref[...], onehot.astype(BF16)) + run_ref[...]
    p1 = jnp.sum(jnp.where(lane == i1, pos, 0.0), axis=-1, keepdims=True).astype(I32)
    p2 = jnp.sum(jnp.where(lane == i2, pos, 0.0), axis=-1, keepdims=True).astype(I32)
    run_ref[...] += jnp.sum(onehot, axis=0, keepdims=True)
    meta_ref[...] = jnp.where(lane == 0, i1, jnp.where(lane == 1, i2, jnp.where(lane == 2, p1, jnp.where(lane == 3, p2, 0))))
    wts_ref[...] = jnp.where(lane == 0, w1, jnp.where(lane == 1, w2, 0.0))
    cnt_ref[...] = run_ref[...]


def moe_router(x, g, wr, tm):
    T, D = x.shape
    tri = jnp.asarray(np.tril(np.ones((tm, tm), np.float32), -1), BF16)
    return pl.pallas_call(
        _router_body,
        grid=(T // tm,),
        in_specs=[pl.BlockSpec((tm, D), lambda i: (i, 0)),
                  pl.BlockSpec((1, D), lambda i: (0, 0)),
                  pl.BlockSpec((D, LANES), lambda i: (0, 0)),
                  pl.BlockSpec((tm, tm), lambda i: (0, 0))],
        out_specs=[pl.BlockSpec((tm, D), lambda i: (i, 0)),
                   pl.BlockSpec((tm, LANES), lambda i: (i, 0)),
                   pl.BlockSpec((tm, LANES), lambda i: (i, 0)),
                   pl.BlockSpec((1, LANES), lambda i: (0, 0))],
        out_shape=[jax.ShapeDtypeStruct((T, D), F32), jax.ShapeDtypeStruct((T, LANES), I32),
                   jax.ShapeDtypeStruct((T, LANES), F32), jax.ShapeDtypeStruct((1, LANES), F32)],
        scratch_shapes=[pltpu.VMEM((1, LANES), F32)],
        compiler_params=_cp("arbitrary"), name="moe_router")(x, g.reshape(1, D), wr, tri)


ROW_DMA_UNROLL = 8


def _row_copy(src_ref, src_row, dst_ref, dst_row, sem):
    return pltpu.make_async_copy(src_ref.at[pl.ds(src_row, 1)], dst_ref.at[pl.ds(dst_row, 1)], sem)


def _dispatch_body(dest_ref, xn_ref, zero_ref, xs_ref, sem, *, tm):
    del zero_ref
    base = pl.program_id(0) * tm

    def issue(t, c):
        for slot in range(TOP_K):
            _row_copy(xn_ref, t, xs_ref, dest_ref[TOP_K * (base + t) + slot], sem).start(priority=slot)
        return c

    lax.fori_loop(0, tm, issue, 0, unroll=ROW_DMA_UNROLL)

    def drain(t, c):
        for slot in range(TOP_K):
            _row_copy(xn_ref, 0, xs_ref, 0, sem).wait()
        return c

    lax.fori_loop(0, tm, drain, 0, unroll=ROW_DMA_UNROLL)


def moe_dispatch(dest, xn, rows, tm):
    T, D = xn.shape
    grid_spec = pltpu.PrefetchScalarGridSpec(
        num_scalar_prefetch=1,
        grid=(T // tm,),
        in_specs=[pl.BlockSpec((tm, D), lambda i, d: (i, 0)),
                  pl.BlockSpec(memory_space=pl.ANY)],
        out_specs=pl.BlockSpec(memory_space=pl.ANY),
        scratch_shapes=[pltpu.SemaphoreType.DMA])
    return pl.pallas_call(
        functools.partial(_dispatch_body, tm=tm),
        grid_spec=grid_spec,
        out_shape=jax.ShapeDtypeStruct((rows, D), F32),
        input_output_aliases={2: 0},
        compiler_params=_cp("arbitrary"), name="moe_dispatch")(dest, xn, jnp.zeros((rows, D), F32))


def _experts_body(te_ref, tv_ref, xs_ref, wg_ref, wu_ref, wd_ref, o_ref, xb_ref):
    i = pl.program_id(0)

    @pl.when(pl.program_id(1) == 0)
    def _():
        xb_ref[...] = xs_ref[...].astype(BF16)
        o_ref[...] = jnp.zeros_like(o_ref)

    @pl.when(tv_ref[i] > 0)
    def _():
        xb = xb_ref[...]
        hg = _dot(xb, wg_ref[0])
        hu = _dot(xb, wu_ref[0])
        o_ref[...] += _dot((hg * jax.nn.sigmoid(hg) * hu).astype(BF16), wd_ref[0])


def moe_experts(tile_expert, tile_valid, xs, wg, wu, wd, tr, tf):
    R, D = xs.shape
    F = wg.shape[2]
    grid_spec = pltpu.PrefetchScalarGridSpec(
        num_scalar_prefetch=2,
        grid=(R // tr, F // tf),
        in_specs=[pl.BlockSpec((tr, D), lambda i, k, te, tv: (i, 0)),
                  pl.BlockSpec((1, D, tf), lambda i, k, te, tv: (te[i], 0, k)),
                  pl.BlockSpec((1, D, tf), lambda i, k, te, tv: (te[i], 0, k)),
                  pl.BlockSpec((1, tf, D), lambda i, k, te, tv: (te[i], k, 0))],
        out_specs=pl.BlockSpec((tr, D), lambda i, k, te, tv: (i, 0)),
        scratch_shapes=[pltpu.VMEM((tr, D), BF16)])
    return pl.pallas_call(
        _experts_body,
        grid_spec=grid_spec,
        out_shape=jax.ShapeDtypeStruct((R, D), F32),
        compiler_params=_cp("parallel", "arbitrary"), name="moe_experts")(tile_expert, tile_valid, xs, wg, wu, wd)


def _combine_body(dest_ref, x_ref, wts_ref, g_ref, ys_ref, o_ref, buf_ref, sem, *, tm):
    base = pl.program_id(0) * tm

    def issue(t, c):
        for slot in range(TOP_K):
            _row_copy(ys_ref, dest_ref[TOP_K * (base + t) + slot], buf_ref.at[slot], t, sem).start(priority=slot)
        return c

    lax.fori_loop(0, tm, issue, 0, unroll=ROW_DMA_UNROLL)

    def drain(t, c):
        for slot in range(TOP_K):
            _row_copy(ys_ref, 0, buf_ref.at[slot], 0, sem).wait()
        return c

    lax.fori_loop(0, tm, drain, 0, unroll=ROW_DMA_UNROLL)
    wts = wts_ref[...]
    y = x_ref[...] + (wts[:, 0:1] * buf_ref[0] + wts[:, 1:2] * buf_ref[1])
    o_ref[...] = _rms(y, g_ref[...])


def moe_combine(dest, x, wts, g_final, ys, tm):
    T, D = x.shape
    grid_spec = pltpu.PrefetchScalarGridSpec(
        num_scalar_prefetch=1,
        grid=(T // tm,),
        in_specs=[pl.BlockSpec((tm, D), lambda i, d: (i, 0)),
                  pl.BlockSpec((tm, LANES), lambda i, d: (i, 0)),
                  pl.BlockSpec((1, D), lambda i, d: (0, 0)),
                  pl.BlockSpec(memory_space=pl.ANY)],
        out_specs=pl.BlockSpec((tm, D), lambda i, d: (i, 0)),
        scratch_shapes=[pltpu.VMEM((TOP_K, tm, D), F32), pltpu.SemaphoreType.DMA])
    return pl.pallas_call(
        functools.partial(_combine_body, tm=tm),
        grid_spec=grid_spec,
        out_shape=jax.ShapeDtypeStruct((T, D), F32),
        compiler_params=_cp("arbitrary"), name="moe_combine")(dest, x, wts, g_final.reshape(1, D), ys)


def moe_block(x, g, w_router, wg, wu, wd, g_final, tr):
    T, D = x.shape
    wr = _pad_cols(w_router, LANES)
    xn, meta, wts, counts = moe_router(x, g, wr, 512)
    cnt = counts[0, :N_EXPERTS].astype(I32)
    gsz = ((cnt + tr - 1) // tr) * tr
    gend = jnp.cumsum(gsz)
    goff = gend - gsz
    dest = (jnp.take(goff, meta[:, 0:TOP_K]) + meta[:, TOP_K:2 * TOP_K]).reshape(-1)
    n_tiles = (TOP_K * T) // tr + N_EXPERTS
    tile_start = jnp.arange(n_tiles, dtype=I32) * tr
    tile_valid = (tile_start < gend[-1]).astype(I32)
    tile_expert = jnp.minimum(jnp.sum((tile_start[:, None] >= gend[None, :]).astype(I32), axis=1), N_EXPERTS - 1)
    last_valid = jnp.take(tile_expert, jnp.maximum(gend[-1] // tr - 1, 0))
    tile_expert = jnp.where(tile_valid > 0, tile_expert, last_valid)
    xs = moe_dispatch(dest, xn, n_tiles * tr, 256)
    ys = moe_experts(tile_expert, tile_valid, xs, wg, wu, wd, tr, FFN_TF)
    return moe_combine(dest, x, wts, g_final, ys, 256)


def _pad_cols(w, width):
    return jnp.pad(w, ((0, 0), (0, width - w.shape[1])))


def _mem_kv(mem2, g, wk, wv, batch):
    w = jnp.concatenate([wk, wv], axis=1).astype(BF16)
    width = wk.shape[1]
    k, v = norm_proj(mem2, g, w, [(width, BF16), (width, BF16)], 256, "mem_kv")
    return k.reshape(batch, -1, width), v.reshape(batch, -1, width)


def kernel(x, mem, mix_norm_0, w_in_0, gla_w_alpha_0, gla_b_alpha_0, gla_out_norm_0, nsa_cmp_pos_k_0, nsa_cmp_w1_k_0, nsa_cmp_w2_k_0, nsa_cmp_pos_v_0, nsa_cmp_w1_v_0, nsa_cmp_w2_v_0, w_out_0, xattn_norm_0, xattn_mem_norm_0, xattn_wq_0, xattn_wk_0, xattn_wv_0, xattn_wo_0, ffn_norm_0, ffn_w_gate_0, ffn_w_up_0, ffn_w_down_0, mix_norm_1, w_in_1, diff_lq1_1, diff_lk1_1, diff_lq2_1, diff_lk2_1, diff_out_norm_1, w_out_1, xattn_norm_1, xattn_mem_norm_1, xattn_wq_1, xattn_wk_1, xattn_wv_1, xattn_wo_1, ffn_norm_1, moe_router_1, moe_w_gate_1, moe_w_up_1, moe_w_down_1, final_norm):
    B, S, D = x.shape
    T = B * S
    x2 = x.reshape(T, D)
    mem2 = mem.reshape(-1, D)
    q_scale = NSA_DH ** -0.5

    sizes = [GLA_HEADS * GLA_DK, GLA_HEADS * GLA_DK, GLA_HEADS * GLA_DV, GLA_HEADS * GLA_DV, GLA_RANK,
             NSA_HEADS * NSA_DH] + [NSA_GROUPS * NSA_DH] * 6 + [3 * NSA_HEADS]
    (g_q, g_k, g_v, g_r, g_a, n_q, n_kc, n_vc, n_ks, n_vs, n_kw, n_vw, n_g) = jnp.split(
        w_in_0, np.cumsum(sizes)[:-1].tolist(), axis=1)
    w0 = jnp.concatenate([n_q[:, NSA_PERM] * q_scale, n_ks, n_vs, n_kw, n_vw, g_q * q_scale, g_k, g_v,
                          _pad_cols(g_a, LANES), g_r, _pad_cols(n_g, LANES), n_kc, n_vc], axis=1).astype(BF16)
    hb, hf = norm_proj(x2, mix_norm_0, w0, [(HB0_WIDTH, BF16), (HF0_WIDTH, F32)], 512, "in_proj0")
    hb = hb.reshape(B, S, HB0_WIDTH)
    hf = hf.reshape(B, S, HF0_WIDTH)
    w_alpha = jnp.pad(gla_w_alpha_0, ((0, LANES - GLA_RANK), (0, 0))).astype(BF16)
    o_a = gla_mixer(hb, hf, w_alpha, gla_b_alpha_0.reshape(1, -1), gla_out_norm_0, GLA_COLS, 8)
    o_b = nsa_mixer(hb, hf, nsa_cmp_pos_k_0, nsa_cmp_w1_k_0, nsa_cmp_w2_k_0,
                    nsa_cmp_pos_v_0, nsa_cmp_w1_v_0, nsa_cmp_w2_v_0)
    n_a = GLA_HEADS * GLA_DV
    k0, v0 = _mem_kv(mem2, xattn_mem_norm_0, xattn_wk_0, xattn_wv_0, B)
    x2 = proj_xattn_block(x2, [o_a.reshape(T, -1), o_b.reshape(T, -1)],
                          [w_out_0[:n_a].astype(BF16), w_out_0[n_a:][NSA_PERM].astype(BF16)],
                          xattn_norm_0, (xattn_wq_0 * q_scale).astype(BF16), k0, v0, xattn_wo_0.astype(BF16),
                          S, 1024, "out_xattn0")
    x2 = ffn_block(x2, ffn_norm_0, ffn_w_gate_0.astype(BF16), ffn_w_up_0.astype(BF16), ffn_w_down_0.astype(BF16),
                   1024, FFN_TF, "ffn0")

    lambda_init = 0.8 - 0.6 * math.exp(-0.3 * 1)
    n_q1 = 2 * DIFF_HEADS * DIFF_DH
    w1 = jnp.concatenate([w_in_1[:, :n_q1] * q_scale, w_in_1[:, n_q1:]], axis=1).astype(BF16)
    (hd,) = norm_proj(x2, mix_norm_1, w1, [(w1.shape[1], BF16)], 512, "in_proj1")
    lam_rows = _pad_cols(jnp.stack([diff_lq1_1, diff_lk1_1, diff_lq2_1, diff_lk2_1]), LANES)
    o_d = diff_mixer(hd.reshape(B, S, -1), lam_rows, diff_out_norm_1, lambda_init, 256, 256, 4)
    k1, v1 = _mem_kv(mem2, xattn_mem_norm_1, xattn_wk_1, xattn_wv_1, B)
    x2 = proj_xattn_block(x2, [o_d.reshape(T, -1)], [w_out_1.astype(BF16)],
                          xattn_norm_1, (xattn_wq_1 * q_scale).astype(BF16), k1, v1, xattn_wo_1.astype(BF16),
                          S, 1024, "out_xattn1")
    out = moe_block(x2, ffn_norm_1, moe_router_1, moe_w_gate_1.astype(BF16), moe_w_up_1.astype(BF16),
                    moe_w_down_1.astype(BF16), final_norm, 512)
    return out.reshape(B, S, D)
```
